```python
import math
import jax, jax.numpy as jnp
from jax import lax
import numpy as np

D_MODEL = 1024
BATCH = 8
SEQ = 2048
DEPTH = 1
DEC_BATCH = 8
DEC_SEQ = 64
PAST_LEN = 4096

CHUNK = 64
EPS = 1e-6
D_S5 = 512
S5_GROUP = 16
S5_GROUPS = D_S5 // S5_GROUP
S5_STATE = 64
ML_HEADS = 4
ML_HEAD_DIM = 128
D_ML = ML_HEADS * ML_HEAD_DIM
N_EXPERTS = 32
TOP_K = 4
D_FF = 1024
SWIGLU_LIMIT = 7.0
SWIGLU_ALPHA = 1.702
MOE_BLOCK = 128
D_IN = D_S5 + 4 * D_ML + 2 * ML_HEADS + 2 * D_MODEL

kernel_name = 'hybrid_s5_mlstm_moe_stream_step'


def rmsnorm(x, w):
    xf = x.astype(jnp.float32)
    xf = xf * lax.rsqrt(jnp.mean(xf * xf, axis=-1, keepdims=True) + EPS)
    return (xf * w.astype(jnp.float32)).astype(x.dtype)


def _complex_affine_combine(e1, e2):
    a1r, a1i, b1r, b1i = e1
    a2r, a2i, b2r, b2i = e2
    return (a2r * a1r - a2i * a1i,
            a2r * a1i + a2i * a1r,
            a2r * b1r - a2i * b1i + b2r,
            a2r * b1i + a2i * b1r + b2i)


def s5_scan(u, h0_re, h0_im, a_re, a_im, log_dt, b_re, b_im, c_re, c_im, d_skip):
    f32 = jnp.float32
    bsz, L, _ = u.shape
    uf = u.astype(f32).reshape(bsz, L, S5_GROUPS, S5_GROUP)
    ar = a_re.astype(f32)
    ai = a_im.astype(f32)
    dt = jnp.exp(log_dt.astype(f32))[:, None]
    mag = jnp.exp(dt * ar)
    abar_re = mag * jnp.cos(dt * ai)
    abar_im = mag * jnp.sin(dt * ai)
    den = ar * ar + ai * ai
    fr = ((abar_re - 1.0) * ar + abar_im * ai) / den
    fi = (abar_im * ar - (abar_re - 1.0) * ai) / den
    br = b_re.astype(f32)
    bi = b_im.astype(f32)
    bbar_re = fr[..., None] * br - fi[..., None] * bi
    bbar_im = fr[..., None] * bi + fi[..., None] * br
    bu_re = jnp.einsum('blgc,gpc->blgp', uf, bbar_re)
    bu_im = jnp.einsum('blgc,gpc->blgp', uf, bbar_im)
    a_l_re = jnp.broadcast_to(abar_re, bu_re.shape)
    a_l_im = jnp.broadcast_to(abar_im, bu_im.shape)
    p_re, p_im, s_re, s_im = lax.associative_scan(
        _complex_affine_combine, (a_l_re, a_l_im, bu_re, bu_im), axis=1)
    h0r = h0_re.astype(f32)[:, None]
    h0i = h0_im.astype(f32)[:, None]
    h_re = s_re + p_re * h0r - p_im * h0i
    h_im = s_im + p_re * h0i + p_im * h0r
    y = (jnp.einsum('blgp,gcp->blgc', h_re, c_re.astype(f32))
         - jnp.einsum('blgp,gcp->blgc', h_im, c_im.astype(f32)))
    y = y.reshape(bsz, L, D_S5) + d_skip.astype(f32) * uf.reshape(bsz, L, D_S5)
    return y, h_re[:, -1], h_im[:, -1]


def mlstm_chunk(carry, blk):
    C, n, m = carry
    q, k, v, ig, lf = blk
    L = q.shape[2]
    b = jnp.cumsum(lf, axis=-1)
    causal = jnp.tril(jnp.ones((L, L), dtype=bool))
    dmat = jnp.where(causal, b[..., :, None] - b[..., None, :] + ig[..., None, :], -jnp.inf)
    inter = b + m[..., None]
    m_t = jnp.maximum(inter, jnp.max(dmat, axis=-1))
    w_inter = jnp.exp(inter - m_t)
    s = jnp.einsum('bhtd,bhsd->bhts', q, k) * jnp.exp(dmat - m_t[..., None])
    num = jnp.einsum('bhts,bhsd->bhtd', s, v) + w_inter[..., None] * jnp.einsum('bhvk,bhtk->bhtv', C, q)
    den_dot = jnp.sum(s, axis=-1) + w_inter * jnp.einsum('bhk,bhtk->bht', n, q)
    den = jnp.maximum(jnp.abs(den_dot), jnp.exp(-m_t))
    h = num / den[..., None]
    m_new = m_t[..., -1]
    w_end = jnp.exp(b[..., -1:] - b + ig - m_new[..., None])
    decay = jnp.exp(inter[..., -1] - m_new)
    C_new = decay[..., None, None] * C + jnp.einsum('bhs,bhsv,bhsk->bhvk', w_end, v, k)
    n_new = decay[..., None] * n + jnp.einsum('bhs,bhsk->bhk', w_end, k)
    return (C_new, n_new, m_new), h


def mixer(xn, h0_re, h0_im, C0, n0, m0, w_in, b_ig, b_fg, a_re, a_im, log_dt, b_re, b_im,
          c_re, c_im, d_skip, w_s5_glu, ml_gn, w_ml_out, w_out):
    f32 = jnp.float32
    bsz, L, _ = xn.shape
    splits = np.cumsum([D_S5, D_ML, D_ML, D_ML, D_ML, ML_HEADS, ML_HEADS, D_MODEL]).tolist()
    u, q, k, v, o, ig, fg, g_s5, g_ml = jnp.split(jnp.dot(xn, w_in), splits, axis=-1)
    y, hr, hi = s5_scan(u, h0_re, h0_im, a_re, a_im, log_dt, b_re, b_im, c_re, c_im, d_skip)
    glu = jnp.dot(jax.nn.gelu(y), w_s5_glu.astype(f32))
    y_s5 = glu[..., :D_MODEL] * jax.nn.sigmoid(glu[..., D_MODEL:])
    cl = min(CHUNK, L)
    nc = L // cl

    def to_chunks(t):
        return t.astype(f32).reshape(bsz, nc, cl, ML_HEADS, -1).transpose(1, 0, 3, 2, 4)

    qc = to_chunks(q) * (ML_HEAD_DIM ** -0.5)
    kc = to_chunks(k)
    vc = to_chunks(v)
    igc = to_chunks(ig.astype(f32) + b_ig.astype(f32))[..., 0]
    lfc = to_chunks(jax.nn.log_sigmoid(fg.astype(f32) + b_fg.astype(f32)))[..., 0]
    init = (C0.astype(f32), n0.astype(f32), m0.astype(f32))
    (C, n, m), hc = lax.scan(mlstm_chunk, init, (qc, kc, vc, igc, lfc))
    h = hc.transpose(1, 0, 3, 2, 4).reshape(bsz, L, ML_HEADS, ML_HEAD_DIM)
    hcent = h - jnp.mean(h, axis=-1, keepdims=True)
    hn = hcent * lax.rsqrt(jnp.mean(hcent * hcent, axis=-1, keepdims=True) + EPS)
    hn = hn.reshape(bsz, L, D_ML) * ml_gn.astype(f32)
    y_ml = jnp.dot(jax.nn.sigmoid(o.astype(f32)) * hn, w_ml_out.astype(f32))
    mix = jax.nn.sigmoid(g_s5.astype(f32)) * y_s5 + jax.nn.sigmoid(g_ml.astype(f32)) * y_ml
    out = jnp.dot(mix.astype(xn.dtype), w_out)
    return out, (hr, hi, C, n, m)


def moe(x, w_router, b_router, w_gate_up, b_gate_up, w_down, b_down):
    f32 = jnp.float32
    T = x.shape[0]
    logits = jnp.dot(x, w_router).astype(f32) + b_router.astype(f32)
    top_val, top_idx = lax.top_k(logits, TOP_K)
    gates = jax.nn.softmax(top_val, axis=-1)
    M = T * TOP_K
    flat_e = top_idx.reshape(M)
    order = jnp.argsort(flat_e)
    sorted_e = flat_e[order]
    counts = jnp.bincount(flat_e, length=N_EXPERTS)
    padded = (counts + MOE_BLOCK - 1) // MOE_BLOCK * MOE_BLOCK
    starts = jnp.cumsum(counts) - counts
    pends = jnp.cumsum(padded)
    pstarts = pends - padded
    dest_sorted = pstarts[sorted_e] + jnp.arange(M, dtype=jnp.int32) - starts[sorted_e]
    dest = jnp.zeros((M,), jnp.int32).at[order].set(dest_sorted.astype(jnp.int32))
    n_blocks = -(-(M + N_EXPERTS * (MOE_BLOCK - 1)) // MOE_BLOCK)
    buf = jnp.zeros((n_blocks * MOE_BLOCK, x.shape[1]), x.dtype).at[dest].set(jnp.repeat(x, TOP_K, axis=0))
    block_e = jnp.minimum(
        jnp.searchsorted(pends, jnp.arange(n_blocks, dtype=jnp.int32) * MOE_BLOCK, side='right'),
        N_EXPERTS - 1)

    def expert_block(args):
        xb, e = args
        gu = jnp.dot(xb, w_gate_up[e]) + b_gate_up[e]
        g = jnp.minimum(gu[:, :D_FF], SWIGLU_LIMIT)
        up = jnp.clip(gu[:, D_FF:], -SWIGLU_LIMIT, SWIGLU_LIMIT)
        hdn = (up + 1.0) * (g * jax.nn.sigmoid(SWIGLU_ALPHA * g))
        return jnp.dot(hdn, w_down[e]) + b_down[e]

    out_blocks = lax.map(expert_block, (buf.reshape(n_blocks, MOE_BLOCK, -1), block_e))
    out_rows = out_blocks.reshape(n_blocks * MOE_BLOCK, -1)
    y = jnp.einsum('tkd,tk->td', out_rows[dest].reshape(T, TOP_K, -1).astype(f32), gates)
    return y.astype(x.dtype)


def encoder_step(x, states, params):
    s5_re, s5_im, ml_C, ml_n, ml_m = states
    (norm_mix, w_in, b_ig, b_fg, s5_a_re, s5_a_im, s5_log_dt, s5_b_re, s5_b_im, s5_c_re, s5_c_im,
     s5_d, w_s5_glu, ml_gn, w_ml_out, w_out, norm_moe, w_router, b_router, w_gate_up, b_gate_up,
     w_down, b_down, norm_final) = params
    new_re, new_im, new_C, new_n, new_m = [], [], [], [], []
    for l in range(DEPTH):
        mix, (hr, hi, C, n, m) = mixer(
            rmsnorm(x, norm_mix[l]), s5_re[l], s5_im[l], ml_C[l], ml_n[l], ml_m[l],
            w_in[l], b_ig[l], b_fg[l], s5_a_re[l], s5_a_im[l], s5_log_dt[l], s5_b_re[l], s5_b_im[l],
            s5_c_re[l], s5_c_im[l], s5_d[l], w_s5_glu[l], ml_gn[l], w_ml_out[l], w_out[l])
        x = x + mix
        bsz, L, _ = x.shape
        ff = moe(rmsnorm(x, norm_moe[l]).reshape(bsz * L, D_MODEL), w_router[l], b_router[l],
                 w_gate_up[l], b_gate_up[l], w_down[l], b_down[l])
        x = x + ff.reshape(bsz, L, D_MODEL)
        new_re.append(hr)
        new_im.append(hi)
        new_C.append(C)
        new_n.append(n)
        new_m.append(m)
    y = rmsnorm(x, norm_final)
    return y, (jnp.stack(new_re), jnp.stack(new_im), jnp.stack(new_C), jnp.stack(new_n), jnp.stack(new_m))


def setup_inputs(seed: int = 0) -> dict:
    key = jax.random.key(seed)
    ks = jax.random.split(key, 32)
    f32 = jnp.float32

    def nrm(k, shape, scale):
        return jax.random.normal(k, shape, f32) * scale

    n_idx = jnp.arange(S5_STATE, dtype=f32)
    s5_shape = (DEPTH, S5_GROUPS, S5_STATE)
    return {
        'x_prompt': nrm(ks[0], (BATCH, SEQ, D_MODEL), 1.0),
        'x_sample': nrm(ks[1], (DEC_BATCH, DEC_SEQ, D_MODEL), 1.0),
        'state_s5_re': nrm(ks[2], (DEPTH, DEC_BATCH, S5_GROUPS, S5_STATE), 0.1),
        'state_s5_im': nrm(ks[3], (DEPTH, DEC_BATCH, S5_GROUPS, S5_STATE), 0.1),
        'state_ml_C': nrm(ks[4], (DEPTH, DEC_BATCH, ML_HEADS, ML_HEAD_DIM, ML_HEAD_DIM), 0.1),
        'state_ml_n': nrm(ks[5], (DEPTH, DEC_BATCH, ML_HEADS, ML_HEAD_DIM), 0.1),
        'state_ml_m': jax.random.uniform(ks[6], (DEPTH, DEC_BATCH, ML_HEADS), f32, 0.0, 4.0),
        'norm_mix': 1.0 + nrm(ks[7], (DEPTH, D_MODEL), 0.01),
        'w_in': nrm(ks[8], (DEPTH, D_MODEL, D_IN), D_MODEL ** -0.5),
        'b_ig': nrm(ks[9], (DEPTH, ML_HEADS), 0.1) - 1.0,
        'b_fg': jnp.linspace(3.0, 6.0, ML_HEADS, dtype=f32) + nrm(ks[10], (DEPTH, ML_HEADS), 0.1),
        's5_a_re': -0.5 + nrm(ks[11], s5_shape, 0.01),
        's5_a_im': math.pi * n_idx + nrm(ks[12], s5_shape, 0.01),
        's5_log_dt': jax.random.uniform(ks[13], (DEPTH, S5_GROUPS), f32, math.log(1e-3), math.log(1e-1)),
        's5_b_re': nrm(ks[14], (DEPTH, S5_GROUPS, S5_STATE, S5_GROUP), (2 * S5_GROUP) ** -0.5),
        's5_b_im': nrm(ks[15], (DEPTH, S5_GROUPS, S5_STATE, S5_GROUP), (2 * S5_GROUP) ** -0.5),
        's5_c_re': nrm(ks[16], (DEPTH, S5_GROUPS, S5_GROUP, S5_STATE), (2 * S5_STATE) ** -0.5),
        's5_c_im': nrm(ks[17], (DEPTH, S5_GROUPS, S5_GROUP, S5_STATE), (2 * S5_STATE) ** -0.5),
        's5_d': nrm(ks[18], (DEPTH, D_S5), 1.0),
        'w_s5_glu': nrm(ks[19], (DEPTH, D_S5, 2 * D_MODEL), D_S5 ** -0.5),
        'ml_gn': 1.0 + nrm(ks[20], (DEPTH, D_ML), 0.01),
        'w_ml_out': nrm(ks[21], (DEPTH, D_ML, D_MODEL), D_ML ** -0.5),
        'w_out': nrm(ks[22], (DEPTH, D_MODEL, D_MODEL), D_MODEL ** -0.5),
        'norm_moe': 1.0 + nrm(ks[23], (DEPTH, D_MODEL), 0.01),
        'w_router': nrm(ks[24], (DEPTH, D_MODEL, N_EXPERTS), D_MODEL ** -0.5),
        'b_router': nrm(ks[25], (DEPTH, N_EXPERTS), 0.01),
        'w_gate_up': nrm(ks[26], (DEPTH, N_EXPERTS, D_MODEL, 2 * D_FF), D_MODEL ** -0.5),
        'b_gate_up': nrm(ks[27], (DEPTH, N_EXPERTS, 2 * D_FF), 0.01),
        'w_down': nrm(ks[28], (DEPTH, N_EXPERTS, D_FF, D_MODEL), D_FF ** -0.5),
        'b_down': nrm(ks[29], (DEPTH, N_EXPERTS, D_MODEL), 0.01),
        'norm_final': 1.0 + nrm(ks[30], (D_MODEL,), 0.01),
    }


def reference(x_prompt, x_sample, state_s5_re, state_s5_im, state_ml_C, state_ml_n, state_ml_m,
              norm_mix, w_in, b_ig, b_fg, s5_a_re, s5_a_im, s5_log_dt, s5_b_re, s5_b_im,
              s5_c_re, s5_c_im, s5_d, w_s5_glu, ml_gn, w_ml_out, w_out, norm_moe, w_router,
              b_router, w_gate_up, b_gate_up, w_down, b_down, norm_final):
    params = (norm_mix, w_in, b_ig, b_fg, s5_a_re, s5_a_im, s5_log_dt, s5_b_re, s5_b_im, s5_c_re,
              s5_c_im, s5_d, w_s5_glu, ml_gn, w_ml_out, w_out, norm_moe, w_router, b_router,
              w_gate_up, b_gate_up, w_down, b_down, norm_final)
    bp = x_prompt.shape[0]
    f32 = jnp.float32
    zero_states = (jnp.zeros((DEPTH, bp, S5_GROUPS, S5_STATE), f32),
                   jnp.zeros((DEPTH, bp, S5_GROUPS, S5_STATE), f32),
                   jnp.zeros((DEPTH, bp, ML_HEADS, ML_HEAD_DIM, ML_HEAD_DIM), f32),
                   jnp.zeros((DEPTH, bp, ML_HEADS, ML_HEAD_DIM), f32),
                   jnp.zeros((DEPTH, bp, ML_HEADS), f32))
    y_prompt, (p_re, p_im, p_C, p_n, p_m) = encoder_step(x_prompt, zero_states, params)
    y_sample, (s_re, s_im, s_C, s_n, s_m) = encoder_step(
        x_sample, (state_s5_re, state_s5_im, state_ml_C, state_ml_n, state_ml_m), params)
    return (y_prompt, y_sample, p_re, p_im, p_C, p_n, p_m, s_re, s_im, s_C, s_n, s_m)
```

```python
import functools
import math

import jax
import jax.numpy as jnp
from jax import lax
from jax.experimental import pallas as pl
from jax.experimental.pallas import tpu as pltpu

F32 = jnp.float32
BF16 = jnp.bfloat16

D_MODEL = 1024
CHUNK = 64
EPS = 1e-6
D_S5 = 512
S5_GROUP = 16
S5_GROUPS = D_S5 // S5_GROUP
S5_STATE = 64
S5_CH = S5_GROUPS * S5_STATE
ML_HEADS = 4
ML_HEAD_DIM = 128
D_ML = ML_HEADS * ML_HEAD_DIM
N_EXPERTS = 32
TOP_K = 4
D_FF = 1024
SWIGLU_LIMIT = 7.0
SWIGLU_ALPHA = 1.702

LANES = 128
SUBLANES = 8
TILE_ROWS = D_MODEL // LANES

OFF_U = 0
OFF_Q = OFF_U + D_S5
OFF_K = OFF_Q + D_ML
OFF_V = OFF_K + D_ML
OFF_O = OFF_V + D_ML
OFF_GS5 = OFF_O + D_ML
OFF_GML = OFF_GS5 + D_MODEL
OFF_GATE = OFF_GML + D_MODEL
W_COLS = OFF_GATE + LANES

S5_LANE_CHUNK = 512
S5_SCAN_TILES = S5_LANE_CHUNK // LANES
MOE_BLOCK = 256
PLAN_TILE = 512
COMBINE_TILE = 256
VMEM_LIMIT = 60 * 1024 * 1024


def _rmsnorm(x, w):
    return x * lax.rsqrt(jnp.mean(x * x, axis=-1, keepdims=True) + EPS) * w


def _const_spec(shape):
    nd = len(shape)
    return pl.BlockSpec(shape, lambda *_: (0,) * nd, pipeline_mode=pl.Buffered(1))


def _mixer_kernel(x_ref, h0r_ref, h0i_ref, c0_ref, n0_ref, m0_ref,
                  nmix_ref, win_ref, gbias_ref, abr_ref, abi_ref, bdr_ref, bdi_ref, cb_ref,
                  dskip_ref, glu_ref, gn_ref, wml_ref, wout_ref, nmoe_ref, wr_ref, br_ref,
                  x1_ref, xm_ref, idx_ref, gate_ref, hr_ref, hi_ref, c_ref, n_ref, m_ref,
                  s_hr, s_hi, s_c, s_n, s_m, s_bur, s_bui, s_q, s_k, s_v, s_o, s_gg, s_bc,
                  s_row, s_gated, *, batch):
    i = pl.program_id(0)
    last = pl.num_programs(0) - 1
    rows = batch * CHUNK

    @pl.when(i == 0)
    def _():
        s_hr[...] = h0r_ref[...]
        s_hi[...] = h0i_ref[...]
        s_c[...] = c0_ref[...]
        s_n[...] = n0_ref[...]
        s_m[...] = m0_ref[...]

    x = x_ref[...].reshape(rows, D_MODEL)
    xn = _rmsnorm(x, nmix_ref[...]).astype(BF16)

    def proj(off, width):
        return jnp.dot(xn, win_ref[:, off:off + width], preferred_element_type=F32)

    u = proj(OFF_U, D_S5)
    ub = u.astype(BF16)
    half_in = D_S5 // 2
    half_ch = S5_CH // 2
    half_tiles = half_ch // LANES
    for k in range(2):
        uk = ub[:, k * half_in:(k + 1) * half_in]
        bur = jnp.dot(uk, bdr_ref[k], preferred_element_type=F32)
        bui = jnp.dot(uk, bdi_ref[k], preferred_element_type=F32)
        for j in range(half_tiles):
            s_bur[k * half_tiles + j] = bur[:, j * LANES:(j + 1) * LANES]
            s_bui[k * half_tiles + j] = bui[:, j * LANES:(j + 1) * LANES]

    for c in range(S5_CH // S5_LANE_CHUNK):
        tiles = range(c * S5_SCAN_TILES, (c + 1) * S5_SCAN_TILES)
        ar = [jnp.broadcast_to(abr_ref[:, j * LANES:(j + 1) * LANES], (batch, LANES)) for j in tiles]
        ai = [jnp.broadcast_to(abi_ref[:, j * LANES:(j + 1) * LANES], (batch, LANES)) for j in tiles]

        def step(t, carry, tiles=tiles, ar=ar, ai=ai):
            slab = pl.ds(t, batch, stride=CHUNK)
            out = []
            for n, j in enumerate(tiles):
                hr, hi = carry[n]
                nr = ar[n] * hr - ai[n] * hi + s_bur[j, slab, :]
                ni = ar[n] * hi + ai[n] * hr + s_bui[j, slab, :]
                s_bur[j, slab, :] = nr
                s_bui[j, slab, :] = ni
                out.append((nr, ni))
            return tuple(out)

        init = tuple((s_hr[:, j * LANES:(j + 1) * LANES], s_hi[:, j * LANES:(j + 1) * LANES])
                     for j in tiles)
        fin = lax.fori_loop(0, CHUNK, step, init, unroll=8)
        for n, j in enumerate(tiles):
            s_hr[:, j * LANES:(j + 1) * LANES] = fin[n][0]
            s_hi[:, j * LANES:(j + 1) * LANES] = fin[n][1]

    ys = []
    for k in range(2):
        h_re = jnp.concatenate([s_bur[k * half_tiles + j] for j in range(half_tiles)], axis=1)
        h_im = jnp.concatenate([s_bui[k * half_tiles + j] for j in range(half_tiles)], axis=1)
        yk = jnp.dot(h_re.astype(BF16), cb_ref[k, :half_ch, :], preferred_element_type=F32)
        yk = yk + jnp.dot(h_im.astype(BF16), cb_ref[k, half_ch:, :], preferred_element_type=F32)
        ys.append(yk)
    y = jnp.concatenate(ys, axis=1) + dskip_ref[...] * u
    glu = jnp.dot(jax.nn.gelu(y).astype(BF16), glu_ref[...], preferred_element_type=F32)
    y_s5 = glu[:, :D_MODEL] * jax.nn.sigmoid(glu[:, D_MODEL:])
    mix = jax.nn.sigmoid(proj(OFF_GS5, D_MODEL)) * y_s5

    s_q[...] = (proj(OFF_Q, D_ML) * (ML_HEAD_DIM ** -0.5)).astype(BF16)
    s_k[...] = proj(OFF_K, D_ML).astype(BF16)
    s_v[...] = proj(OFF_V, D_ML).astype(BF16)
    s_o[...] = proj(OFF_O, D_ML)
    gates = proj(OFF_GATE, LANES) + gbias_ref[...]
    lane_g = lax.broadcasted_iota(jnp.int32, (rows, LANES), 1)
    gg = jnp.where(lane_g < ML_HEADS, gates, jax.nn.log_sigmoid(gates))
    s_gg[...] = gg
    gt8 = gg.T[0:SUBLANES, :]
    pos = lax.broadcasted_iota(jnp.int32, (SUBLANES, rows), 1) % CHUNK
    cum = gt8
    shift = 1
    while shift < CHUNK:
        cum = cum + jnp.where(pos >= shift, pltpu.roll(cum, shift, axis=1), 0.0)
        shift *= 2
    sub = lax.broadcasted_iota(jnp.int32, (SUBLANES, rows), 0)
    row8 = jnp.where(sub < ML_HEADS, gt8, cum)
    for b in range(batch):
        s_row[b] = row8[:, b * CHUNK:(b + 1) * CHUNK]
    s_bc[...] = jnp.concatenate([row8, jnp.zeros((LANES - SUBLANES, rows), F32)], axis=0).T

    tri = (lax.broadcasted_iota(jnp.int32, (CHUNK, CHUNK), 0)
           >= lax.broadcasted_iota(jnp.int32, (CHUNK, CHUNK), 1))

    def stream_body(b, carry):
        r0 = pl.multiple_of(b * CHUNK, CHUNK)
        rs = pl.ds(r0, CHUNK)
        row = s_row[b]
        for h in range(ML_HEADS):
            hs = slice(h * ML_HEAD_DIM, (h + 1) * ML_HEAD_DIM)
            bh = b * ML_HEADS + h
            q = s_q[rs, hs]
            kk = s_k[rs, hs]
            v = s_v[rs, hs]
            b_c = s_bc[rs, ML_HEADS + h:ML_HEADS + h + 1]
            ig_c = s_gg[rs, h:h + 1]
            b_r = row[ML_HEADS + h:ML_HEADS + h + 1, :]
            ig_r = row[h:h + 1, :]
            cmat = s_c[bh]
            nvec = s_n[pl.ds(bh, 1), :]
            m_prev = s_m[pl.ds(bh, 1), :][:, 0:1]

            dmat = jnp.where(tri, b_c - b_r + ig_r, -jnp.inf)
            inter = b_c + m_prev
            m_t = jnp.maximum(inter, jnp.max(dmat, axis=1, keepdims=True))
            w_inter = jnp.exp(inter - m_t)
            s = lax.dot_general(q, kk, (((1,), (1,)), ((), ())), preferred_element_type=F32)
            s = s * jnp.exp(dmat - m_t)
            cq = lax.dot_general(q, cmat.astype(BF16), (((1,), (1,)), ((), ())),
                                 preferred_element_type=F32)
            num = jnp.dot(s.astype(BF16), v, preferred_element_type=F32) + w_inter * cq
            nq = jnp.sum(q.astype(F32) * nvec, axis=1, keepdims=True)
            den_dot = jnp.sum(s, axis=1, keepdims=True) + w_inter * nq
            den = jnp.maximum(jnp.abs(den_dot), jnp.exp(-m_t))
            hout = num / den

            m_new = m_t[CHUNK - 1:CHUNK, :]
            w_end = jnp.exp(b_c[CHUNK - 1:CHUNK, :] - b_c + ig_c - m_new)
            decay = jnp.exp(inter[CHUNK - 1:CHUNK, :] - m_new)
            wv_t = (w_end * v.astype(F32)).T.astype(BF16)
            s_c[bh] = decay * cmat + jnp.dot(wv_t, kk, preferred_element_type=F32)
            s_n[pl.ds(bh, 1), :] = decay * nvec + jnp.sum(w_end * kk.astype(F32), axis=0, keepdims=True)
            s_m[pl.ds(bh, 1), :] = jnp.broadcast_to(m_new, (1, LANES))

            hc = hout - jnp.mean(hout, axis=1, keepdims=True)
            hn = hc * lax.rsqrt(jnp.mean(hc * hc, axis=1, keepdims=True) + EPS)
            s_gated[rs, hs] = (jax.nn.sigmoid(s_o[rs, hs]) * (hn * gn_ref[:, hs])).astype(BF16)
        return carry

    lax.fori_loop(0, batch, stream_body, 0)

    y_ml = jnp.dot(s_gated[...], wml_ref[...], preferred_element_type=F32)
    mix = mix + jax.nn.sigmoid(proj(OFF_GML, D_MODEL)) * y_ml
    x1 = x + jnp.dot(mix.astype(BF16), wout_ref[...], preferred_element_type=F32)

    xm = _rmsnorm(x1, nmoe_ref[...])
    logits = jnp.dot(xm.astype(BF16), wr_ref[...], preferred_element_type=F32) + br_ref[...]
    lane = lax.broadcasted_iota(jnp.int32, (rows, LANES), 1)
    logits = jnp.where(lane < N_EXPERTS, logits, -jnp.inf)
    vals, idxs = [], []
    for _ in range(TOP_K):
        mx = jnp.max(logits, axis=1, keepdims=True)
        am = jnp.min(jnp.where(logits == mx, lane, LANES), axis=1, keepdims=True)
        vals.append(mx)
        idxs.append(am)
        logits = jnp.where(lane == am, -jnp.inf, logits)
    exps = [jnp.exp(v - vals[0]) for v in vals]
    esum = exps[0] + exps[1] + exps[2] + exps[3]
    idx_w = jnp.zeros((rows, LANES), jnp.int32)
    gate_w = jnp.zeros((rows, LANES), F32)
    for k in range(TOP_K):
        idx_w = jnp.where(lane == k, idxs[k], idx_w)
        gate_w = jnp.where(lane == k, exps[k] / esum, gate_w)
    idx_ref[...] = idx_w[:, :TOP_K].reshape(batch, CHUNK, TOP_K)
    gate_ref[...] = gate_w[:, :TOP_K].reshape(batch, CHUNK, TOP_K)
    x1_ref[...] = x1.reshape(batch, CHUNK, D_MODEL)

    for b in range(batch):
        bs = slice(b * CHUNK, (b + 1) * CHUNK)
        for s in range(TILE_ROWS):
            tile_rows = pl.ds(s, CHUNK, stride=TILE_ROWS)
            ls = slice(s * LANES, (s + 1) * LANES)
            xm_ref[b, tile_rows, :] = xm[bs, ls]

    @pl.when(i == last)
    def _():
        hr_ref[...] = s_hr[...]
        hi_ref[...] = s_hi[...]
        c_ref[...] = s_c[...]
        n_ref[...] = s_n[...]
        m_ref[...] = s_m[...]


def _mixer(x, h0r, h0i, c0, n0, m0, w):
    batch, seq, _ = x.shape
    assert seq % CHUNK == 0
    nblk = seq // CHUNK
    rows = batch * CHUNK
    bh = batch * ML_HEADS
    tile_block = (batch, CHUNK * TILE_ROWS, LANES)
    state_specs = [
        _const_spec((batch, S5_CH)), _const_spec((batch, S5_CH)),
        _const_spec((bh, ML_HEAD_DIM, ML_HEAD_DIM)), _const_spec((bh, ML_HEAD_DIM)),
        _const_spec((bh, LANES)),
    ]
    weights = (w['norm_mix'], w['w_in'], w['gate_bias'], w['abar_re'], w['abar_im'], w['bd_re'],
               w['bd_im'], w['cb'], w['d_skip'], w['w_glu'], w['ml_gn'], w['w_ml_out'], w['w_out'],
               w['norm_moe'], w['w_router'], w['b_router'])
    in_specs = ([pl.BlockSpec((batch, CHUNK, D_MODEL), lambda i: (0, i, 0))] + state_specs
                + [_const_spec(a.shape) for a in weights])
    out_shape = (
        jax.ShapeDtypeStruct((batch, seq, D_MODEL), F32),
        jax.ShapeDtypeStruct((batch, seq * TILE_ROWS, LANES), F32),
        jax.ShapeDtypeStruct((batch, seq, TOP_K), jnp.int32),
        jax.ShapeDtypeStruct((batch, seq, TOP_K), F32),
        jax.ShapeDtypeStruct((batch, S5_CH), F32),
        jax.ShapeDtypeStruct((batch, S5_CH), F32),
        jax.ShapeDtypeStruct((bh, ML_HEAD_DIM, ML_HEAD_DIM), F32),
        jax.ShapeDtypeStruct((bh, ML_HEAD_DIM), F32),
        jax.ShapeDtypeStruct((bh, LANES), F32),
    )
    out_specs = (
        pl.BlockSpec((batch, CHUNK, D_MODEL), lambda i: (0, i, 0)),
        pl.BlockSpec(tile_block, lambda i: (0, i, 0)),
        pl.BlockSpec((batch, CHUNK, TOP_K), lambda i: (0, i, 0)),
        pl.BlockSpec((batch, CHUNK, TOP_K), lambda i: (0, i, 0)),
        pl.BlockSpec((batch, S5_CH), lambda i: (0, 0)),
        pl.BlockSpec((batch, S5_CH), lambda i: (0, 0)),
        pl.BlockSpec((bh, ML_HEAD_DIM, ML_HEAD_DIM), lambda i: (0, 0, 0)),
        pl.BlockSpec((bh, ML_HEAD_DIM), lambda i: (0, 0)),
        pl.BlockSpec((bh, LANES), lambda i: (0, 0)),
    )
    scratch = [
        pltpu.VMEM((batch, S5_CH), F32), pltpu.VMEM((batch, S5_CH), F32),
        pltpu.VMEM((bh, ML_HEAD_DIM, ML_HEAD_DIM), F32), pltpu.VMEM((bh, ML_HEAD_DIM), F32),
        pltpu.VMEM((bh, LANES), F32),
        pltpu.VMEM((S5_CH // LANES, rows, LANES), F32), pltpu.VMEM((S5_CH // LANES, rows, LANES), F32),
        pltpu.VMEM((rows, D_ML), BF16), pltpu.VMEM((rows, D_ML), BF16), pltpu.VMEM((rows, D_ML), BF16),
        pltpu.VMEM((rows, D_ML), F32),
        pltpu.VMEM((rows, LANES), F32), pltpu.VMEM((rows, LANES), F32),
        pltpu.VMEM((batch, SUBLANES, CHUNK), F32),
        pltpu.VMEM((rows, D_ML), BF16),
    ]
    return pl.pallas_call(
        functools.partial(_mixer_kernel, batch=batch),
        grid=(nblk,),
        in_specs=in_specs,
        out_specs=out_specs,
        out_shape=out_shape,
        scratch_shapes=scratch,
        compiler_params=pltpu.CompilerParams(dimension_semantics=("arbitrary",),
                                             vmem_limit_bytes=VMEM_LIMIT),
        name="mixer",
    )(x, h0r, h0i, c0, n0, m0, *weights)


def _plan_kernel(idx_ref, dest_ref, meta_ref, s_cnt, s_run, s_pstart, *, n_blocks_pad):
    phase = pl.program_id(0)
    i = pl.program_id(1)
    nt = pl.num_programs(1)
    lane = lax.broadcasted_iota(jnp.int32, (PLAN_TILE, LANES), 1)
    idx = idx_ref[...]
    onehots = [(lane == idx[:, k:k + 1]).astype(F32) for k in range(TOP_K)]
    mask = onehots[0] + onehots[1] + onehots[2] + onehots[3]

    @pl.when(jnp.logical_and(phase == 0, i == 0))
    def _():
        s_cnt[...] = jnp.zeros_like(s_cnt)

    @pl.when(phase == 0)
    def _():
        s_cnt[...] += jnp.sum(mask, axis=0, keepdims=True)

    @pl.when(jnp.logical_and(phase == 1, i == 0))
    def _():
        cnt = s_cnt[...]
        nblk = jnp.floor((cnt + (MOE_BLOCK - 1)) * (1.0 / MOE_BLOCK))
        lane1 = lax.broadcasted_iota(jnp.int32, (1, LANES), 1)
        pend = nblk
        shift = 1
        while shift < LANES:
            pend = pend + jnp.where(lane1 >= shift, pltpu.roll(pend, shift, axis=1), 0.0)
            shift *= 2
        s_pstart[...] = (pend - nblk) * MOE_BLOCK
        s_run[...] = jnp.zeros_like(s_run)
        blk = lax.broadcasted_iota(jnp.int32, (n_blocks_pad, LANES), 0).astype(F32)
        lane_b = lax.broadcasted_iota(jnp.int32, (n_blocks_pad, LANES), 1)
        is_e = lane_b < N_EXPERTS
        done = jnp.logical_and(is_e, pend <= blk)
        e_of = jnp.minimum(jnp.sum(done.astype(F32), axis=1, keepdims=True), N_EXPERTS - 1.0)
        mine = lane_b.astype(F32) == e_of
        rows_left = jnp.sum(jnp.where(mine, cnt - (blk - (pend - nblk)) * MOE_BLOCK, 0.0),
                            axis=1, keepdims=True)
        valid = jnp.clip(rows_left, 0.0, float(MOE_BLOCK))
        used = jnp.sum(jnp.where(lane1 == N_EXPERTS - 1, pend, 0.0), axis=1, keepdims=True)
        meta = jnp.where(lane_b == 0, e_of, jnp.where(lane_b == 1, valid, used))
        meta_ref[...] = meta.astype(jnp.int32)

    @pl.when(phase == 1)
    def _():
        r = lax.broadcasted_iota(jnp.int32, (PLAN_TILE, PLAN_TILE), 0)
        c = lax.broadcasted_iota(jnp.int32, (PLAN_TILE, PLAN_TILE), 1)
        lower = (c < r).astype(BF16)
        before = jnp.dot(lower, mask.astype(BF16), preferred_element_type=F32)
        base = before + s_run[...] + s_pstart[...]
        lane4 = lax.broadcasted_iota(jnp.int32, (PLAN_TILE, LANES), 1)
        dest = jnp.zeros((PLAN_TILE, LANES), F32)
        for k in range(TOP_K):
            dk = jnp.sum(onehots[k] * base, axis=1, keepdims=True)
            dest = jnp.where(lane4 == k, dk, dest)
        dest_ref[...] = dest[:, :TOP_K].astype(jnp.int32)
        s_run[...] += jnp.sum(mask, axis=0, keepdims=True)


def _plan(idx_all, n_blocks_pad):
    t_all = idx_all.shape[0]
    nt = t_all // PLAN_TILE
    return pl.pallas_call(
        functools.partial(_plan_kernel, n_blocks_pad=n_blocks_pad),
        grid=(2, nt),
        in_specs=[pl.BlockSpec((PLAN_TILE, TOP_K), lambda p, i: (i, 0))],
        out_specs=(pl.BlockSpec((PLAN_TILE, TOP_K), lambda p, i: (i * p, 0)),
                   pl.BlockSpec((n_blocks_pad, LANES), lambda p, i: (0, 0))),
        out_shape=(jax.ShapeDtypeStruct((t_all, TOP_K), jnp.int32),
                   jax.ShapeDtypeStruct((n_blocks_pad, LANES), jnp.int32)),
        scratch_shapes=[pltpu.VMEM((1, LANES), F32), pltpu.VMEM((1, LANES), F32),
                        pltpu.VMEM((1, LANES), F32)],
        compiler_params=pltpu.CompilerParams(dimension_semantics=("arbitrary", "arbitrary")),
        name="moe_plan",
    )(idx_all)


def _token_tile(ref, t):
    return ref.at[pl.ds(pl.multiple_of(t * TILE_ROWS, TILE_ROWS), TILE_ROWS)]


def _dispatch_kernel(valid_ref, dest_ref, xm_ref, buf_ref, zeros, sem, zsem, *, n_blocks):
    base = pl.program_id(0) * PLAN_TILE
    blk_rows = MOE_BLOCK * TILE_ROWS

    @pl.when(pl.program_id(0) == 0)
    def _():
        zeros[...] = jnp.zeros_like(zeros)

        def block_copy(i):
            return pltpu.make_async_copy(
                zeros, buf_ref.at[pl.ds(pl.multiple_of(i * blk_rows, blk_rows), blk_rows)], zsem)

        def fill(i, carry):
            @pl.when(valid_ref[i] < MOE_BLOCK)
            def _():
                block_copy(i).start()
            return carry

        def drain(i, carry):
            @pl.when(valid_ref[i] < MOE_BLOCK)
            def _():
                block_copy(i).wait()
            return carry

        lax.fori_loop(0, n_blocks, fill, 0)
        lax.fori_loop(0, n_blocks, drain, 0)

    def issue(j, carry):
        src = _token_tile(xm_ref, base + j)
        for k in range(TOP_K):
            pltpu.make_async_copy(src, _token_tile(buf_ref, dest_ref[j * TOP_K + k]), sem).start()
        return carry

    lax.fori_loop(0, PLAN_TILE, issue, 0, unroll=4)
    n_rows = PLAN_TILE * TOP_K * TILE_ROWS
    pltpu.make_async_copy(buf_ref.at[pl.ds(0, n_rows)], buf_ref.at[pl.ds(0, n_rows)], sem).wait()


def _dispatch(block_valid, dest_flat, xm_tiles, n_blocks):
    t_all = xm_tiles.shape[0] // TILE_ROWS
    grid_spec = pltpu.PrefetchScalarGridSpec(
        num_scalar_prefetch=1,
        grid=(t_all // PLAN_TILE,),
        in_specs=[pl.BlockSpec((PLAN_TILE * TOP_K,), lambda i, va: (i,), memory_space=pltpu.SMEM),
                  pl.BlockSpec(memory_space=pl.ANY)],
        out_specs=pl.BlockSpec(memory_space=pl.ANY),
        scratch_shapes=[pltpu.VMEM((MOE_BLOCK * TILE_ROWS, LANES), F32),
                        pltpu.SemaphoreType.DMA(()), pltpu.SemaphoreType.DMA(())],
    )
    return pl.pallas_call(
        functools.partial(_dispatch_kernel, n_blocks=n_blocks),
        grid_spec=grid_spec,
        out_shape=jax.ShapeDtypeStruct((n_blocks * MOE_BLOCK * TILE_ROWS, LANES), F32),
        compiler_params=pltpu.CompilerParams(dimension_semantics=("arbitrary",)),
        name="moe_dispatch",
    )(block_valid, dest_flat, xm_tiles)


def _expert_kernel(be_ref, used_ref, x_ref, wgu_ref, bgu_ref, wd_ref, bd_ref, o_ref):
    i = pl.program_id(0)

    @pl.when(i >= used_ref[0])
    def _():
        o_ref[...] = jnp.zeros_like(o_ref)

    @pl.when(i < used_ref[0])
    def _():
        pieces = [x_ref[pl.ds(s, MOE_BLOCK, stride=TILE_ROWS), :] for s in range(TILE_ROWS)]
        x = jnp.concatenate(pieces, axis=1).astype(BF16)
        gu = jnp.dot(x, wgu_ref[0], preferred_element_type=F32) + bgu_ref[0]
        g = jnp.minimum(gu[:, :D_FF], SWIGLU_LIMIT)
        up = jnp.clip(gu[:, D_FF:], -SWIGLU_LIMIT, SWIGLU_LIMIT)
        hdn = (up + 1.0) * (g * jax.nn.sigmoid(SWIGLU_ALPHA * g))
        out = jnp.dot(hdn.astype(BF16), wd_ref[0], preferred_element_type=F32) + bd_ref[0]
        for s in range(TILE_ROWS):
            o_ref[pl.ds(s, MOE_BLOCK, stride=TILE_ROWS), :] = out[:, s * LANES:(s + 1) * LANES]


def _experts(block_e, n_used, buf, wgu, bgu, wd, bd, n_blocks):
    blk_rows = MOE_BLOCK * TILE_ROWS

    def in_row_map(i, be, nu):
        return (jnp.minimum(i, nu[0] - 1), 0)

    def w_map(i, be, nu):
        return (be[i], 0, 0)

    grid_spec = pltpu.PrefetchScalarGridSpec(
        num_scalar_prefetch=2,
        grid=(n_blocks,),
        in_specs=[pl.BlockSpec((blk_rows, LANES), in_row_map),
                  pl.BlockSpec((1, D_MODEL, 2 * D_FF), w_map),
                  pl.BlockSpec((1, 1, 2 * D_FF), w_map),
                  pl.BlockSpec((1, D_FF, D_MODEL), w_map),
                  pl.BlockSpec((1, 1, D_MODEL), w_map)],
        out_specs=pl.BlockSpec((blk_rows, LANES), lambda i, be, nu: (i, 0)),
    )
    return pl.pallas_call(
        _expert_kernel,
        grid_spec=grid_spec,
        out_shape=jax.ShapeDtypeStruct(buf.shape, F32),
        compiler_params=pltpu.CompilerParams(dimension_semantics=("arbitrary",),
                                             vmem_limit_bytes=VMEM_LIMIT),
        name="moe_experts",
    )(block_e, n_used, buf, wgu, bgu, wd, bd)


def _combine_kernel(dest_ref, gate_ref, x1_ref, rows_ref, nf_ref, y_ref, gbuf, sem):
    tile_rows = COMBINE_TILE * TILE_ROWS

    def issue(j, carry):
        for k in range(TOP_K):
            pltpu.make_async_copy(_token_tile(rows_ref, dest_ref[j * TOP_K + k]),
                                  _token_tile(gbuf.at[k], j), sem).start()
        return carry

    lax.fori_loop(0, COMBINE_TILE, issue, 0, unroll=4)
    for k in range(TOP_K):
        pltpu.make_async_copy(rows_ref.at[pl.ds(0, tile_rows)], gbuf.at[k], sem).wait()

    gates = gate_ref[...]
    pieces = []
    for s in range(TILE_ROWS):
        piece = x1_ref[:, s * LANES:(s + 1) * LANES]
        for k in range(TOP_K):
            piece = piece + gates[:, k:k + 1] * gbuf[k, pl.ds(s, COMBINE_TILE, stride=TILE_ROWS), :]
        pieces.append(piece)
    v = jnp.concatenate(pieces, axis=1)
    y_ref[...] = _rmsnorm(v, nf_ref[...])


def _combine(dest_flat, gates, x1, out_rows, norm_final):
    t_all = x1.shape[0]
    tile_rows = COMBINE_TILE * TILE_ROWS
    return pl.pallas_call(
        _combine_kernel,
        grid=(t_all // COMBINE_TILE,),
        in_specs=[pl.BlockSpec((COMBINE_TILE * TOP_K,), lambda i: (i,), memory_space=pltpu.SMEM),
                  pl.BlockSpec((COMBINE_TILE, TOP_K), lambda i: (i, 0)),
                  pl.BlockSpec((COMBINE_TILE, D_MODEL), lambda i: (i, 0)),
                  pl.BlockSpec(memory_space=pl.ANY),
                  pl.BlockSpec((1, D_MODEL), lambda i: (0, 0))],
        out_specs=pl.BlockSpec((COMBINE_TILE, D_MODEL), lambda i: (i, 0)),
        out_shape=jax.ShapeDtypeStruct((t_all, D_MODEL), F32),
        scratch_shapes=[pltpu.VMEM((TOP_K, tile_rows, LANES), F32),
                        pltpu.SemaphoreType.DMA(())],
        compiler_params=pltpu.CompilerParams(dimension_semantics=("arbitrary",)),
        name="moe_combine",
    )(dest_flat, gates, x1, out_rows, norm_final)


def _prep_weights(norm_mix, w_in, b_ig, b_fg, s5_a_re, s5_a_im, s5_log_dt, s5_b_re, s5_b_im,
                  s5_c_re, s5_c_im, s5_d, w_s5_glu, ml_gn, w_ml_out, w_out, norm_moe, w_router,
                  b_router):
    w = w_in[0]
    o_ig = D_S5 + 4 * D_ML
    o_gs5 = o_ig + 2 * ML_HEADS
    w_p = jnp.concatenate(
        [w[:, :o_ig], w[:, o_gs5:], w[:, o_ig:o_gs5],
         jnp.zeros((D_MODEL, LANES - 2 * ML_HEADS), F32)], axis=1).astype(BF16)
    gate_bias = jnp.concatenate([b_ig[0], b_fg[0], jnp.zeros((LANES - 2 * ML_HEADS,), F32)])[None]

    ar, ai = s5_a_re[0], s5_a_im[0]
    dt = jnp.exp(s5_log_dt[0])[:, None]
    mag = jnp.exp(dt * ar)
    abar_re = mag * jnp.cos(dt * ai)
    abar_im = mag * jnp.sin(dt * ai)
    den = ar * ar + ai * ai
    fr = ((abar_re - 1.0) * ar + abar_im * ai) / den
    fi = (abar_im * ar - (abar_re - 1.0) * ai) / den
    br, bi = s5_b_re[0], s5_b_im[0]
    bbar_re = fr[..., None] * br - fi[..., None] * bi
    bbar_im = fr[..., None] * bi + fi[..., None] * br
    gh = S5_GROUPS // 2
    eye = jnp.eye(gh, dtype=F32)

    def blockdiag_in(bb):
        return jnp.einsum('kgpc,gh->kgchp', bb.reshape(2, gh, S5_STATE, S5_GROUP), eye).reshape(
            2, gh * S5_GROUP, gh * S5_STATE).astype(BF16)

    def blockdiag_out(cc):
        return jnp.einsum('kgcp,gh->kgphc', cc.reshape(2, gh, S5_GROUP, S5_STATE), eye).reshape(
            2, gh * S5_STATE, gh * S5_GROUP)

    cb = jnp.concatenate([blockdiag_out(s5_c_re[0]), -blockdiag_out(s5_c_im[0])], axis=1).astype(BF16)
    w_router_p = jnp.concatenate(
        [w_router[0], jnp.zeros((D_MODEL, LANES - N_EXPERTS), F32)], axis=1).astype(BF16)
    b_router_p = jnp.concatenate([b_router[0], jnp.zeros((LANES - N_EXPERTS,), F32)])[None]
    return dict(
        norm_mix=norm_mix[0][None], w_in=w_p, gate_bias=gate_bias,
        abar_re=abar_re.reshape(1, S5_CH), abar_im=abar_im.reshape(1, S5_CH),
        bd_re=blockdiag_in(bbar_re), bd_im=blockdiag_in(bbar_im), cb=cb,
        d_skip=s5_d[0][None], w_glu=w_s5_glu[0].astype(BF16), ml_gn=ml_gn[0][None],
        w_ml_out=w_ml_out[0].astype(BF16), w_out=w_out[0].astype(BF16),
        norm_moe=norm_moe[0][None], w_router=w_router_p, b_router=b_router_p)


def _run_mixer(x, s5_re, s5_im, ml_c, ml_n, ml_m, w):
    batch = x.shape[0]
    bh = batch * ML_HEADS
    outs = _mixer(
        x, s5_re.reshape(batch, S5_CH), s5_im.reshape(batch, S5_CH),
        ml_c.reshape(bh, ML_HEAD_DIM, ML_HEAD_DIM), ml_n.reshape(bh, ML_HEAD_DIM),
        jnp.broadcast_to(ml_m.reshape(bh, 1), (bh, LANES)), w)
    x1, xm, idx, gates, hr, hi, c, n, m = outs
    states = (hr.reshape(1, batch, S5_GROUPS, S5_STATE), hi.reshape(1, batch, S5_GROUPS, S5_STATE),
              c.reshape(1, batch, ML_HEADS, ML_HEAD_DIM, ML_HEAD_DIM),
              n.reshape(1, batch, ML_HEADS, ML_HEAD_DIM), m[:, 0].reshape(1, batch, ML_HEADS))
    tokens = batch * x.shape[1]
    return (x1.reshape(tokens, D_MODEL), xm.reshape(tokens * TILE_ROWS, LANES),
            idx.reshape(tokens, TOP_K), gates.reshape(tokens, TOP_K), states)


def kernel(x_prompt, x_sample, state_s5_re, state_s5_im, state_ml_C, state_ml_n, state_ml_m, norm_mix, w_in, b_ig, b_fg, s5_a_re, s5_a_im, s5_log_dt, s5_b_re, s5_b_im, s5_c_re, s5_c_im, s5_d, w_s5_glu, ml_gn, w_ml_out, w_out, norm_moe, w_router, b_router, w_gate_up, b_gate_up, w_down, b_down, norm_final):
    w = _prep_weights(norm_mix, w_in, b_ig, b_fg, s5_a_re, s5_a_im, s5_log_dt, s5_b_re, s5_b_im,
                      s5_c_re, s5_c_im, s5_d, w_s5_glu, ml_gn, w_ml_out, w_out, norm_moe,
                      w_router, b_router)
    bp, lp, _ = x_prompt.shape
    bs, ls, _ = x_sample.shape
    zeros = lambda *shape: jnp.zeros(shape, F32)
    p = _run_mixer(x_prompt, zeros(bp, S5_CH), zeros(bp, S5_CH),
                   zeros(bp, ML_HEADS, ML_HEAD_DIM, ML_HEAD_DIM), zeros(bp, ML_HEADS, ML_HEAD_DIM),
                   zeros(bp, ML_HEADS), w)
    s = _run_mixer(x_sample, state_s5_re[0], state_s5_im[0], state_ml_C[0], state_ml_n[0],
                   state_ml_m[0], w)

    x1 = jnp.concatenate([p[0], s[0]], axis=0)
    xm = jnp.concatenate([p[1], s[1]], axis=0)
    idx = jnp.concatenate([p[2], s[2]], axis=0)
    gates = jnp.concatenate([p[3], s[3]], axis=0)
    t_all = idx.shape[0]
    assert t_all % PLAN_TILE == 0 and t_all % COMBINE_TILE == 0
    n_blocks = -(-(t_all * TOP_K + N_EXPERTS * (MOE_BLOCK - 1)) // MOE_BLOCK)
    n_blocks_pad = -(-n_blocks // SUBLANES) * SUBLANES

    dest, meta = _plan(idx, n_blocks_pad)
    dest_flat = dest.reshape(t_all * TOP_K)
    buf = _dispatch(meta[:n_blocks, 1], dest_flat, xm, n_blocks)
    out_rows = _experts(meta[:n_blocks, 0], meta[:1, 2], buf,
                        w_gate_up[0].astype(BF16), b_gate_up[0][:, None, :],
                        w_down[0].astype(BF16), b_down[0][:, None, :], n_blocks)
    y = _combine(dest_flat, gates, x1, out_rows, norm_final[None])
    tp = bp * lp
    y_prompt = y[:tp].reshape(bp, lp, D_MODEL)
    y_sample = y[tp:].reshape(bs, ls, D_MODEL)
    return (y_prompt, y_sample) + p[4] + s[4]
```

```python
import functools
import math

import jax
import jax.numpy as jnp
from jax import lax
from jax.experimental import pallas as pl
from jax.experimental.pallas import tpu as pltpu

F32 = jnp.float32
BF16 = jnp.bfloat16

D_MODEL = 1024
CHUNK = 64
EPS = 1e-6
D_S5 = 512
S5_GROUP = 16
S5_GROUPS = D_S5 // S5_GROUP
S5_STATE = 64
S5_CH = S5_GROUPS * S5_STATE
ML_HEADS = 4
ML_HEAD_DIM = 128
D_ML = ML_HEADS * ML_HEAD_DIM
N_EXPERTS = 32
TOP_K = 4
D_FF = 1024
SWIGLU_LIMIT = 7.0
SWIGLU_ALPHA = 1.702

LANES = 128
SUBLANES = 8
TILE_ROWS = D_MODEL // LANES

OFF_U = 0
OFF_Q = OFF_U + D_S5
OFF_K = OFF_Q + D_ML
OFF_V = OFF_K + D_ML
OFF_O = OFF_V + D_ML
OFF_GS5 = OFF_O + D_ML
OFF_GML = OFF_GS5 + D_MODEL
OFF_GATE = OFF_GML + D_MODEL
W_COLS = OFF_GATE + LANES

S5_LANE_CHUNK = 512
S5_SCAN_TILES = S5_LANE_CHUNK // LANES
MOE_BLOCK = 256
PLAN_TILE = 512
COMBINE_TILE = 256
VMEM_LIMIT = 60 * 1024 * 1024


def _rmsnorm(x, w):
    return x * lax.rsqrt(jnp.mean(x * x, axis=-1, keepdims=True) + EPS) * w


def _const_spec(shape):
    nd = len(shape)
    return pl.BlockSpec(shape, lambda *_: (0,) * nd, pipeline_mode=pl.Buffered(1))


def _mixer_kernel(x_ref, h0r_ref, h0i_ref, c0_ref, n0_ref, m0_ref,
                  nmix_ref, win_ref, gbias_ref, abr_ref, abi_ref, bdr_ref, bdi_ref, cb_ref,
                  dskip_ref, glu_ref, gn_ref, wml_ref, wout_ref, nmoe_ref, wr_ref, br_ref,
                  x1_ref, xm_ref, idx_ref, gate_ref, hr_ref, hi_ref, c_ref, n_ref, m_ref,
                  s_hr, s_hi, s_c, s_n, s_m, s_bur, s_bui, s_q, s_k, s_v, s_o, s_gg, s_bc,
                  s_row, s_gated, *, batch):
    i = pl.program_id(0)
    last = pl.num_programs(0) - 1
    rows = batch * CHUNK

    @pl.when(i == 0)
    def _():
        s_hr[...] = h0r_ref[...]
        s_hi[...] = h0i_ref[...]
        s_c[...] = c0_ref[...]
        s_n[...] = n0_ref[...]
        s_m[...] = m0_ref[...]

    x = x_ref[...].reshape(rows, D_MODEL)
    xn = _rmsnorm(x, nmix_ref[...]).astype(BF16)

    def proj(off, width):
        return jnp.dot(xn, win_ref[:, off:off + width], preferred_element_type=F32)

    u = proj(OFF_U, D_S5)
    ub = u.astype(BF16)
    half_in = D_S5 // 2
    half_ch = S5_CH // 2
    half_tiles = half_ch // LANES
    for k in range(2):
        uk = ub[:, k * half_in:(k + 1) * half_in]
        bur = jnp.dot(uk, bdr_ref[k], preferred_element_type=F32)
        bui = jnp.dot(uk, bdi_ref[k], preferred_element_type=F32)
        for j in range(half_tiles):
            for b in range(batch):
                dst = pl.ds(b, CHUNK, stride=batch)
                src = slice(b * CHUNK, (b + 1) * CHUNK)
                s_bur[k * half_tiles + j, dst, :] = bur[src, j * LANES:(j + 1) * LANES]
                s_bui[k * half_tiles + j, dst, :] = bui[src, j * LANES:(j + 1) * LANES]

    for c in range(S5_CH // S5_LANE_CHUNK):
        tiles = range(c * S5_SCAN_TILES, (c + 1) * S5_SCAN_TILES)
        ar = [jnp.broadcast_to(abr_ref[:, j * LANES:(j + 1) * LANES], (batch, LANES)) for j in tiles]
        ai = [jnp.broadcast_to(abi_ref[:, j * LANES:(j + 1) * LANES], (batch, LANES)) for j in tiles]

        def step(t, carry, tiles=tiles, ar=ar, ai=ai):
            slab = pl.ds(pl.multiple_of(t * batch, batch), batch)
            out = []
            for n, j in enumerate(tiles):
                hr, hi = carry[n]
                nr = ar[n] * hr - ai[n] * hi + s_bur[j, slab, :]
                ni = ar[n] * hi + ai[n] * hr + s_bui[j, slab, :]
                s_bur[j, slab, :] = nr
                s_bui[j, slab, :] = ni
                out.append((nr, ni))
            return tuple(out)

        init = tuple((s_hr[:, j * LANES:(j + 1) * LANES], s_hi[:, j * LANES:(j + 1) * LANES])
                     for j in tiles)
        fin = lax.fori_loop(0, CHUNK, step, init, unroll=8)
        for n, j in enumerate(tiles):
            s_hr[:, j * LANES:(j + 1) * LANES] = fin[n][0]
            s_hi[:, j * LANES:(j + 1) * LANES] = fin[n][1]

    ys = []
    for k in range(2):
        def stream_major(ref):
            return jnp.concatenate(
                [jnp.concatenate([ref[k * half_tiles + j, pl.ds(b, CHUNK, stride=batch), :]
                                  for j in range(half_tiles)], axis=1) for b in range(batch)], axis=0)

        h_re = stream_major(s_bur)
        h_im = stream_major(s_bui)
        yk = jnp.dot(h_re.astype(BF16), cb_ref[k, :half_ch, :], preferred_element_type=F32)
        yk = yk + jnp.dot(h_im.astype(BF16), cb_ref[k, half_ch:, :], preferred_element_type=F32)
        ys.append(yk)
    y = jnp.concatenate(ys, axis=1) + dskip_ref[...] * u
    glu = jnp.dot(jax.nn.gelu(y).astype(BF16), glu_ref[...], preferred_element_type=F32)
    y_s5 = glu[:, :D_MODEL] * jax.nn.sigmoid(glu[:, D_MODEL:])
    mix = jax.nn.sigmoid(proj(OFF_GS5, D_MODEL)) * y_s5

    s_q[...] = (proj(OFF_Q, D_ML) * (ML_HEAD_DIM ** -0.5)).astype(BF16)
    s_k[...] = proj(OFF_K, D_ML).astype(BF16)
    s_v[...] = proj(OFF_V, D_ML).astype(BF16)
    s_o[...] = proj(OFF_O, D_ML)
    gates = proj(OFF_GATE, LANES) + gbias_ref[...]
    lane_g = lax.broadcasted_iota(jnp.int32, (rows, LANES), 1)
    gg = jnp.where(lane_g < ML_HEADS, gates, jax.nn.log_sigmoid(gates))
    s_gg[...] = gg
    gt8 = gg.T[0:SUBLANES, :]
    pos = lax.broadcasted_iota(jnp.int32, (SUBLANES, rows), 1) % CHUNK
    cum = gt8
    shift = 1
    while shift < CHUNK:
        cum = cum + jnp.where(pos >= shift, pltpu.roll(cum, shift, axis=1), 0.0)
        shift *= 2
    sub = lax.broadcasted_iota(jnp.int32, (SUBLANES, rows), 0)
    row8 = jnp.where(sub < ML_HEADS, gt8, cum)
    for b in range(batch):
        s_row[b] = row8[:, b * CHUNK:(b + 1) * CHUNK]
    s_bc[...] = jnp.concatenate([row8, jnp.zeros((LANES - SUBLANES, rows), F32)], axis=0).T

    tri = (lax.broadcasted_iota(jnp.int32, (CHUNK, CHUNK), 0)
           >= lax.broadcasted_iota(jnp.int32, (CHUNK, CHUNK), 1))

    def stream_body(b, carry):
        r0 = pl.multiple_of(b * CHUNK, CHUNK)
        rs = pl.ds(r0, CHUNK)
        row = s_row[b]
        for h in range(ML_HEADS):
            hs = slice(h * ML_HEAD_DIM, (h + 1) * ML_HEAD_DIM)
            bh = b * ML_HEADS + h
            q = s_q[rs, hs]
            kk = s_k[rs, hs]
            v = s_v[rs, hs]
            b_c = s_bc[rs, ML_HEADS + h:ML_HEADS + h + 1]
            ig_c = s_gg[rs, h:h + 1]
            b_r = row[ML_HEADS + h:ML_HEADS + h + 1, :]
            ig_r = row[h:h + 1, :]
            cmat = s_c[bh]
            nvec = s_n[pl.ds(bh, 1), :]
            m_prev = s_m[pl.ds(bh, 1), :][:, 0:1]

            dmat = jnp.where(tri, b_c - b_r + ig_r, -jnp.inf)
            inter = b_c + m_prev
            m_t = jnp.maximum(inter, jnp.max(dmat, axis=1, keepdims=True))
            w_inter = jnp.exp(inter - m_t)
            s = lax.dot_general(q, kk, (((1,), (1,)), ((), ())), preferred_element_type=F32)
            s = s * jnp.exp(dmat - m_t)
            cq = lax.dot_general(q, cmat.astype(BF16), (((1,), (1,)), ((), ())),
                                 preferred_element_type=F32)
            num = jnp.dot(s.astype(BF16), v, preferred_element_type=F32) + w_inter * cq
            nq = jnp.sum(q.astype(F32) * nvec, axis=1, keepdims=True)
            den_dot = jnp.sum(s, axis=1, keepdims=True) + w_inter * nq
            den = jnp.maximum(jnp.abs(den_dot), jnp.exp(-m_t))
            hout = num / den

            m_new = m_t[CHUNK - 1:CHUNK, :]
            w_end = jnp.exp(b_c[CHUNK - 1:CHUNK, :] - b_c + ig_c - m_new)
            decay = jnp.exp(inter[CHUNK - 1:CHUNK, :] - m_new)
            wv_t = (w_end * v.astype(F32)).T.astype(BF16)
            s_c[bh] = decay * cmat + jnp.dot(wv_t, kk, preferred_element_type=F32)
            s_n[pl.ds(bh, 1), :] = decay * nvec + jnp.sum(w_end * kk.astype(F32), axis=0, keepdims=True)
            s_m[pl.ds(bh, 1), :] = jnp.broadcast_to(m_new, (1, LANES))

            hc = hout - jnp.mean(hout, axis=1, keepdims=True)
            hn = hc * lax.rsqrt(jnp.mean(hc * hc, axis=1, keepdims=True) + EPS)
            s_gated[rs, hs] = (jax.nn.sigmoid(s_o[rs, hs]) * (hn * gn_ref[:, hs])).astype(BF16)
        return carry

    lax.fori_loop(0, batch, stream_body, 0)

    y_ml = jnp.dot(s_gated[...], wml_ref[...], preferred_element_type=F32)
    mix = mix + jax.nn.sigmoid(proj(OFF_GML, D_MODEL)) * y_ml
    x1 = x + jnp.dot(mix.astype(BF16), wout_ref[...], preferred_element_type=F32)

    xm = _rmsnorm(x1, nmoe_ref[...])
    logits = jnp.dot(xm.astype(BF16), wr_ref[...], preferred_element_type=F32) + br_ref[...]
    lane = lax.broadcasted_iota(jnp.int32, (rows, LANES), 1)
    logits = jnp.where(lane < N_EXPERTS, logits, -jnp.inf)
    vals, idxs = [], []
    for _ in range(TOP_K):
        mx = jnp.max(logits, axis=1, keepdims=True)
        am = jnp.min(jnp.where(logits == mx, lane, LANES), axis=1, keepdims=True)
        vals.append(mx)
        idxs.append(am)
        logits = jnp.where(lane == am, -jnp.inf, logits)
    exps = [jnp.exp(v - vals[0]) for v in vals]
    esum = exps[0] + exps[1] + exps[2] + exps[3]
    idx_w = jnp.zeros((rows, LANES), jnp.int32)
    gate_w = jnp.zeros((rows, LANES), F32)
    for k in range(TOP_K):
        idx_w = jnp.where(lane == k, idxs[k], idx_w)
        gate_w = jnp.where(lane == k, exps[k] / esum, gate_w)
    idx_ref[...] = idx_w[:, :TOP_K].reshape(batch, CHUNK, TOP_K)
    gate_ref[...] = gate_w[:, :TOP_K].reshape(batch, CHUNK, TOP_K)
    x1_ref[...] = x1.reshape(batch, CHUNK, D_MODEL)

    for b in range(batch):
        bs = slice(b * CHUNK, (b + 1) * CHUNK)
        for s in range(TILE_ROWS):
            tile_rows = pl.ds(s, CHUNK, stride=TILE_ROWS)
            ls = slice(s * LANES, (s + 1) * LANES)
            xm_ref[b, tile_rows, :] = xm[bs, ls]

    @pl.when(i == last)
    def _():
        hr_ref[...] = s_hr[...]
        hi_ref[...] = s_hi[...]
        c_ref[...] = s_c[...]
        n_ref[...] = s_n[...]
        m_ref[...] = s_m[...]


def _mixer(x, h0r, h0i, c0, n0, m0, w):
    batch, seq, _ = x.shape
    assert seq % CHUNK == 0
    nblk = seq // CHUNK
    rows = batch * CHUNK
    bh = batch * ML_HEADS
    tile_block = (batch, CHUNK * TILE_ROWS, LANES)
    state_specs = [
        _const_spec((batch, S5_CH)), _const_spec((batch, S5_CH)),
        _const_spec((bh, ML_HEAD_DIM, ML_HEAD_DIM)), _const_spec((bh, ML_HEAD_DIM)),
        _const_spec((bh, LANES)),
    ]
    weights = (w['norm_mix'], w['w_in'], w['gate_bias'], w['abar_re'], w['abar_im'], w['bd_re'],
               w['bd_im'], w['cb'], w['d_skip'], w['w_glu'], w['ml_gn'], w['w_ml_out'], w['w_out'],
               w['norm_moe'], w['w_router'], w['b_router'])
    in_specs = ([pl.BlockSpec((batch, CHUNK, D_MODEL), lambda i: (0, i, 0))] + state_specs
                + [_const_spec(a.shape) for a in weights])
    out_shape = (
        jax.ShapeDtypeStruct((batch, seq, D_MODEL), F32),
        jax.ShapeDtypeStruct((batch, seq * TILE_ROWS, LANES), F32),
        jax.ShapeDtypeStruct((batch, seq, TOP_K), jnp.int32),
        jax.ShapeDtypeStruct((batch, seq, TOP_K), F32),
        jax.ShapeDtypeStruct((batch, S5_CH), F32),
        jax.ShapeDtypeStruct((batch, S5_CH), F32),
        jax.ShapeDtypeStruct((bh, ML_HEAD_DIM, ML_HEAD_DIM), F32),
        jax.ShapeDtypeStruct((bh, ML_HEAD_DIM), F32),
        jax.ShapeDtypeStruct((bh, LANES), F32),
    )
    out_specs = (
        pl.BlockSpec((batch, CHUNK, D_MODEL), lambda i: (0, i, 0)),
        pl.BlockSpec(tile_block, lambda i: (0, i, 0)),
        pl.BlockSpec((batch, CHUNK, TOP_K), lambda i: (0, i, 0)),
        pl.BlockSpec((batch, CHUNK, TOP_K), lambda i: (0, i, 0)),
        pl.BlockSpec((batch, S5_CH), lambda i: (0, 0)),
        pl.BlockSpec((batch, S5_CH), lambda i: (0, 0)),
        pl.BlockSpec((bh, ML_HEAD_DIM, ML_HEAD_DIM), lambda i: (0, 0, 0)),
        pl.BlockSpec((bh, ML_HEAD_DIM), lambda i: (0, 0)),
        pl.BlockSpec((bh, LANES), lambda i: (0, 0)),
    )
    scratch = [
        pltpu.VMEM((batch, S5_CH), F32), pltpu.VMEM((batch, S5_CH), F32),
        pltpu.VMEM((bh, ML_HEAD_DIM, ML_HEAD_DIM), F32), pltpu.VMEM((bh, ML_HEAD_DIM), F32),
        pltpu.VMEM((bh, LANES), F32),
        pltpu.VMEM((S5_CH // LANES, rows, LANES), F32), pltpu.VMEM((S5_CH // LANES, rows, LANES), F32),
        pltpu.VMEM((rows, D_ML), BF16), pltpu.VMEM((rows, D_ML), BF16), pltpu.VMEM((rows, D_ML), BF16),
        pltpu.VMEM((rows, D_ML), F32),
        pltpu.VMEM((rows, LANES), F32), pltpu.VMEM((rows, LANES), F32),
        pltpu.VMEM((batch, SUBLANES, CHUNK), F32),
        pltpu.VMEM((rows, D_ML), BF16),
    ]
    return pl.pallas_call(
        functools.partial(_mixer_kernel, batch=batch),
        grid=(nblk,),
        in_specs=in_specs,
        out_specs=out_specs,
        out_shape=out_shape,
        scratch_shapes=scratch,
        compiler_params=pltpu.CompilerParams(dimension_semantics=("arbitrary",),
                                             vmem_limit_bytes=VMEM_LIMIT),
        name="mixer",
    )(x, h0r, h0i, c0, n0, m0, *weights)


def _plan_kernel(idx_ref, dest_ref, meta_ref, s_cnt, s_run, s_pstart, *, n_blocks_pad):
    phase = pl.program_id(0)
    i = pl.program_id(1)
    nt = pl.num_programs(1)
    lane = lax.broadcasted_iota(jnp.int32, (PLAN_TILE, LANES), 1)
    idx = idx_ref[...]
    onehots = [(lane == idx[:, k:k + 1]).astype(F32) for k in range(TOP_K)]
    mask = onehots[0] + onehots[1] + onehots[2] + onehots[3]

    @pl.when(jnp.logical_and(phase == 0, i == 0))
    def _():
        s_cnt[...] = jnp.zeros_like(s_cnt)

    @pl.when(phase == 0)
    def _():
        s_cnt[...] += jnp.sum(mask, axis=0, keepdims=True)

    @pl.when(jnp.logical_and(phase == 1, i == 0))
    def _():
        cnt = s_cnt[...]
        nblk = jnp.floor((cnt + (MOE_BLOCK - 1)) * (1.0 / MOE_BLOCK))
        lane1 = lax.broadcasted_iota(jnp.int32, (1, LANES), 1)
        pend = nblk
        shift = 1
        while shift < LANES:
            pend = pend + jnp.where(lane1 >= shift, pltpu.roll(pend, shift, axis=1), 0.0)
            shift *= 2
        s_pstart[...] = (pend - nblk) * MOE_BLOCK
        s_run[...] = jnp.zeros_like(s_run)
        blk = lax.broadcasted_iota(jnp.int32, (n_blocks_pad, LANES), 0).astype(F32)
        lane_b = lax.broadcasted_iota(jnp.int32, (n_blocks_pad, LANES), 1)
        is_e = lane_b < N_EXPERTS
        done = jnp.logical_and(is_e, pend <= blk)
        e_of = jnp.minimum(jnp.sum(done.astype(F32), axis=1, keepdims=True), N_EXPERTS - 1.0)
        mine = lane_b.astype(F32) == e_of
        rows_left = jnp.sum(jnp.where(mine, cnt - (blk - (pend - nblk)) * MOE_BLOCK, 0.0),
                            axis=1, keepdims=True)
        valid = jnp.clip(rows_left, 0.0, float(MOE_BLOCK))
        used = jnp.sum(jnp.where(lane1 == N_EXPERTS - 1, pend, 0.0), axis=1, keepdims=True)
        meta = jnp.where(lane_b == 0, e_of, jnp.where(lane_b == 1, valid, used))
        meta_ref[...] = meta.astype(jnp.int32)

    @pl.when(phase == 1)
    def _():
        r = lax.broadcasted_iota(jnp.int32, (PLAN_TILE, PLAN_TILE), 0)
        c = lax.broadcasted_iota(jnp.int32, (PLAN_TILE, PLAN_TILE), 1)
        lower = (c < r).astype(BF16)
        before = jnp.dot(lower, mask.astype(BF16), preferred_element_type=F32)
        base = before + s_run[...] + s_pstart[...]
        lane4 = lax.broadcasted_iota(jnp.int32, (PLAN_TILE, LANES), 1)
        dest = jnp.zeros((PLAN_TILE, LANES), F32)
        for k in range(TOP_K):
            dk = jnp.sum(onehots[k] * base, axis=1, keepdims=True)
            dest = jnp.where(lane4 == k, dk, dest)
        dest_ref[...] = dest[:, :TOP_K].astype(jnp.int32)
        s_run[...] += jnp.sum(mask, axis=0, keepdims=True)


def _plan(idx_all, n_blocks_pad):
    t_all = idx_all.shape[0]
    nt = t_all // PLAN_TILE
    return pl.pallas_call(
        functools.partial(_plan_kernel, n_blocks_pad=n_blocks_pad),
        grid=(2, nt),
        in_specs=[pl.BlockSpec((PLAN_TILE, TOP_K), lambda p, i: (i, 0))],
        out_specs=(pl.BlockSpec((PLAN_TILE, TOP_K), lambda p, i: (i * p, 0)),
                   pl.BlockSpec((n_blocks_pad, LANES), lambda p, i: (0, 0))),
        out_shape=(jax.ShapeDtypeStruct((t_all, TOP_K), jnp.int32),
                   jax.ShapeDtypeStruct((n_blocks_pad, LANES), jnp.int32)),
        scratch_shapes=[pltpu.VMEM((1, LANES), F32), pltpu.VMEM((1, LANES), F32),
                        pltpu.VMEM((1, LANES), F32)],
        compiler_params=pltpu.CompilerParams(dimension_semantics=("arbitrary", "arbitrary")),
        name="moe_plan",
    )(idx_all)


def _token_tile(ref, t):
    return ref.at[pl.ds(pl.multiple_of(t * TILE_ROWS, TILE_ROWS), TILE_ROWS)]


def _dispatch_kernel(valid_ref, dest_ref, xm_ref, buf_ref, zeros, sem, zsem, *, n_blocks):
    blk_rows = MOE_BLOCK * TILE_ROWS

    @pl.when(pl.program_id(0) == 0)
    def _():
        zeros[...] = jnp.zeros_like(zeros)

        def block_copy(i):
            return pltpu.make_async_copy(
                zeros, buf_ref.at[pl.ds(pl.multiple_of(i * blk_rows, blk_rows), blk_rows)], zsem)

        def fill(i, carry):
            @pl.when(valid_ref[i] < MOE_BLOCK)
            def _():
                block_copy(i).start()
            return carry

        def drain(i, carry):
            @pl.when(valid_ref[i] < MOE_BLOCK)
            def _():
                block_copy(i).wait()
            return carry

        lax.fori_loop(0, n_blocks, fill, 0)
        lax.fori_loop(0, n_blocks, drain, 0)

    def issue(j, carry):
        src = _token_tile(xm_ref, j)
        for k in range(TOP_K):
            pltpu.make_async_copy(src, _token_tile(buf_ref, dest_ref[j * TOP_K + k]), sem).start()
        return carry

    lax.fori_loop(0, PLAN_TILE, issue, 0, unroll=4)
    n_rows = PLAN_TILE * TOP_K * TILE_ROWS
    pltpu.make_async_copy(buf_ref.at[pl.ds(0, n_rows)], buf_ref.at[pl.ds(0, n_rows)], sem).wait()


def _dispatch(block_valid, dest_flat, xm_tiles, n_blocks):
    t_all = xm_tiles.shape[0] // TILE_ROWS
    grid_spec = pltpu.PrefetchScalarGridSpec(
        num_scalar_prefetch=1,
        grid=(t_all // PLAN_TILE,),
        in_specs=[pl.BlockSpec((PLAN_TILE * TOP_K,), lambda i, va: (i,), memory_space=pltpu.SMEM),
                  pl.BlockSpec((PLAN_TILE * TILE_ROWS, LANES), lambda i, va: (i, 0))],
        out_specs=pl.BlockSpec(memory_space=pl.ANY),
        scratch_shapes=[pltpu.VMEM((MOE_BLOCK * TILE_ROWS, LANES), F32),
                        pltpu.SemaphoreType.DMA(()), pltpu.SemaphoreType.DMA(())],
    )
    return pl.pallas_call(
        functools.partial(_dispatch_kernel, n_blocks=n_blocks),
        grid_spec=grid_spec,
        out_shape=jax.ShapeDtypeStruct((n_blocks * MOE_BLOCK * TILE_ROWS, LANES), F32),
        compiler_params=pltpu.CompilerParams(dimension_semantics=("arbitrary",)),
        name="moe_dispatch",
    )(block_valid, dest_flat, xm_tiles)


def _expert_kernel(be_ref, used_ref, x_ref, wgu_ref, bgu_ref, wd_ref, bd_ref, o_ref):
    i = pl.program_id(0)

    @pl.when(i >= used_ref[0])
    def _():
        o_ref[...] = jnp.zeros_like(o_ref)

    @pl.when(i < used_ref[0])
    def _():
        pieces = [x_ref[pl.ds(s, MOE_BLOCK, stride=TILE_ROWS), :] for s in range(TILE_ROWS)]
        x = jnp.concatenate(pieces, axis=1).astype(BF16)
        gu = jnp.dot(x, wgu_ref[0], preferred_element_type=F32) + bgu_ref[0]
        g = jnp.minimum(gu[:, :D_FF], SWIGLU_LIMIT)
        up = jnp.clip(gu[:, D_FF:], -SWIGLU_LIMIT, SWIGLU_LIMIT)
        hdn = (up + 1.0) * (g * jax.nn.sigmoid(SWIGLU_ALPHA * g))
        out = jnp.dot(hdn.astype(BF16), wd_ref[0], preferred_element_type=F32) + bd_ref[0]
        for s in range(TILE_ROWS):
            o_ref[pl.ds(s, MOE_BLOCK, stride=TILE_ROWS), :] = out[:, s * LANES:(s + 1) * LANES]


def _experts(block_e, n_used, buf, wgu, bgu, wd, bd, n_blocks):
    blk_rows = MOE_BLOCK * TILE_ROWS

    def in_row_map(i, be, nu):
        return (jnp.minimum(i, nu[0] - 1), 0)

    def w_map(i, be, nu):
        return (be[i], 0, 0)

    grid_spec = pltpu.PrefetchScalarGridSpec(
        num_scalar_prefetch=2,
        grid=(n_blocks,),
        in_specs=[pl.BlockSpec((blk_rows, LANES), in_row_map),
                  pl.BlockSpec((1, D_MODEL, 2 * D_FF), w_map),
                  pl.BlockSpec((1, 1, 2 * D_FF), w_map),
                  pl.BlockSpec((1, D_FF, D_MODEL), w_map),
                  pl.BlockSpec((1, 1, D_MODEL), w_map)],
        out_specs=pl.BlockSpec((blk_rows, LANES), lambda i, be, nu: (i, 0)),
    )
    return pl.pallas_call(
        _expert_kernel,
        grid_spec=grid_spec,
        out_shape=jax.ShapeDtypeStruct(buf.shape, F32),
        compiler_params=pltpu.CompilerParams(dimension_semantics=("arbitrary",),
                                             vmem_limit_bytes=VMEM_LIMIT),
        name="moe_experts",
    )(block_e, n_used, buf, wgu, bgu, wd, bd)


def _combine_kernel(dest_ref, gate_ref, x1_ref, rows_ref, nf_ref, y_ref, gbuf, sem):
    tile_rows = COMBINE_TILE * TILE_ROWS

    def issue(j, carry):
        for k in range(TOP_K):
            pltpu.make_async_copy(_token_tile(rows_ref, dest_ref[j * TOP_K + k]),
                                  _token_tile(gbuf.at[k], j), sem).start()
        return carry

    lax.fori_loop(0, COMBINE_TILE, issue, 0, unroll=4)
    for k in range(TOP_K):
        pltpu.make_async_copy(rows_ref.at[pl.ds(0, tile_rows)], gbuf.at[k], sem).wait()

    gates = gate_ref[...]
    pieces = []
    for s in range(TILE_ROWS):
        piece = x1_ref[:, s * LANES:(s + 1) * LANES]
        for k in range(TOP_K):
            piece = piece + gates[:, k:k + 1] * gbuf[k, pl.ds(s, COMBINE_TILE, stride=TILE_ROWS), :]
        pieces.append(piece)
    v = jnp.concatenate(pieces, axis=1)
    y_ref[...] = _rmsnorm(v, nf_ref[...])


def _combine(dest_flat, gates, x1, out_rows, norm_final):
    t_all = x1.shape[0]
    tile_rows = COMBINE_TILE * TILE_ROWS
    return pl.pallas_call(
        _combine_kernel,
        grid=(t_all // COMBINE_TILE,),
        in_specs=[pl.BlockSpec((COMBINE_TILE * TOP_K,), lambda i: (i,), memory_space=pltpu.SMEM),
                  pl.BlockSpec((COMBINE_TILE, TOP_K), lambda i: (i, 0)),
                  pl.BlockSpec((COMBINE_TILE, D_MODEL), lambda i: (i, 0)),
                  pl.BlockSpec(memory_space=pl.ANY),
                  pl.BlockSpec((1, D_MODEL), lambda i: (0, 0))],
        out_specs=pl.BlockSpec((COMBINE_TILE, D_MODEL), lambda i: (i, 0)),
        out_shape=jax.ShapeDtypeStruct((t_all, D_MODEL), F32),
        scratch_shapes=[pltpu.VMEM((TOP_K, tile_rows, LANES), F32),
                        pltpu.SemaphoreType.DMA(())],
        compiler_params=pltpu.CompilerParams(dimension_semantics=("arbitrary",)),
        name="moe_combine",
    )(dest_flat, gates, x1, out_rows, norm_final)


def _prep_weights(norm_mix, w_in, b_ig, b_fg, s5_a_re, s5_a_im, s5_log_dt, s5_b_re, s5_b_im,
                  s5_c_re, s5_c_im, s5_d, w_s5_glu, ml_gn, w_ml_out, w_out, norm_moe, w_router,
                  b_router):
    w = w_in[0]
    o_ig = D_S5 + 4 * D_ML
    o_gs5 = o_ig + 2 * ML_HEADS
    w_p = jnp.concatenate(
        [w[:, :o_ig], w[:, o_gs5:], w[:, o_ig:o_gs5],
         jnp.zeros((D_MODEL, LANES - 2 * ML_HEADS), F32)], axis=1).astype(BF16)
    gate_bias = jnp.concatenate([b_ig[0], b_fg[0], jnp.zeros((LANES - 2 * ML_HEADS,), F32)])[None]

    ar, ai = s5_a_re[0], s5_a_im[0]
    dt = jnp.exp(s5_log_dt[0])[:, None]
    mag = jnp.exp(dt * ar)
    abar_re = mag * jnp.cos(dt * ai)
    abar_im = mag * jnp.sin(dt * ai)
    den = ar * ar + ai * ai
    fr = ((abar_re - 1.0) * ar + abar_im * ai) / den
    fi = (abar_im * ar - (abar_re - 1.0) * ai) / den
    br, bi = s5_b_re[0], s5_b_im[0]
    bbar_re = fr[..., None] * br - fi[..., None] * bi
    bbar_im = fr[..., None] * bi + fi[..., None] * br
    gh = S5_GROUPS // 2
    eye = jnp.eye(gh, dtype=F32)

    def blockdiag_in(bb):
        return jnp.einsum('kgpc,gh->kgchp', bb.reshape(2, gh, S5_STATE, S5_GROUP), eye).reshape(
            2, gh * S5_GROUP, gh * S5_STATE).astype(BF16)

    def blockdiag_out(cc):
        return jnp.einsum('kgcp,gh->kgphc', cc.reshape(2, gh, S5_GROUP, S5_STATE), eye).reshape(
            2, gh * S5_STATE, gh * S5_GROUP)

    cb = jnp.concatenate([blockdiag_out(s5_c_re[0]), -blockdiag_out(s5_c_im[0])], axis=1).astype(BF16)
    w_router_p = jnp.concatenate(
        [w_router[0], jnp.zeros((D_MODEL, LANES - N_EXPERTS), F32)], axis=1).astype(BF16)
    b_router_p = jnp.concatenate([b_router[0], jnp.zeros((LANES - N_EXPERTS,), F32)])[None]
    return dict(
        norm_mix=norm_mix[0][None], w_in=w_p, gate_bias=gate_bias,
        abar_re=abar_re.reshape(1, S5_CH), abar_im=abar_im.reshape(1, S5_CH),
        bd_re=blockdiag_in(bbar_re), bd_im=blockdiag_in(bbar_im), cb=cb,
        d_skip=s5_d[0][None], w_glu=w_s5_glu[0].astype(BF16), ml_gn=ml_gn[0][None],
        w_ml_out=w_ml_out[0].astype(BF16), w_out=w_out[0].astype(BF16),
        norm_moe=norm_moe[0][None], w_router=w_router_p, b_router=b_router_p)


def _run_mixer(x, s5_re, s5_im, ml_c, ml_n, ml_m, w):
    batch = x.shape[0]
    bh = batch * ML_HEADS
    outs = _mixer(
        x, s5_re.reshape(batch, S5_CH), s5_im.reshape(batch, S5_CH),
        ml_c.reshape(bh, ML_HEAD_DIM, ML_HEAD_DIM), ml_n.reshape(bh, ML_HEAD_DIM),
        jnp.broadcast_to(ml_m.reshape(bh, 1), (bh, LANES)), w)
    x1, xm, idx, gates, hr, hi, c, n, m = outs
    states = (hr.reshape(1, batch, S5_GROUPS, S5_STATE), hi.reshape(1, batch, S5_GROUPS, S5_STATE),
              c.reshape(1, batch, ML_HEADS, ML_HEAD_DIM, ML_HEAD_DIM),
              n.reshape(1, batch, ML_HEADS, ML_HEAD_DIM), m[:, 0].reshape(1, batch, ML_HEADS))
    tokens = batch * x.shape[1]
    return (x1.reshape(tokens, D_MODEL), xm.reshape(tokens * TILE_ROWS, LANES),
            idx.reshape(tokens, TOP_K), gates.reshape(tokens, TOP_K), states)


def kernel(x_prompt, x_sample, state_s5_re, state_s5_im, state_ml_C, state_ml_n, state_ml_m, norm_mix, w_in, b_ig, b_fg, s5_a_re, s5_a_im, s5_log_dt, s5_b_re, s5_b_im, s5_c_re, s5_c_im, s5_d, w_s5_glu, ml_gn, w_ml_out, w_out, norm_moe, w_router, b_router, w_gate_up, b_gate_up, w_down, b_down, norm_final):
    w = _prep_weights(norm_mix, w_in, b_ig, b_fg, s5_a_re, s5_a_im, s5_log_dt, s5_b_re, s5_b_im,
                      s5_c_re, s5_c_im, s5_d, w_s5_glu, ml_gn, w_ml_out, w_out, norm_moe,
                      w_router, b_router)
    bp, lp, _ = x_prompt.shape
    bs, ls, _ = x_sample.shape
    zeros = lambda *shape: jnp.zeros(shape, F32)
    p = _run_mixer(x_prompt, zeros(bp, S5_CH), zeros(bp, S5_CH),
                   zeros(bp, ML_HEADS, ML_HEAD_DIM, ML_HEAD_DIM), zeros(bp, ML_HEADS, ML_HEAD_DIM),
                   zeros(bp, ML_HEADS), w)
    s = _run_mixer(x_sample, state_s5_re[0], state_s5_im[0], state_ml_C[0], state_ml_n[0],
                   state_ml_m[0], w)

    x1 = jnp.concatenate([p[0], s[0]], axis=0)
    xm = jnp.concatenate([p[1], s[1]], axis=0)
    idx = jnp.concatenate([p[2], s[2]], axis=0)
    gates = jnp.concatenate([p[3], s[3]], axis=0)
    t_all = idx.shape[0]
    assert t_all % PLAN_TILE == 0 and t_all % COMBINE_TILE == 0
    n_blocks = -(-(t_all * TOP_K + N_EXPERTS * (MOE_BLOCK - 1)) // MOE_BLOCK)
    n_blocks_pad = -(-n_blocks // SUBLANES) * SUBLANES

    dest, meta = _plan(idx, n_blocks_pad)
    dest_flat = dest.reshape(t_all * TOP_K)
    buf = _dispatch(meta[:n_blocks, 1], dest_flat, xm, n_blocks)
    out_rows = _experts(meta[:n_blocks, 0], meta[:1, 2], buf,
                        w_gate_up[0].astype(BF16), b_gate_up[0][:, None, :],
                        w_down[0].astype(BF16), b_down[0][:, None, :], n_blocks)
    y = _combine(dest_flat, gates, x1, out_rows, norm_final[None])
    tp = bp * lp
    y_prompt = y[:tp].reshape(bp, lp, D_MODEL)
    y_sample = y[tp:].reshape(bs, ls, D_MODEL)
    return (y_prompt, y_sample) + p[4] + s[4]
```

```python
import functools

import jax
import jax.numpy as jnp
from jax import lax
from jax.experimental import pallas as pl
from jax.experimental.pallas import tpu as pltpu

F32 = jnp.float32
BF16 = jnp.bfloat16

D_MODEL = 1024
CHUNK = 64
EPS = 1e-6
D_S5 = 512
S5_GROUP = 16
S5_GROUPS = D_S5 // S5_GROUP
S5_STATE = 64
S5_CH = S5_GROUPS * S5_STATE
ML_HEADS = 4
ML_HEAD_DIM = 128
D_ML = ML_HEADS * ML_HEAD_DIM
N_EXPERTS = 32
TOP_K = 4
D_FF = 1024
SWIGLU_LIMIT = 7.0
SWIGLU_ALPHA = 1.702

LANES = 128
SUBLANES = 8
TILE_ROWS = D_MODEL // LANES

OFF_U = 0
OFF_Q = OFF_U + D_S5
OFF_K = OFF_Q + D_ML
OFF_V = OFF_K + D_ML
OFF_O = OFF_V + D_ML
OFF_GS5 = OFF_O + D_ML
OFF_GML = OFF_GS5 + D_MODEL
OFF_GATE = OFF_GML + D_MODEL
W_COLS = OFF_GATE + LANES

S5_HALF_IN = D_S5 // 2
S5_HALF_CH = S5_CH // 2
S5_HALF_TILES = S5_HALF_CH // LANES
S5_SCAN_TILES = 4
MOE_BLOCK = 256
VMEM_LIMIT = 60 * 1024 * 1024


def _rmsnorm(x, w):
    return x * lax.rsqrt(jnp.mean(x * x, axis=-1, keepdims=True) + EPS) * w


def _const_spec(shape):
    nd = len(shape)
    return pl.BlockSpec(shape, lambda *_: (0,) * nd, pipeline_mode=pl.Buffered(1))


def _mixer_kernel(xp_ref, xs_ref, h0r_ref, h0i_ref, c0_ref, n0_ref, m0_ref,
                  nmix_ref, win_ref, gbias_ref, abr_ref, abi_ref, bdr_ref, bdi_ref, cb_ref,
                  dskip_ref, glu_ref, gn_ref, wml_ref, wout_ref, nmoe_ref, wr_ref, br_ref,
                  x1_ref, xm_ref, idx_ref, gate_ref, hr_ref, hi_ref, c_ref, n_ref, m_ref,
                  s_bur, s_bui, s_q, s_k, s_v, s_o, s_gg, s_bc, s_gated, *, batch, n_prompt):
    i = pl.program_id(0)
    rows = batch * CHUNK

    @pl.when(i == 0)
    def _():
        hr_ref[...] = jnp.zeros_like(hr_ref)
        hi_ref[...] = jnp.zeros_like(hi_ref)
        c_ref[...] = jnp.zeros_like(c_ref)
        n_ref[...] = jnp.zeros_like(n_ref)
        m_ref[...] = jnp.zeros_like(m_ref)

    @pl.when(i == n_prompt)
    def _():
        hr_ref[0] = h0r_ref[...]
        hi_ref[0] = h0i_ref[...]
        c_ref[0] = c0_ref[...]
        n_ref[0] = n0_ref[...]
        m_ref[0] = m0_ref[...]

    @pl.when(i < n_prompt)
    def _():
        x1_ref[...] = xp_ref[...].reshape(rows, D_MODEL)

    @pl.when(i >= n_prompt)
    def _():
        x1_ref[...] = xs_ref[...].reshape(rows, D_MODEL)

    xn = _rmsnorm(x1_ref[...], nmix_ref[...]).astype(BF16)

    def proj(off, width):
        return jnp.dot(xn, win_ref[:, off:off + width], preferred_element_type=F32)

    s_q[...] = (proj(OFF_Q, D_ML) * (ML_HEAD_DIM ** -0.5)).astype(BF16)
    s_k[...] = proj(OFF_K, D_ML).astype(BF16)
    s_v[...] = proj(OFF_V, D_ML).astype(BF16)
    s_o[...] = proj(OFF_O, D_ML)
    gates = proj(OFF_GATE, LANES) + gbias_ref[...]
    lane_g = lax.broadcasted_iota(jnp.int32, (rows, LANES), 1)
    gg = jnp.where(lane_g < ML_HEADS, gates, jax.nn.log_sigmoid(gates))
    s_gg[...] = gg
    gt8 = gg.T[0:SUBLANES, :]
    pos = lax.broadcasted_iota(jnp.int32, (SUBLANES, rows), 1) % CHUNK
    cum = gt8
    shift = 1
    while shift < CHUNK:
        cum = cum + jnp.where(pos >= shift, pltpu.roll(cum, shift, axis=1), 0.0)
        shift *= 2
    sub = lax.broadcasted_iota(jnp.int32, (SUBLANES, rows), 0)
    row8 = jnp.where(sub < ML_HEADS, gt8, cum)
    s_bc[...] = jnp.concatenate([row8, jnp.zeros((LANES - SUBLANES, rows), F32)], axis=0).T

    tri = (lax.broadcasted_iota(jnp.int32, (CHUNK, CHUNK), 0)
           >= lax.broadcasted_iota(jnp.int32, (CHUNK, CHUNK), 1))

    for b in range(batch):
        rs = slice(b * CHUNK, (b + 1) * CHUNK)
        for h in range(ML_HEADS):
            hs = slice(h * ML_HEAD_DIM, (h + 1) * ML_HEAD_DIM)
            bh = b * ML_HEADS + h
            q = s_q[rs, hs]
            kk = s_k[rs, hs]
            v = s_v[rs, hs]
            b_c = s_bc[rs, ML_HEADS + h:ML_HEADS + h + 1]
            ig_c = s_gg[rs, h:h + 1]
            b_r = row8[ML_HEADS + h:ML_HEADS + h + 1, rs]
            ig_r = row8[h:h + 1, rs]
            cmat = c_ref[0, bh]
            nvec = n_ref[0, bh:bh + 1, :]
            m_prev = m_ref[0, bh:bh + 1, 0:1]

            dmat = jnp.where(tri, b_c - b_r + ig_r, -jnp.inf)
            inter = b_c + m_prev
            m_t = jnp.maximum(inter, jnp.max(dmat, axis=1, keepdims=True))
            w_inter = jnp.exp(inter - m_t)
            s = lax.dot_general(q, kk, (((1,), (1,)), ((), ())), preferred_element_type=F32)
            s = s * jnp.exp(dmat - m_t)
            cq = lax.dot_general(q, cmat.astype(BF16), (((1,), (1,)), ((), ())),
                                 preferred_element_type=F32)
            num = jnp.dot(s.astype(BF16), v, preferred_element_type=F32) + w_inter * cq
            nq = jnp.sum(q.astype(F32) * nvec, axis=1, keepdims=True)
            den_dot = jnp.sum(s, axis=1, keepdims=True) + w_inter * nq
            den = jnp.maximum(jnp.abs(den_dot), jnp.exp(-m_t))
            hout = num / den

            m_new = m_t[CHUNK - 1:CHUNK, :]
            w_end = jnp.exp(b_c[CHUNK - 1:CHUNK, :] - b_c + ig_c - m_new)
            decay = jnp.exp(inter[CHUNK - 1:CHUNK, :] - m_new)
            wv_t = (w_end * v.astype(F32)).T.astype(BF16)
            c_ref[0, bh] = decay * cmat + jnp.dot(wv_t, kk, preferred_element_type=F32)
            n_ref[0, bh:bh + 1, :] = decay * nvec + jnp.sum(w_end * kk.astype(F32), axis=0,
                                                            keepdims=True)
            m_ref[0, bh:bh + 1, :] = jnp.broadcast_to(m_new, (1, LANES))

            hc = hout - jnp.mean(hout, axis=1, keepdims=True)
            hn = hc * lax.rsqrt(jnp.mean(hc * hc, axis=1, keepdims=True) + EPS)
            s_gated[rs, hs] = (jax.nn.sigmoid(s_o[rs, hs]) * (hn * gn_ref[:, hs])).astype(BF16)

    mix = jax.nn.sigmoid(proj(OFF_GML, D_MODEL)) * jnp.dot(
        s_gated[...], wml_ref[...], preferred_element_type=F32)

    u = proj(OFF_U, D_S5)
    ub = u.astype(BF16)
    ys = []
    for k in range(2):
        uk = ub[:, k * S5_HALF_IN:(k + 1) * S5_HALF_IN]
        bur = jnp.dot(uk, bdr_ref[k], preferred_element_type=F32)
        bui = jnp.dot(uk, bdi_ref[k], preferred_element_type=F32)
        for j in range(S5_HALF_TILES):
            for b in range(batch):
                dst = pl.ds(b, CHUNK, stride=batch)
                src = slice(b * CHUNK, (b + 1) * CHUNK)
                s_bur[j, dst, :] = bur[src, j * LANES:(j + 1) * LANES]
                s_bui[j, dst, :] = bui[src, j * LANES:(j + 1) * LANES]

        for c in range(S5_HALF_TILES // S5_SCAN_TILES):
            tiles = range(c * S5_SCAN_TILES, (c + 1) * S5_SCAN_TILES)
            lanes = [slice(k * S5_HALF_CH + j * LANES, k * S5_HALF_CH + (j + 1) * LANES) for j in tiles]
            ar = [jnp.broadcast_to(abr_ref[:, ls], (batch, LANES)) for ls in lanes]
            ai = [jnp.broadcast_to(abi_ref[:, ls], (batch, LANES)) for ls in lanes]

            def step(t, carry, tiles=tiles, ar=ar, ai=ai):
                slab = pl.ds(pl.multiple_of(t * batch, batch), batch)
                out = []
                for n, j in enumerate(tiles):
                    hr, hi = carry[n]
                    nr = ar[n] * hr - ai[n] * hi + s_bur[j, slab, :]
                    ni = ar[n] * hi + ai[n] * hr + s_bui[j, slab, :]
                    s_bur[j, slab, :] = nr
                    s_bui[j, slab, :] = ni
                    out.append((nr, ni))
                return tuple(out)

            init = tuple((hr_ref[0, :, ls], hi_ref[0, :, ls]) for ls in lanes)
            fin = lax.fori_loop(0, CHUNK, step, init, unroll=8)
            for n, ls in enumerate(lanes):
                hr_ref[0, :, ls] = fin[n][0]
                hi_ref[0, :, ls] = fin[n][1]

        def stream_major(ref):
            return jnp.concatenate(
                [jnp.concatenate([ref[j, pl.ds(b, CHUNK, stride=batch), :]
                                  for j in range(S5_HALF_TILES)], axis=1) for b in range(batch)], axis=0)

        yk = jnp.dot(stream_major(s_bur).astype(BF16), cb_ref[k, :S5_HALF_CH, :],
                     preferred_element_type=F32)
        yk = yk + jnp.dot(stream_major(s_bui).astype(BF16), cb_ref[k, S5_HALF_CH:, :],
                          preferred_element_type=F32)
        ys.append(yk)
    y = jnp.concatenate(ys, axis=1) + dskip_ref[...] * u
    glu = jnp.dot(jax.nn.gelu(y).astype(BF16), glu_ref[...], preferred_element_type=F32)
    y_s5 = glu[:, :D_MODEL] * jax.nn.sigmoid(glu[:, D_MODEL:])
    mix = mix + jax.nn.sigmoid(proj(OFF_GS5, D_MODEL)) * y_s5

    x1 = x1_ref[...] + jnp.dot(mix.astype(BF16), wout_ref[...], preferred_element_type=F32)
    x1_ref[...] = x1

    xm = _rmsnorm(x1, nmoe_ref[...])
    logits = jnp.dot(xm.astype(BF16), wr_ref[...], preferred_element_type=F32) + br_ref[...]
    lane = lax.broadcasted_iota(jnp.int32, (rows, LANES), 1)
    logits = jnp.where(lane < N_EXPERTS, logits, -jnp.inf)
    vals, idxs = [], []
    for _ in range(TOP_K):
        mx = jnp.max(logits, axis=1, keepdims=True)
        am = jnp.min(jnp.where(logits == mx, lane, LANES), axis=1, keepdims=True)
        vals.append(mx)
        idxs.append(am)
        logits = jnp.where(lane == am, -jnp.inf, logits)
    exps = [jnp.exp(v - vals[0]) for v in vals]
    esum = exps[0] + exps[1] + exps[2] + exps[3]
    idx_w = jnp.zeros((rows, LANES), jnp.int32)
    gate_w = jnp.zeros((rows, LANES), F32)
    for k in range(TOP_K):
        idx_w = jnp.where(lane == k, idxs[k], idx_w)
        gate_w = jnp.where(lane == k, exps[k] / esum, gate_w)
    idx_ref[...] = idx_w[:, :TOP_K]
    gate_ref[...] = gate_w[:, :TOP_K]

    for s in range(TILE_ROWS):
        xm_ref[pl.ds(s, rows, stride=TILE_ROWS), :] = xm[:, s * LANES:(s + 1) * LANES]


def _mixer(x_prompt, x_sample, h0r, h0i, c0, n0, m0, w):
    batch, lp, _ = x_prompt.shape
    assert x_sample.shape[0] == batch and lp % CHUNK == 0 and x_sample.shape[1] % CHUNK == 0
    n_prompt = lp // CHUNK
    n_sample = x_sample.shape[1] // CHUNK
    nblk = n_prompt + n_sample
    rows = batch * CHUNK
    t_all = nblk * rows
    bh = batch * ML_HEADS

    def phase(i):
        return jnp.minimum(i // n_prompt, 1)

    state_specs = [
        _const_spec((batch, S5_CH)), _const_spec((batch, S5_CH)),
        _const_spec((bh, ML_HEAD_DIM, ML_HEAD_DIM)), _const_spec((bh, ML_HEAD_DIM)),
        _const_spec((bh, LANES)),
    ]
    weights = (w['norm_mix'], w['w_in'], w['gate_bias'], w['abar_re'], w['abar_im'], w['bd_re'],
               w['bd_im'], w['cb'], w['d_skip'], w['w_glu'], w['ml_gn'], w['w_ml_out'], w['w_out'],
               w['norm_moe'], w['w_router'], w['b_router'])
    in_specs = ([pl.BlockSpec((batch, CHUNK, D_MODEL), lambda i: (0, jnp.minimum(i, n_prompt - 1), 0)),
                 pl.BlockSpec((batch, CHUNK, D_MODEL), lambda i: (0, jnp.maximum(i - n_prompt, 0), 0))]
                + state_specs + [_const_spec(a.shape) for a in weights])
    out_shape = (
        jax.ShapeDtypeStruct((t_all, D_MODEL), F32),
        jax.ShapeDtypeStruct((t_all * TILE_ROWS, LANES), F32),
        jax.ShapeDtypeStruct((t_all, TOP_K), jnp.int32),
        jax.ShapeDtypeStruct((t_all, TOP_K), F32),
        jax.ShapeDtypeStruct((2, batch, S5_CH), F32),
        jax.ShapeDtypeStruct((2, batch, S5_CH), F32),
        jax.ShapeDtypeStruct((2, bh, ML_HEAD_DIM, ML_HEAD_DIM), F32),
        jax.ShapeDtypeStruct((2, bh, ML_HEAD_DIM), F32),
        jax.ShapeDtypeStruct((2, bh, LANES), F32),
    )
    out_specs = (
        pl.BlockSpec((rows, D_MODEL), lambda i: (i, 0)),
        pl.BlockSpec((rows * TILE_ROWS, LANES), lambda i: (i, 0)),
        pl.BlockSpec((rows, TOP_K), lambda i: (i, 0)),
        pl.BlockSpec((rows, TOP_K), lambda i: (i, 0)),
        pl.BlockSpec((1, batch, S5_CH), lambda i: (phase(i), 0, 0)),
        pl.BlockSpec((1, batch, S5_CH), lambda i: (phase(i), 0, 0)),
        pl.BlockSpec((1, bh, ML_HEAD_DIM, ML_HEAD_DIM), lambda i: (phase(i), 0, 0, 0)),
        pl.BlockSpec((1, bh, ML_HEAD_DIM), lambda i: (phase(i), 0, 0)),
        pl.BlockSpec((1, bh, LANES), lambda i: (phase(i), 0, 0)),
    )
    scratch = [
        pltpu.VMEM((S5_HALF_TILES, rows, LANES), F32), pltpu.VMEM((S5_HALF_TILES, rows, LANES), F32),
        pltpu.VMEM((rows, D_ML), BF16), pltpu.VMEM((rows, D_ML), BF16), pltpu.VMEM((rows, D_ML), BF16),
        pltpu.VMEM((rows, D_ML), F32),
        pltpu.VMEM((rows, LANES), F32), pltpu.VMEM((rows, LANES), F32),
        pltpu.VMEM((rows, D_ML), BF16),
    ]
    return pl.pallas_call(
        functools.partial(_mixer_kernel, batch=batch, n_prompt=n_prompt),
        grid=(nblk,),
        in_specs=in_specs,
        out_specs=out_specs,
        out_shape=out_shape,
        scratch_shapes=scratch,
        compiler_params=pltpu.CompilerParams(dimension_semantics=("arbitrary",),
                                             vmem_limit_bytes=VMEM_LIMIT),
        name="mixer",
    )(x_prompt, x_sample, h0r, h0i, c0, n0, m0, *weights)


def _plan_kernel(idx_ref, dest_ref, meta_ref, s_cnt, s_run, s_pstart, *, tile, n_blocks_pad):
    phase = pl.program_id(0)
    i = pl.program_id(1)
    lane = lax.broadcasted_iota(jnp.int32, (tile, LANES), 1)
    idx = idx_ref[...]
    onehots = [(lane == idx[:, k:k + 1]).astype(F32) for k in range(TOP_K)]
    mask = onehots[0] + onehots[1] + onehots[2] + onehots[3]

    @pl.when(jnp.logical_and(phase == 0, i == 0))
    def _():
        s_cnt[...] = jnp.zeros_like(s_cnt)

    @pl.when(phase == 0)
    def _():
        s_cnt[...] += jnp.sum(mask, axis=0, keepdims=True)

    @pl.when(jnp.logical_and(phase == 1, i == 0))
    def _():
        cnt = s_cnt[...]
        nblk = jnp.floor((cnt + (MOE_BLOCK - 1)) * (1.0 / MOE_BLOCK))
        lane1 = lax.broadcasted_iota(jnp.int32, (1, LANES), 1)
        pend = nblk
        shift = 1
        while shift < LANES:
            pend = pend + jnp.where(lane1 >= shift, pltpu.roll(pend, shift, axis=1), 0.0)
            shift *= 2
        s_pstart[...] = (pend - nblk) * MOE_BLOCK
        s_run[...] = jnp.zeros_like(s_run)
        blk = lax.broadcasted_iota(jnp.int32, (n_blocks_pad, LANES), 0).astype(F32)
        lane_b = lax.broadcasted_iota(jnp.int32, (n_blocks_pad, LANES), 1)
        is_e = lane_b < N_EXPERTS
        done = jnp.logical_and(is_e, pend <= blk)
        e_of = jnp.minimum(jnp.sum(done.astype(F32), axis=1, keepdims=True), N_EXPERTS - 1.0)
        mine = lane_b.astype(F32) == e_of
        rows_left = jnp.sum(jnp.where(mine, cnt - (blk - (pend - nblk)) * MOE_BLOCK, 0.0),
                            axis=1, keepdims=True)
        valid = jnp.clip(rows_left, 0.0, float(MOE_BLOCK))
        used = jnp.sum(jnp.where(lane1 == N_EXPERTS - 1, pend, 0.0), axis=1, keepdims=True)
        meta = jnp.where(lane_b == 0, e_of, jnp.where(lane_b == 1, valid, used))
        meta_ref[...] = meta.astype(jnp.int32)

    @pl.when(phase == 1)
    def _():
        r = lax.broadcasted_iota(jnp.int32, (tile, tile), 0)
        c = lax.broadcasted_iota(jnp.int32, (tile, tile), 1)
        lower = (c < r).astype(BF16)
        before = jnp.dot(lower, mask.astype(BF16), preferred_element_type=F32)
        base = before + s_run[...] + s_pstart[...]
        dest = jnp.zeros((tile, LANES), F32)
        for k in range(TOP_K):
            dk = jnp.sum(onehots[k] * base, axis=1, keepdims=True)
            dest = jnp.where(lane == k, dk, dest)
        dest_ref[...] = dest[:, :TOP_K].astype(jnp.int32)
        s_run[...] += jnp.sum(mask, axis=0, keepdims=True)


def _plan(idx_all, tile, n_blocks_pad):
    t_all = idx_all.shape[0]
    return pl.pallas_call(
        functools.partial(_plan_kernel, tile=tile, n_blocks_pad=n_blocks_pad),
        grid=(2, t_all // tile),
        in_specs=[pl.BlockSpec((tile, TOP_K), lambda p, i: (i, 0))],
        out_specs=(pl.BlockSpec((tile, TOP_K), lambda p, i: (i * p, 0)),
                   pl.BlockSpec((n_blocks_pad, LANES), lambda p, i: (0, 0))),
        out_shape=(jax.ShapeDtypeStruct((t_all, TOP_K), jnp.int32),
                   jax.ShapeDtypeStruct((n_blocks_pad, LANES), jnp.int32)),
        scratch_shapes=[pltpu.VMEM((1, LANES), F32), pltpu.VMEM((1, LANES), F32),
                        pltpu.VMEM((1, LANES), F32)],
        compiler_params=pltpu.CompilerParams(dimension_semantics=("arbitrary", "arbitrary")),
        name="moe_plan",
    )(idx_all)


def _token_tile(ref, t):
    return ref.at[pl.ds(pl.multiple_of(t * TILE_ROWS, TILE_ROWS), TILE_ROWS)]


def _dispatch_kernel(valid_ref, dest_ref, xm_ref, buf_ref, zeros, sem, zsem, *, tile, n_blocks):
    blk_rows = MOE_BLOCK * TILE_ROWS

    @pl.when(pl.program_id(0) == 0)
    def _():
        zeros[...] = jnp.zeros_like(zeros)

        def block_copy(i):
            return pltpu.make_async_copy(
                zeros, buf_ref.at[pl.ds(pl.multiple_of(i * blk_rows, blk_rows), blk_rows)], zsem)

        def fill(i, carry):
            @pl.when(valid_ref[i] < MOE_BLOCK)
            def _():
                block_copy(i).start()
            return carry

        def drain(i, carry):
            @pl.when(valid_ref[i] < MOE_BLOCK)
            def _():
                block_copy(i).wait()
            return carry

        lax.fori_loop(0, n_blocks, fill, 0)
        lax.fori_loop(0, n_blocks, drain, 0)

    def issue(j, carry):
        src = _token_tile(xm_ref, j)
        for k in range(TOP_K):
            pltpu.make_async_copy(src, _token_tile(buf_ref, dest_ref[j * TOP_K + k]), sem).start(
                priority=k % 2)
        return carry

    lax.fori_loop(0, tile, issue, 0, unroll=4)
    n_rows = tile * TOP_K * TILE_ROWS
    pltpu.make_async_copy(buf_ref.at[pl.ds(0, n_rows)], buf_ref.at[pl.ds(0, n_rows)], sem).wait()


def _dispatch(block_valid, dest_flat, xm_tiles, tile, n_blocks):
    t_all = xm_tiles.shape[0] // TILE_ROWS
    grid_spec = pltpu.PrefetchScalarGridSpec(
        num_scalar_prefetch=1,
        grid=(t_all // tile,),
        in_specs=[pl.BlockSpec((tile * TOP_K,), lambda i, va: (i,), memory_space=pltpu.SMEM),
                  pl.BlockSpec((tile * TILE_ROWS, LANES), lambda i, va: (i, 0))],
        out_specs=pl.BlockSpec(memory_space=pl.ANY),
        scratch_shapes=[pltpu.VMEM((MOE_BLOCK * TILE_ROWS, LANES), F32),
                        pltpu.SemaphoreType.DMA(()), pltpu.SemaphoreType.DMA(())],
    )
    return pl.pallas_call(
        functools.partial(_dispatch_kernel, tile=tile, n_blocks=n_blocks),
        grid_spec=grid_spec,
        out_shape=jax.ShapeDtypeStruct((n_blocks * MOE_BLOCK * TILE_ROWS, LANES), F32),
        compiler_params=pltpu.CompilerParams(dimension_semantics=("arbitrary",)),
        name="moe_dispatch",
    )(block_valid, dest_flat, xm_tiles)


def _expert_kernel(be_ref, used_ref, x_ref, wgu_ref, bgu_ref, wd_ref, bd_ref, o_ref):
    i = pl.program_id(0)

    @pl.when(i >= used_ref[0])
    def _():
        o_ref[...] = jnp.zeros_like(o_ref)

    @pl.when(i < used_ref[0])
    def _():
        pieces = [x_ref[pl.ds(s, MOE_BLOCK, stride=TILE_ROWS), :] for s in range(TILE_ROWS)]
        x = jnp.concatenate(pieces, axis=1).astype(BF16)
        gu = jnp.dot(x, wgu_ref[0], preferred_element_type=F32) + bgu_ref[0]
        g = jnp.minimum(gu[:, :D_FF], SWIGLU_LIMIT)
        up = jnp.clip(gu[:, D_FF:], -SWIGLU_LIMIT, SWIGLU_LIMIT)
        hdn = (up + 1.0) * (g * jax.nn.sigmoid(SWIGLU_ALPHA * g))
        out = jnp.dot(hdn.astype(BF16), wd_ref[0], preferred_element_type=F32) + bd_ref[0]
        for s in range(TILE_ROWS):
            o_ref[pl.ds(s, MOE_BLOCK, stride=TILE_ROWS), :] = out[:, s * LANES:(s + 1) * LANES]


def _experts(block_e, n_used, buf, wgu, bgu, wd, bd, n_blocks):
    blk_rows = MOE_BLOCK * TILE_ROWS

    def in_row_map(i, be, nu):
        return (jnp.minimum(i, nu[0] - 1), 0)

    def w_map(i, be, nu):
        return (be[i], 0, 0)

    grid_spec = pltpu.PrefetchScalarGridSpec(
        num_scalar_prefetch=2,
        grid=(n_blocks,),
        in_specs=[pl.BlockSpec((blk_rows, LANES), in_row_map),
                  pl.BlockSpec((1, D_MODEL, 2 * D_FF), w_map),
                  pl.BlockSpec((1, 1, 2 * D_FF), w_map),
                  pl.BlockSpec((1, D_FF, D_MODEL), w_map),
                  pl.BlockSpec((1, 1, D_MODEL), w_map)],
        out_specs=pl.BlockSpec((blk_rows, LANES), lambda i, be, nu: (i, 0)),
    )
    return pl.pallas_call(
        _expert_kernel,
        grid_spec=grid_spec,
        out_shape=jax.ShapeDtypeStruct(buf.shape, F32),
        compiler_params=pltpu.CompilerParams(dimension_semantics=("arbitrary",),
                                             vmem_limit_bytes=VMEM_LIMIT),
        name="moe_experts",
    )(block_e, n_used, buf, wgu, bgu, wd, bd)


def _combine_kernel(dest_ref, gate_ref, x1_ref, rows_ref, nf_ref, yp_ref, ys_ref, gbuf, sem, *,
                    batch, n_prompt):
    i = pl.program_id(0)
    tile = batch * CHUNK
    tile_rows = tile * TILE_ROWS

    def issue(j, carry):
        for k in range(TOP_K):
            pltpu.make_async_copy(_token_tile(rows_ref, dest_ref[j * TOP_K + k]),
                                  _token_tile(gbuf.at[k], j), sem).start(priority=k % 2)
        return carry

    lax.fori_loop(0, tile, issue, 0, unroll=4)
    for k in range(TOP_K):
        pltpu.make_async_copy(rows_ref.at[pl.ds(0, tile_rows)], gbuf.at[k], sem).wait()

    gates = gate_ref[...]
    pieces = []
    for s in range(TILE_ROWS):
        piece = x1_ref[:, s * LANES:(s + 1) * LANES]
        for k in range(TOP_K):
            piece = piece + gates[:, k:k + 1] * gbuf[k, pl.ds(s, tile, stride=TILE_ROWS), :]
        pieces.append(piece)
    y = _rmsnorm(jnp.concatenate(pieces, axis=1), nf_ref[...]).reshape(batch, CHUNK, D_MODEL)

    @pl.when(i < n_prompt)
    def _():
        yp_ref[...] = y

    @pl.when(i >= n_prompt)
    def _():
        ys_ref[...] = y


def _combine(dest_flat, gates, x1, out_rows, norm_final, batch, lp, ls):
    tile = batch * CHUNK
    n_prompt = lp // CHUNK
    n_sample = ls // CHUNK
    return pl.pallas_call(
        functools.partial(_combine_kernel, batch=batch, n_prompt=n_prompt),
        grid=(n_prompt + n_sample,),
        in_specs=[pl.BlockSpec((tile * TOP_K,), lambda i: (i,), memory_space=pltpu.SMEM),
                  pl.BlockSpec((tile, TOP_K), lambda i: (i, 0)),
                  pl.BlockSpec((tile, D_MODEL), lambda i: (i, 0)),
                  pl.BlockSpec(memory_space=pl.ANY),
                  pl.BlockSpec((1, D_MODEL), lambda i: (0, 0))],
        out_specs=(pl.BlockSpec((batch, CHUNK, D_MODEL), lambda i: (0, jnp.minimum(i, n_prompt - 1), 0)),
                   pl.BlockSpec((batch, CHUNK, D_MODEL), lambda i: (0, jnp.maximum(i - n_prompt, 0), 0))),
        out_shape=(jax.ShapeDtypeStruct((batch, lp, D_MODEL), F32),
                   jax.ShapeDtypeStruct((batch, ls, D_MODEL), F32)),
        scratch_shapes=[pltpu.VMEM((TOP_K, tile * TILE_ROWS, LANES), F32),
                        pltpu.SemaphoreType.DMA(())],
        compiler_params=pltpu.CompilerParams(dimension_semantics=("arbitrary",),
                                             vmem_limit_bytes=VMEM_LIMIT),
        name="moe_combine",
    )(dest_flat, gates, x1, out_rows, norm_final)


def _prep_weights(norm_mix, w_in, b_ig, b_fg, s5_a_re, s5_a_im, s5_log_dt, s5_b_re, s5_b_im,
                  s5_c_re, s5_c_im, s5_d, w_s5_glu, ml_gn, w_ml_out, w_out, norm_moe, w_router,
                  b_router):
    w = w_in[0]
    o_ig = D_S5 + 4 * D_ML
    o_gs5 = o_ig + 2 * ML_HEADS
    w_p = jnp.concatenate(
        [w[:, :o_ig].astype(BF16), w[:, o_gs5:].astype(BF16), w[:, o_ig:o_gs5].astype(BF16),
         jnp.zeros((D_MODEL, LANES - 2 * ML_HEADS), BF16)], axis=1)
    gate_bias = jnp.concatenate([b_ig[0], b_fg[0], jnp.zeros((LANES - 2 * ML_HEADS,), F32)])[None]

    ar, ai = s5_a_re[0], s5_a_im[0]
    dt = jnp.exp(s5_log_dt[0])[:, None]
    mag = jnp.exp(dt * ar)
    abar_re = mag * jnp.cos(dt * ai)
    abar_im = mag * jnp.sin(dt * ai)
    den = ar * ar + ai * ai
    fr = ((abar_re - 1.0) * ar + abar_im * ai) / den
    fi = (abar_im * ar - (abar_re - 1.0) * ai) / den
    br, bi = s5_b_re[0], s5_b_im[0]
    bbar_re = fr[..., None] * br - fi[..., None] * bi
    bbar_im = fr[..., None] * bi + fi[..., None] * br
    gh = S5_GROUPS // 2
    same_group = (jnp.arange(gh)[:, None, None, None] == jnp.arange(gh)[None, None, :, None])

    def blockdiag_in(bb):
        t = bb.reshape(2, gh, S5_STATE, S5_GROUP).transpose(0, 1, 3, 2)
        full = jnp.where(same_group[None], t[:, :, :, None, :], 0.0)
        return full.reshape(2, gh * S5_GROUP, gh * S5_STATE).astype(BF16)

    def blockdiag_out(cc):
        t = cc.reshape(2, gh, S5_GROUP, S5_STATE).transpose(0, 1, 3, 2)
        full = jnp.where(same_group[None], t[:, :, :, None, :], 0.0)
        return full.reshape(2, gh * S5_STATE, gh * S5_GROUP)

    cb = jnp.concatenate([blockdiag_out(s5_c_re[0]), -blockdiag_out(s5_c_im[0])], axis=1).astype(BF16)
    w_router_p = jnp.concatenate(
        [w_router[0], jnp.zeros((D_MODEL, LANES - N_EXPERTS), F32)], axis=1).astype(BF16)
    b_router_p = jnp.concatenate([b_router[0], jnp.zeros((LANES - N_EXPERTS,), F32)])[None]
    return dict(
        norm_mix=norm_mix[0][None], w_in=w_p, gate_bias=gate_bias,
        abar_re=abar_re.reshape(1, S5_CH), abar_im=abar_im.reshape(1, S5_CH),
        bd_re=blockdiag_in(bbar_re), bd_im=blockdiag_in(bbar_im), cb=cb,
        d_skip=s5_d[0][None], w_glu=w_s5_glu[0].astype(BF16), ml_gn=ml_gn[0][None],
        w_ml_out=w_ml_out[0].astype(BF16), w_out=w_out[0].astype(BF16),
        norm_moe=norm_moe[0][None], w_router=w_router_p, b_router=b_router_p)


def kernel(x_prompt, x_sample, state_s5_re, state_s5_im, state_ml_C, state_ml_n, state_ml_m, norm_mix, w_in, b_ig, b_fg, s5_a_re, s5_a_im, s5_log_dt, s5_b_re, s5_b_im, s5_c_re, s5_c_im, s5_d, w_s5_glu, ml_gn, w_ml_out, w_out, norm_moe, w_router, b_router, w_gate_up, b_gate_up, w_down, b_down, norm_final):
    w = _prep_weights(norm_mix, w_in, b_ig, b_fg, s5_a_re, s5_a_im, s5_log_dt, s5_b_re, s5_b_im,
                      s5_c_re, s5_c_im, s5_d, w_s5_glu, ml_gn, w_ml_out, w_out, norm_moe,
                      w_router, b_router)
    batch, lp, _ = x_prompt.shape
    ls = x_sample.shape[1]
    bh = batch * ML_HEADS
    x1, xm, idx, gates, hr, hi, c, n, m = _mixer(
        x_prompt, x_sample,
        state_s5_re[0].reshape(batch, S5_CH), state_s5_im[0].reshape(batch, S5_CH),
        state_ml_C[0].reshape(bh, ML_HEAD_DIM, ML_HEAD_DIM), state_ml_n[0].reshape(bh, ML_HEAD_DIM),
        jnp.broadcast_to(state_ml_m[0].reshape(bh, 1), (bh, LANES)), w)

    tile = batch * CHUNK
    t_all = idx.shape[0]
    n_blocks = -(-(t_all * TOP_K + N_EXPERTS * (MOE_BLOCK - 1)) // MOE_BLOCK)
    n_blocks_pad = -(-n_blocks // SUBLANES) * SUBLANES
    dest, meta = _plan(idx, tile, n_blocks_pad)
    dest_flat = dest.reshape(t_all * TOP_K)
    buf = _dispatch(meta[:n_blocks, 1], dest_flat, xm, tile, n_blocks)
    out_rows = _experts(meta[:n_blocks, 0], meta[:1, 2], buf,
                        w_gate_up[0].astype(BF16), b_gate_up[0][:, None, :],
                        w_down[0].astype(BF16), b_down[0][:, None, :], n_blocks)
    y_prompt, y_sample = _combine(dest_flat, gates, x1, out_rows, norm_final[None], batch, lp, ls)

    def states(p):
        return (hr[p].reshape(1, batch, S5_GROUPS, S5_STATE), hi[p].reshape(1, batch, S5_GROUPS, S5_STATE),
                c[p].reshape(1, batch, ML_HEADS, ML_HEAD_DIM, ML_HEAD_DIM),
                n[p].reshape(1, batch, ML_HEADS, ML_HEAD_DIM), m[p, :, 0].reshape(1, batch, ML_HEADS))

    return (y_prompt, y_sample) + states(0) + states(1)
```

```python
import functools

import jax
import jax.numpy as jnp
from jax import lax
from jax.experimental import pallas as pl
from jax.experimental.pallas import tpu as pltpu

F32 = jnp.float32
BF16 = jnp.bfloat16

D_MODEL = 1024
CHUNK = 64
EPS = 1e-6
D_S5 = 512
S5_GROUP = 16
S5_GROUPS = D_S5 // S5_GROUP
S5_STATE = 64
S5_CH = S5_GROUPS * S5_STATE
ML_HEADS = 4
ML_HEAD_DIM = 128
D_ML = ML_HEADS * ML_HEAD_DIM
N_EXPERTS = 32
TOP_K = 4
D_FF = 1024
SWIGLU_LIMIT = 7.0
SWIGLU_ALPHA = 1.702

LANES = 128
SUBLANES = 8
TILE_ROWS = D_MODEL // LANES

OFF_U = 0
OFF_Q = OFF_U + D_S5
OFF_K = OFF_Q + D_ML
OFF_V = OFF_K + D_ML
OFF_O = OFF_V + D_ML
OFF_GS5 = OFF_O + D_ML
OFF_GML = OFF_GS5 + D_MODEL
OFF_GATE = OFF_GML + D_MODEL
W_COLS = OFF_GATE + LANES

S5_HALF_IN = D_S5 // 2
S5_HALF_CH = S5_CH // 2
S5_HALF_TILES = S5_HALF_CH // LANES
S5_SCAN_TILES = 4
ML_GROUP = 2
MOE_BLOCK = 256
VMEM_LIMIT = 60 * 1024 * 1024


def _rmsnorm(x, w):
    return x * lax.rsqrt(jnp.mean(x * x, axis=-1, keepdims=True) + EPS) * w


def _const_spec(shape):
    nd = len(shape)
    return pl.BlockSpec(shape, lambda *_: (0,) * nd, pipeline_mode=pl.Buffered(1))


def _mixer_kernel(xp_ref, xs_ref, h0r_ref, h0i_ref, cx0_ref, m0_ref,
                  nmix_ref, win_ref, gbias_ref, abr_ref, abi_ref, bdr_ref, bdi_ref, cb_ref,
                  dskip_ref, glu_ref, gn_ref, wml_ref, wout_ref, nmoe_ref, wr_ref, br_ref, sel_ref,
                  x1_ref, xm_ref, idx_ref, gate_ref, hr_ref, hi_ref, cx_ref, m_ref,
                  s_bur, s_bui, s_q, s_k, s_v, s_o, s_col, s_gated, s_cx, sem, *, batch, n_prompt):
    i = pl.program_id(0)
    rows = batch * CHUNK

    @pl.when(i == 0)
    def _():
        hr_ref[...] = jnp.zeros_like(hr_ref)
        hi_ref[...] = jnp.zeros_like(hi_ref)
        m_ref[...] = jnp.zeros_like(m_ref)
        s_cx[...] = jnp.zeros_like(s_cx)

    @pl.when(i == n_prompt)
    def _():
        hr_ref[0] = h0r_ref[...]
        hi_ref[0] = h0i_ref[...]
        m_ref[0] = m0_ref[...]
        load = pltpu.make_async_copy(cx0_ref, s_cx, sem)
        load.start()
        load.wait()

    @pl.when(i < n_prompt)
    def _():
        x1_ref[...] = xp_ref[...].reshape(rows, D_MODEL)

    @pl.when(i >= n_prompt)
    def _():
        x1_ref[...] = xs_ref[...].reshape(rows, D_MODEL)

    xn = _rmsnorm(x1_ref[...], nmix_ref[...]).astype(BF16)

    def proj(off, width):
        return jnp.dot(xn, win_ref[:, off:off + width], preferred_element_type=F32)

    s_q[...] = (proj(OFF_Q, D_ML) * (ML_HEAD_DIM ** -0.5)).astype(BF16)
    s_k[...] = proj(OFF_K, D_ML).astype(BF16)
    s_v[...] = proj(OFF_V, D_ML).astype(BF16)
    s_o[...] = proj(OFF_O, D_ML)
    gates = proj(OFF_GATE, LANES) + gbias_ref[...]
    lane_g = lax.broadcasted_iota(jnp.int32, (rows, LANES), 1)
    gg = jnp.where(lane_g < ML_HEADS, gates, jax.nn.log_sigmoid(gates))
    gt8 = gg.T[0:SUBLANES, :]
    pos = lax.broadcasted_iota(jnp.int32, (SUBLANES, rows), 1) % CHUNK
    cum = gt8
    shift = 1
    while shift < CHUNK:
        cum = cum + jnp.where(pos >= shift, pltpu.roll(cum, shift, axis=1), 0.0)
        shift *= 2
    g8 = gt8 - pltpu.roll(cum, ML_HEADS, axis=0)
    mx8 = g8
    shift = 1
    while shift < CHUNK:
        mx8 = jnp.maximum(mx8, jnp.where(pos >= shift, pltpu.roll(mx8, shift, axis=1), -jnp.inf))
        shift *= 2
    sub = lax.broadcasted_iota(jnp.int32, (SUBLANES, rows), 0)
    top8 = jnp.where(sub < ML_HEADS, g8, cum)
    s_col[...] = jnp.concatenate(
        [top8, mx8, jnp.zeros((LANES - 2 * SUBLANES, rows), F32)], axis=0).T

    tri = (lax.broadcasted_iota(jnp.int32, (CHUNK, CHUNK), 0)
           >= lax.broadcasted_iota(jnp.int32, (CHUNK, CHUNK), 1))[None]
    bdims = ((0,), (0,))

    def rowsl(b):
        return slice(b * CHUNK, (b + 1) * CHUNK)

    def headsl(h):
        return slice(h * ML_HEAD_DIM, (h + 1) * ML_HEAD_DIM)

    for g0 in range(0, batch, ML_GROUP):
        items = [(b, h) for b in range(g0, g0 + ML_GROUP) for h in range(ML_HEADS)]
        cols = s_col[g0 * CHUNK:(g0 + ML_GROUP) * CHUNK, :]
        hi = cols.astype(BF16)
        r1 = cols - hi.astype(F32)
        mid = r1.astype(BF16)
        lo = (r1 - mid.astype(F32)).astype(BF16)
        rep = (jnp.dot(hi, sel_ref[...], preferred_element_type=F32)
               + jnp.dot(mid, sel_ref[...], preferred_element_type=F32)
               + jnp.dot(lo, sel_ref[...], preferred_element_type=F32))

        def col(j, rep=rep, items=items, g0=g0):
            return jnp.stack([rep[(b - g0) * CHUNK:(b - g0 + 1) * CHUNK,
                                  (j + h) * LANES:(j + h + 1) * LANES] for b, h in items])

        g_c, b_c, mx_c = col(0), col(ML_HEADS), col(2 * ML_HEADS)
        g_r = jnp.stack([g8[h:h + 1, rowsl(b)] for b, h in items])
        m_prev = jnp.stack([m_ref[0, b * ML_HEADS + h:b * ML_HEADS + h + 1, :] for b, h in items])
        q3 = jnp.stack([s_q[rowsl(b), headsl(h)] for b, h in items])
        k3 = jnp.stack([s_k[rowsl(b), headsl(h)] for b, h in items])
        v3 = jnp.stack([s_v[rowsl(b), headsl(h)] for b, h in items])
        cx = jnp.stack([s_cx[b * ML_HEADS + h] for b, h in items])

        big_m = jnp.maximum(m_prev, mx_c)
        p = jnp.exp(jnp.where(tri, g_r - big_m[:, :, :CHUNK], -jnp.inf))
        w_inter = jnp.exp(m_prev - big_m)
        s = lax.dot_general(q3, k3, (((2,), (2,)), bdims), preferred_element_type=F32) * p
        cqx = lax.dot_general(q3, cx.astype(BF16), (((2,), (1,)), bdims), preferred_element_type=F32)
        num = (lax.dot_general(s.astype(BF16), v3, (((2,), (1,)), bdims), preferred_element_type=F32)
               + w_inter * cqx[:, :, :ML_HEAD_DIM])
        den_dot = jnp.sum(s, axis=2, keepdims=True) + w_inter * cqx[:, :, ML_HEAD_DIM:]
        m_t = b_c + big_m
        hout = num / jnp.maximum(jnp.abs(den_dot), jnp.exp(-m_t))

        m_last = big_m[:, CHUNK - 1:CHUNK, :]
        w_end = jnp.exp(g_c - m_last)
        decay = jnp.exp(m_prev - m_last)
        wvx = jnp.concatenate([w_end * v3.astype(F32), w_end], axis=2).astype(BF16)
        k_t = jnp.stack([s_k[rowsl(b), headsl(h)].astype(F32).T.astype(BF16) for b, h in items])
        cx_new = (jnp.concatenate([decay, decay], axis=2) * cx
                  + lax.dot_general(k_t, wvx, (((2,), (1,)), bdims), preferred_element_type=F32))
        m_new = m_t[:, CHUNK - 1:CHUNK, :]

        hc = hout - jnp.mean(hout, axis=2, keepdims=True)
        hn = hc * lax.rsqrt(jnp.mean(hc * hc, axis=2, keepdims=True) + EPS)
        for n, (b, h) in enumerate(items):
            bh = b * ML_HEADS + h
            s_cx[bh] = cx_new[n]
            m_ref[0, bh:bh + 1, :] = m_new[n]
            s_gated[rowsl(b), headsl(h)] = (jax.nn.sigmoid(s_o[rowsl(b), headsl(h)])
                                            * (hn[n] * gn_ref[:, headsl(h)])).astype(BF16)

    mix = jax.nn.sigmoid(proj(OFF_GML, D_MODEL)) * jnp.dot(
        s_gated[...], wml_ref[...], preferred_element_type=F32)

    u = proj(OFF_U, D_S5)
    ub = u.astype(BF16)
    ys = []
    for k in range(2):
        uk = ub[:, k * S5_HALF_IN:(k + 1) * S5_HALF_IN]
        bur = jnp.dot(uk, bdr_ref[k], preferred_element_type=F32)
        bui = jnp.dot(uk, bdi_ref[k], preferred_element_type=F32)
        for j in range(S5_HALF_TILES):
            for b in range(batch):
                dst = pl.ds(b, CHUNK, stride=batch)
                src = slice(b * CHUNK, (b + 1) * CHUNK)
                s_bur[j, dst, :] = bur[src, j * LANES:(j + 1) * LANES]
                s_bui[j, dst, :] = bui[src, j * LANES:(j + 1) * LANES]

        for c in range(S5_HALF_TILES // S5_SCAN_TILES):
            tiles = range(c * S5_SCAN_TILES, (c + 1) * S5_SCAN_TILES)
            lanes = [slice(k * S5_HALF_CH + j * LANES, k * S5_HALF_CH + (j + 1) * LANES) for j in tiles]
            ar = [jnp.broadcast_to(abr_ref[:, ls], (batch, LANES)) for ls in lanes]
            ai = [jnp.broadcast_to(abi_ref[:, ls], (batch, LANES)) for ls in lanes]

            def step(t, carry, tiles=tiles, ar=ar, ai=ai):
                slab = pl.ds(pl.multiple_of(t * batch, batch), batch)
                out = []
                for n, j in enumerate(tiles):
                    hr, hi = carry[n]
                    nr = ar[n] * hr - ai[n] * hi + s_bur[j, slab, :]
                    ni = ar[n] * hi + ai[n] * hr + s_bui[j, slab, :]
                    s_bur[j, slab, :] = nr
                    s_bui[j, slab, :] = ni
                    out.append((nr, ni))
                return tuple(out)

            init = tuple((hr_ref[0, :, ls], hi_ref[0, :, ls]) for ls in lanes)
            fin = lax.fori_loop(0, CHUNK, step, init, unroll=8)
            for n, ls in enumerate(lanes):
                hr_ref[0, :, ls] = fin[n][0]
                hi_ref[0, :, ls] = fin[n][1]

        def stream_major(ref):
            return jnp.concatenate(
                [jnp.concatenate([ref[j, pl.ds(b, CHUNK, stride=batch), :]
                                  for j in range(S5_HALF_TILES)], axis=1) for b in range(batch)], axis=0)

        yk = jnp.dot(stream_major(s_bur).astype(BF16), cb_ref[k, :S5_HALF_CH, :],
                     preferred_element_type=F32)
        yk = yk + jnp.dot(stream_major(s_bui).astype(BF16), cb_ref[k, S5_HALF_CH:, :],
                          preferred_element_type=F32)
        ys.append(yk)
    y = jnp.concatenate(ys, axis=1) + dskip_ref[...] * u
    glu = jnp.dot(jax.nn.gelu(y).astype(BF16), glu_ref[...], preferred_element_type=F32)
    y_s5 = glu[:, :D_MODEL] * jax.nn.sigmoid(glu[:, D_MODEL:])
    mix = mix + jax.nn.sigmoid(proj(OFF_GS5, D_MODEL)) * y_s5

    x1 = x1_ref[...] + jnp.dot(mix.astype(BF16), wout_ref[...], preferred_element_type=F32)
    x1_ref[...] = x1

    xm = _rmsnorm(x1, nmoe_ref[...])
    logits = jnp.dot(xm.astype(BF16), wr_ref[...], preferred_element_type=F32) + br_ref[...]
    lane = lax.broadcasted_iota(jnp.int32, (rows, LANES), 1)
    logits = jnp.where(lane < N_EXPERTS, logits, -jnp.inf)
    vals, idxs = [], []
    for _ in range(TOP_K):
        mx = jnp.max(logits, axis=1, keepdims=True)
        am = jnp.min(jnp.where(logits == mx, lane, LANES), axis=1, keepdims=True)
        vals.append(mx)
        idxs.append(am)
        logits = jnp.where(lane == am, -jnp.inf, logits)
    exps = [jnp.exp(v - vals[0]) for v in vals]
    esum = exps[0] + exps[1] + exps[2] + exps[3]
    idx_w = jnp.zeros((rows, LANES), jnp.int32)
    gate_w = jnp.zeros((rows, LANES), F32)
    for k in range(TOP_K):
        idx_w = jnp.where(lane == k, idxs[k], idx_w)
        gate_w = jnp.where(lane == k, exps[k] / esum, gate_w)
    idx_ref[...] = idx_w[:, :TOP_K]
    gate_ref[...] = gate_w[:, :TOP_K]

    for s in range(TILE_ROWS):
        xm_ref[pl.ds(s, rows, stride=TILE_ROWS), :] = xm[:, s * LANES:(s + 1) * LANES]

    for phase_id, final_step in ((0, n_prompt - 1), (1, pl.num_programs(0) - 1)):
        @pl.when(i == final_step)
        def _(phase_id=phase_id):
            store = pltpu.make_async_copy(s_cx, cx_ref.at[phase_id], sem)
            store.start()
            store.wait()


def _mixer(x_prompt, x_sample, h0r, h0i, cx0, m0, w):
    batch, lp, _ = x_prompt.shape
    assert x_sample.shape[0] == batch and lp % CHUNK == 0 and x_sample.shape[1] % CHUNK == 0
    assert batch % ML_GROUP == 0
    n_prompt = lp // CHUNK
    n_sample = x_sample.shape[1] // CHUNK
    nblk = n_prompt + n_sample
    rows = batch * CHUNK
    t_all = nblk * rows
    bh = batch * ML_HEADS

    def phase(i):
        return jnp.minimum(i // n_prompt, 1)

    state_specs = [
        _const_spec((batch, S5_CH)), _const_spec((batch, S5_CH)),
        pl.BlockSpec(memory_space=pl.ANY),
        _const_spec((bh, LANES)),
    ]
    n_sel = 3 * ML_HEADS
    sel = (jnp.arange(LANES)[:, None] == (jnp.arange(n_sel * LANES) // LANES)[None, :]).astype(BF16)
    weights = (w['norm_mix'], w['w_in'], w['gate_bias'], w['abar_re'], w['abar_im'], w['bd_re'],
               w['bd_im'], w['cb'], w['d_skip'], w['w_glu'], w['ml_gn'], w['w_ml_out'], w['w_out'],
               w['norm_moe'], w['w_router'], w['b_router'], sel)
    in_specs = ([pl.BlockSpec((batch, CHUNK, D_MODEL), lambda i: (0, jnp.minimum(i, n_prompt - 1), 0)),
                 pl.BlockSpec((batch, CHUNK, D_MODEL), lambda i: (0, jnp.maximum(i - n_prompt, 0), 0))]
                + state_specs + [_const_spec(a.shape) for a in weights])
    out_shape = (
        jax.ShapeDtypeStruct((t_all, D_MODEL), F32),
        jax.ShapeDtypeStruct((t_all * TILE_ROWS, LANES), F32),
        jax.ShapeDtypeStruct((t_all, TOP_K), jnp.int32),
        jax.ShapeDtypeStruct((t_all, TOP_K), F32),
        jax.ShapeDtypeStruct((2, batch, S5_CH), F32),
        jax.ShapeDtypeStruct((2, batch, S5_CH), F32),
        jax.ShapeDtypeStruct((2, bh, ML_HEAD_DIM, 2 * ML_HEAD_DIM), F32),
        jax.ShapeDtypeStruct((2, bh, LANES), F32),
    )
    out_specs = (
        pl.BlockSpec((rows, D_MODEL), lambda i: (i, 0)),
        pl.BlockSpec((rows * TILE_ROWS, LANES), lambda i: (i, 0)),
        pl.BlockSpec((rows, TOP_K), lambda i: (i, 0)),
        pl.BlockSpec((rows, TOP_K), lambda i: (i, 0)),
        pl.BlockSpec((1, batch, S5_CH), lambda i: (phase(i), 0, 0)),
        pl.BlockSpec((1, batch, S5_CH), lambda i: (phase(i), 0, 0)),
        pl.BlockSpec(memory_space=pl.ANY),
        pl.BlockSpec((1, bh, LANES), lambda i: (phase(i), 0, 0)),
    )
    scratch = [
        pltpu.VMEM((S5_HALF_TILES, rows, LANES), F32), pltpu.VMEM((S5_HALF_TILES, rows, LANES), F32),
        pltpu.VMEM((rows, D_ML), BF16), pltpu.VMEM((rows, D_ML), BF16), pltpu.VMEM((rows, D_ML), BF16),
        pltpu.VMEM((rows, D_ML), F32),
        pltpu.VMEM((rows, LANES), F32),
        pltpu.VMEM((rows, D_ML), BF16),
        pltpu.VMEM((bh, ML_HEAD_DIM, 2 * ML_HEAD_DIM), F32),
        pltpu.SemaphoreType.DMA(()),
    ]
    return pl.pallas_call(
        functools.partial(_mixer_kernel, batch=batch, n_prompt=n_prompt),
        grid=(nblk,),
        in_specs=in_specs,
        out_specs=out_specs,
        out_shape=out_shape,
        scratch_shapes=scratch,
        compiler_params=pltpu.CompilerParams(dimension_semantics=("arbitrary",),
                                             vmem_limit_bytes=VMEM_LIMIT),
        name="mixer",
    )(x_prompt, x_sample, h0r, h0i, cx0, m0, *weights)


def _plan_kernel(idx_ref, dest_ref, meta_ref, s_cnt, s_run, s_pstart, *, tile, n_blocks_pad):
    phase = pl.program_id(0)
    i = pl.program_id(1)
    lane = lax.broadcasted_iota(jnp.int32, (tile, LANES), 1)
    idx = idx_ref[...]
    onehots = [(lane == idx[:, k:k + 1]).astype(F32) for k in range(TOP_K)]
    mask = onehots[0] + onehots[1] + onehots[2] + onehots[3]

    @pl.when(jnp.logical_and(phase == 0, i == 0))
    def _():
        s_cnt[...] = jnp.zeros_like(s_cnt)

    @pl.when(phase == 0)
    def _():
        s_cnt[...] += jnp.sum(mask, axis=0, keepdims=True)

    @pl.when(jnp.logical_and(phase == 1, i == 0))
    def _():
        cnt = s_cnt[...]
        nblk = jnp.floor((cnt + (MOE_BLOCK - 1)) * (1.0 / MOE_BLOCK))
        lane1 = lax.broadcasted_iota(jnp.int32, (1, LANES), 1)
        pend = nblk
        shift = 1
        while shift < LANES:
            pend = pend + jnp.where(lane1 >= shift, pltpu.roll(pend, shift, axis=1), 0.0)
            shift *= 2
        s_pstart[...] = (pend - nblk) * MOE_BLOCK
        s_run[...] = jnp.zeros_like(s_run)
        blk = lax.broadcasted_iota(jnp.int32, (n_blocks_pad, LANES), 0).astype(F32)
        lane_b = lax.broadcasted_iota(jnp.int32, (n_blocks_pad, LANES), 1)
        is_e = lane_b < N_EXPERTS
        done = jnp.logical_and(is_e, pend <= blk)
        e_of = jnp.minimum(jnp.sum(done.astype(F32), axis=1, keepdims=True), N_EXPERTS - 1.0)
        mine = lane_b.astype(F32) == e_of
        rows_left = jnp.sum(jnp.where(mine, cnt - (blk - (pend - nblk)) * MOE_BLOCK, 0.0),
                            axis=1, keepdims=True)
        valid = jnp.clip(rows_left, 0.0, float(MOE_BLOCK))
        used = jnp.sum(jnp.where(lane1 == N_EXPERTS - 1, pend, 0.0), axis=1, keepdims=True)
        meta = jnp.where(lane_b == 0, e_of, jnp.where(lane_b == 1, valid, used))
        meta_ref[...] = meta.astype(jnp.int32)

    @pl.when(phase == 1)
    def _():
        r = lax.broadcasted_iota(jnp.int32, (tile, tile), 0)
        c = lax.broadcasted_iota(jnp.int32, (tile, tile), 1)
        lower = (c < r).astype(BF16)
        before = jnp.dot(lower, mask.astype(BF16), preferred_element_type=F32)
        base = before + s_run[...] + s_pstart[...]
        dest = jnp.zeros((tile, LANES), F32)
        for k in range(TOP_K):
            dk = jnp.sum(onehots[k] * base, axis=1, keepdims=True)
            dest = jnp.where(lane == k, dk, dest)
        dest_ref[...] = dest[:, :TOP_K].astype(jnp.int32)
        s_run[...] += jnp.sum(mask, axis=0, keepdims=True)


def _plan(idx_all, tile, n_blocks_pad):
    t_all = idx_all.shape[0]
    return pl.pallas_call(
        functools.partial(_plan_kernel, tile=tile, n_blocks_pad=n_blocks_pad),
        grid=(2, t_all // tile),
        in_specs=[pl.BlockSpec((tile, TOP_K), lambda p, i: (i, 0))],
        out_specs=(pl.BlockSpec((tile, TOP_K), lambda p, i: (i * p, 0)),
                   pl.BlockSpec((n_blocks_pad, LANES), lambda p, i: (0, 0))),
        out_shape=(jax.ShapeDtypeStruct((t_all, TOP_K), jnp.int32),
                   jax.ShapeDtypeStruct((n_blocks_pad, LANES), jnp.int32)),
        scratch_shapes=[pltpu.VMEM((1, LANES), F32), pltpu.VMEM((1, LANES), F32),
                        pltpu.VMEM((1, LANES), F32)],
        compiler_params=pltpu.CompilerParams(dimension_semantics=("arbitrary", "arbitrary")),
        name="moe_plan",
    )(idx_all)


def _token_tile(ref, t):
    return ref.at[pl.ds(pl.multiple_of(t * TILE_ROWS, TILE_ROWS), TILE_ROWS)]


def _dispatch_kernel(valid_ref, dest_ref, xm_ref, buf_ref, zeros, sem, zsem, *, tile, n_blocks):
    blk_rows = MOE_BLOCK * TILE_ROWS

    @pl.when(pl.program_id(0) == 0)
    def _():
        zeros[...] = jnp.zeros_like(zeros)

        def block_copy(i):
            return pltpu.make_async_copy(
                zeros, buf_ref.at[pl.ds(pl.multiple_of(i * blk_rows, blk_rows), blk_rows)], zsem)

        def fill(i, carry):
            @pl.when(valid_ref[i] < MOE_BLOCK)
            def _():
                block_copy(i).start()
            return carry

        def drain(i, carry):
            @pl.when(valid_ref[i] < MOE_BLOCK)
            def _():
                block_copy(i).wait()
            return carry

        lax.fori_loop(0, n_blocks, fill, 0)
        lax.fori_loop(0, n_blocks, drain, 0)

    def issue(j, carry):
        src = _token_tile(xm_ref, j)
        for k in range(TOP_K):
            pltpu.make_async_copy(src, _token_tile(buf_ref, dest_ref[j * TOP_K + k]), sem).start(
                priority=k % 2)
        return carry

    lax.fori_loop(0, tile, issue, 0, unroll=4)
    n_rows = tile * TOP_K * TILE_ROWS
    pltpu.make_async_copy(buf_ref.at[pl.ds(0, n_rows)], buf_ref.at[pl.ds(0, n_rows)], sem).wait()


def _dispatch(block_valid, dest_flat, xm_tiles, tile, n_blocks):
    t_all = xm_tiles.shape[0] // TILE_ROWS
    grid_spec = pltpu.PrefetchScalarGridSpec(
        num_scalar_prefetch=1,
        grid=(t_all // tile,),
        in_specs=[pl.BlockSpec((tile * TOP_K,), lambda i, va: (i,), memory_space=pltpu.SMEM),
                  pl.BlockSpec((tile * TILE_ROWS, LANES), lambda i, va: (i, 0))],
        out_specs=pl.BlockSpec(memory_space=pl.ANY),
        scratch_shapes=[pltpu.VMEM((MOE_BLOCK * TILE_ROWS, LANES), F32),
                        pltpu.SemaphoreType.DMA(()), pltpu.SemaphoreType.DMA(())],
    )
    return pl.pallas_call(
        functools.partial(_dispatch_kernel, tile=tile, n_blocks=n_blocks),
        grid_spec=grid_spec,
        out_shape=jax.ShapeDtypeStruct((n_blocks * MOE_BLOCK * TILE_ROWS, LANES), F32),
        compiler_params=pltpu.CompilerParams(dimension_semantics=("arbitrary",)),
        name="moe_dispatch",
    )(block_valid, dest_flat, xm_tiles)


def _expert_kernel(be_ref, used_ref, x_ref, wgu_ref, bgu_ref, wd_ref, bd_ref, o_ref, s_wgu, s_wd):
    i = pl.program_id(0)

    @pl.when(i >= used_ref[0])
    def _():
        o_ref[...] = jnp.zeros_like(o_ref)

    @pl.when(jnp.logical_or(i == 0, be_ref[i] != be_ref[jnp.maximum(i - 1, 0)]))
    def _():
        s_wgu[...] = wgu_ref[0].astype(BF16)
        s_wd[...] = wd_ref[0].astype(BF16)

    @pl.when(i < used_ref[0])
    def _():
        pieces = [x_ref[pl.ds(s, MOE_BLOCK, stride=TILE_ROWS), :] for s in range(TILE_ROWS)]
        x = jnp.concatenate(pieces, axis=1).astype(BF16)
        gu = jnp.dot(x, s_wgu[...], preferred_element_type=F32) + bgu_ref[0]
        g = jnp.minimum(gu[:, :D_FF], SWIGLU_LIMIT)
        up = jnp.clip(gu[:, D_FF:], -SWIGLU_LIMIT, SWIGLU_LIMIT)
        hdn = (up + 1.0) * (g * jax.nn.sigmoid(SWIGLU_ALPHA * g))
        out = jnp.dot(hdn.astype(BF16), s_wd[...], preferred_element_type=F32) + bd_ref[0]
        for s in range(TILE_ROWS):
            o_ref[pl.ds(s, MOE_BLOCK, stride=TILE_ROWS), :] = out[:, s * LANES:(s + 1) * LANES]


def _experts(block_e, n_used, buf, wgu, bgu, wd, bd, n_blocks):
    blk_rows = MOE_BLOCK * TILE_ROWS

    def in_row_map(i, be, nu):
        return (jnp.minimum(i, nu[0] - 1), 0)

    def w_map(i, be, nu):
        return (be[i], 0, 0)

    grid_spec = pltpu.PrefetchScalarGridSpec(
        num_scalar_prefetch=2,
        grid=(n_blocks,),
        in_specs=[pl.BlockSpec((blk_rows, LANES), in_row_map),
                  pl.BlockSpec((1, D_MODEL, 2 * D_FF), w_map),
                  pl.BlockSpec((1, 1, 2 * D_FF), w_map),
                  pl.BlockSpec((1, D_FF, D_MODEL), w_map),
                  pl.BlockSpec((1, 1, D_MODEL), w_map)],
        out_specs=pl.BlockSpec((blk_rows, LANES), lambda i, be, nu: (i, 0)),
        scratch_shapes=[pltpu.VMEM((D_MODEL, 2 * D_FF), BF16), pltpu.VMEM((D_FF, D_MODEL), BF16)],
    )
    return pl.pallas_call(
        _expert_kernel,
        grid_spec=grid_spec,
        out_shape=jax.ShapeDtypeStruct(buf.shape, F32),
        compiler_params=pltpu.CompilerParams(dimension_semantics=("arbitrary",),
                                             vmem_limit_bytes=VMEM_LIMIT),
        name="moe_experts",
    )(block_e, n_used, buf, wgu, bgu, wd, bd)


def _combine_kernel(dest_ref, gate_ref, x1_ref, rows_ref, nf_ref, yp_ref, ys_ref, gbuf, sem, *,
                    batch, n_prompt):
    i = pl.program_id(0)
    tile = batch * CHUNK
    tile_rows = tile * TILE_ROWS

    def issue(j, carry):
        for k in range(TOP_K):
            pltpu.make_async_copy(_token_tile(rows_ref, dest_ref[j * TOP_K + k]),
                                  _token_tile(gbuf.at[k], j), sem).start(priority=k % 2)
        return carry

    lax.fori_loop(0, tile, issue, 0, unroll=4)
    for k in range(TOP_K):
        pltpu.make_async_copy(rows_ref.at[pl.ds(0, tile_rows)], gbuf.at[k], sem).wait()

    gates = gate_ref[...]
    pieces = []
    for s in range(TILE_ROWS):
        piece = x1_ref[:, s * LANES:(s + 1) * LANES]
        for k in range(TOP_K):
            piece = piece + gates[:, k:k + 1] * gbuf[k, pl.ds(s, tile, stride=TILE_ROWS), :]
        pieces.append(piece)
    y = _rmsnorm(jnp.concatenate(pieces, axis=1), nf_ref[...]).reshape(batch, CHUNK, D_MODEL)

    @pl.when(i < n_prompt)
    def _():
        yp_ref[...] = y

    @pl.when(i >= n_prompt)
    def _():
        ys_ref[...] = y


def _combine(dest_flat, gates, x1, out_rows, norm_final, batch, lp, ls):
    tile = batch * CHUNK
    n_prompt = lp // CHUNK
    n_sample = ls // CHUNK
    return pl.pallas_call(
        functools.partial(_combine_kernel, batch=batch, n_prompt=n_prompt),
        grid=(n_prompt + n_sample,),
        in_specs=[pl.BlockSpec((tile * TOP_K,), lambda i: (i,), memory_space=pltpu.SMEM),
                  pl.BlockSpec((tile, TOP_K), lambda i: (i, 0)),
                  pl.BlockSpec((tile, D_MODEL), lambda i: (i, 0)),
                  pl.BlockSpec(memory_space=pl.ANY),
                  pl.BlockSpec((1, D_MODEL), lambda i: (0, 0))],
        out_specs=(pl.BlockSpec((batch, CHUNK, D_MODEL), lambda i: (0, jnp.minimum(i, n_prompt - 1), 0)),
                   pl.BlockSpec((batch, CHUNK, D_MODEL), lambda i: (0, jnp.maximum(i - n_prompt, 0), 0))),
        out_shape=(jax.ShapeDtypeStruct((batch, lp, D_MODEL), F32),
                   jax.ShapeDtypeStruct((batch, ls, D_MODEL), F32)),
        scratch_shapes=[pltpu.VMEM((TOP_K, tile * TILE_ROWS, LANES), F32),
                        pltpu.SemaphoreType.DMA(())],
        compiler_params=pltpu.CompilerParams(dimension_semantics=("arbitrary",),
                                             vmem_limit_bytes=VMEM_LIMIT),
        name="moe_combine",
    )(dest_flat, gates, x1, out_rows, norm_final)


def _prep_weights(norm_mix, w_in, b_ig, b_fg, s5_a_re, s5_a_im, s5_log_dt, s5_b_re, s5_b_im,
                  s5_c_re, s5_c_im, s5_d, w_s5_glu, ml_gn, w_ml_out, w_out, norm_moe, w_router,
                  b_router):
    w = w_in[0]
    o_ig = D_S5 + 4 * D_ML
    o_gs5 = o_ig + 2 * ML_HEADS
    w_p = jnp.concatenate(
        [w[:, :o_ig].astype(BF16), w[:, o_gs5:].astype(BF16), w[:, o_ig:o_gs5].astype(BF16),
         jnp.zeros((D_MODEL, LANES - 2 * ML_HEADS), BF16)], axis=1)
    gate_bias = jnp.concatenate([b_ig[0], b_fg[0], jnp.zeros((LANES - 2 * ML_HEADS,), F32)])[None]

    ar, ai = s5_a_re[0], s5_a_im[0]
    dt = jnp.exp(s5_log_dt[0])[:, None]
    mag = jnp.exp(dt * ar)
    abar_re = mag * jnp.cos(dt * ai)
    abar_im = mag * jnp.sin(dt * ai)
    den = ar * ar + ai * ai
    fr = ((abar_re - 1.0) * ar + abar_im * ai) / den
    fi = (abar_im * ar - (abar_re - 1.0) * ai) / den
    br, bi = s5_b_re[0], s5_b_im[0]
    bbar_re = fr[..., None] * br - fi[..., None] * bi
    bbar_im = fr[..., None] * bi + fi[..., None] * br
    gh = S5_GROUPS // 2
    same_group = (jnp.arange(gh)[:, None, None, None] == jnp.arange(gh)[None, None, :, None])

    def blockdiag_in(bb):
        t = bb.reshape(2, gh, S5_STATE, S5_GROUP).transpose(0, 1, 3, 2)
        full = jnp.where(same_group[None], t[:, :, :, None, :], 0.0)
        return full.reshape(2, gh * S5_GROUP, gh * S5_STATE).astype(BF16)

    def blockdiag_out(cc):
        t = cc.reshape(2, gh, S5_GROUP, S5_STATE).transpose(0, 1, 3, 2)
        full = jnp.where(same_group[None], t[:, :, :, None, :], 0.0)
        return full.reshape(2, gh * S5_STATE, gh * S5_GROUP)

    cb = jnp.concatenate([blockdiag_out(s5_c_re[0]), -blockdiag_out(s5_c_im[0])], axis=1).astype(BF16)
    w_router_p = jnp.concatenate(
        [w_router[0], jnp.zeros((D_MODEL, LANES - N_EXPERTS), F32)], axis=1).astype(BF16)
    b_router_p = jnp.concatenate([b_router[0], jnp.zeros((LANES - N_EXPERTS,), F32)])[None]
    return dict(
        norm_mix=norm_mix[0][None], w_in=w_p, gate_bias=gate_bias,
        abar_re=abar_re.reshape(1, S5_CH), abar_im=abar_im.reshape(1, S5_CH),
        bd_re=blockdiag_in(bbar_re), bd_im=blockdiag_in(bbar_im), cb=cb,
        d_skip=s5_d[0][None], w_glu=w_s5_glu[0].astype(BF16), ml_gn=ml_gn[0][None],
        w_ml_out=w_ml_out[0].astype(BF16), w_out=w_out[0].astype(BF16),
        norm_moe=norm_moe[0][None], w_router=w_router_p, b_router=b_router_p)


def kernel(x_prompt, x_sample, state_s5_re, state_s5_im, state_ml_C, state_ml_n, state_ml_m, norm_mix, w_in, b_ig, b_fg, s5_a_re, s5_a_im, s5_log_dt, s5_b_re, s5_b_im, s5_c_re, s5_c_im, s5_d, w_s5_glu, ml_gn, w_ml_out, w_out, norm_moe, w_router, b_router, w_gate_up, b_gate_up, w_down, b_down, norm_final):
    w = _prep_weights(norm_mix, w_in, b_ig, b_fg, s5_a_re, s5_a_im, s5_log_dt, s5_b_re, s5_b_im,
                      s5_c_re, s5_c_im, s5_d, w_s5_glu, ml_gn, w_ml_out, w_out, norm_moe,
                      w_router, b_router)
    batch, lp, _ = x_prompt.shape
    ls = x_sample.shape[1]
    bh = batch * ML_HEADS
    cx0 = jnp.concatenate(
        [jnp.swapaxes(state_ml_C[0].reshape(bh, ML_HEAD_DIM, ML_HEAD_DIM), 1, 2),
         jnp.broadcast_to(state_ml_n[0].reshape(bh, ML_HEAD_DIM, 1), (bh, ML_HEAD_DIM, ML_HEAD_DIM))],
        axis=2)
    x1, xm, idx, gates, hr, hi, cx, m = _mixer(
        x_prompt, x_sample,
        state_s5_re[0].reshape(batch, S5_CH), state_s5_im[0].reshape(batch, S5_CH), cx0,
        jnp.broadcast_to(state_ml_m[0].reshape(bh, 1), (bh, LANES)), w)
    c = jnp.swapaxes(cx[:, :, :, :ML_HEAD_DIM], 2, 3)
    n = cx[:, :, :, ML_HEAD_DIM]

    tile = batch * CHUNK
    t_all = idx.shape[0]
    n_blocks = -(-(t_all * TOP_K + N_EXPERTS * (MOE_BLOCK - 1)) // MOE_BLOCK)
    n_blocks_pad = -(-n_blocks // SUBLANES) * SUBLANES
    dest, meta = _plan(idx, tile, n_blocks_pad)
    dest_flat = dest.reshape(t_all * TOP_K)
    buf = _dispatch(meta[:n_blocks, 1], dest_flat, xm, tile, n_blocks)
    out_rows = _experts(meta[:n_blocks, 0], meta[:1, 2], buf,
                        w_gate_up[0], b_gate_up[0][:, None, :], w_down[0], b_down[0][:, None, :],
                        n_blocks)
    y_prompt, y_sample = _combine(dest_flat, gates, x1, out_rows, norm_final[None], batch, lp, ls)

    def states(p):
        return (hr[p].reshape(1, batch, S5_GROUPS, S5_STATE), hi[p].reshape(1, batch, S5_GROUPS, S5_STATE),
                c[p].reshape(1, batch, ML_HEADS, ML_HEAD_DIM, ML_HEAD_DIM),
                n[p].reshape(1, batch, ML_HEADS, ML_HEAD_DIM), m[p, :, 0].reshape(1, batch, ML_HEADS))

    return (y_prompt, y_sample) + states(0) + states(1)
```

```python
import functools

import jax
import jax.numpy as jnp
from jax import lax
from jax.experimental import pallas as pl
from jax.experimental.pallas import tpu as pltpu

F32 = jnp.float32
BF16 = jnp.bfloat16

D_MODEL = 1024
CHUNK = 64
EPS = 1e-6
D_S5 = 512
S5_GROUP = 16
S5_GROUPS = D_S5 // S5_GROUP
S5_STATE = 64
S5_CH = S5_GROUPS * S5_STATE
ML_HEADS = 4
ML_HEAD_DIM = 128
D_ML = ML_HEADS * ML_HEAD_DIM
N_EXPERTS = 32
TOP_K = 4
D_FF = 1024
SWIGLU_LIMIT = 7.0
SWIGLU_ALPHA = 1.702

LANES = 128
SUBLANES = 8
TILE_ROWS = D_MODEL // LANES

OFF_U = 0
OFF_Q = OFF_U + D_S5
OFF_K = OFF_Q + D_ML
OFF_V = OFF_K + D_ML
OFF_O = OFF_V + D_ML
OFF_GS5 = OFF_O + D_ML
OFF_GML = OFF_GS5 + D_MODEL
OFF_GATE = OFF_GML + D_MODEL
W_COLS = OFF_GATE + LANES

S5_HALF_IN = D_S5 // 2
S5_HALF_CH = S5_CH // 2
S5_HALF_TILES = S5_HALF_CH // LANES
S5_SCAN_TILES = 4
ML_GROUP = 2
MOE_BLOCK = 512
VMEM_LIMIT = 60 * 1024 * 1024


def _rmsnorm(x, w):
    return x * lax.rsqrt(jnp.mean(x * x, axis=-1, keepdims=True) + EPS) * w


def _const_spec(shape):
    nd = len(shape)
    return pl.BlockSpec(shape, lambda *_: (0,) * nd, pipeline_mode=pl.Buffered(1))


def _mixer_kernel(xp_ref, xs_ref, h0r_ref, h0i_ref, cx0_ref, m0_ref,
                  nmix_ref, wina_ref, winb_ref, wing_ref, gbias_ref, abr_ref, abi_ref, bdr_ref, bdi_ref, cb_ref,
                  dskip_ref, glu_ref, gn_ref, wml_ref, wout_ref, nmoe_ref, wr_ref, br_ref, sel_ref,
                  x1_ref, xm_ref, idx_ref, gate_ref, hr_ref, hi_ref, cx_ref, m_ref,
                  s_bur, s_bui, s_q, s_k, s_v, s_o, s_col, s_gated, s_cx, sem, *, batch, n_prompt):
    i = pl.program_id(0)
    rows = batch * CHUNK

    @pl.when(i == 0)
    def _():
        hr_ref[...] = jnp.zeros_like(hr_ref)
        hi_ref[...] = jnp.zeros_like(hi_ref)
        m_ref[...] = jnp.zeros_like(m_ref)
        s_cx[...] = jnp.zeros_like(s_cx)

    @pl.when(i == n_prompt)
    def _():
        hr_ref[0] = h0r_ref[...]
        hi_ref[0] = h0i_ref[...]
        m_ref[0] = m0_ref[...]
        load = pltpu.make_async_copy(cx0_ref, s_cx, sem)
        load.start()
        load.wait()

    @pl.when(i < n_prompt)
    def _():
        x1_ref[...] = xp_ref[...].reshape(rows, D_MODEL)

    @pl.when(i >= n_prompt)
    def _():
        x1_ref[...] = xs_ref[...].reshape(rows, D_MODEL)

    xn = _rmsnorm(x1_ref[...], nmix_ref[...]).astype(BF16)

    def proj(off, width):
        if off >= OFF_GATE:
            w_cols = wing_ref[...]
        elif off >= OFF_GS5:
            w_cols = winb_ref[:, off - OFF_GS5:off - OFF_GS5 + width]
        else:
            w_cols = wina_ref[:, off:off + width]
        return jnp.dot(xn, w_cols, preferred_element_type=F32)

    s_q[...] = (proj(OFF_Q, D_ML) * (ML_HEAD_DIM ** -0.5)).astype(BF16)
    s_k[...] = proj(OFF_K, D_ML).astype(BF16)
    s_v[...] = proj(OFF_V, D_ML).astype(BF16)
    s_o[...] = proj(OFF_O, D_ML)
    gates = proj(OFF_GATE, LANES) + gbias_ref[...]
    lane_g = lax.broadcasted_iota(jnp.int32, (rows, LANES), 1)
    gg = jnp.where(lane_g < ML_HEADS, gates, jax.nn.log_sigmoid(gates))
    gt8 = gg.T[0:SUBLANES, :]
    pos = lax.broadcasted_iota(jnp.int32, (SUBLANES, rows), 1) % CHUNK
    cum = gt8
    shift = 1
    while shift < CHUNK:
        cum = cum + jnp.where(pos >= shift, pltpu.roll(cum, shift, axis=1), 0.0)
        shift *= 2
    g8 = gt8 - pltpu.roll(cum, ML_HEADS, axis=0)
    mx8 = g8
    shift = 1
    while shift < CHUNK:
        mx8 = jnp.maximum(mx8, jnp.where(pos >= shift, pltpu.roll(mx8, shift, axis=1), -jnp.inf))
        shift *= 2
    sub = lax.broadcasted_iota(jnp.int32, (SUBLANES, rows), 0)
    top8 = jnp.where(sub < ML_HEADS, g8, cum)
    s_col[...] = jnp.concatenate(
        [top8, mx8, jnp.zeros((LANES - 2 * SUBLANES, rows), F32)], axis=0).T

    tri = (lax.broadcasted_iota(jnp.int32, (CHUNK, CHUNK), 0)
           >= lax.broadcasted_iota(jnp.int32, (CHUNK, CHUNK), 1))[None]
    bdims = ((0,), (0,))

    def rowsl(b):
        return slice(b * CHUNK, (b + 1) * CHUNK)

    def headsl(h):
        return slice(h * ML_HEAD_DIM, (h + 1) * ML_HEAD_DIM)

    for g0 in range(0, batch, ML_GROUP):
        items = [(b, h) for b in range(g0, g0 + ML_GROUP) for h in range(ML_HEADS)]
        cols = s_col[g0 * CHUNK:(g0 + ML_GROUP) * CHUNK, :]
        hi = cols.astype(BF16)
        r1 = cols - hi.astype(F32)
        mid = r1.astype(BF16)
        lo = (r1 - mid.astype(F32)).astype(BF16)
        rep = (jnp.dot(hi, sel_ref[...], preferred_element_type=F32)
               + jnp.dot(mid, sel_ref[...], preferred_element_type=F32)
               + jnp.dot(lo, sel_ref[...], preferred_element_type=F32))

        def col(j, rep=rep, items=items, g0=g0):
            return jnp.stack([rep[(b - g0) * CHUNK:(b - g0 + 1) * CHUNK,
                                  (j + h) * LANES:(j + h + 1) * LANES] for b, h in items])

        g_c, b_c, mx_c = col(0), col(ML_HEADS), col(2 * ML_HEADS)
        g_r = jnp.stack([g8[h:h + 1, rowsl(b)] for b, h in items])
        m_prev = jnp.stack([m_ref[0, b * ML_HEADS + h:b * ML_HEADS + h + 1, :] for b, h in items])
        q3 = jnp.stack([s_q[rowsl(b), headsl(h)] for b, h in items])
        k3 = jnp.stack([s_k[rowsl(b), headsl(h)] for b, h in items])
        v3 = jnp.stack([s_v[rowsl(b), headsl(h)] for b, h in items])
        cx = jnp.stack([s_cx[b * ML_HEADS + h] for b, h in items])

        big_m = jnp.maximum(m_prev, mx_c)
        p = jnp.exp(jnp.where(tri, g_r - big_m[:, :, :CHUNK], -jnp.inf))
        w_inter = jnp.exp(m_prev - big_m)
        s = lax.dot_general(q3, k3, (((2,), (2,)), bdims), preferred_element_type=F32) * p
        cqx = lax.dot_general(q3, cx.astype(BF16), (((2,), (1,)), bdims), preferred_element_type=F32)
        num = (lax.dot_general(s.astype(BF16), v3, (((2,), (1,)), bdims), preferred_element_type=F32)
               + w_inter * cqx[:, :, :ML_HEAD_DIM])
        den_dot = jnp.sum(s, axis=2, keepdims=True) + w_inter * cqx[:, :, ML_HEAD_DIM:]
        m_t = b_c + big_m
        hout = num / jnp.maximum(jnp.abs(den_dot), jnp.exp(-m_t))

        m_last = big_m[:, CHUNK - 1:CHUNK, :]
        w_end = jnp.exp(g_c - m_last)
        decay = jnp.exp(m_prev - m_last)
        wvx = jnp.concatenate([w_end * v3.astype(F32), w_end], axis=2).astype(BF16)
        k_t = jnp.stack([s_k[rowsl(b), headsl(h)].astype(F32).T.astype(BF16) for b, h in items])
        cx_new = (jnp.concatenate([decay, decay], axis=2) * cx
                  + lax.dot_general(k_t, wvx, (((2,), (1,)), bdims), preferred_element_type=F32))
        m_new = m_t[:, CHUNK - 1:CHUNK, :]

        hc = hout - jnp.mean(hout, axis=2, keepdims=True)
        hn = hc * lax.rsqrt(jnp.mean(hc * hc, axis=2, keepdims=True) + EPS)
        for n, (b, h) in enumerate(items):
            bh = b * ML_HEADS + h
            s_cx[bh] = cx_new[n]
            m_ref[0, bh:bh + 1, :] = m_new[n]
            s_gated[rowsl(b), headsl(h)] = (jax.nn.sigmoid(s_o[rowsl(b), headsl(h)])
                                            * (hn[n] * gn_ref[:, headsl(h)])).astype(BF16)

    mix = jax.nn.sigmoid(proj(OFF_GML, D_MODEL)) * jnp.dot(
        s_gated[...], wml_ref[...], preferred_element_type=F32)

    u = proj(OFF_U, D_S5)
    ub = u.astype(BF16)
    ys = []
    for k in range(2):
        uk = ub[:, k * S5_HALF_IN:(k + 1) * S5_HALF_IN]
        bur = jnp.dot(uk, bdr_ref[k], preferred_element_type=F32)
        bui = jnp.dot(uk, bdi_ref[k], preferred_element_type=F32)
        for j in range(S5_HALF_TILES):
            for b in range(batch):
                dst = pl.ds(b, CHUNK, stride=batch)
                src = slice(b * CHUNK, (b + 1) * CHUNK)
                s_bur[j, dst, :] = bur[src, j * LANES:(j + 1) * LANES]
                s_bui[j, dst, :] = bui[src, j * LANES:(j + 1) * LANES]

        for c in range(S5_HALF_TILES // S5_SCAN_TILES):
            tiles = range(c * S5_SCAN_TILES, (c + 1) * S5_SCAN_TILES)
            lanes = [slice(k * S5_HALF_CH + j * LANES, k * S5_HALF_CH + (j + 1) * LANES) for j in tiles]
            ar = [jnp.broadcast_to(abr_ref[:, ls], (batch, LANES)) for ls in lanes]
            ai = [jnp.broadcast_to(abi_ref[:, ls], (batch, LANES)) for ls in lanes]

            def step(t, carry, tiles=tiles, ar=ar, ai=ai):
                slab = pl.ds(pl.multiple_of(t * batch, batch), batch)
                out = []
                for n, j in enumerate(tiles):
                    hr, hi = carry[n]
                    nr = ar[n] * hr - ai[n] * hi + s_bur[j, slab, :]
                    ni = ar[n] * hi + ai[n] * hr + s_bui[j, slab, :]
                    s_bur[j, slab, :] = nr
                    s_bui[j, slab, :] = ni
                    out.append((nr, ni))
                return tuple(out)

            init = tuple((hr_ref[0, :, ls], hi_ref[0, :, ls]) for ls in lanes)
            fin = lax.fori_loop(0, CHUNK, step, init, unroll=8)
            for n, ls in enumerate(lanes):
                hr_ref[0, :, ls] = fin[n][0]
                hi_ref[0, :, ls] = fin[n][1]

        def stream_major(ref):
            return jnp.concatenate(
                [jnp.concatenate([ref[j, pl.ds(b, CHUNK, stride=batch), :]
                                  for j in range(S5_HALF_TILES)], axis=1) for b in range(batch)], axis=0)

        yk = jnp.dot(stream_major(s_bur).astype(BF16), cb_ref[k, :S5_HALF_CH, :],
                     preferred_element_type=F32)
        yk = yk + jnp.dot(stream_major(s_bui).astype(BF16), cb_ref[k, S5_HALF_CH:, :],
                          preferred_element_type=F32)
        ys.append(yk)
    y = jnp.concatenate(ys, axis=1) + dskip_ref[...] * u
    glu = jnp.dot(jax.nn.gelu(y).astype(BF16), glu_ref[...], preferred_element_type=F32)
    y_s5 = glu[:, :D_MODEL] * jax.nn.sigmoid(glu[:, D_MODEL:])
    mix = mix + jax.nn.sigmoid(proj(OFF_GS5, D_MODEL)) * y_s5

    x1 = x1_ref[...] + jnp.dot(mix.astype(BF16), wout_ref[...], preferred_element_type=F32)
    x1_ref[...] = x1

    xm = _rmsnorm(x1, nmoe_ref[...])
    logits = jnp.dot(xm.astype(BF16), wr_ref[...], preferred_element_type=F32) + br_ref[...]
    lane = lax.broadcasted_iota(jnp.int32, (rows, LANES), 1)
    logits = jnp.where(lane < N_EXPERTS, logits, -jnp.inf)
    vals, idxs = [], []
    for _ in range(TOP_K):
        mx = jnp.max(logits, axis=1, keepdims=True)
        am = jnp.min(jnp.where(logits == mx, lane, LANES), axis=1, keepdims=True)
        vals.append(mx)
        idxs.append(am)
        logits = jnp.where(lane == am, -jnp.inf, logits)
    exps = [jnp.exp(v - vals[0]) for v in vals]
    esum = exps[0] + exps[1] + exps[2] + exps[3]
    idx_w = jnp.zeros((rows, LANES), jnp.int32)
    gate_w = jnp.zeros((rows, LANES), F32)
    for k in range(TOP_K):
        idx_w = jnp.where(lane == k, idxs[k], idx_w)
        gate_w = jnp.where(lane == k, exps[k] / esum, gate_w)
    idx_ref[...] = idx_w[:, :TOP_K]
    gate_ref[...] = gate_w[:, :TOP_K]

    for s in range(TILE_ROWS):
        xm_ref[pl.ds(s, rows, stride=TILE_ROWS), :] = xm[:, s * LANES:(s + 1) * LANES]

    for phase_id, final_step in ((0, n_prompt - 1), (1, pl.num_programs(0) - 1)):
        @pl.when(i == final_step)
        def _(phase_id=phase_id):
            store = pltpu.make_async_copy(s_cx, cx_ref.at[phase_id], sem)
            store.start()
            store.wait()


def _mixer(x_prompt, x_sample, h0r, h0i, cx0, m0, w):
    batch, lp, _ = x_prompt.shape
    assert x_sample.shape[0] == batch and lp % CHUNK == 0 and x_sample.shape[1] % CHUNK == 0
    assert batch % ML_GROUP == 0
    n_prompt = lp // CHUNK
    n_sample = x_sample.shape[1] // CHUNK
    nblk = n_prompt + n_sample
    rows = batch * CHUNK
    t_all = nblk * rows
    bh = batch * ML_HEADS

    def phase(i):
        return jnp.minimum(i // n_prompt, 1)

    state_specs = [
        _const_spec((batch, S5_CH)), _const_spec((batch, S5_CH)),
        pl.BlockSpec(memory_space=pl.ANY),
        _const_spec((bh, LANES)),
    ]
    n_sel = 3 * ML_HEADS
    sel = (jnp.arange(LANES)[:, None] == (jnp.arange(n_sel * LANES) // LANES)[None, :]).astype(BF16)
    weights = (w['norm_mix'], w['w_in_a'], w['w_in_b'], w['w_in_g'], w['gate_bias'], w['abar_re'], w['abar_im'], w['bd_re'],
               w['bd_im'], w['cb'], w['d_skip'], w['w_glu'], w['ml_gn'], w['w_ml_out'], w['w_out'],
               w['norm_moe'], w['w_router'], w['b_router'], sel)
    in_specs = ([pl.BlockSpec((batch, CHUNK, D_MODEL), lambda i: (0, jnp.minimum(i, n_prompt - 1), 0)),
                 pl.BlockSpec((batch, CHUNK, D_MODEL), lambda i: (0, jnp.maximum(i - n_prompt, 0), 0))]
                + state_specs + [_const_spec(a.shape) for a in weights])
    out_shape = (
        jax.ShapeDtypeStruct((t_all, D_MODEL), F32),
        jax.ShapeDtypeStruct((t_all * TILE_ROWS, LANES), F32),
        jax.ShapeDtypeStruct((t_all, TOP_K), jnp.int32),
        jax.ShapeDtypeStruct((t_all, TOP_K), F32),
        jax.ShapeDtypeStruct((2, batch, S5_CH), F32),
        jax.ShapeDtypeStruct((2, batch, S5_CH), F32),
        jax.ShapeDtypeStruct((2, bh, ML_HEAD_DIM, 2 * ML_HEAD_DIM), F32),
        jax.ShapeDtypeStruct((2, bh, LANES), F32),
    )
    out_specs = (
        pl.BlockSpec((rows, D_MODEL), lambda i: (i, 0)),
        pl.BlockSpec((rows * TILE_ROWS, LANES), lambda i: (i, 0)),
        pl.BlockSpec((rows, TOP_K), lambda i: (i, 0)),
        pl.BlockSpec((rows, TOP_K), lambda i: (i, 0)),
        pl.BlockSpec((1, batch, S5_CH), lambda i: (phase(i), 0, 0)),
        pl.BlockSpec((1, batch, S5_CH), lambda i: (phase(i), 0, 0)),
        pl.BlockSpec(memory_space=pl.ANY),
        pl.BlockSpec((1, bh, LANES), lambda i: (phase(i), 0, 0)),
    )
    scratch = [
        pltpu.VMEM((S5_HALF_TILES, rows, LANES), F32), pltpu.VMEM((S5_HALF_TILES, rows, LANES), F32),
        pltpu.VMEM((rows, D_ML), BF16), pltpu.VMEM((rows, D_ML), BF16), pltpu.VMEM((rows, D_ML), BF16),
        pltpu.VMEM((rows, D_ML), F32),
        pltpu.VMEM((rows, LANES), F32),
        pltpu.VMEM((rows, D_ML), BF16),
        pltpu.VMEM((bh, ML_HEAD_DIM, 2 * ML_HEAD_DIM), F32),
        pltpu.SemaphoreType.DMA(()),
    ]
    return pl.pallas_call(
        functools.partial(_mixer_kernel, batch=batch, n_prompt=n_prompt),
        grid=(nblk,),
        in_specs=in_specs,
        out_specs=out_specs,
        out_shape=out_shape,
        scratch_shapes=scratch,
        compiler_params=pltpu.CompilerParams(dimension_semantics=("arbitrary",),
                                             vmem_limit_bytes=VMEM_LIMIT),
        name="mixer",
    )(x_prompt, x_sample, h0r, h0i, cx0, m0, *weights)


def _plan_kernel(idx_ref, dest_ref, meta_ref, s_cnt, s_run, s_pstart, *, tile, n_blocks_pad):
    phase = pl.program_id(0)
    i = pl.program_id(1)
    lane = lax.broadcasted_iota(jnp.int32, (tile, LANES), 1)
    idx = idx_ref[...]
    onehots = [(lane == idx[:, k:k + 1]).astype(F32) for k in range(TOP_K)]
    mask = onehots[0] + onehots[1] + onehots[2] + onehots[3]

    @pl.when(jnp.logical_and(phase == 0, i == 0))
    def _():
        s_cnt[...] = jnp.zeros_like(s_cnt)

    @pl.when(phase == 0)
    def _():
        s_cnt[...] += jnp.sum(mask, axis=0, keepdims=True)

    @pl.when(jnp.logical_and(phase == 1, i == 0))
    def _():
        cnt = s_cnt[...]
        nblk = jnp.floor((cnt + (MOE_BLOCK - 1)) * (1.0 / MOE_BLOCK))
        lane1 = lax.broadcasted_iota(jnp.int32, (1, LANES), 1)
        pend = nblk
        shift = 1
        while shift < LANES:
            pend = pend + jnp.where(lane1 >= shift, pltpu.roll(pend, shift, axis=1), 0.0)
            shift *= 2
        s_pstart[...] = (pend - nblk) * MOE_BLOCK
        s_run[...] = jnp.zeros_like(s_run)
        blk = lax.broadcasted_iota(jnp.int32, (n_blocks_pad, LANES), 0).astype(F32)
        lane_b = lax.broadcasted_iota(jnp.int32, (n_blocks_pad, LANES), 1)
        is_e = lane_b < N_EXPERTS
        done = jnp.logical_and(is_e, pend <= blk)
        e_of = jnp.minimum(jnp.sum(done.astype(F32), axis=1, keepdims=True), N_EXPERTS - 1.0)
        mine = lane_b.astype(F32) == e_of
        rows_left = jnp.sum(jnp.where(mine, cnt - (blk - (pend - nblk)) * MOE_BLOCK, 0.0),
                            axis=1, keepdims=True)
        valid = jnp.clip(rows_left, 0.0, float(MOE_BLOCK))
        used = jnp.sum(jnp.where(lane1 == N_EXPERTS - 1, pend, 0.0), axis=1, keepdims=True)
        meta = jnp.where(lane_b == 0, e_of, jnp.where(lane_b == 1, valid, used))
        meta_ref[...] = meta.astype(jnp.int32)

    @pl.when(phase == 1)
    def _():
        r = lax.broadcasted_iota(jnp.int32, (tile, tile), 0)
        c = lax.broadcasted_iota(jnp.int32, (tile, tile), 1)
        lower = (c < r).astype(BF16)
        before = jnp.dot(lower, mask.astype(BF16), preferred_element_type=F32)
        base = before + s_run[...] + s_pstart[...]
        dest = jnp.zeros((tile, LANES), F32)
        for k in range(TOP_K):
            dk = jnp.sum(onehots[k] * base, axis=1, keepdims=True)
            dest = jnp.where(lane == k, dk, dest)
        dest_ref[...] = dest[:, :TOP_K].astype(jnp.int32)
        s_run[...] += jnp.sum(mask, axis=0, keepdims=True)


def _plan(idx_all, tile, n_blocks_pad):
    t_all = idx_all.shape[0]
    return pl.pallas_call(
        functools.partial(_plan_kernel, tile=tile, n_blocks_pad=n_blocks_pad),
        grid=(2, t_all // tile),
        in_specs=[pl.BlockSpec((tile, TOP_K), lambda p, i: (i, 0))],
        out_specs=(pl.BlockSpec((tile, TOP_K), lambda p, i: (i * p, 0)),
                   pl.BlockSpec((n_blocks_pad, LANES), lambda p, i: (0, 0))),
        out_shape=(jax.ShapeDtypeStruct((t_all, TOP_K), jnp.int32),
                   jax.ShapeDtypeStruct((n_blocks_pad, LANES), jnp.int32)),
        scratch_shapes=[pltpu.VMEM((1, LANES), F32), pltpu.VMEM((1, LANES), F32),
                        pltpu.VMEM((1, LANES), F32)],
        compiler_params=pltpu.CompilerParams(dimension_semantics=("arbitrary", "arbitrary")),
        name="moe_plan",
    )(idx_all)


def _token_tile(ref, t):
    return ref.at[pl.ds(pl.multiple_of(t * TILE_ROWS, TILE_ROWS), TILE_ROWS)]


def _dispatch_kernel(valid_ref, dest_ref, xm_ref, buf_ref, zeros, sem, zsem, *, tile, n_blocks):
    blk_rows = MOE_BLOCK * TILE_ROWS

    @pl.when(pl.program_id(0) == 0)
    def _():
        zeros[...] = jnp.zeros_like(zeros)

        def block_copy(i):
            return pltpu.make_async_copy(
                zeros, buf_ref.at[pl.ds(pl.multiple_of(i * blk_rows, blk_rows), blk_rows)], zsem)

        def fill(i, carry):
            @pl.when(valid_ref[i] < MOE_BLOCK)
            def _():
                block_copy(i).start()
            return carry

        def drain(i, carry):
            @pl.when(valid_ref[i] < MOE_BLOCK)
            def _():
                block_copy(i).wait()
            return carry

        lax.fori_loop(0, n_blocks, fill, 0)
        lax.fori_loop(0, n_blocks, drain, 0)

    def issue(j, carry):
        src = _token_tile(xm_ref, j)
        for k in range(TOP_K):
            pltpu.make_async_copy(src, _token_tile(buf_ref, dest_ref[j * TOP_K + k]), sem).start(
                priority=k % 2)
        return carry

    lax.fori_loop(0, tile, issue, 0, unroll=4)
    n_rows = tile * TOP_K * TILE_ROWS
    pltpu.make_async_copy(buf_ref.at[pl.ds(0, n_rows)], buf_ref.at[pl.ds(0, n_rows)], sem).wait()


def _dispatch(block_valid, dest_flat, xm_tiles, tile, n_blocks):
    t_all = xm_tiles.shape[0] // TILE_ROWS
    grid_spec = pltpu.PrefetchScalarGridSpec(
        num_scalar_prefetch=1,
        grid=(t_all // tile,),
        in_specs=[pl.BlockSpec((tile * TOP_K,), lambda i, va: (i,), memory_space=pltpu.SMEM),
                  pl.BlockSpec((tile * TILE_ROWS, LANES), lambda i, va: (i, 0))],
        out_specs=pl.BlockSpec(memory_space=pl.ANY),
        scratch_shapes=[pltpu.VMEM((MOE_BLOCK * TILE_ROWS, LANES), F32),
                        pltpu.SemaphoreType.DMA(()), pltpu.SemaphoreType.DMA(())],
    )
    return pl.pallas_call(
        functools.partial(_dispatch_kernel, tile=tile, n_blocks=n_blocks),
        grid_spec=grid_spec,
        out_shape=jax.ShapeDtypeStruct((n_blocks * MOE_BLOCK * TILE_ROWS, LANES), F32),
        compiler_params=pltpu.CompilerParams(dimension_semantics=("arbitrary",)),
        name="moe_dispatch",
    )(block_valid, dest_flat, xm_tiles)


def _expert_kernel(be_ref, used_ref, x_ref, wgu_ref, bgu_ref, wd_ref, bd_ref, o_ref, s_wgu, s_wd):
    i = pl.program_id(0)

    @pl.when(i >= used_ref[0])
    def _():
        o_ref[...] = jnp.zeros_like(o_ref)

    @pl.when(jnp.logical_or(i == 0, be_ref[i] != be_ref[jnp.maximum(i - 1, 0)]))
    def _():
        s_wgu[...] = wgu_ref[0].astype(BF16)
        s_wd[...] = wd_ref[0].astype(BF16)

    @pl.when(i < used_ref[0])
    def _():
        pieces = [x_ref[pl.ds(s, MOE_BLOCK, stride=TILE_ROWS), :] for s in range(TILE_ROWS)]
        x = jnp.concatenate(pieces, axis=1).astype(BF16)
        gu = jnp.dot(x, s_wgu[...], preferred_element_type=F32) + bgu_ref[0]
        g = jnp.minimum(gu[:, :D_FF], SWIGLU_LIMIT)
        up = jnp.clip(gu[:, D_FF:], -SWIGLU_LIMIT, SWIGLU_LIMIT)
        hdn = (up + 1.0) * (g * jax.nn.sigmoid(SWIGLU_ALPHA * g))
        out = jnp.dot(hdn.astype(BF16), s_wd[...], preferred_element_type=F32) + bd_ref[0]
        for s in range(TILE_ROWS):
            o_ref[pl.ds(s, MOE_BLOCK, stride=TILE_ROWS), :] = out[:, s * LANES:(s + 1) * LANES]


def _experts(block_e, n_used, buf, wgu, bgu, wd, bd, n_blocks):
    blk_rows = MOE_BLOCK * TILE_ROWS

    def in_row_map(i, be, nu):
        return (jnp.minimum(i, nu[0] - 1), 0)

    def w_map(i, be, nu):
        return (be[i], 0, 0)

    grid_spec = pltpu.PrefetchScalarGridSpec(
        num_scalar_prefetch=2,
        grid=(n_blocks,),
        in_specs=[pl.BlockSpec((blk_rows, LANES), in_row_map),
                  pl.BlockSpec((1, D_MODEL, 2 * D_FF), w_map),
                  pl.BlockSpec((1, 1, 2 * D_FF), w_map),
                  pl.BlockSpec((1, D_FF, D_MODEL), w_map),
                  pl.BlockSpec((1, 1, D_MODEL), w_map)],
        out_specs=pl.BlockSpec((blk_rows, LANES), lambda i, be, nu: (i, 0)),
        scratch_shapes=[pltpu.VMEM((D_MODEL, 2 * D_FF), BF16), pltpu.VMEM((D_FF, D_MODEL), BF16)],
    )
    return pl.pallas_call(
        _expert_kernel,
        grid_spec=grid_spec,
        out_shape=jax.ShapeDtypeStruct(buf.shape, F32),
        compiler_params=pltpu.CompilerParams(dimension_semantics=("arbitrary",),
                                             vmem_limit_bytes=VMEM_LIMIT),
        name="moe_experts",
    )(block_e, n_used, buf, wgu, bgu, wd, bd)


def _combine_kernel(dest_ref, next_dest_ref, gate_ref, x1_ref, rows_ref, nf_ref, yp_ref, ys_ref,
                    gbuf, sem, *, batch, n_prompt):
    i = pl.program_id(0)
    tile = batch * CHUNK
    tile_rows = tile * TILE_ROWS

    slot = i % 2

    def gather(dref, slot_id):
        def issue(j, carry):
            for k in range(TOP_K):
                pltpu.make_async_copy(_token_tile(rows_ref, dref[j * TOP_K + k]),
                                      _token_tile(gbuf.at[slot_id, k], j),
                                      sem.at[slot_id]).start(priority=k % 2)
            return carry

        lax.fori_loop(0, tile, issue, 0, unroll=4)

    @pl.when(i == 0)
    def _():
        gather(dest_ref, 0)

    @pl.when(i + 1 < pl.num_programs(0))
    def _():
        gather(next_dest_ref, 1 - slot)

    for k in range(TOP_K):
        pltpu.make_async_copy(rows_ref.at[pl.ds(0, tile_rows)], gbuf.at[slot, k], sem.at[slot]).wait()

    gates = gate_ref[...]
    pieces = []
    for s in range(TILE_ROWS):
        piece = x1_ref[:, s * LANES:(s + 1) * LANES]
        for k in range(TOP_K):
            piece = piece + gates[:, k:k + 1] * gbuf[slot, k, pl.ds(s, tile, stride=TILE_ROWS), :]
        pieces.append(piece)
    y = _rmsnorm(jnp.concatenate(pieces, axis=1), nf_ref[...]).reshape(batch, CHUNK, D_MODEL)

    @pl.when(i < n_prompt)
    def _():
        yp_ref[...] = y

    @pl.when(i >= n_prompt)
    def _():
        ys_ref[...] = y


def _combine(dest_flat, gates, x1, out_rows, norm_final, batch, lp, ls):
    tile = batch * CHUNK
    n_prompt = lp // CHUNK
    n_sample = ls // CHUNK
    last = n_prompt + n_sample - 1
    return pl.pallas_call(
        functools.partial(_combine_kernel, batch=batch, n_prompt=n_prompt),
        grid=(n_prompt + n_sample,),
        in_specs=[pl.BlockSpec((tile * TOP_K,), lambda i: (i,), memory_space=pltpu.SMEM),
                  pl.BlockSpec((tile * TOP_K,), lambda i: (jnp.minimum(i + 1, last),),
                               memory_space=pltpu.SMEM),
                  pl.BlockSpec((tile, TOP_K), lambda i: (i, 0)),
                  pl.BlockSpec((tile, D_MODEL), lambda i: (i, 0)),
                  pl.BlockSpec(memory_space=pl.ANY),
                  pl.BlockSpec((1, D_MODEL), lambda i: (0, 0))],
        out_specs=(pl.BlockSpec((batch, CHUNK, D_MODEL), lambda i: (0, jnp.minimum(i, n_prompt - 1), 0)),
                   pl.BlockSpec((batch, CHUNK, D_MODEL), lambda i: (0, jnp.maximum(i - n_prompt, 0), 0))),
        out_shape=(jax.ShapeDtypeStruct((batch, lp, D_MODEL), F32),
                   jax.ShapeDtypeStruct((batch, ls, D_MODEL), F32)),
        scratch_shapes=[pltpu.VMEM((2, TOP_K, tile * TILE_ROWS, LANES), F32),
                        pltpu.SemaphoreType.DMA((2,))],
        compiler_params=pltpu.CompilerParams(dimension_semantics=("arbitrary",),
                                             vmem_limit_bytes=VMEM_LIMIT),
        name="moe_combine",
    )(dest_flat, dest_flat, gates, x1, out_rows, norm_final)


def _prep_weights(norm_mix, w_in, b_ig, b_fg, s5_a_re, s5_a_im, s5_log_dt, s5_b_re, s5_b_im,
                  s5_c_re, s5_c_im, s5_d, w_s5_glu, ml_gn, w_ml_out, w_out, norm_moe, w_router,
                  b_router):
    w = w_in[0]
    o_ig = D_S5 + 4 * D_ML
    o_gs5 = o_ig + 2 * ML_HEADS
    w_a = w[:, :o_ig].astype(BF16)
    w_b = w[:, o_gs5:].astype(BF16)
    w_g = jnp.concatenate([w[:, o_ig:o_gs5], jnp.zeros((D_MODEL, LANES - 2 * ML_HEADS), F32)],
                          axis=1).astype(BF16)
    gate_bias = jnp.concatenate([b_ig[0], b_fg[0], jnp.zeros((LANES - 2 * ML_HEADS,), F32)])[None]

    ar, ai = s5_a_re[0], s5_a_im[0]
    dt = jnp.exp(s5_log_dt[0])[:, None]
    mag = jnp.exp(dt * ar)
    abar_re = mag * jnp.cos(dt * ai)
    abar_im = mag * jnp.sin(dt * ai)
    den = ar * ar + ai * ai
    fr = ((abar_re - 1.0) * ar + abar_im * ai) / den
    fi = (abar_im * ar - (abar_re - 1.0) * ai) / den
    br, bi = s5_b_re[0], s5_b_im[0]
    bbar_re = fr[..., None] * br - fi[..., None] * bi
    bbar_im = fr[..., None] * bi + fi[..., None] * br
    gh = S5_GROUPS // 2
    same_group = (jnp.arange(gh)[:, None, None, None] == jnp.arange(gh)[None, None, :, None])

    def blockdiag_in(bb):
        t = bb.reshape(2, gh, S5_STATE, S5_GROUP).transpose(0, 1, 3, 2)
        full = jnp.where(same_group[None], t[:, :, :, None, :], 0.0)
        return full.reshape(2, gh * S5_GROUP, gh * S5_STATE).astype(BF16)

    def blockdiag_out(cc):
        t = cc.reshape(2, gh, S5_GROUP, S5_STATE).transpose(0, 1, 3, 2)
        full = jnp.where(same_group[None], t[:, :, :, None, :], 0.0)
        return full.reshape(2, gh * S5_STATE, gh * S5_GROUP)

    cb = jnp.concatenate([blockdiag_out(s5_c_re[0]), -blockdiag_out(s5_c_im[0])], axis=1).astype(BF16)
    w_router_p = jnp.concatenate(
        [w_router[0], jnp.zeros((D_MODEL, LANES - N_EXPERTS), F32)], axis=1).astype(BF16)
    b_router_p = jnp.concatenate([b_router[0], jnp.zeros((LANES - N_EXPERTS,), F32)])[None]
    return dict(
        norm_mix=norm_mix[0][None], w_in_a=w_a, w_in_b=w_b, w_in_g=w_g, gate_bias=gate_bias,
        abar_re=abar_re.reshape(1, S5_CH), abar_im=abar_im.reshape(1, S5_CH),
        bd_re=blockdiag_in(bbar_re), bd_im=blockdiag_in(bbar_im), cb=cb,
        d_skip=s5_d[0][None], w_glu=w_s5_glu[0].astype(BF16), ml_gn=ml_gn[0][None],
        w_ml_out=w_ml_out[0].astype(BF16), w_out=w_out[0].astype(BF16),
        norm_moe=norm_moe[0][None], w_router=w_router_p, b_router=b_router_p)


def kernel(x_prompt, x_sample, state_s5_re, state_s5_im, state_ml_C, state_ml_n, state_ml_m, norm_mix, w_in, b_ig, b_fg, s5_a_re, s5_a_im, s5_log_dt, s5_b_re, s5_b_im, s5_c_re, s5_c_im, s5_d, w_s5_glu, ml_gn, w_ml_out, w_out, norm_moe, w_router, b_router, w_gate_up, b_gate_up, w_down, b_down, norm_final):
    w = _prep_weights(norm_mix, w_in, b_ig, b_fg, s5_a_re, s5_a_im, s5_log_dt, s5_b_re, s5_b_im,
                      s5_c_re, s5_c_im, s5_d, w_s5_glu, ml_gn, w_ml_out, w_out, norm_moe,
                      w_router, b_router)
    batch, lp, _ = x_prompt.shape
    ls = x_sample.shape[1]
    bh = batch * ML_HEADS
    cx0 = jnp.concatenate(
        [jnp.swapaxes(state_ml_C[0].reshape(bh, ML_HEAD_DIM, ML_HEAD_DIM), 1, 2),
         jnp.broadcast_to(state_ml_n[0].reshape(bh, ML_HEAD_DIM, 1), (bh, ML_HEAD_DIM, ML_HEAD_DIM))],
        axis=2)
    x1, xm, idx, gates, hr, hi, cx, m = _mixer(
        x_prompt, x_sample,
        state_s5_re[0].reshape(batch, S5_CH), state_s5_im[0].reshape(batch, S5_CH), cx0,
        jnp.broadcast_to(state_ml_m[0].reshape(bh, 1), (bh, LANES)), w)
    c = jnp.swapaxes(cx[:, :, :, :ML_HEAD_DIM], 2, 3)
    n = cx[:, :, :, ML_HEAD_DIM]

    tile = batch * CHUNK
    t_all = idx.shape[0]
    n_blocks = -(-(t_all * TOP_K + N_EXPERTS * (MOE_BLOCK - 1)) // MOE_BLOCK)
    n_blocks_pad = -(-n_blocks // SUBLANES) * SUBLANES
    dest, meta = _plan(idx, tile, n_blocks_pad)
    dest_flat = dest.reshape(t_all * TOP_K)
    buf = _dispatch(meta[:n_blocks, 1], dest_flat, xm, tile, n_blocks)
    out_rows = _experts(meta[:n_blocks, 0], meta[:1, 2], buf,
                        w_gate_up[0], b_gate_up[0][:, None, :], w_down[0], b_down[0][:, None, :],
                        n_blocks)
    y_prompt, y_sample = _combine(dest_flat, gates, x1, out_rows, norm_final[None], batch, lp, ls)

    def states(p):
        return (hr[p].reshape(1, batch, S5_GROUPS, S5_STATE), hi[p].reshape(1, batch, S5_GROUPS, S5_STATE),
                c[p].reshape(1, batch, ML_HEADS, ML_HEAD_DIM, ML_HEAD_DIM),
                n[p].reshape(1, batch, ML_HEADS, ML_HEAD_DIM), m[p, :, 0].reshape(1, batch, ML_HEADS))

    return (y_prompt, y_sample) + states(0) + states(1)
```

```python
import functools

import jax
import jax.numpy as jnp
from jax import lax
from jax.experimental import pallas as pl
from jax.experimental.pallas import tpu as pltpu

F32 = jnp.float32
BF16 = jnp.bfloat16

D_MODEL = 1024
CHUNK = 64
EPS = 1e-6
D_S5 = 512
S5_GROUP = 16
S5_GROUPS = D_S5 // S5_GROUP
S5_STATE = 64
S5_CH = S5_GROUPS * S5_STATE
ML_HEADS = 4
ML_HEAD_DIM = 128
D_ML = ML_HEADS * ML_HEAD_DIM
N_EXPERTS = 32
TOP_K = 4
D_FF = 1024
SWIGLU_LIMIT = 7.0
SWIGLU_ALPHA = 1.702

LANES = 128
SUBLANES = 8
TILE_ROWS = D_MODEL // LANES

OFF_U = 0
OFF_Q = OFF_U + D_S5
OFF_K = OFF_Q + D_ML
OFF_V = OFF_K + D_ML
OFF_O = OFF_V + D_ML
OFF_GS5 = OFF_O + D_ML
OFF_GML = OFF_GS5 + D_MODEL
OFF_GATE = OFF_GML + D_MODEL
W_COLS = OFF_GATE + LANES

S5_HALF_IN = D_S5 // 2
S5_HALF_CH = S5_CH // 2
S5_HALF_TILES = S5_HALF_CH // LANES
S5_SCAN_TILES = 4
ML_GROUP = 2
MOE_BLOCK = 512
VMEM_LIMIT = 60 * 1024 * 1024


def _rmsnorm(x, w):
    return x * lax.rsqrt(jnp.mean(x * x, axis=-1, keepdims=True) + EPS) * w


def _const_spec(shape):
    nd = len(shape)
    return pl.BlockSpec(shape, lambda *_: (0,) * nd, pipeline_mode=pl.Buffered(1))


def _mixer_kernel(xp_ref, xs_ref, h0r_ref, h0i_ref, cx0_ref, m0_ref,
                  nmix_ref, wina_ref, winb_ref, wing_ref, gbias_ref, abr_ref, abi_ref, bdr_ref, bdi_ref, cb_ref,
                  dskip_ref, glu_ref, gn_ref, wml_ref, wout_ref, nmoe_ref, wr_ref, br_ref,
                  x1_ref, xm_ref, idx_ref, gate_ref, hr_ref, hi_ref, cx_ref, m_ref,
                  s_bur, s_bui, s_q, s_k, s_v, s_o, s_col, s_gated, s_cx, sem, *, batch, n_prompt):
    i = pl.program_id(0)
    rows = batch * CHUNK

    @pl.when(i == 0)
    def _():
        hr_ref[...] = jnp.zeros_like(hr_ref)
        hi_ref[...] = jnp.zeros_like(hi_ref)
        m_ref[...] = jnp.zeros_like(m_ref)
        s_cx[...] = jnp.zeros_like(s_cx)

    @pl.when(i == n_prompt)
    def _():
        hr_ref[0] = h0r_ref[...]
        hi_ref[0] = h0i_ref[...]
        m_ref[0] = m0_ref[...]
        load = pltpu.make_async_copy(cx0_ref, s_cx, sem)
        load.start()
        load.wait()

    @pl.when(i < n_prompt)
    def _():
        x1_ref[...] = xp_ref[...].reshape(rows, D_MODEL)

    @pl.when(i >= n_prompt)
    def _():
        x1_ref[...] = xs_ref[...].reshape(rows, D_MODEL)

    xn = _rmsnorm(x1_ref[...], nmix_ref[...]).astype(BF16)

    def proj(off, width):
        if off >= OFF_GATE:
            w_cols = wing_ref[...]
        elif off >= OFF_GS5:
            w_cols = winb_ref[:, off - OFF_GS5:off - OFF_GS5 + width]
        else:
            w_cols = wina_ref[:, off:off + width]
        return jnp.dot(xn, w_cols, preferred_element_type=F32)

    s_q[...] = (proj(OFF_Q, D_ML) * (ML_HEAD_DIM ** -0.5)).astype(BF16)
    s_k[...] = proj(OFF_K, D_ML).astype(BF16)
    s_v[...] = proj(OFF_V, D_ML).astype(BF16)
    s_o[...] = proj(OFF_O, D_ML)
    gates = proj(OFF_GATE, LANES) + gbias_ref[...]
    lane_g = lax.broadcasted_iota(jnp.int32, (rows, LANES), 1)
    gg = jnp.where(lane_g < ML_HEADS, gates, jax.nn.log_sigmoid(gates))
    gt8 = gg.T[0:SUBLANES, :]
    pos = lax.broadcasted_iota(jnp.int32, (SUBLANES, rows), 1) % CHUNK
    cum = gt8
    shift = 1
    while shift < CHUNK:
        cum = cum + jnp.where(pos >= shift, pltpu.roll(cum, shift, axis=1), 0.0)
        shift *= 2
    g8 = gt8 - pltpu.roll(cum, ML_HEADS, axis=0)
    mx8 = g8
    shift = 1
    while shift < CHUNK:
        mx8 = jnp.maximum(mx8, jnp.where(pos >= shift, pltpu.roll(mx8, shift, axis=1), -jnp.inf))
        shift *= 2
    sub = lax.broadcasted_iota(jnp.int32, (SUBLANES, rows), 0)
    top8 = jnp.where(sub < ML_HEADS, g8, cum)
    s_col[...] = jnp.concatenate(
        [top8, mx8, jnp.zeros((LANES - 2 * SUBLANES, rows), F32)], axis=0).T

    tri = (lax.broadcasted_iota(jnp.int32, (CHUNK, CHUNK), 0)
           >= lax.broadcasted_iota(jnp.int32, (CHUNK, CHUNK), 1))[None]
    bdims = ((0,), (0,))

    def rowsl(b):
        return slice(b * CHUNK, (b + 1) * CHUNK)

    def headsl(h):
        return slice(h * ML_HEAD_DIM, (h + 1) * ML_HEAD_DIM)

    for g0 in range(0, batch, ML_GROUP):
        items = [(b, h) for b in range(g0, g0 + ML_GROUP) for h in range(ML_HEADS)]
        def col(j, items=items):
            return jnp.stack([jnp.broadcast_to(s_col[rowsl(b), j + h:j + h + 1], (CHUNK, LANES))
                              for b, h in items])

        g_c, b_c, mx_c = col(0), col(ML_HEADS), col(2 * ML_HEADS)
        g_r = jnp.stack([g8[h:h + 1, rowsl(b)] for b, h in items])
        m_prev = jnp.stack([m_ref[0, b * ML_HEADS + h:b * ML_HEADS + h + 1, :] for b, h in items])
        q3 = jnp.stack([s_q[rowsl(b), headsl(h)] for b, h in items])
        k3 = jnp.stack([s_k[rowsl(b), headsl(h)] for b, h in items])
        v3 = jnp.stack([s_v[rowsl(b), headsl(h)] for b, h in items])
        cx = jnp.stack([s_cx[b * ML_HEADS + h] for b, h in items])

        big_m = jnp.maximum(m_prev, mx_c)
        p = jnp.exp(jnp.where(tri, g_r - big_m[:, :, :CHUNK], -jnp.inf))
        w_inter = jnp.exp(m_prev - big_m)
        s = lax.dot_general(q3, k3, (((2,), (2,)), bdims), preferred_element_type=F32) * p
        cqx = lax.dot_general(q3, cx.astype(BF16), (((2,), (1,)), bdims), preferred_element_type=F32)
        num = (lax.dot_general(s.astype(BF16), v3, (((2,), (1,)), bdims), preferred_element_type=F32)
               + w_inter * cqx[:, :, :ML_HEAD_DIM])
        den_dot = jnp.sum(s, axis=2, keepdims=True) + w_inter * cqx[:, :, ML_HEAD_DIM:]
        m_t = b_c + big_m
        hout = num / jnp.maximum(jnp.abs(den_dot), jnp.exp(-m_t))

        m_last = big_m[:, CHUNK - 1:CHUNK, :]
        w_end = jnp.exp(g_c - m_last)
        decay = jnp.exp(m_prev - m_last)
        wvx = jnp.concatenate([w_end * v3.astype(F32), w_end], axis=2).astype(BF16)
        k_t = jnp.stack([s_k[rowsl(b), headsl(h)].astype(F32).T.astype(BF16) for b, h in items])
        cx_new = (jnp.concatenate([decay, decay], axis=2) * cx
                  + lax.dot_general(k_t, wvx, (((2,), (1,)), bdims), preferred_element_type=F32))
        m_new = m_t[:, CHUNK - 1:CHUNK, :]

        hc = hout - jnp.mean(hout, axis=2, keepdims=True)
        hn = hc * lax.rsqrt(jnp.mean(hc * hc, axis=2, keepdims=True) + EPS)
        for n, (b, h) in enumerate(items):
            bh = b * ML_HEADS + h
            s_cx[bh] = cx_new[n]
            m_ref[0, bh:bh + 1, :] = m_new[n]
            s_gated[rowsl(b), headsl(h)] = (jax.nn.sigmoid(s_o[rowsl(b), headsl(h)])
                                            * (hn[n] * gn_ref[:, headsl(h)])).astype(BF16)

    mix = jax.nn.sigmoid(proj(OFF_GML, D_MODEL)) * jnp.dot(
        s_gated[...], wml_ref[...], preferred_element_type=F32)

    u = proj(OFF_U, D_S5)
    ub = u.astype(BF16)
    ys = []
    for k in range(2):
        uk = ub[:, k * S5_HALF_IN:(k + 1) * S5_HALF_IN]
        bur = jnp.dot(uk, bdr_ref[k], preferred_element_type=F32)
        bui = jnp.dot(uk, bdi_ref[k], preferred_element_type=F32)
        for j in range(S5_HALF_TILES):
            for b in range(batch):
                dst = pl.ds(b, CHUNK, stride=batch)
                src = slice(b * CHUNK, (b + 1) * CHUNK)
                s_bur[j, dst, :] = bur[src, j * LANES:(j + 1) * LANES]
                s_bui[j, dst, :] = bui[src, j * LANES:(j + 1) * LANES]

        for c in range(S5_HALF_TILES // S5_SCAN_TILES):
            tiles = range(c * S5_SCAN_TILES, (c + 1) * S5_SCAN_TILES)
            lanes = [slice(k * S5_HALF_CH + j * LANES, k * S5_HALF_CH + (j + 1) * LANES) for j in tiles]
            ar = [jnp.broadcast_to(abr_ref[:, ls], (batch, LANES)) for ls in lanes]
            ai = [jnp.broadcast_to(abi_ref[:, ls], (batch, LANES)) for ls in lanes]

            def step(t, carry, tiles=tiles, ar=ar, ai=ai):
                slab = pl.ds(pl.multiple_of(t * batch, batch), batch)
                out = []
                for n, j in enumerate(tiles):
                    hr, hi = carry[n]
                    nr = ar[n] * hr - ai[n] * hi + s_bur[j, slab, :]
                    ni = ar[n] * hi + ai[n] * hr + s_bui[j, slab, :]
                    s_bur[j, slab, :] = nr
                    s_bui[j, slab, :] = ni
                    out.append((nr, ni))
                return tuple(out)

            init = tuple((hr_ref[0, :, ls], hi_ref[0, :, ls]) for ls in lanes)
            fin = lax.fori_loop(0, CHUNK, step, init, unroll=True)
            for n, ls in enumerate(lanes):
                hr_ref[0, :, ls] = fin[n][0]
                hi_ref[0, :, ls] = fin[n][1]

        def stream_major(ref):
            return jnp.concatenate(
                [jnp.concatenate([ref[j, pl.ds(b, CHUNK, stride=batch), :]
                                  for j in range(S5_HALF_TILES)], axis=1) for b in range(batch)], axis=0)

        yk = jnp.dot(stream_major(s_bur).astype(BF16), cb_ref[k, :S5_HALF_CH, :],
                     preferred_element_type=F32)
        yk = yk + jnp.dot(stream_major(s_bui).astype(BF16), cb_ref[k, S5_HALF_CH:, :],
                          preferred_element_type=F32)
        ys.append(yk)
    y = jnp.concatenate(ys, axis=1) + dskip_ref[...] * u
    glu = jnp.dot(jax.nn.gelu(y).astype(BF16), glu_ref[...], preferred_element_type=F32)
    y_s5 = glu[:, :D_MODEL] * jax.nn.sigmoid(glu[:, D_MODEL:])
    mix = mix + jax.nn.sigmoid(proj(OFF_GS5, D_MODEL)) * y_s5

    x1 = x1_ref[...] + jnp.dot(mix.astype(BF16), wout_ref[...], preferred_element_type=F32)
    x1_ref[...] = x1

    xm = _rmsnorm(x1, nmoe_ref[...])
    logits = jnp.dot(xm.astype(BF16), wr_ref[...], preferred_element_type=F32) + br_ref[...]
    lane = lax.broadcasted_iota(jnp.int32, (rows, LANES), 1)
    logits = jnp.where(lane < N_EXPERTS, logits, -jnp.inf)
    vals, idxs = [], []
    for _ in range(TOP_K):
        mx = jnp.max(logits, axis=1, keepdims=True)
        am = jnp.min(jnp.where(logits == mx, lane, LANES), axis=1, keepdims=True)
        vals.append(mx)
        idxs.append(am)
        logits = jnp.where(lane == am, -jnp.inf, logits)
    exps = [jnp.exp(v - vals[0]) for v in vals]
    esum = exps[0] + exps[1] + exps[2] + exps[3]
    idx_w = jnp.zeros((rows, LANES), jnp.int32)
    gate_w = jnp.zeros((rows, LANES), F32)
    for k in range(TOP_K):
        idx_w = jnp.where(lane == k, idxs[k], idx_w)
        gate_w = jnp.where(lane == k, exps[k] / esum, gate_w)
    idx_ref[...] = idx_w[:, :TOP_K]
    gate_ref[...] = gate_w[:, :TOP_K]

    for s in range(TILE_ROWS):
        xm_ref[pl.ds(s, rows, stride=TILE_ROWS), :] = xm[:, s * LANES:(s + 1) * LANES]

    for phase_id, final_step in ((0, n_prompt - 1), (1, pl.num_programs(0) - 1)):
        @pl.when(i == final_step)
        def _(phase_id=phase_id):
            store = pltpu.make_async_copy(s_cx, cx_ref.at[phase_id], sem)
            store.start()
            store.wait()


def _mixer(x_prompt, x_sample, h0r, h0i, cx0, m0, w):
    batch, lp, _ = x_prompt.shape
    assert x_sample.shape[0] == batch and lp % CHUNK == 0 and x_sample.shape[1] % CHUNK == 0
    assert batch % ML_GROUP == 0
    n_prompt = lp // CHUNK
    n_sample = x_sample.shape[1] // CHUNK
    nblk = n_prompt + n_sample
    rows = batch * CHUNK
    t_all = nblk * rows
    bh = batch * ML_HEADS

    def phase(i):
        return jnp.minimum(i // n_prompt, 1)

    state_specs = [
        _const_spec((batch, S5_CH)), _const_spec((batch, S5_CH)),
        pl.BlockSpec(memory_space=pl.ANY),
        _const_spec((bh, LANES)),
    ]
    weights = (w['norm_mix'], w['w_in_a'], w['w_in_b'], w['w_in_g'], w['gate_bias'], w['abar_re'], w['abar_im'], w['bd_re'],
               w['bd_im'], w['cb'], w['d_skip'], w['w_glu'], w['ml_gn'], w['w_ml_out'], w['w_out'],
               w['norm_moe'], w['w_router'], w['b_router'])
    in_specs = ([pl.BlockSpec((batch, CHUNK, D_MODEL), lambda i: (0, jnp.minimum(i, n_prompt - 1), 0)),
                 pl.BlockSpec((batch, CHUNK, D_MODEL), lambda i: (0, jnp.maximum(i - n_prompt, 0), 0))]
                + state_specs + [_const_spec(a.shape) for a in weights])
    out_shape = (
        jax.ShapeDtypeStruct((t_all, D_MODEL), F32),
        jax.ShapeDtypeStruct((t_all * TILE_ROWS, LANES), F32),
        jax.ShapeDtypeStruct((t_all, TOP_K), jnp.int32),
        jax.ShapeDtypeStruct((t_all, TOP_K), F32),
        jax.ShapeDtypeStruct((2, batch, S5_CH), F32),
        jax.ShapeDtypeStruct((2, batch, S5_CH), F32),
        jax.ShapeDtypeStruct((2, bh, ML_HEAD_DIM, 2 * ML_HEAD_DIM), F32),
        jax.ShapeDtypeStruct((2, bh, LANES), F32),
    )
    out_specs = (
        pl.BlockSpec((rows, D_MODEL), lambda i: (i, 0)),
        pl.BlockSpec((rows * TILE_ROWS, LANES), lambda i: (i, 0)),
        pl.BlockSpec((rows, TOP_K), lambda i: (i, 0)),
        pl.BlockSpec((rows, TOP_K), lambda i: (i, 0)),
        pl.BlockSpec((1, batch, S5_CH), lambda i: (phase(i), 0, 0)),
        pl.BlockSpec((1, batch, S5_CH), lambda i: (phase(i), 0, 0)),
        pl.BlockSpec(memory_space=pl.ANY),
        pl.BlockSpec((1, bh, LANES), lambda i: (phase(i), 0, 0)),
    )
    scratch = [
        pltpu.VMEM((S5_HALF_TILES, rows, LANES), F32), pltpu.VMEM((S5_HALF_TILES, rows, LANES), F32),
        pltpu.VMEM((rows, D_ML), BF16), pltpu.VMEM((rows, D_ML), BF16), pltpu.VMEM((rows, D_ML), BF16),
        pltpu.VMEM((rows, D_ML), F32),
        pltpu.VMEM((rows, LANES), F32),
        pltpu.VMEM((rows, D_ML), BF16),
        pltpu.VMEM((bh, ML_HEAD_DIM, 2 * ML_HEAD_DIM), F32),
        pltpu.SemaphoreType.DMA(()),
    ]
    return pl.pallas_call(
        functools.partial(_mixer_kernel, batch=batch, n_prompt=n_prompt),
        grid=(nblk,),
        in_specs=in_specs,
        out_specs=out_specs,
        out_shape=out_shape,
        scratch_shapes=scratch,
        compiler_params=pltpu.CompilerParams(dimension_semantics=("arbitrary",),
                                             vmem_limit_bytes=VMEM_LIMIT),
        name="mixer",
    )(x_prompt, x_sample, h0r, h0i, cx0, m0, *weights)


def _plan_kernel(idx_ref, dest_ref, meta_ref, s_cnt, s_run, s_pstart, *, tile, n_blocks_pad):
    phase = pl.program_id(0)
    i = pl.program_id(1)
    lane = lax.broadcasted_iota(jnp.int32, (tile, LANES), 1)
    idx = idx_ref[...]
    onehots = [(lane == idx[:, k:k + 1]).astype(F32) for k in range(TOP_K)]
    mask = onehots[0] + onehots[1] + onehots[2] + onehots[3]

    @pl.when(jnp.logical_and(phase == 0, i == 0))
    def _():
        s_cnt[...] = jnp.zeros_like(s_cnt)

    @pl.when(phase == 0)
    def _():
        s_cnt[...] += jnp.sum(mask, axis=0, keepdims=True)

    @pl.when(jnp.logical_and(phase == 1, i == 0))
    def _():
        cnt = s_cnt[...]
        nblk = jnp.floor((cnt + (MOE_BLOCK - 1)) * (1.0 / MOE_BLOCK))
        lane1 = lax.broadcasted_iota(jnp.int32, (1, LANES), 1)
        pend = nblk
        shift = 1
        while shift < LANES:
            pend = pend + jnp.where(lane1 >= shift, pltpu.roll(pend, shift, axis=1), 0.0)
            shift *= 2
        s_pstart[...] = (pend - nblk) * MOE_BLOCK
        s_run[...] = jnp.zeros_like(s_run)
        blk = lax.broadcasted_iota(jnp.int32, (n_blocks_pad, LANES), 0).astype(F32)
        lane_b = lax.broadcasted_iota(jnp.int32, (n_blocks_pad, LANES), 1)
        lane_f = lane_b.astype(F32)
        is_e = lane_b < N_EXPERTS
        done = jnp.logical_and(is_e, pend <= blk)
        e_of = jnp.minimum(jnp.sum(done.astype(F32), axis=1, keepdims=True), N_EXPERTS - 1.0)
        mine = lane_f == e_of
        blk_in_e = jnp.sum(jnp.where(mine, blk - (pend - nblk), 0.0), axis=1, keepdims=True)
        cnt_e = jnp.sum(jnp.where(mine, cnt, 0.0), axis=1, keepdims=True)
        valid = jnp.clip(cnt_e - blk_in_e * MOE_BLOCK, 0.0, float(MOE_BLOCK))
        used = jnp.sum(jnp.where(lane1 == N_EXPERTS - 1, pend, 0.0), axis=1, keepdims=True)
        owns = jnp.logical_and(is_e, nblk > 0.0)
        first = jnp.logical_and(blk_in_e == 0.0, blk[:, :1] < used).astype(F32)
        later = jnp.logical_and(owns, lane_f > e_of)
        nxt = jnp.min(jnp.where(later, lane_f, float(LANES)), axis=1, keepdims=True)
        nxt = jnp.where(nxt < float(LANES), nxt, -1.0)
        run = jnp.sum(jnp.logical_and(owns, lane_f < e_of).astype(F32), axis=1, keepdims=True)
        parity = run - 2.0 * jnp.floor(run * 0.5)
        meta = jnp.zeros((n_blocks_pad, LANES), F32)
        for col, val in enumerate((e_of, valid, used, first, nxt, parity)):
            meta = jnp.where(lane_b == col, val, meta)
        meta_ref[...] = meta.astype(jnp.int32)

    @pl.when(phase == 1)
    def _():
        r = lax.broadcasted_iota(jnp.int32, (tile, tile), 0)
        c = lax.broadcasted_iota(jnp.int32, (tile, tile), 1)
        lower = (c < r).astype(BF16)
        before = jnp.dot(lower, mask.astype(BF16), preferred_element_type=F32)
        base = before + s_run[...] + s_pstart[...]
        dest = jnp.zeros((tile, LANES), F32)
        for k in range(TOP_K):
            dk = jnp.sum(onehots[k] * base, axis=1, keepdims=True)
            dest = jnp.where(lane == k, dk, dest)
        dest_ref[...] = dest[:, :TOP_K].astype(jnp.int32)
        s_run[...] += jnp.sum(mask, axis=0, keepdims=True)


def _plan(idx_all, tile, n_blocks_pad):
    t_all = idx_all.shape[0]
    return pl.pallas_call(
        functools.partial(_plan_kernel, tile=tile, n_blocks_pad=n_blocks_pad),
        grid=(2, t_all // tile),
        in_specs=[pl.BlockSpec((tile, TOP_K), lambda p, i: (i, 0))],
        out_specs=(pl.BlockSpec((tile, TOP_K), lambda p, i: (i * p, 0)),
                   pl.BlockSpec((n_blocks_pad, LANES), lambda p, i: (0, 0))),
        out_shape=(jax.ShapeDtypeStruct((t_all, TOP_K), jnp.int32),
                   jax.ShapeDtypeStruct((n_blocks_pad, LANES), jnp.int32)),
        scratch_shapes=[pltpu.VMEM((1, LANES), F32), pltpu.VMEM((1, LANES), F32),
                        pltpu.VMEM((1, LANES), F32)],
        compiler_params=pltpu.CompilerParams(dimension_semantics=("arbitrary", "arbitrary")),
        name="moe_plan",
    )(idx_all)


def _token_tile(ref, t):
    return ref.at[pl.ds(pl.multiple_of(t * TILE_ROWS, TILE_ROWS), TILE_ROWS)]


def _dispatch_kernel(valid_ref, dest_ref, xm_ref, buf_ref, zeros, sem, zsem, *, tile, n_blocks):
    blk_rows = MOE_BLOCK * TILE_ROWS

    @pl.when(pl.program_id(0) == 0)
    def _():
        zeros[...] = jnp.zeros_like(zeros)

        def block_copy(i):
            return pltpu.make_async_copy(
                zeros, buf_ref.at[pl.ds(pl.multiple_of(i * blk_rows, blk_rows), blk_rows)], zsem)

        def fill(i, carry):
            @pl.when(valid_ref[i] < MOE_BLOCK)
            def _():
                block_copy(i).start()
            return carry

        def drain(i, carry):
            @pl.when(valid_ref[i] < MOE_BLOCK)
            def _():
                block_copy(i).wait()
            return carry

        lax.fori_loop(0, n_blocks, fill, 0)
        lax.fori_loop(0, n_blocks, drain, 0)

    def issue(j, carry):
        src = _token_tile(xm_ref, j)
        for k in range(TOP_K):
            pltpu.make_async_copy(src, _token_tile(buf_ref, dest_ref[j * TOP_K + k]), sem).start(
                priority=k % 2)
        return carry

    lax.fori_loop(0, tile, issue, 0, unroll=4)
    n_rows = tile * TOP_K * TILE_ROWS
    pltpu.make_async_copy(buf_ref.at[pl.ds(0, n_rows)], buf_ref.at[pl.ds(0, n_rows)], sem).wait()


def _dispatch(block_valid, dest_flat, xm_tiles, tile, n_blocks):
    t_all = xm_tiles.shape[0] // TILE_ROWS
    grid_spec = pltpu.PrefetchScalarGridSpec(
        num_scalar_prefetch=1,
        grid=(t_all // tile,),
        in_specs=[pl.BlockSpec((tile * TOP_K,), lambda i, va: (i,), memory_space=pltpu.SMEM),
                  pl.BlockSpec((tile * TILE_ROWS, LANES), lambda i, va: (i, 0))],
        out_specs=pl.BlockSpec(memory_space=pl.ANY),
        scratch_shapes=[pltpu.VMEM((MOE_BLOCK * TILE_ROWS, LANES), F32),
                        pltpu.SemaphoreType.DMA(()), pltpu.SemaphoreType.DMA(())],
    )
    return pl.pallas_call(
        functools.partial(_dispatch_kernel, tile=tile, n_blocks=n_blocks),
        grid_spec=grid_spec,
        out_shape=jax.ShapeDtypeStruct((n_blocks * MOE_BLOCK * TILE_ROWS, LANES), F32),
        compiler_params=pltpu.CompilerParams(dimension_semantics=("arbitrary",)),
        name="moe_dispatch",
    )(block_valid, dest_flat, xm_tiles)


def _expert_kernel(be_ref, used_ref, first_ref, next_ref, slot_ref, x_ref, wgu_ref, bgu_ref, wd_ref,
                   bd_ref, o_ref, wbuf_gu, wbuf_d, s_wgu, s_wd, sem):
    i = pl.program_id(0)

    @pl.when(i >= used_ref[0])
    def _():
        o_ref[...] = jnp.zeros_like(o_ref)

    def weight_copies(e, slot):
        return (pltpu.make_async_copy(wgu_ref.at[e], wbuf_gu.at[slot], sem.at[0, slot]),
                pltpu.make_async_copy(wd_ref.at[e], wbuf_d.at[slot], sem.at[1, slot]))

    @pl.when(i == 0)
    def _():
        for cp in weight_copies(be_ref[0], slot_ref[0]):
            cp.start()

    @pl.when(first_ref[i] == 1)
    def _():
        slot = slot_ref[i]
        for cp in weight_copies(be_ref[i], slot):
            cp.wait()

        @pl.when(next_ref[i] >= 0)
        def _():
            for cp in weight_copies(next_ref[i], 1 - slot):
                cp.start()

        s_wgu[...] = wbuf_gu[slot].astype(BF16)
        s_wd[...] = wbuf_d[slot].astype(BF16)

    @pl.when(i < used_ref[0])
    def _():
        pieces = [x_ref[pl.ds(s, MOE_BLOCK, stride=TILE_ROWS), :] for s in range(TILE_ROWS)]
        x = jnp.concatenate(pieces, axis=1).astype(BF16)
        gu = jnp.dot(x, s_wgu[...], preferred_element_type=F32) + bgu_ref[0]
        g = jnp.minimum(gu[:, :D_FF], SWIGLU_LIMIT)
        up = jnp.clip(gu[:, D_FF:], -SWIGLU_LIMIT, SWIGLU_LIMIT)
        hdn = (up + 1.0) * (g * jax.nn.sigmoid(SWIGLU_ALPHA * g))
        out = jnp.dot(hdn.astype(BF16), s_wd[...], preferred_element_type=F32) + bd_ref[0]
        for s in range(TILE_ROWS):
            o_ref[pl.ds(s, MOE_BLOCK, stride=TILE_ROWS), :] = out[:, s * LANES:(s + 1) * LANES]


def _experts(meta, buf, wgu, bgu, wd, bd, n_blocks):
    blk_rows = MOE_BLOCK * TILE_ROWS
    block_e, n_used = meta[:n_blocks, 0], meta[:1, 2]
    first, nxt, slot = meta[:n_blocks, 3], meta[:n_blocks, 4], meta[:n_blocks, 5]

    def in_row_map(i, be, nu, *_):
        return (jnp.minimum(i, nu[0] - 1), 0)

    def b_map(i, be, *_):
        return (be[i], 0, 0)

    grid_spec = pltpu.PrefetchScalarGridSpec(
        num_scalar_prefetch=5,
        grid=(n_blocks,),
        in_specs=[pl.BlockSpec((blk_rows, LANES), in_row_map),
                  pl.BlockSpec(memory_space=pl.ANY),
                  pl.BlockSpec((1, 1, 2 * D_FF), b_map),
                  pl.BlockSpec(memory_space=pl.ANY),
                  pl.BlockSpec((1, 1, D_MODEL), b_map)],
        out_specs=pl.BlockSpec((blk_rows, LANES), lambda i, *_: (i, 0)),
        scratch_shapes=[pltpu.VMEM((2, D_MODEL, 2 * D_FF), F32), pltpu.VMEM((2, D_FF, D_MODEL), F32),
                        pltpu.VMEM((D_MODEL, 2 * D_FF), BF16), pltpu.VMEM((D_FF, D_MODEL), BF16),
                        pltpu.SemaphoreType.DMA((2, 2))],
    )
    return pl.pallas_call(
        _expert_kernel,
        grid_spec=grid_spec,
        out_shape=jax.ShapeDtypeStruct(buf.shape, F32),
        compiler_params=pltpu.CompilerParams(dimension_semantics=("arbitrary",),
                                             vmem_limit_bytes=VMEM_LIMIT),
        name="moe_experts",
    )(block_e, n_used, first, nxt, slot, buf, wgu, bgu, wd, bd)


def _combine_kernel(dest_ref, next_dest_ref, gate_ref, x1_ref, rows_ref, nf_ref, yp_ref, ys_ref,
                    gbuf, sem, *, batch, n_prompt):
    i = pl.program_id(0)
    tile = batch * CHUNK
    tile_rows = tile * TILE_ROWS

    slot = i % 2

    def gather(dref, slot_id):
        def issue(j, carry):
            for k in range(TOP_K):
                pltpu.make_async_copy(_token_tile(rows_ref, dref[j * TOP_K + k]),
                                      _token_tile(gbuf.at[slot_id, k], j),
                                      sem.at[slot_id]).start(priority=k % 2)
            return carry

        lax.fori_loop(0, tile, issue, 0, unroll=4)

    @pl.when(i == 0)
    def _():
        gather(dest_ref, 0)

    @pl.when(i + 1 < pl.num_programs(0))
    def _():
        gather(next_dest_ref, 1 - slot)

    for k in range(TOP_K):
        pltpu.make_async_copy(rows_ref.at[pl.ds(0, tile_rows)], gbuf.at[slot, k], sem.at[slot]).wait()

    gates = gate_ref[...]
    pieces = []
    for s in range(TILE_ROWS):
        piece = x1_ref[:, s * LANES:(s + 1) * LANES]
        for k in range(TOP_K):
            piece = piece + gates[:, k:k + 1] * gbuf[slot, k, pl.ds(s, tile, stride=TILE_ROWS), :]
        pieces.append(piece)
    y = _rmsnorm(jnp.concatenate(pieces, axis=1), nf_ref[...]).reshape(batch, CHUNK, D_MODEL)

    @pl.when(i < n_prompt)
    def _():
        yp_ref[...] = y

    @pl.when(i >= n_prompt)
    def _():
        ys_ref[...] = y


def _combine(dest_flat, gates, x1, out_rows, norm_final, batch, lp, ls):
    tile = batch * CHUNK
    n_prompt = lp // CHUNK
    n_sample = ls // CHUNK
    last = n_prompt + n_sample - 1
    return pl.pallas_call(
        functools.partial(_combine_kernel, batch=batch, n_prompt=n_prompt),
        grid=(n_prompt + n_sample,),
        in_specs=[pl.BlockSpec((tile * TOP_K,), lambda i: (i,), memory_space=pltpu.SMEM),
                  pl.BlockSpec((tile * TOP_K,), lambda i: (jnp.minimum(i + 1, last),),
                               memory_space=pltpu.SMEM),
                  pl.BlockSpec((tile, TOP_K), lambda i: (i, 0)),
                  pl.BlockSpec((tile, D_MODEL), lambda i: (i, 0)),
                  pl.BlockSpec(memory_space=pl.ANY),
                  pl.BlockSpec((1, D_MODEL), lambda i: (0, 0))],
        out_specs=(pl.BlockSpec((batch, CHUNK, D_MODEL), lambda i: (0, jnp.minimum(i, n_prompt - 1), 0)),
                   pl.BlockSpec((batch, CHUNK, D_MODEL), lambda i: (0, jnp.maximum(i - n_prompt, 0), 0))),
        out_shape=(jax.ShapeDtypeStruct((batch, lp, D_MODEL), F32),
                   jax.ShapeDtypeStruct((batch, ls, D_MODEL), F32)),
        scratch_shapes=[pltpu.VMEM((2, TOP_K, tile * TILE_ROWS, LANES), F32),
                        pltpu.SemaphoreType.DMA((2,))],
        compiler_params=pltpu.CompilerParams(dimension_semantics=("arbitrary",),
                                             vmem_limit_bytes=VMEM_LIMIT),
        name="moe_combine",
    )(dest_flat, dest_flat, gates, x1, out_rows, norm_final)


def _prep_weights(norm_mix, w_in, b_ig, b_fg, s5_a_re, s5_a_im, s5_log_dt, s5_b_re, s5_b_im,
                  s5_c_re, s5_c_im, s5_d, w_s5_glu, ml_gn, w_ml_out, w_out, norm_moe, w_router,
                  b_router):
    w = w_in[0]
    o_ig = D_S5 + 4 * D_ML
    o_gs5 = o_ig + 2 * ML_HEADS
    w_a = w[:, :o_ig].astype(BF16)
    w_b = w[:, o_gs5:].astype(BF16)
    w_g = jnp.concatenate([w[:, o_ig:o_gs5], jnp.zeros((D_MODEL, LANES - 2 * ML_HEADS), F32)],
                          axis=1).astype(BF16)
    gate_bias = jnp.concatenate([b_ig[0], b_fg[0], jnp.zeros((LANES - 2 * ML_HEADS,), F32)])[None]

    ar, ai = s5_a_re[0], s5_a_im[0]
    dt = jnp.exp(s5_log_dt[0])[:, None]
    mag = jnp.exp(dt * ar)
    abar_re = mag * jnp.cos(dt * ai)
    abar_im = mag * jnp.sin(dt * ai)
    den = ar * ar + ai * ai
    fr = ((abar_re - 1.0) * ar + abar_im * ai) / den
    fi = (abar_im * ar - (abar_re - 1.0) * ai) / den
    br, bi = s5_b_re[0], s5_b_im[0]
    bbar_re = fr[..., None] * br - fi[..., None] * bi
    bbar_im = fr[..., None] * bi + fi[..., None] * br
    gh = S5_GROUPS // 2
    same_group = (jnp.arange(gh)[:, None, None, None] == jnp.arange(gh)[None, None, :, None])

    def blockdiag_in(bb):
        t = bb.reshape(2, gh, S5_STATE, S5_GROUP).transpose(0, 1, 3, 2)
        full = jnp.where(same_group[None], t[:, :, :, None, :], 0.0)
        return full.reshape(2, gh * S5_GROUP, gh * S5_STATE).astype(BF16)

    def blockdiag_out(cc):
        t = cc.reshape(2, gh, S5_GROUP, S5_STATE).transpose(0, 1, 3, 2)
        full = jnp.where(same_group[None], t[:, :, :, None, :], 0.0)
        return full.reshape(2, gh * S5_STATE, gh * S5_GROUP)

    cb = jnp.concatenate([blockdiag_out(s5_c_re[0]), -blockdiag_out(s5_c_im[0])], axis=1).astype(BF16)
    w_router_p = jnp.concatenate(
        [w_router[0], jnp.zeros((D_MODEL, LANES - N_EXPERTS), F32)], axis=1).astype(BF16)
    b_router_p = jnp.concatenate([b_router[0], jnp.zeros((LANES - N_EXPERTS,), F32)])[None]
    return dict(
        norm_mix=norm_mix[0][None], w_in_a=w_a, w_in_b=w_b, w_in_g=w_g, gate_bias=gate_bias,
        abar_re=abar_re.reshape(1, S5_CH), abar_im=abar_im.reshape(1, S5_CH),
        bd_re=blockdiag_in(bbar_re), bd_im=blockdiag_in(bbar_im), cb=cb,
        d_skip=s5_d[0][None], w_glu=w_s5_glu[0].astype(BF16), ml_gn=ml_gn[0][None],
        w_ml_out=w_ml_out[0].astype(BF16), w_out=w_out[0].astype(BF16),
        norm_moe=norm_moe[0][None], w_router=w_router_p, b_router=b_router_p)


def kernel(x_prompt, x_sample, state_s5_re, state_s5_im, state_ml_C, state_ml_n, state_ml_m, norm_mix, w_in, b_ig, b_fg, s5_a_re, s5_a_im, s5_log_dt, s5_b_re, s5_b_im, s5_c_re, s5_c_im, s5_d, w_s5_glu, ml_gn, w_ml_out, w_out, norm_moe, w_router, b_router, w_gate_up, b_gate_up, w_down, b_down, norm_final):
    w = _prep_weights(norm_mix, w_in, b_ig, b_fg, s5_a_re, s5_a_im, s5_log_dt, s5_b_re, s5_b_im,
                      s5_c_re, s5_c_im, s5_d, w_s5_glu, ml_gn, w_ml_out, w_out, norm_moe,
                      w_router, b_router)
    batch, lp, _ = x_prompt.shape
    ls = x_sample.shape[1]
    bh = batch * ML_HEADS
    cx0 = jnp.concatenate(
        [jnp.swapaxes(state_ml_C[0].reshape(bh, ML_HEAD_DIM, ML_HEAD_DIM), 1, 2),
         jnp.broadcast_to(state_ml_n[0].reshape(bh, ML_HEAD_DIM, 1), (bh, ML_HEAD_DIM, ML_HEAD_DIM))],
        axis=2)
    x1, xm, idx, gates, hr, hi, cx, m = _mixer(
        x_prompt, x_sample,
        state_s5_re[0].reshape(batch, S5_CH), state_s5_im[0].reshape(batch, S5_CH), cx0,
        jnp.broadcast_to(state_ml_m[0].reshape(bh, 1), (bh, LANES)), w)
    c = jnp.swapaxes(cx[:, :, :, :ML_HEAD_DIM], 2, 3)
    n = cx[:, :, :, ML_HEAD_DIM]

    tile = batch * CHUNK
    t_all = idx.shape[0]
    n_blocks = -(-(t_all * TOP_K + N_EXPERTS * (MOE_BLOCK - 1)) // MOE_BLOCK)
    n_blocks_pad = -(-n_blocks // SUBLANES) * SUBLANES
    dest, meta = _plan(idx, tile, n_blocks_pad)
    dest_flat = dest.reshape(t_all * TOP_K)
    buf = _dispatch(meta[:n_blocks, 1], dest_flat, xm, tile, n_blocks)
    out_rows = _experts(meta, buf, w_gate_up[0], b_gate_up[0][:, None, :], w_down[0],
                        b_down[0][:, None, :], n_blocks)
    y_prompt, y_sample = _combine(dest_flat, gates, x1, out_rows, norm_final[None], batch, lp, ls)

    def states(p):
        return (hr[p].reshape(1, batch, S5_GROUPS, S5_STATE), hi[p].reshape(1, batch, S5_GROUPS, S5_STATE),
                c[p].reshape(1, batch, ML_HEADS, ML_HEAD_DIM, ML_HEAD_DIM),
                n[p].reshape(1, batch, ML_HEADS, ML_HEAD_DIM), m[p, :, 0].reshape(1, batch, ML_HEADS))

    return (y_prompt, y_sample) + states(0) + states(1)
```

```python
import functools

import jax
import jax.numpy as jnp
from jax import lax
from jax.experimental import pallas as pl
from jax.experimental.pallas import tpu as pltpu

F32 = jnp.float32
BF16 = jnp.bfloat16

D_MODEL = 1024
CHUNK = 64
EPS = 1e-6
D_S5 = 512
S5_GROUP = 16
S5_GROUPS = D_S5 // S5_GROUP
S5_STATE = 64
S5_CH = S5_GROUPS * S5_STATE
ML_HEADS = 4
ML_HEAD_DIM = 128
D_ML = ML_HEADS * ML_HEAD_DIM
N_EXPERTS = 32
TOP_K = 4
D_FF = 1024
SWIGLU_LIMIT = 7.0
SWIGLU_ALPHA = 1.702

LANES = 128
SUBLANES = 8
TILE_ROWS = D_MODEL // LANES

OFF_U = 0
OFF_Q = OFF_U + D_S5
OFF_K = OFF_Q + D_ML
OFF_V = OFF_K + D_ML
OFF_O = OFF_V + D_ML
OFF_GS5 = OFF_O + D_ML
OFF_GML = OFF_GS5 + D_MODEL
OFF_GATE = OFF_GML + D_MODEL
W_COLS = OFF_GATE + LANES

S5_HALF_IN = D_S5 // 2
S5_HALF_CH = S5_CH // 2
S5_HALF_TILES = S5_HALF_CH // LANES
S5_SCAN_TILES = 4
ML_GROUP = 2
MOE_BLOCK = 512
VMEM_LIMIT = 60 * 1024 * 1024


def _rmsnorm(x, w):
    return x * lax.rsqrt(jnp.mean(x * x, axis=-1, keepdims=True) + EPS) * w


def _const_spec(shape):
    nd = len(shape)
    return pl.BlockSpec(shape, lambda *_: (0,) * nd, pipeline_mode=pl.Buffered(1))


def _mixer_kernel(xp_ref, xs_ref, h0r_ref, h0i_ref, cx0_ref, m0_ref,
                  nmix_ref, wina_ref, winb_ref, wing_ref, gbias_ref, abr_ref, abi_ref, bdr_ref, bdi_ref, cb_ref,
                  dskip_ref, glu_ref, gn_ref, wml_ref, wout_ref, nmoe_ref, wr_ref, br_ref,
                  x1_ref, xm_ref, idx_ref, gate_ref, hr_ref, hi_ref, cx_ref, m_ref,
                  s_bur, s_bui, s_q, s_k, s_v, s_o, s_col, s_gated, s_cx, sem, *, batch, n_prompt):
    i = pl.program_id(0)
    rows = batch * CHUNK

    @pl.when(i == 0)
    def _():
        hr_ref[...] = jnp.zeros_like(hr_ref)
        hi_ref[...] = jnp.zeros_like(hi_ref)
        m_ref[...] = jnp.zeros_like(m_ref)
        s_cx[...] = jnp.zeros_like(s_cx)

    @pl.when(i == n_prompt)
    def _():
        hr_ref[0] = h0r_ref[...]
        hi_ref[0] = h0i_ref[...]
        m_ref[0] = m0_ref[...]
        load = pltpu.make_async_copy(cx0_ref, s_cx, sem)
        load.start()
        load.wait()

    @pl.when(i < n_prompt)
    def _():
        x1_ref[...] = xp_ref[...].reshape(rows, D_MODEL)

    @pl.when(i >= n_prompt)
    def _():
        x1_ref[...] = xs_ref[...].reshape(rows, D_MODEL)

    xn = _rmsnorm(x1_ref[...], nmix_ref[...]).astype(BF16)

    def proj(off, width):
        if off >= OFF_GATE:
            w_cols = wing_ref[...]
        elif off >= OFF_GS5:
            w_cols = winb_ref[:, off - OFF_GS5:off - OFF_GS5 + width]
        else:
            w_cols = wina_ref[:, off:off + width]
        return jnp.dot(xn, w_cols, preferred_element_type=F32)

    gates = proj(OFF_GATE, LANES) + gbias_ref[...]
    lane_g = lax.broadcasted_iota(jnp.int32, (rows, LANES), 1)
    gg = jnp.where(lane_g < ML_HEADS, gates, jax.nn.log_sigmoid(gates))
    gt8 = gg.T[0:SUBLANES, :]
    pos = lax.broadcasted_iota(jnp.int32, (SUBLANES, rows), 1) % CHUNK
    cum = gt8
    shift = 1
    while shift < CHUNK:
        cum = cum + jnp.where(pos >= shift, pltpu.roll(cum, shift, axis=1), 0.0)
        shift *= 2
    g8 = gt8 - pltpu.roll(cum, ML_HEADS, axis=0)
    mx8 = g8
    shift = 1
    while shift < CHUNK:
        mx8 = jnp.maximum(mx8, jnp.where(pos >= shift, pltpu.roll(mx8, shift, axis=1), -jnp.inf))
        shift *= 2
    sub = lax.broadcasted_iota(jnp.int32, (SUBLANES, rows), 0)
    top8 = jnp.where(sub < ML_HEADS, g8, cum)
    s_col[...] = jnp.concatenate(
        [top8, mx8, jnp.zeros((LANES - 2 * SUBLANES, rows), F32)], axis=0).T
    s_q[...] = (proj(OFF_Q, D_ML) * (ML_HEAD_DIM ** -0.5)).astype(BF16)
    s_k[...] = proj(OFF_K, D_ML).astype(BF16)
    s_v[...] = proj(OFF_V, D_ML).astype(BF16)
    s_o[...] = proj(OFF_O, D_ML)

    tri = (lax.broadcasted_iota(jnp.int32, (CHUNK, CHUNK), 0)
           >= lax.broadcasted_iota(jnp.int32, (CHUNK, CHUNK), 1))[None]
    bdims = ((0,), (0,))

    def rowsl(b):
        return slice(b * CHUNK, (b + 1) * CHUNK)

    def headsl(h):
        return slice(h * ML_HEAD_DIM, (h + 1) * ML_HEAD_DIM)

    for g0 in range(0, batch, ML_GROUP):
        items = [(b, h) for b in range(g0, g0 + ML_GROUP) for h in range(ML_HEADS)]
        def col(j, items=items):
            return jnp.stack([jnp.broadcast_to(s_col[rowsl(b), j + h:j + h + 1], (CHUNK, LANES))
                              for b, h in items])

        g_c, b_c, mx_c = col(0), col(ML_HEADS), col(2 * ML_HEADS)
        g_r = jnp.stack([g8[h:h + 1, rowsl(b)] for b, h in items])
        m_prev = jnp.stack([m_ref[0, b * ML_HEADS + h:b * ML_HEADS + h + 1, :] for b, h in items])
        q3 = jnp.stack([s_q[rowsl(b), headsl(h)] for b, h in items])
        k3 = jnp.stack([s_k[rowsl(b), headsl(h)] for b, h in items])
        v3 = jnp.stack([s_v[rowsl(b), headsl(h)] for b, h in items])
        cx = jnp.stack([s_cx[b * ML_HEADS + h] for b, h in items])

        big_m = jnp.maximum(m_prev, mx_c)
        p = jnp.exp(jnp.where(tri, g_r - big_m[:, :, :CHUNK], -jnp.inf))
        w_inter = jnp.exp(m_prev - big_m)
        s = lax.dot_general(q3, k3, (((2,), (2,)), bdims), preferred_element_type=F32) * p
        cqx = lax.dot_general(q3, cx.astype(BF16), (((2,), (1,)), bdims), preferred_element_type=F32)
        num = (lax.dot_general(s.astype(BF16), v3, (((2,), (1,)), bdims), preferred_element_type=F32)
               + w_inter * cqx[:, :, :ML_HEAD_DIM])
        den_dot = jnp.sum(s, axis=2, keepdims=True) + w_inter * cqx[:, :, ML_HEAD_DIM:]
        m_t = b_c + big_m
        hout = num / jnp.maximum(jnp.abs(den_dot), jnp.exp(-m_t))

        m_last = big_m[:, CHUNK - 1:CHUNK, :]
        w_end = jnp.exp(g_c - m_last)
        decay = jnp.exp(m_prev - m_last)
        wvx = jnp.concatenate([w_end * v3.astype(F32), w_end], axis=2).astype(BF16)
        k_t = jnp.stack([s_k[rowsl(b), headsl(h)].astype(F32).T.astype(BF16) for b, h in items])
        cx_new = (jnp.concatenate([decay, decay], axis=2) * cx
                  + lax.dot_general(k_t, wvx, (((2,), (1,)), bdims), preferred_element_type=F32))
        m_new = m_t[:, CHUNK - 1:CHUNK, :]

        hc = hout - jnp.mean(hout, axis=2, keepdims=True)
        hn = hc * lax.rsqrt(jnp.mean(hc * hc, axis=2, keepdims=True) + EPS)
        for n, (b, h) in enumerate(items):
            bh = b * ML_HEADS + h
            s_cx[bh] = cx_new[n]
            m_ref[0, bh:bh + 1, :] = m_new[n]
            s_gated[rowsl(b), headsl(h)] = (jax.nn.sigmoid(s_o[rowsl(b), headsl(h)])
                                            * (hn[n] * gn_ref[:, headsl(h)])).astype(BF16)

    u = proj(OFF_U, D_S5)
    ub = u.astype(BF16)
    ys = []
    merge = []
    for k in range(2):
        uk = ub[:, k * S5_HALF_IN:(k + 1) * S5_HALF_IN]
        bur = jnp.dot(uk, bdr_ref[k], preferred_element_type=F32)
        bui = jnp.dot(uk, bdi_ref[k], preferred_element_type=F32)
        for j in range(S5_HALF_TILES):
            for b in range(batch):
                dst = pl.ds(b, CHUNK, stride=batch)
                src = slice(b * CHUNK, (b + 1) * CHUNK)
                s_bur[j, dst, :] = bur[src, j * LANES:(j + 1) * LANES]
                s_bui[j, dst, :] = bui[src, j * LANES:(j + 1) * LANES]

        if k == 0:
            merge.append(jax.nn.sigmoid(proj(OFF_GML, D_MODEL)) * jnp.dot(
                s_gated[...], wml_ref[...], preferred_element_type=F32))
        else:
            merge.append(jax.nn.sigmoid(proj(OFF_GS5, D_MODEL)))

        for c in range(S5_HALF_TILES // S5_SCAN_TILES):
            tiles = range(c * S5_SCAN_TILES, (c + 1) * S5_SCAN_TILES)
            lanes = [slice(k * S5_HALF_CH + j * LANES, k * S5_HALF_CH + (j + 1) * LANES) for j in tiles]
            ar = [jnp.broadcast_to(abr_ref[:, ls], (batch, LANES)) for ls in lanes]
            ai = [jnp.broadcast_to(abi_ref[:, ls], (batch, LANES)) for ls in lanes]

            def step(t, carry, tiles=tiles, ar=ar, ai=ai):
                slab = pl.ds(pl.multiple_of(t * batch, batch), batch)
                out = []
                for n, j in enumerate(tiles):
                    hr, hi = carry[n]
                    nr = ar[n] * hr - ai[n] * hi + s_bur[j, slab, :]
                    ni = ar[n] * hi + ai[n] * hr + s_bui[j, slab, :]
                    s_bur[j, slab, :] = nr
                    s_bui[j, slab, :] = ni
                    out.append((nr, ni))
                return tuple(out)

            init = tuple((hr_ref[0, :, ls], hi_ref[0, :, ls]) for ls in lanes)
            fin = lax.fori_loop(0, CHUNK, step, init, unroll=True)
            for n, ls in enumerate(lanes):
                hr_ref[0, :, ls] = fin[n][0]
                hi_ref[0, :, ls] = fin[n][1]

        def stream_major(ref):
            return jnp.concatenate(
                [jnp.concatenate([ref[j, pl.ds(b, CHUNK, stride=batch), :]
                                  for j in range(S5_HALF_TILES)], axis=1) for b in range(batch)], axis=0)

        yk = jnp.dot(stream_major(s_bur).astype(BF16), cb_ref[k, :S5_HALF_CH, :],
                     preferred_element_type=F32)
        yk = yk + jnp.dot(stream_major(s_bui).astype(BF16), cb_ref[k, S5_HALF_CH:, :],
                          preferred_element_type=F32)
        ys.append(yk)
    y = jnp.concatenate(ys, axis=1) + dskip_ref[...] * u
    glu = jnp.dot(jax.nn.gelu(y).astype(BF16), glu_ref[...], preferred_element_type=F32)
    y_s5 = glu[:, :D_MODEL] * jax.nn.sigmoid(glu[:, D_MODEL:])
    mix = merge[0] + merge[1] * y_s5

    x1 = x1_ref[...] + jnp.dot(mix.astype(BF16), wout_ref[...], preferred_element_type=F32)
    x1_ref[...] = x1

    xm = _rmsnorm(x1, nmoe_ref[...])
    logits = jnp.dot(xm.astype(BF16), wr_ref[...], preferred_element_type=F32) + br_ref[...]
    lane = lax.broadcasted_iota(jnp.int32, (rows, LANES), 1)
    logits = jnp.where(lane < N_EXPERTS, logits, -jnp.inf)
    vals, idxs = [], []
    for _ in range(TOP_K):
        mx = jnp.max(logits, axis=1, keepdims=True)
        am = jnp.min(jnp.where(logits == mx, lane, LANES), axis=1, keepdims=True)
        vals.append(mx)
        idxs.append(am)
        logits = jnp.where(lane == am, -jnp.inf, logits)
    exps = [jnp.exp(v - vals[0]) for v in vals]
    esum = exps[0] + exps[1] + exps[2] + exps[3]
    idx_w = jnp.zeros((rows, LANES), jnp.int32)
    gate_w = jnp.zeros((rows, LANES), F32)
    for k in range(TOP_K):
        idx_w = jnp.where(lane == k, idxs[k], idx_w)
        gate_w = jnp.where(lane == k, exps[k] / esum, gate_w)
    idx_ref[...] = idx_w[:, :TOP_K]
    gate_ref[...] = gate_w[:, :TOP_K]

    for s in range(TILE_ROWS):
        xm_ref[pl.ds(s, rows, stride=TILE_ROWS), :] = xm[:, s * LANES:(s + 1) * LANES]

    for phase_id, final_step in ((0, n_prompt - 1), (1, pl.num_programs(0) - 1)):
        @pl.when(i == final_step)
        def _(phase_id=phase_id):
            store = pltpu.make_async_copy(s_cx, cx_ref.at[phase_id], sem)
            store.start()
            store.wait()


def _mixer(x_prompt, x_sample, h0r, h0i, cx0, m0, w):
    batch, lp, _ = x_prompt.shape
    assert x_sample.shape[0] == batch and lp % CHUNK == 0 and x_sample.shape[1] % CHUNK == 0
    assert batch % ML_GROUP == 0
    n_prompt = lp // CHUNK
    n_sample = x_sample.shape[1] // CHUNK
    nblk = n_prompt + n_sample
    rows = batch * CHUNK
    t_all = nblk * rows
    bh = batch * ML_HEADS

    def phase(i):
        return jnp.minimum(i // n_prompt, 1)

    state_specs = [
        _const_spec((batch, S5_CH)), _const_spec((batch, S5_CH)),
        pl.BlockSpec(memory_space=pl.ANY),
        _const_spec((bh, LANES)),
    ]
    weights = (w['norm_mix'], w['w_in_a'], w['w_in_b'], w['w_in_g'], w['gate_bias'], w['abar_re'], w['abar_im'], w['bd_re'],
               w['bd_im'], w['cb'], w['d_skip'], w['w_glu'], w['ml_gn'], w['w_ml_out'], w['w_out'],
               w['norm_moe'], w['w_router'], w['b_router'])
    in_specs = ([pl.BlockSpec((batch, CHUNK, D_MODEL), lambda i: (0, jnp.minimum(i, n_prompt - 1), 0)),
                 pl.BlockSpec((batch, CHUNK, D_MODEL), lambda i: (0, jnp.maximum(i - n_prompt, 0), 0))]
                + state_specs + [_const_spec(a.shape) for a in weights])
    out_shape = (
        jax.ShapeDtypeStruct((t_all, D_MODEL), F32),
        jax.ShapeDtypeStruct((t_all * TILE_ROWS, LANES), F32),
        jax.ShapeDtypeStruct((t_all, TOP_K), jnp.int32),
        jax.ShapeDtypeStruct((t_all, TOP_K), F32),
        jax.ShapeDtypeStruct((2, batch, S5_CH), F32),
        jax.ShapeDtypeStruct((2, batch, S5_CH), F32),
        jax.ShapeDtypeStruct((2, bh, ML_HEAD_DIM, 2 * ML_HEAD_DIM), F32),
        jax.ShapeDtypeStruct((2, bh, LANES), F32),
    )
    out_specs = (
        pl.BlockSpec((rows, D_MODEL), lambda i: (i, 0)),
        pl.BlockSpec((rows * TILE_ROWS, LANES), lambda i: (i, 0)),
        pl.BlockSpec((rows, TOP_K), lambda i: (i, 0)),
        pl.BlockSpec((rows, TOP_K), lambda i: (i, 0)),
        pl.BlockSpec((1, batch, S5_CH), lambda i: (phase(i), 0, 0)),
        pl.BlockSpec((1, batch, S5_CH), lambda i: (phase(i), 0, 0)),
        pl.BlockSpec(memory_space=pl.ANY),
        pl.BlockSpec((1, bh, LANES), lambda i: (phase(i), 0, 0)),
    )
    scratch = [
        pltpu.VMEM((S5_HALF_TILES, rows, LANES), F32), pltpu.VMEM((S5_HALF_TILES, rows, LANES), F32),
        pltpu.VMEM((rows, D_ML), BF16), pltpu.VMEM((rows, D_ML), BF16), pltpu.VMEM((rows, D_ML), BF16),
        pltpu.VMEM((rows, D_ML), F32),
        pltpu.VMEM((rows, LANES), F32),
        pltpu.VMEM((rows, D_ML), BF16),
        pltpu.VMEM((bh, ML_HEAD_DIM, 2 * ML_HEAD_DIM), F32),
        pltpu.SemaphoreType.DMA(()),
    ]
    return pl.pallas_call(
        functools.partial(_mixer_kernel, batch=batch, n_prompt=n_prompt),
        grid=(nblk,),
        in_specs=in_specs,
        out_specs=out_specs,
        out_shape=out_shape,
        scratch_shapes=scratch,
        compiler_params=pltpu.CompilerParams(dimension_semantics=("arbitrary",),
                                             vmem_limit_bytes=VMEM_LIMIT),
        name="mixer",
    )(x_prompt, x_sample, h0r, h0i, cx0, m0, *weights)


def _plan_kernel(idx_ref, dest_ref, meta_ref, s_cnt, s_run, s_pstart, *, tile, n_blocks_pad):
    phase = pl.program_id(0)
    i = pl.program_id(1)
    lane = lax.broadcasted_iota(jnp.int32, (tile, LANES), 1)
    idx = idx_ref[...]
    onehots = [(lane == idx[:, k:k + 1]).astype(F32) for k in range(TOP_K)]
    mask = onehots[0] + onehots[1] + onehots[2] + onehots[3]

    @pl.when(jnp.logical_and(phase == 0, i == 0))
    def _():
        s_cnt[...] = jnp.zeros_like(s_cnt)

    @pl.when(phase == 0)
    def _():
        s_cnt[...] += jnp.sum(mask, axis=0, keepdims=True)

    @pl.when(jnp.logical_and(phase == 1, i == 0))
    def _():
        cnt = s_cnt[...]
        nblk = jnp.floor((cnt + (MOE_BLOCK - 1)) * (1.0 / MOE_BLOCK))
        lane1 = lax.broadcasted_iota(jnp.int32, (1, LANES), 1)
        pend = nblk
        shift = 1
        while shift < LANES:
            pend = pend + jnp.where(lane1 >= shift, pltpu.roll(pend, shift, axis=1), 0.0)
            shift *= 2
        s_pstart[...] = (pend - nblk) * MOE_BLOCK
        s_run[...] = jnp.zeros_like(s_run)
        blk = lax.broadcasted_iota(jnp.int32, (n_blocks_pad, LANES), 0).astype(F32)
        lane_b = lax.broadcasted_iota(jnp.int32, (n_blocks_pad, LANES), 1)
        lane_f = lane_b.astype(F32)
        is_e = lane_b < N_EXPERTS
        done = jnp.logical_and(is_e, pend <= blk)
        e_of = jnp.minimum(jnp.sum(done.astype(F32), axis=1, keepdims=True), N_EXPERTS - 1.0)
        mine = lane_f == e_of
        blk_in_e = jnp.sum(jnp.where(mine, blk - (pend - nblk), 0.0), axis=1, keepdims=True)
        cnt_e = jnp.sum(jnp.where(mine, cnt, 0.0), axis=1, keepdims=True)
        valid = jnp.clip(cnt_e - blk_in_e * MOE_BLOCK, 0.0, float(MOE_BLOCK))
        used = jnp.sum(jnp.where(lane1 == N_EXPERTS - 1, pend, 0.0), axis=1, keepdims=True)
        owns = jnp.logical_and(is_e, nblk > 0.0)
        first = jnp.logical_and(blk_in_e == 0.0, blk[:, :1] < used).astype(F32)
        later = jnp.logical_and(owns, lane_f > e_of)
        nxt = jnp.min(jnp.where(later, lane_f, float(LANES)), axis=1, keepdims=True)
        nxt = jnp.where(nxt < float(LANES), nxt, -1.0)
        run = jnp.sum(jnp.logical_and(owns, lane_f < e_of).astype(F32), axis=1, keepdims=True)
        parity = run - 2.0 * jnp.floor(run * 0.5)
        meta = jnp.zeros((n_blocks_pad, LANES), F32)
        for col, val in enumerate((e_of, valid, used, first, nxt, parity)):
            meta = jnp.where(lane_b == col, val, meta)
        meta_ref[...] = meta.astype(jnp.int32)

    @pl.when(phase == 1)
    def _():
        r = lax.broadcasted_iota(jnp.int32, (tile, tile), 0)
        c = lax.broadcasted_iota(jnp.int32, (tile, tile), 1)
        lower = (c < r).astype(BF16)
        before = jnp.dot(lower, mask.astype(BF16), preferred_element_type=F32)
        base = before + s_run[...] + s_pstart[...]
        dest = jnp.zeros((tile, LANES), F32)
        for k in range(TOP_K):
            dk = jnp.sum(onehots[k] * base, axis=1, keepdims=True)
            dest = jnp.where(lane == k, dk, dest)
        dest_ref[...] = dest[:, :TOP_K].astype(jnp.int32)
        s_run[...] += jnp.sum(mask, axis=0, keepdims=True)


def _plan(idx_all, tile, n_blocks_pad):
    t_all = idx_all.shape[0]
    return pl.pallas_call(
        functools.partial(_plan_kernel, tile=tile, n_blocks_pad=n_blocks_pad),
        grid=(2, t_all // tile),
        in_specs=[pl.BlockSpec((tile, TOP_K), lambda p, i: (i, 0))],
        out_specs=(pl.BlockSpec((tile, TOP_K), lambda p, i: (i * p, 0)),
                   pl.BlockSpec((n_blocks_pad, LANES), lambda p, i: (0, 0))),
        out_shape=(jax.ShapeDtypeStruct((t_all, TOP_K), jnp.int32),
                   jax.ShapeDtypeStruct((n_blocks_pad, LANES), jnp.int32)),
        scratch_shapes=[pltpu.VMEM((1, LANES), F32), pltpu.VMEM((1, LANES), F32),
                        pltpu.VMEM((1, LANES), F32)],
        compiler_params=pltpu.CompilerParams(dimension_semantics=("arbitrary", "arbitrary")),
        name="moe_plan",
    )(idx_all)


def _token_tile(ref, t):
    return ref.at[pl.ds(pl.multiple_of(t * TILE_ROWS, TILE_ROWS), TILE_ROWS)]


def _dispatch_kernel(valid_ref, dest_ref, xm_ref, buf_ref, zeros, sem, zsem, *, tile, n_blocks):
    blk_rows = MOE_BLOCK * TILE_ROWS

    @pl.when(pl.program_id(0) == 0)
    def _():
        zeros[...] = jnp.zeros_like(zeros)

        def block_copy(i):
            return pltpu.make_async_copy(
                zeros, buf_ref.at[pl.ds(pl.multiple_of(i * blk_rows, blk_rows), blk_rows)], zsem)

        def fill(i, carry):
            @pl.when(valid_ref[i] < MOE_BLOCK)
            def _():
                block_copy(i).start()
            return carry

        def drain(i, carry):
            @pl.when(valid_ref[i] < MOE_BLOCK)
            def _():
                block_copy(i).wait()
            return carry

        lax.fori_loop(0, n_blocks, fill, 0)
        lax.fori_loop(0, n_blocks, drain, 0)

    def issue(j, carry):
        src = _token_tile(xm_ref, j)
        for k in range(TOP_K):
            pltpu.make_async_copy(src, _token_tile(buf_ref, dest_ref[j * TOP_K + k]), sem).start(
                priority=k % 2)
        return carry

    lax.fori_loop(0, tile, issue, 0, unroll=4)
    n_rows = tile * TOP_K * TILE_ROWS
    pltpu.make_async_copy(buf_ref.at[pl.ds(0, n_rows)], buf_ref.at[pl.ds(0, n_rows)], sem).wait()


def _dispatch(block_valid, dest_flat, xm_tiles, tile, n_blocks):
    t_all = xm_tiles.shape[0] // TILE_ROWS
    grid_spec = pltpu.PrefetchScalarGridSpec(
        num_scalar_prefetch=1,
        grid=(t_all // tile,),
        in_specs=[pl.BlockSpec((tile * TOP_K,), lambda i, va: (i,), memory_space=pltpu.SMEM),
                  pl.BlockSpec((tile * TILE_ROWS, LANES), lambda i, va: (i, 0))],
        out_specs=pl.BlockSpec(memory_space=pl.ANY),
        scratch_shapes=[pltpu.VMEM((MOE_BLOCK * TILE_ROWS, LANES), F32),
                        pltpu.SemaphoreType.DMA(()), pltpu.SemaphoreType.DMA(())],
    )
    return pl.pallas_call(
        functools.partial(_dispatch_kernel, tile=tile, n_blocks=n_blocks),
        grid_spec=grid_spec,
        out_shape=jax.ShapeDtypeStruct((n_blocks * MOE_BLOCK * TILE_ROWS, LANES), F32),
        compiler_params=pltpu.CompilerParams(dimension_semantics=("arbitrary",)),
        name="moe_dispatch",
    )(block_valid, dest_flat, xm_tiles)


def _expert_kernel(be_ref, used_ref, first_ref, next_ref, slot_ref, x_ref, wgu_ref, bgu_ref, wd_ref,
                   bd_ref, o_ref, wbuf_gu, wbuf_d, s_wgu, s_wd, sem):
    i = pl.program_id(0)

    @pl.when(i >= used_ref[0])
    def _():
        o_ref[...] = jnp.zeros_like(o_ref)

    def weight_copies(e, slot):
        return (pltpu.make_async_copy(wgu_ref.at[e], wbuf_gu.at[slot], sem.at[0, slot]),
                pltpu.make_async_copy(wd_ref.at[e], wbuf_d.at[slot], sem.at[1, slot]))

    @pl.when(i == 0)
    def _():
        for cp in weight_copies(be_ref[0], slot_ref[0]):
            cp.start()

    @pl.when(first_ref[i] == 1)
    def _():
        slot = slot_ref[i]
        for cp in weight_copies(be_ref[i], slot):
            cp.wait()

        @pl.when(next_ref[i] >= 0)
        def _():
            for cp in weight_copies(next_ref[i], 1 - slot):
                cp.start()

        s_wgu[...] = wbuf_gu[slot].astype(BF16)
        s_wd[...] = wbuf_d[slot].astype(BF16)

    @pl.when(i < used_ref[0])
    def _():
        pieces = [x_ref[pl.ds(s, MOE_BLOCK, stride=TILE_ROWS), :] for s in range(TILE_ROWS)]
        x = jnp.concatenate(pieces, axis=1).astype(BF16)
        gu = jnp.dot(x, s_wgu[...], preferred_element_type=F32) + bgu_ref[0]
        g = jnp.minimum(gu[:, :D_FF], SWIGLU_LIMIT)
        up = jnp.clip(gu[:, D_FF:], -SWIGLU_LIMIT, SWIGLU_LIMIT)
        hdn = (up + 1.0) * (g * jax.nn.sigmoid(SWIGLU_ALPHA * g))
        out = jnp.dot(hdn.astype(BF16), s_wd[...], preferred_element_type=F32) + bd_ref[0]
        for s in range(TILE_ROWS):
            o_ref[pl.ds(s, MOE_BLOCK, stride=TILE_ROWS), :] = out[:, s * LANES:(s + 1) * LANES]


def _experts(meta, buf, wgu, bgu, wd, bd, n_blocks):
    blk_rows = MOE_BLOCK * TILE_ROWS
    block_e, n_used = meta[:n_blocks, 0], meta[:1, 2]
    first, nxt, slot = meta[:n_blocks, 3], meta[:n_blocks, 4], meta[:n_blocks, 5]

    def in_row_map(i, be, nu, *_):
        return (jnp.minimum(i, nu[0] - 1), 0)

    def b_map(i, be, *_):
        return (be[i], 0, 0)

    grid_spec = pltpu.PrefetchScalarGridSpec(
        num_scalar_prefetch=5,
        grid=(n_blocks,),
        in_specs=[pl.BlockSpec((blk_rows, LANES), in_row_map),
                  pl.BlockSpec(memory_space=pl.ANY),
                  pl.BlockSpec((1, 1, 2 * D_FF), b_map),
                  pl.BlockSpec(memory_space=pl.ANY),
                  pl.BlockSpec((1, 1, D_MODEL), b_map)],
        out_specs=pl.BlockSpec((blk_rows, LANES), lambda i, *_: (i, 0)),
        scratch_shapes=[pltpu.VMEM((2, D_MODEL, 2 * D_FF), F32), pltpu.VMEM((2, D_FF, D_MODEL), F32),
                        pltpu.VMEM((D_MODEL, 2 * D_FF), BF16), pltpu.VMEM((D_FF, D_MODEL), BF16),
                        pltpu.SemaphoreType.DMA((2, 2))],
    )
    return pl.pallas_call(
        _expert_kernel,
        grid_spec=grid_spec,
        out_shape=jax.ShapeDtypeStruct(buf.shape, F32),
        compiler_params=pltpu.CompilerParams(dimension_semantics=("arbitrary",),
                                             vmem_limit_bytes=VMEM_LIMIT),
        name="moe_experts",
    )(block_e, n_used, first, nxt, slot, buf, wgu, bgu, wd, bd)


def _combine_kernel(dest_ref, next_dest_ref, gate_ref, x1_ref, rows_ref, nf_ref, yp_ref, ys_ref,
                    gbuf, sem, *, batch, n_prompt):
    i = pl.program_id(0)
    tile = batch * CHUNK
    tile_rows = tile * TILE_ROWS

    slot = i % 2

    def gather(dref, slot_id):
        def issue(j, carry):
            for k in range(TOP_K):
                pltpu.make_async_copy(_token_tile(rows_ref, dref[j * TOP_K + k]),
                                      _token_tile(gbuf.at[slot_id, k], j),
                                      sem.at[slot_id]).start(priority=k % 2)
            return carry

        lax.fori_loop(0, tile, issue, 0, unroll=4)

    @pl.when(i == 0)
    def _():
        gather(dest_ref, 0)

    @pl.when(i + 1 < pl.num_programs(0))
    def _():
        gather(next_dest_ref, 1 - slot)

    for k in range(TOP_K):
        pltpu.make_async_copy(rows_ref.at[pl.ds(0, tile_rows)], gbuf.at[slot, k], sem.at[slot]).wait()

    gates = gate_ref[...]
    pieces = []
    for s in range(TILE_ROWS):
        piece = x1_ref[:, s * LANES:(s + 1) * LANES]
        for k in range(TOP_K):
            piece = piece + gates[:, k:k + 1] * gbuf[slot, k, pl.ds(s, tile, stride=TILE_ROWS), :]
        pieces.append(piece)
    y = _rmsnorm(jnp.concatenate(pieces, axis=1), nf_ref[...]).reshape(batch, CHUNK, D_MODEL)

    @pl.when(i < n_prompt)
    def _():
        yp_ref[...] = y

    @pl.when(i >= n_prompt)
    def _():
        ys_ref[...] = y


def _combine(dest_flat, gates, x1, out_rows, norm_final, batch, lp, ls):
    tile = batch * CHUNK
    n_prompt = lp // CHUNK
    n_sample = ls // CHUNK
    last = n_prompt + n_sample - 1
    return pl.pallas_call(
        functools.partial(_combine_kernel, batch=batch, n_prompt=n_prompt),
        grid=(n_prompt + n_sample,),
        in_specs=[pl.BlockSpec((tile * TOP_K,), lambda i: (i,), memory_space=pltpu.SMEM),
                  pl.BlockSpec((tile * TOP_K,), lambda i: (jnp.minimum(i + 1, last),),
                               memory_space=pltpu.SMEM),
                  pl.BlockSpec((tile, TOP_K), lambda i: (i, 0)),
                  pl.BlockSpec((tile, D_MODEL), lambda i: (i, 0)),
                  pl.BlockSpec(memory_space=pl.ANY),
                  pl.BlockSpec((1, D_MODEL), lambda i: (0, 0))],
        out_specs=(pl.BlockSpec((batch, CHUNK, D_MODEL), lambda i: (0, jnp.minimum(i, n_prompt - 1), 0)),
                   pl.BlockSpec((batch, CHUNK, D_MODEL), lambda i: (0, jnp.maximum(i - n_prompt, 0), 0))),
        out_shape=(jax.ShapeDtypeStruct((batch, lp, D_MODEL), F32),
                   jax.ShapeDtypeStruct((batch, ls, D_MODEL), F32)),
        scratch_shapes=[pltpu.VMEM((2, TOP_K, tile * TILE_ROWS, LANES), F32),
                        pltpu.SemaphoreType.DMA((2,))],
        compiler_params=pltpu.CompilerParams(dimension_semantics=("arbitrary",),
                                             vmem_limit_bytes=VMEM_LIMIT),
        name="moe_combine",
    )(dest_flat, dest_flat, gates, x1, out_rows, norm_final)


def _prep_weights(norm_mix, w_in, b_ig, b_fg, s5_a_re, s5_a_im, s5_log_dt, s5_b_re, s5_b_im,
                  s5_c_re, s5_c_im, s5_d, w_s5_glu, ml_gn, w_ml_out, w_out, norm_moe, w_router,
                  b_router):
    w = w_in[0]
    o_ig = D_S5 + 4 * D_ML
    o_gs5 = o_ig + 2 * ML_HEADS
    w_a = w[:, :o_ig].astype(BF16)
    w_b = w[:, o_gs5:].astype(BF16)
    w_g = jnp.concatenate([w[:, o_ig:o_gs5], jnp.zeros((D_MODEL, LANES - 2 * ML_HEADS), F32)],
                          axis=1).astype(BF16)
    gate_bias = jnp.concatenate([b_ig[0], b_fg[0], jnp.zeros((LANES - 2 * ML_HEADS,), F32)])[None]

    ar, ai = s5_a_re[0], s5_a_im[0]
    dt = jnp.exp(s5_log_dt[0])[:, None]
    mag = jnp.exp(dt * ar)
    abar_re = mag * jnp.cos(dt * ai)
    abar_im = mag * jnp.sin(dt * ai)
    den = ar * ar + ai * ai
    fr = ((abar_re - 1.0) * ar + abar_im * ai) / den
    fi = (abar_im * ar - (abar_re - 1.0) * ai) / den
    br, bi = s5_b_re[0], s5_b_im[0]
    bbar_re = fr[..., None] * br - fi[..., None] * bi
    bbar_im = fr[..., None] * bi + fi[..., None] * br
    gh = S5_GROUPS // 2
    same_group = (jnp.arange(gh)[:, None, None, None] == jnp.arange(gh)[None, None, :, None])

    def blockdiag_in(bb):
        t = bb.reshape(2, gh, S5_STATE, S5_GROUP).transpose(0, 1, 3, 2)
        full = jnp.where(same_group[None], t[:, :, :, None, :], 0.0)
        return full.reshape(2, gh * S5_GROUP, gh * S5_STATE).astype(BF16)

    def blockdiag_out(cc):
        t = cc.reshape(2, gh, S5_GROUP, S5_STATE).transpose(0, 1, 3, 2)
        full = jnp.where(same_group[None], t[:, :, :, None, :], 0.0)
        return full.reshape(2, gh * S5_STATE, gh * S5_GROUP)

    cb = jnp.concatenate([blockdiag_out(s5_c_re[0]), -blockdiag_out(s5_c_im[0])], axis=1).astype(BF16)
    w_router_p = jnp.concatenate(
        [w_router[0], jnp.zeros((D_MODEL, LANES - N_EXPERTS), F32)], axis=1).astype(BF16)
    b_router_p = jnp.concatenate([b_router[0], jnp.zeros((LANES - N_EXPERTS,), F32)])[None]
    return dict(
        norm_mix=norm_mix[0][None], w_in_a=w_a, w_in_b=w_b, w_in_g=w_g, gate_bias=gate_bias,
        abar_re=abar_re.reshape(1, S5_CH), abar_im=abar_im.reshape(1, S5_CH),
        bd_re=blockdiag_in(bbar_re), bd_im=blockdiag_in(bbar_im), cb=cb,
        d_skip=s5_d[0][None], w_glu=w_s5_glu[0].astype(BF16), ml_gn=ml_gn[0][None],
        w_ml_out=w_ml_out[0].astype(BF16), w_out=w_out[0].astype(BF16),
        norm_moe=norm_moe[0][None], w_router=w_router_p, b_router=b_router_p)


def kernel(x_prompt, x_sample, state_s5_re, state_s5_im, state_ml_C, state_ml_n, state_ml_m, norm_mix, w_in, b_ig, b_fg, s5_a_re, s5_a_im, s5_log_dt, s5_b_re, s5_b_im, s5_c_re, s5_c_im, s5_d, w_s5_glu, ml_gn, w_ml_out, w_out, norm_moe, w_router, b_router, w_gate_up, b_gate_up, w_down, b_down, norm_final):
    w = _prep_weights(norm_mix, w_in, b_ig, b_fg, s5_a_re, s5_a_im, s5_log_dt, s5_b_re, s5_b_im,
                      s5_c_re, s5_c_im, s5_d, w_s5_glu, ml_gn, w_ml_out, w_out, norm_moe,
                      w_router, b_router)
    batch, lp, _ = x_prompt.shape
    ls = x_sample.shape[1]
    bh = batch * ML_HEADS
    cx0 = jnp.concatenate(
        [jnp.swapaxes(state_ml_C[0].reshape(bh, ML_HEAD_DIM, ML_HEAD_DIM), 1, 2),
         jnp.broadcast_to(state_ml_n[0].reshape(bh, ML_HEAD_DIM, 1), (bh, ML_HEAD_DIM, ML_HEAD_DIM))],
        axis=2)
    x1, xm, idx, gates, hr, hi, cx, m = _mixer(
        x_prompt, x_sample,
        state_s5_re[0].reshape(batch, S5_CH), state_s5_im[0].reshape(batch, S5_CH), cx0,
        jnp.broadcast_to(state_ml_m[0].reshape(bh, 1), (bh, LANES)), w)
    c = jnp.swapaxes(cx[:, :, :, :ML_HEAD_DIM], 2, 3)
    n = cx[:, :, :, ML_HEAD_DIM]

    tile = batch * CHUNK
    t_all = idx.shape[0]
    n_blocks = -(-(t_all * TOP_K + N_EXPERTS * (MOE_BLOCK - 1)) // MOE_BLOCK)
    n_blocks_pad = -(-n_blocks // SUBLANES) * SUBLANES
    dest, meta = _plan(idx, tile, n_blocks_pad)
    dest_flat = dest.reshape(t_all * TOP_K)
    buf = _dispatch(meta[:n_blocks, 1], dest_flat, xm, tile, n_blocks)
    out_rows = _experts(meta, buf, w_gate_up[0], b_gate_up[0][:, None, :], w_down[0],
                        b_down[0][:, None, :], n_blocks)
    y_prompt, y_sample = _combine(dest_flat, gates, x1, out_rows, norm_final[None], batch, lp, ls)

    def states(p):
        return (hr[p].reshape(1, batch, S5_GROUPS, S5_STATE), hi[p].reshape(1, batch, S5_GROUPS, S5_STATE),
                c[p].reshape(1, batch, ML_HEADS, ML_HEAD_DIM, ML_HEAD_DIM),
                n[p].reshape(1, batch, ML_HEADS, ML_HEAD_DIM), m[p, :, 0].reshape(1, batch, ML_HEADS))

    return (y_prompt, y_sample) + states(0) + states(1)
```

```python
import functools

import jax
import jax.numpy as jnp
from jax import lax
from jax.experimental import pallas as pl
from jax.experimental.pallas import tpu as pltpu

F32 = jnp.float32
BF16 = jnp.bfloat16

D_MODEL = 1024
CHUNK = 64
EPS = 1e-6
D_S5 = 512
S5_GROUP = 16
S5_GROUPS = D_S5 // S5_GROUP
S5_STATE = 64
S5_CH = S5_GROUPS * S5_STATE
ML_HEADS = 4
ML_HEAD_DIM = 128
D_ML = ML_HEADS * ML_HEAD_DIM
N_EXPERTS = 32
TOP_K = 4
D_FF = 1024
SWIGLU_LIMIT = 7.0
SWIGLU_ALPHA = 1.702

LANES = 128
SUBLANES = 8
TILE_ROWS = D_MODEL // LANES

OFF_U = 0
OFF_Q = OFF_U + D_S5
OFF_K = OFF_Q + D_ML
OFF_V = OFF_K + D_ML
OFF_O = OFF_V + D_ML
OFF_GS5 = OFF_O + D_ML
OFF_GML = OFF_GS5 + D_MODEL
OFF_GATE = OFF_GML + D_MODEL
W_COLS = OFF_GATE + LANES

S5_HALF_IN = D_S5 // 2
S5_HALF_CH = S5_CH // 2
S5_HALF_TILES = S5_HALF_CH // LANES
S5_SCAN_TILES = 4
ML_GROUP = 2
MOE_BLOCK = 512
VMEM_LIMIT = 60 * 1024 * 1024


def _rmsnorm(x, w):
    return x * lax.rsqrt(jnp.mean(x * x, axis=-1, keepdims=True) + EPS) * w


def _const_spec(shape):
    nd = len(shape)
    return pl.BlockSpec(shape, lambda *_: (0,) * nd, pipeline_mode=pl.Buffered(1))


def _mixer_kernel(xp_ref, xs_ref, h0r_ref, h0i_ref, cx0_ref, m0_ref,
                  nmix_ref, wina_ref, winb_ref, wing_ref, gbias_ref, abr_ref, abi_ref, bdr_ref, bdi_ref, cb_ref,
                  dskip_ref, glu_ref, gn_ref, wml_ref, wout_ref, nmoe_ref, wr_ref, br_ref,
                  x1_ref, xm_ref, idx_ref, gate_ref, cnt_ref, hr_ref, hi_ref, cx_ref, m_ref,
                  s_bur, s_bui, s_q, s_k, s_v, s_o, s_col, s_gated, s_cx, sem, *, batch, n_prompt):
    i = pl.program_id(0)
    rows = batch * CHUNK

    @pl.when(i == 0)
    def _():
        hr_ref[...] = jnp.zeros_like(hr_ref)
        hi_ref[...] = jnp.zeros_like(hi_ref)
        m_ref[...] = jnp.zeros_like(m_ref)
        s_cx[...] = jnp.zeros_like(s_cx)
        cnt_ref[...] = jnp.zeros_like(cnt_ref)

    @pl.when(i == n_prompt)
    def _():
        hr_ref[0] = h0r_ref[...]
        hi_ref[0] = h0i_ref[...]
        m_ref[0] = m0_ref[...]
        load = pltpu.make_async_copy(cx0_ref, s_cx, sem)
        load.start()
        load.wait()

    @pl.when(i < n_prompt)
    def _():
        x1_ref[...] = xp_ref[...].reshape(rows, D_MODEL)

    @pl.when(i >= n_prompt)
    def _():
        x1_ref[...] = xs_ref[...].reshape(rows, D_MODEL)

    xn = _rmsnorm(x1_ref[...], nmix_ref[...]).astype(BF16)

    def proj(off, width):
        if off >= OFF_GATE:
            w_cols = wing_ref[...]
        elif off >= OFF_GS5:
            w_cols = winb_ref[:, off - OFF_GS5:off - OFF_GS5 + width]
        else:
            w_cols = wina_ref[:, off:off + width]
        return jnp.dot(xn, w_cols, preferred_element_type=F32)

    gates = proj(OFF_GATE, LANES) + gbias_ref[...]
    lane_g = lax.broadcasted_iota(jnp.int32, (rows, LANES), 1)
    gg = jnp.where(lane_g < ML_HEADS, gates, jax.nn.log_sigmoid(gates))
    gt8 = gg.T[0:SUBLANES, :]
    pos = lax.broadcasted_iota(jnp.int32, (SUBLANES, rows), 1) % CHUNK
    cum = gt8
    shift = 1
    while shift < CHUNK:
        cum = cum + jnp.where(pos >= shift, pltpu.roll(cum, shift, axis=1), 0.0)
        shift *= 2
    g8 = gt8 - pltpu.roll(cum, ML_HEADS, axis=0)
    mx8 = g8
    shift = 1
    while shift < CHUNK:
        mx8 = jnp.maximum(mx8, jnp.where(pos >= shift, pltpu.roll(mx8, shift, axis=1), -jnp.inf))
        shift *= 2
    sub = lax.broadcasted_iota(jnp.int32, (SUBLANES, rows), 0)
    top8 = jnp.where(sub < ML_HEADS, g8, cum)
    s_col[...] = jnp.concatenate(
        [top8, mx8, jnp.zeros((LANES - 2 * SUBLANES, rows), F32)], axis=0).T
    s_q[...] = (proj(OFF_Q, D_ML) * (ML_HEAD_DIM ** -0.5)).astype(BF16)
    s_k[...] = proj(OFF_K, D_ML).astype(BF16)
    s_v[...] = proj(OFF_V, D_ML).astype(BF16)
    s_o[...] = proj(OFF_O, D_ML)

    tri = (lax.broadcasted_iota(jnp.int32, (CHUNK, CHUNK), 0)
           >= lax.broadcasted_iota(jnp.int32, (CHUNK, CHUNK), 1))[None]
    bdims = ((0,), (0,))

    def rowsl(b):
        return slice(b * CHUNK, (b + 1) * CHUNK)

    def headsl(h):
        return slice(h * ML_HEAD_DIM, (h + 1) * ML_HEAD_DIM)

    for g0 in range(0, batch, ML_GROUP):
        items = [(b, h) for b in range(g0, g0 + ML_GROUP) for h in range(ML_HEADS)]
        def col(j, items=items):
            return jnp.stack([jnp.broadcast_to(s_col[rowsl(b), j + h:j + h + 1], (CHUNK, LANES))
                              for b, h in items])

        g_c, b_c, mx_c = col(0), col(ML_HEADS), col(2 * ML_HEADS)
        g_r = jnp.stack([g8[h:h + 1, rowsl(b)] for b, h in items])
        m_prev = jnp.stack([m_ref[0, b * ML_HEADS + h:b * ML_HEADS + h + 1, :] for b, h in items])
        q3 = jnp.stack([s_q[rowsl(b), headsl(h)] for b, h in items])
        k3 = jnp.stack([s_k[rowsl(b), headsl(h)] for b, h in items])
        v3 = jnp.stack([s_v[rowsl(b), headsl(h)] for b, h in items])
        cx = jnp.stack([s_cx[b * ML_HEADS + h] for b, h in items])

        big_m = jnp.maximum(m_prev, mx_c)
        p = jnp.exp(jnp.where(tri, g_r - big_m[:, :, :CHUNK], -jnp.inf))
        w_inter = jnp.exp(m_prev - big_m)
        s = lax.dot_general(q3, k3, (((2,), (2,)), bdims), preferred_element_type=F32) * p
        cqx = lax.dot_general(q3, cx.astype(BF16), (((2,), (1,)), bdims), preferred_element_type=F32)
        num = (lax.dot_general(s.astype(BF16), v3, (((2,), (1,)), bdims), preferred_element_type=F32)
               + w_inter * cqx[:, :, :ML_HEAD_DIM])
        den_dot = jnp.sum(s, axis=2, keepdims=True) + w_inter * cqx[:, :, ML_HEAD_DIM:]
        m_t = b_c + big_m
        hout = num / jnp.maximum(jnp.abs(den_dot), jnp.exp(-m_t))

        m_last = big_m[:, CHUNK - 1:CHUNK, :]
        w_end = jnp.exp(g_c - m_last)
        decay = jnp.exp(m_prev - m_last)
        wvx = jnp.concatenate([w_end * v3.astype(F32), w_end], axis=2).astype(BF16)
        k_t = jnp.stack([s_k[rowsl(b), headsl(h)].astype(F32).T.astype(BF16) for b, h in items])
        cx_new = (jnp.concatenate([decay, decay], axis=2) * cx
                  + lax.dot_general(k_t, wvx, (((2,), (1,)), bdims), preferred_element_type=F32))
        m_new = m_t[:, CHUNK - 1:CHUNK, :]

        hc = hout - jnp.mean(hout, axis=2, keepdims=True)
        hn = hc * lax.rsqrt(jnp.mean(hc * hc, axis=2, keepdims=True) + EPS)
        for n, (b, h) in enumerate(items):
            bh = b * ML_HEADS + h
            s_cx[bh] = cx_new[n]
            m_ref[0, bh:bh + 1, :] = m_new[n]
            s_gated[rowsl(b), headsl(h)] = (jax.nn.sigmoid(s_o[rowsl(b), headsl(h)])
                                            * (hn[n] * gn_ref[:, headsl(h)])).astype(BF16)

    u = proj(OFF_U, D_S5)
    ub = u.astype(BF16)
    ys = []
    merge = []
    for k in range(2):
        uk = ub[:, k * S5_HALF_IN:(k + 1) * S5_HALF_IN]
        bur = jnp.dot(uk, bdr_ref[k], preferred_element_type=F32)
        bui = jnp.dot(uk, bdi_ref[k], preferred_element_type=F32)
        for j in range(S5_HALF_TILES):
            for b in range(batch):
                dst = pl.ds(b, CHUNK, stride=batch)
                src = slice(b * CHUNK, (b + 1) * CHUNK)
                s_bur[j, dst, :] = bur[src, j * LANES:(j + 1) * LANES]
                s_bui[j, dst, :] = bui[src, j * LANES:(j + 1) * LANES]

        if k == 0:
            merge.append(jax.nn.sigmoid(proj(OFF_GML, D_MODEL)) * jnp.dot(
                s_gated[...], wml_ref[...], preferred_element_type=F32))
        else:
            merge.append(jax.nn.sigmoid(proj(OFF_GS5, D_MODEL)))

        for c in range(S5_HALF_TILES // S5_SCAN_TILES):
            tiles = range(c * S5_SCAN_TILES, (c + 1) * S5_SCAN_TILES)
            lanes = [slice(k * S5_HALF_CH + j * LANES, k * S5_HALF_CH + (j + 1) * LANES) for j in tiles]
            ar = [jnp.broadcast_to(abr_ref[:, ls], (batch, LANES)) for ls in lanes]
            ai = [jnp.broadcast_to(abi_ref[:, ls], (batch, LANES)) for ls in lanes]

            def step(t, carry, tiles=tiles, ar=ar, ai=ai):
                slab = pl.ds(pl.multiple_of(t * batch, batch), batch)
                out = []
                for n, j in enumerate(tiles):
                    hr, hi = carry[n]
                    nr = ar[n] * hr - ai[n] * hi + s_bur[j, slab, :]
                    ni = ar[n] * hi + ai[n] * hr + s_bui[j, slab, :]
                    s_bur[j, slab, :] = nr
                    s_bui[j, slab, :] = ni
                    out.append((nr, ni))
                return tuple(out)

            init = tuple((hr_ref[0, :, ls], hi_ref[0, :, ls]) for ls in lanes)
            fin = lax.fori_loop(0, CHUNK, step, init, unroll=True)
            for n, ls in enumerate(lanes):
                hr_ref[0, :, ls] = fin[n][0]
                hi_ref[0, :, ls] = fin[n][1]

        def stream_major(ref):
            return jnp.concatenate(
                [jnp.concatenate([ref[j, pl.ds(b, CHUNK, stride=batch), :]
                                  for j in range(S5_HALF_TILES)], axis=1) for b in range(batch)], axis=0)

        yk = jnp.dot(stream_major(s_bur).astype(BF16), cb_ref[k, :S5_HALF_CH, :],
                     preferred_element_type=F32)
        yk = yk + jnp.dot(stream_major(s_bui).astype(BF16), cb_ref[k, S5_HALF_CH:, :],
                          preferred_element_type=F32)
        ys.append(yk)
    y = jnp.concatenate(ys, axis=1) + dskip_ref[...] * u
    glu = jnp.dot(jax.nn.gelu(y).astype(BF16), glu_ref[...], preferred_element_type=F32)
    y_s5 = glu[:, :D_MODEL] * jax.nn.sigmoid(glu[:, D_MODEL:])
    mix = merge[0] + merge[1] * y_s5

    x1 = x1_ref[...] + jnp.dot(mix.astype(BF16), wout_ref[...], preferred_element_type=F32)
    x1_ref[...] = x1

    xm = _rmsnorm(x1, nmoe_ref[...])
    logits = jnp.dot(xm.astype(BF16), wr_ref[...], preferred_element_type=F32) + br_ref[...]
    lane = lax.broadcasted_iota(jnp.int32, (rows, LANES), 1)
    logits = jnp.where(lane < N_EXPERTS, logits, -jnp.inf)
    vals, idxs = [], []
    for _ in range(TOP_K):
        mx = jnp.max(logits, axis=1, keepdims=True)
        am = jnp.min(jnp.where(logits == mx, lane, LANES), axis=1, keepdims=True)
        vals.append(mx)
        idxs.append(am)
        logits = jnp.where(lane == am, -jnp.inf, logits)
    exps = [jnp.exp(v - vals[0]) for v in vals]
    esum = exps[0] + exps[1] + exps[2] + exps[3]
    idx_w = jnp.zeros((rows, LANES), jnp.int32)
    gate_w = jnp.zeros((rows, LANES), F32)
    for k in range(TOP_K):
        idx_w = jnp.where(lane == k, idxs[k], idx_w)
        gate_w = jnp.where(lane == k, exps[k] / esum, gate_w)
    idx_ref[...] = idx_w[:, :TOP_K]
    gate_ref[...] = gate_w[:, :TOP_K]
    chosen = jnp.zeros((rows, LANES), F32)
    for k in range(TOP_K):
        chosen = chosen + (lane == idxs[k]).astype(F32)
    cnt_ref[...] += jnp.sum(chosen, axis=0, keepdims=True)

    for s in range(TILE_ROWS):
        xm_ref[pl.ds(s, rows, stride=TILE_ROWS), :] = xm[:, s * LANES:(s + 1) * LANES]

    for phase_id, final_step in ((0, n_prompt - 1), (1, pl.num_programs(0) - 1)):
        @pl.when(i == final_step)
        def _(phase_id=phase_id):
            store = pltpu.make_async_copy(s_cx, cx_ref.at[phase_id], sem)
            store.start()
            store.wait()


def _mixer(x_prompt, x_sample, h0r, h0i, cx0, m0, w):
    batch, lp, _ = x_prompt.shape
    assert x_sample.shape[0] == batch and lp % CHUNK == 0 and x_sample.shape[1] % CHUNK == 0
    assert batch % ML_GROUP == 0
    n_prompt = lp // CHUNK
    n_sample = x_sample.shape[1] // CHUNK
    nblk = n_prompt + n_sample
    rows = batch * CHUNK
    t_all = nblk * rows
    bh = batch * ML_HEADS

    def phase(i):
        return jnp.minimum(i // n_prompt, 1)

    state_specs = [
        _const_spec((batch, S5_CH)), _const_spec((batch, S5_CH)),
        pl.BlockSpec(memory_space=pl.ANY),
        _const_spec((bh, LANES)),
    ]
    weights = (w['norm_mix'], w['w_in_a'], w['w_in_b'], w['w_in_g'], w['gate_bias'], w['abar_re'], w['abar_im'], w['bd_re'],
               w['bd_im'], w['cb'], w['d_skip'], w['w_glu'], w['ml_gn'], w['w_ml_out'], w['w_out'],
               w['norm_moe'], w['w_router'], w['b_router'])
    in_specs = ([pl.BlockSpec((batch, CHUNK, D_MODEL), lambda i: (0, jnp.minimum(i, n_prompt - 1), 0)),
                 pl.BlockSpec((batch, CHUNK, D_MODEL), lambda i: (0, jnp.maximum(i - n_prompt, 0), 0))]
                + state_specs + [_const_spec(a.shape) for a in weights])
    out_shape = (
        jax.ShapeDtypeStruct((t_all, D_MODEL), F32),
        jax.ShapeDtypeStruct((t_all * TILE_ROWS, LANES), F32),
        jax.ShapeDtypeStruct((t_all, TOP_K), jnp.int32),
        jax.ShapeDtypeStruct((t_all, TOP_K), F32),
        jax.ShapeDtypeStruct((1, LANES), F32),
        jax.ShapeDtypeStruct((2, batch, S5_CH), F32),
        jax.ShapeDtypeStruct((2, batch, S5_CH), F32),
        jax.ShapeDtypeStruct((2, bh, ML_HEAD_DIM, 2 * ML_HEAD_DIM), F32),
        jax.ShapeDtypeStruct((2, bh, LANES), F32),
    )
    out_specs = (
        pl.BlockSpec((rows, D_MODEL), lambda i: (i, 0)),
        pl.BlockSpec((rows * TILE_ROWS, LANES), lambda i: (i, 0)),
        pl.BlockSpec((rows, TOP_K), lambda i: (i, 0)),
        pl.BlockSpec((rows, TOP_K), lambda i: (i, 0)),
        pl.BlockSpec((1, LANES), lambda i: (0, 0)),
        pl.BlockSpec((1, batch, S5_CH), lambda i: (phase(i), 0, 0)),
        pl.BlockSpec((1, batch, S5_CH), lambda i: (phase(i), 0, 0)),
        pl.BlockSpec(memory_space=pl.ANY),
        pl.BlockSpec((1, bh, LANES), lambda i: (phase(i), 0, 0)),
    )
    scratch = [
        pltpu.VMEM((S5_HALF_TILES, rows, LANES), F32), pltpu.VMEM((S5_HALF_TILES, rows, LANES), F32),
        pltpu.VMEM((rows, D_ML), BF16), pltpu.VMEM((rows, D_ML), BF16), pltpu.VMEM((rows, D_ML), BF16),
        pltpu.VMEM((rows, D_ML), F32),
        pltpu.VMEM((rows, LANES), F32),
        pltpu.VMEM((rows, D_ML), BF16),
        pltpu.VMEM((bh, ML_HEAD_DIM, 2 * ML_HEAD_DIM), F32),
        pltpu.SemaphoreType.DMA(()),
    ]
    return pl.pallas_call(
        functools.partial(_mixer_kernel, batch=batch, n_prompt=n_prompt),
        grid=(nblk,),
        in_specs=in_specs,
        out_specs=out_specs,
        out_shape=out_shape,
        scratch_shapes=scratch,
        compiler_params=pltpu.CompilerParams(dimension_semantics=("arbitrary",),
                                             vmem_limit_bytes=VMEM_LIMIT),
        name="mixer",
    )(x_prompt, x_sample, h0r, h0i, cx0, m0, *weights)


def _plan_kernel(cnt_ref, idx_ref, dest_ref, meta_ref, s_run, s_pstart, *, tile, n_blocks_pad):
    i = pl.program_id(0)
    lane = lax.broadcasted_iota(jnp.int32, (tile, LANES), 1)
    idx = idx_ref[...]
    onehots = [(lane == idx[:, k:k + 1]).astype(F32) for k in range(TOP_K)]
    mask = onehots[0] + onehots[1] + onehots[2] + onehots[3]

    @pl.when(i == 0)
    def _():
        cnt = cnt_ref[...]
        nblk = jnp.floor((cnt + (MOE_BLOCK - 1)) * (1.0 / MOE_BLOCK))
        lane1 = lax.broadcasted_iota(jnp.int32, (1, LANES), 1)
        pend = nblk
        shift = 1
        while shift < LANES:
            pend = pend + jnp.where(lane1 >= shift, pltpu.roll(pend, shift, axis=1), 0.0)
            shift *= 2
        s_pstart[...] = (pend - nblk) * MOE_BLOCK
        s_run[...] = jnp.zeros_like(s_run)
        blk = lax.broadcasted_iota(jnp.int32, (n_blocks_pad, LANES), 0).astype(F32)
        lane_b = lax.broadcasted_iota(jnp.int32, (n_blocks_pad, LANES), 1)
        lane_f = lane_b.astype(F32)
        is_e = lane_b < N_EXPERTS
        done = jnp.logical_and(is_e, pend <= blk)
        e_of = jnp.minimum(jnp.sum(done.astype(F32), axis=1, keepdims=True), N_EXPERTS - 1.0)
        mine = lane_f == e_of
        blk_in_e = jnp.sum(jnp.where(mine, blk - (pend - nblk), 0.0), axis=1, keepdims=True)
        cnt_e = jnp.sum(jnp.where(mine, cnt, 0.0), axis=1, keepdims=True)
        valid = jnp.clip(cnt_e - blk_in_e * MOE_BLOCK, 0.0, float(MOE_BLOCK))
        used = jnp.sum(jnp.where(lane1 == N_EXPERTS - 1, pend, 0.0), axis=1, keepdims=True)
        owns = jnp.logical_and(is_e, nblk > 0.0)
        first = jnp.logical_and(blk_in_e == 0.0, blk[:, :1] < used).astype(F32)
        later = jnp.logical_and(owns, lane_f > e_of)
        nxt = jnp.min(jnp.where(later, lane_f, float(LANES)), axis=1, keepdims=True)
        nxt = jnp.where(nxt < float(LANES), nxt, -1.0)
        run = jnp.sum(jnp.logical_and(owns, lane_f < e_of).astype(F32), axis=1, keepdims=True)
        parity = run - 2.0 * jnp.floor(run * 0.5)
        meta = jnp.zeros((n_blocks_pad, LANES), F32)
        for col, val in enumerate((e_of, valid, used, first, nxt, parity)):
            meta = jnp.where(lane_b == col, val, meta)
        meta_ref[...] = meta.astype(jnp.int32)

    r = lax.broadcasted_iota(jnp.int32, (tile, tile), 0)
    c = lax.broadcasted_iota(jnp.int32, (tile, tile), 1)
    lower = (c < r).astype(BF16)
    before = jnp.dot(lower, mask.astype(BF16), preferred_element_type=F32)
    base = before + s_run[...] + s_pstart[...]
    dest = jnp.zeros((tile, LANES), F32)
    for k in range(TOP_K):
        dk = jnp.sum(onehots[k] * base, axis=1, keepdims=True)
        dest = jnp.where(lane == k, dk, dest)
    dest_ref[...] = dest[:, :TOP_K].astype(jnp.int32)
    s_run[...] += jnp.sum(mask, axis=0, keepdims=True)


def _plan(counts, idx_all, tile, n_blocks_pad):
    t_all = idx_all.shape[0]
    return pl.pallas_call(
        functools.partial(_plan_kernel, tile=tile, n_blocks_pad=n_blocks_pad),
        grid=(t_all // tile,),
        in_specs=[pl.BlockSpec((1, LANES), lambda i: (0, 0)),
                  pl.BlockSpec((tile, TOP_K), lambda i: (i, 0))],
        out_specs=(pl.BlockSpec((tile, TOP_K), lambda i: (i, 0)),
                   pl.BlockSpec((n_blocks_pad, LANES), lambda i: (0, 0))),
        out_shape=(jax.ShapeDtypeStruct((t_all, TOP_K), jnp.int32),
                   jax.ShapeDtypeStruct((n_blocks_pad, LANES), jnp.int32)),
        scratch_shapes=[pltpu.VMEM((1, LANES), F32), pltpu.VMEM((1, LANES), F32)],
        compiler_params=pltpu.CompilerParams(dimension_semantics=("arbitrary",)),
        name="moe_plan",
    )(counts, idx_all)


def _token_tile(ref, t):
    return ref.at[pl.ds(pl.multiple_of(t * TILE_ROWS, TILE_ROWS), TILE_ROWS)]


def _dispatch_kernel(valid_ref, dest_ref, xm_ref, buf_ref, zeros, sem, zsem, *, tile, n_blocks):
    blk_rows = MOE_BLOCK * TILE_ROWS

    @pl.when(pl.program_id(0) == 0)
    def _():
        zeros[...] = jnp.zeros_like(zeros)

        def block_copy(i):
            return pltpu.make_async_copy(
                zeros, buf_ref.at[pl.ds(pl.multiple_of(i * blk_rows, blk_rows), blk_rows)], zsem)

        def fill(i, carry):
            @pl.when(valid_ref[i] < MOE_BLOCK)
            def _():
                block_copy(i).start()
            return carry

        def drain(i, carry):
            @pl.when(valid_ref[i] < MOE_BLOCK)
            def _():
                block_copy(i).wait()
            return carry

        lax.fori_loop(0, n_blocks, fill, 0)
        lax.fori_loop(0, n_blocks, drain, 0)

    def issue(j, carry):
        src = _token_tile(xm_ref, j)
        for k in range(TOP_K):
            pltpu.make_async_copy(src, _token_tile(buf_ref, dest_ref[j * TOP_K + k]), sem).start(
                priority=k % 2)
        return carry

    lax.fori_loop(0, tile, issue, 0, unroll=4)
    n_rows = tile * TOP_K * TILE_ROWS
    pltpu.make_async_copy(buf_ref.at[pl.ds(0, n_rows)], buf_ref.at[pl.ds(0, n_rows)], sem).wait()


def _dispatch(block_valid, dest_flat, xm_tiles, tile, n_blocks):
    t_all = xm_tiles.shape[0] // TILE_ROWS
    grid_spec = pltpu.PrefetchScalarGridSpec(
        num_scalar_prefetch=1,
        grid=(t_all // tile,),
        in_specs=[pl.BlockSpec((tile * TOP_K,), lambda i, va: (i,), memory_space=pltpu.SMEM),
                  pl.BlockSpec((tile * TILE_ROWS, LANES), lambda i, va: (i, 0))],
        out_specs=pl.BlockSpec(memory_space=pl.ANY),
        scratch_shapes=[pltpu.VMEM((MOE_BLOCK * TILE_ROWS, LANES), F32),
                        pltpu.SemaphoreType.DMA(()), pltpu.SemaphoreType.DMA(())],
    )
    return pl.pallas_call(
        functools.partial(_dispatch_kernel, tile=tile, n_blocks=n_blocks),
        grid_spec=grid_spec,
        out_shape=jax.ShapeDtypeStruct((n_blocks * MOE_BLOCK * TILE_ROWS, LANES), F32),
        compiler_params=pltpu.CompilerParams(dimension_semantics=("arbitrary",)),
        name="moe_dispatch",
    )(block_valid, dest_flat, xm_tiles)


def _expert_kernel(be_ref, used_ref, first_ref, next_ref, slot_ref, valid_ref, x_ref, wgu_ref, bgu_ref,
                   wd_ref, bd_ref, o_ref, wbuf_gu, wbuf_d, s_wgu, s_wd, sem):
    i = pl.program_id(0)

    @pl.when(i >= used_ref[0])
    def _():
        o_ref[...] = jnp.zeros_like(o_ref)

    def weight_copies(e, slot):
        return (pltpu.make_async_copy(wgu_ref.at[e], wbuf_gu.at[slot], sem.at[0, slot]),
                pltpu.make_async_copy(wd_ref.at[e], wbuf_d.at[slot], sem.at[1, slot]))

    @pl.when(i == 0)
    def _():
        for cp in weight_copies(be_ref[0], slot_ref[0]):
            cp.start()

    @pl.when(first_ref[i] == 1)
    def _():
        slot = slot_ref[i]
        for cp in weight_copies(be_ref[i], slot):
            cp.wait()

        @pl.when(next_ref[i] >= 0)
        def _():
            for cp in weight_copies(next_ref[i], 1 - slot):
                cp.start()

        s_wgu[...] = wbuf_gu[slot].astype(BF16)
        s_wd[...] = wbuf_d[slot].astype(BF16)

    def ffn(nrows):
        pieces = [x_ref[pl.ds(s, nrows, stride=TILE_ROWS), :] for s in range(TILE_ROWS)]
        x = jnp.concatenate(pieces, axis=1).astype(BF16)
        gu = jnp.dot(x, s_wgu[...], preferred_element_type=F32) + bgu_ref[0]
        g = jnp.minimum(gu[:, :D_FF], SWIGLU_LIMIT)
        up = jnp.clip(gu[:, D_FF:], -SWIGLU_LIMIT, SWIGLU_LIMIT)
        hdn = (up + 1.0) * (g * jax.nn.sigmoid(SWIGLU_ALPHA * g))
        out = jnp.dot(hdn.astype(BF16), s_wd[...], preferred_element_type=F32) + bd_ref[0]
        for s in range(TILE_ROWS):
            o_ref[pl.ds(s, nrows, stride=TILE_ROWS), :] = out[:, s * LANES:(s + 1) * LANES]
        if nrows < MOE_BLOCK:
            o_ref[nrows * TILE_ROWS:, :] = jnp.zeros(((MOE_BLOCK - nrows) * TILE_ROWS, LANES), F32)

    in_use = i < used_ref[0]
    half_empty = valid_ref[i] <= MOE_BLOCK // 2

    @pl.when(jnp.logical_and(in_use, jnp.logical_not(half_empty)))
    def _():
        ffn(MOE_BLOCK)

    @pl.when(jnp.logical_and(in_use, half_empty))
    def _():
        ffn(MOE_BLOCK // 2)


def _experts(meta, buf, wgu, bgu, wd, bd, n_blocks):
    blk_rows = MOE_BLOCK * TILE_ROWS
    block_e, n_used = meta[:n_blocks, 0], meta[:1, 2]
    first, nxt, slot = meta[:n_blocks, 3], meta[:n_blocks, 4], meta[:n_blocks, 5]
    valid = meta[:n_blocks, 1]

    def in_row_map(i, be, nu, *_):
        return (jnp.minimum(i, nu[0] - 1), 0)

    def b_map(i, be, *_):
        return (be[i], 0, 0)

    grid_spec = pltpu.PrefetchScalarGridSpec(
        num_scalar_prefetch=6,
        grid=(n_blocks,),
        in_specs=[pl.BlockSpec((blk_rows, LANES), in_row_map),
                  pl.BlockSpec(memory_space=pl.ANY),
                  pl.BlockSpec((1, 1, 2 * D_FF), b_map),
                  pl.BlockSpec(memory_space=pl.ANY),
                  pl.BlockSpec((1, 1, D_MODEL), b_map)],
        out_specs=pl.BlockSpec((blk_rows, LANES), lambda i, *_: (i, 0)),
        scratch_shapes=[pltpu.VMEM((2, D_MODEL, 2 * D_FF), F32), pltpu.VMEM((2, D_FF, D_MODEL), F32),
                        pltpu.VMEM((D_MODEL, 2 * D_FF), BF16), pltpu.VMEM((D_FF, D_MODEL), BF16),
                        pltpu.SemaphoreType.DMA((2, 2))],
    )
    return pl.pallas_call(
        _expert_kernel,
        grid_spec=grid_spec,
        out_shape=jax.ShapeDtypeStruct(buf.shape, F32),
        compiler_params=pltpu.CompilerParams(dimension_semantics=("arbitrary",),
                                             vmem_limit_bytes=VMEM_LIMIT),
        name="moe_experts",
    )(block_e, n_used, first, nxt, slot, valid, buf, wgu, bgu, wd, bd)


def _combine_kernel(dest_ref, next_dest_ref, gate_ref, x1_ref, rows_ref, nf_ref, yp_ref, ys_ref,
                    gbuf, sem, *, batch, n_prompt):
    i = pl.program_id(0)
    tile = batch * CHUNK
    tile_rows = tile * TILE_ROWS

    slot = i % 2

    def gather(dref, slot_id):
        def issue(j, carry):
            for k in range(TOP_K):
                pltpu.make_async_copy(_token_tile(rows_ref, dref[j * TOP_K + k]),
                                      _token_tile(gbuf.at[slot_id, k], j),
                                      sem.at[slot_id]).start(priority=k % 2)
            return carry

        lax.fori_loop(0, tile, issue, 0, unroll=4)

    @pl.when(i == 0)
    def _():
        gather(dest_ref, 0)

    @pl.when(i + 1 < pl.num_programs(0))
    def _():
        gather(next_dest_ref, 1 - slot)

    for k in range(TOP_K):
        pltpu.make_async_copy(rows_ref.at[pl.ds(0, tile_rows)], gbuf.at[slot, k], sem.at[slot]).wait()

    gates = gate_ref[...]
    pieces = []
    for s in range(TILE_ROWS):
        piece = x1_ref[:, s * LANES:(s + 1) * LANES]
        for k in range(TOP_K):
            piece = piece + gates[:, k:k + 1] * gbuf[slot, k, pl.ds(s, tile, stride=TILE_ROWS), :]
        pieces.append(piece)
    y = _rmsnorm(jnp.concatenate(pieces, axis=1), nf_ref[...]).reshape(batch, CHUNK, D_MODEL)

    @pl.when(i < n_prompt)
    def _():
        yp_ref[...] = y

    @pl.when(i >= n_prompt)
    def _():
        ys_ref[...] = y


def _combine(dest_flat, gates, x1, out_rows, norm_final, batch, lp, ls):
    tile = batch * CHUNK
    n_prompt = lp // CHUNK
    n_sample = ls // CHUNK
    last = n_prompt + n_sample - 1
    return pl.pallas_call(
        functools.partial(_combine_kernel, batch=batch, n_prompt=n_prompt),
        grid=(n_prompt + n_sample,),
        in_specs=[pl.BlockSpec((tile * TOP_K,), lambda i: (i,), memory_space=pltpu.SMEM),
                  pl.BlockSpec((tile * TOP_K,), lambda i: (jnp.minimum(i + 1, last),),
                               memory_space=pltpu.SMEM),
                  pl.BlockSpec((tile, TOP_K), lambda i: (i, 0)),
                  pl.BlockSpec((tile, D_MODEL), lambda i: (i, 0)),
                  pl.BlockSpec(memory_space=pl.ANY),
                  pl.BlockSpec((1, D_MODEL), lambda i: (0, 0))],
        out_specs=(pl.BlockSpec((batch, CHUNK, D_MODEL), lambda i: (0, jnp.minimum(i, n_prompt - 1), 0)),
                   pl.BlockSpec((batch, CHUNK, D_MODEL), lambda i: (0, jnp.maximum(i - n_prompt, 0), 0))),
        out_shape=(jax.ShapeDtypeStruct((batch, lp, D_MODEL), F32),
                   jax.ShapeDtypeStruct((batch, ls, D_MODEL), F32)),
        scratch_shapes=[pltpu.VMEM((2, TOP_K, tile * TILE_ROWS, LANES), F32),
                        pltpu.SemaphoreType.DMA((2,))],
        compiler_params=pltpu.CompilerParams(dimension_semantics=("arbitrary",),
                                             vmem_limit_bytes=VMEM_LIMIT),
        name="moe_combine",
    )(dest_flat, dest_flat, gates, x1, out_rows, norm_final)


def _prep_weights(norm_mix, w_in, b_ig, b_fg, s5_a_re, s5_a_im, s5_log_dt, s5_b_re, s5_b_im,
                  s5_c_re, s5_c_im, s5_d, w_s5_glu, ml_gn, w_ml_out, w_out, norm_moe, w_router,
                  b_router):
    w = w_in[0]
    o_ig = D_S5 + 4 * D_ML
    o_gs5 = o_ig + 2 * ML_HEADS
    w_a = w[:, :o_ig].astype(BF16)
    w_b = w[:, o_gs5:].astype(BF16)
    w_g = jnp.concatenate([w[:, o_ig:o_gs5], jnp.zeros((D_MODEL, LANES - 2 * ML_HEADS), F32)],
                          axis=1).astype(BF16)
    gate_bias = jnp.concatenate([b_ig[0], b_fg[0], jnp.zeros((LANES - 2 * ML_HEADS,), F32)])[None]

    ar, ai = s5_a_re[0], s5_a_im[0]
    dt = jnp.exp(s5_log_dt[0])[:, None]
    mag = jnp.exp(dt * ar)
    abar_re = mag * jnp.cos(dt * ai)
    abar_im = mag * jnp.sin(dt * ai)
    den = ar * ar + ai * ai
    fr = ((abar_re - 1.0) * ar + abar_im * ai) / den
    fi = (abar_im * ar - (abar_re - 1.0) * ai) / den
    br, bi = s5_b_re[0], s5_b_im[0]
    bbar_re = fr[..., None] * br - fi[..., None] * bi
    bbar_im = fr[..., None] * bi + fi[..., None] * br
    gh = S5_GROUPS // 2
    same_group = (jnp.arange(gh)[:, None, None, None] == jnp.arange(gh)[None, None, :, None])

    def blockdiag_in(bb):
        t = bb.reshape(2, gh, S5_STATE, S5_GROUP).transpose(0, 1, 3, 2)
        full = jnp.where(same_group[None], t[:, :, :, None, :], 0.0)
        return full.reshape(2, gh * S5_GROUP, gh * S5_STATE).astype(BF16)

    def blockdiag_out(cc):
        t = cc.reshape(2, gh, S5_GROUP, S5_STATE).transpose(0, 1, 3, 2)
        full = jnp.where(same_group[None], t[:, :, :, None, :], 0.0)
        return full.reshape(2, gh * S5_STATE, gh * S5_GROUP)

    cb = jnp.concatenate([blockdiag_out(s5_c_re[0]), -blockdiag_out(s5_c_im[0])], axis=1).astype(BF16)
    w_router_p = jnp.concatenate(
        [w_router[0], jnp.zeros((D_MODEL, LANES - N_EXPERTS), F32)], axis=1).astype(BF16)
    b_router_p = jnp.concatenate([b_router[0], jnp.zeros((LANES - N_EXPERTS,), F32)])[None]
    return dict(
        norm_mix=norm_mix[0][None], w_in_a=w_a, w_in_b=w_b, w_in_g=w_g, gate_bias=gate_bias,
        abar_re=abar_re.reshape(1, S5_CH), abar_im=abar_im.reshape(1, S5_CH),
        bd_re=blockdiag_in(bbar_re), bd_im=blockdiag_in(bbar_im), cb=cb,
        d_skip=s5_d[0][None], w_glu=w_s5_glu[0].astype(BF16), ml_gn=ml_gn[0][None],
        w_ml_out=w_ml_out[0].astype(BF16), w_out=w_out[0].astype(BF16),
        norm_moe=norm_moe[0][None], w_router=w_router_p, b_router=b_router_p)


def kernel(x_prompt, x_sample, state_s5_re, state_s5_im, state_ml_C, state_ml_n, state_ml_m, norm_mix, w_in, b_ig, b_fg, s5_a_re, s5_a_im, s5_log_dt, s5_b_re, s5_b_im, s5_c_re, s5_c_im, s5_d, w_s5_glu, ml_gn, w_ml_out, w_out, norm_moe, w_router, b_router, w_gate_up, b_gate_up, w_down, b_down, norm_final):
    w = _prep_weights(norm_mix, w_in, b_ig, b_fg, s5_a_re, s5_a_im, s5_log_dt, s5_b_re, s5_b_im,
                      s5_c_re, s5_c_im, s5_d, w_s5_glu, ml_gn, w_ml_out, w_out, norm_moe,
                      w_router, b_router)
    batch, lp, _ = x_prompt.shape
    ls = x_sample.shape[1]
    bh = batch * ML_HEADS
    cx0 = jnp.concatenate(
        [jnp.swapaxes(state_ml_C[0].reshape(bh, ML_HEAD_DIM, ML_HEAD_DIM), 1, 2),
         jnp.broadcast_to(state_ml_n[0].reshape(bh, ML_HEAD_DIM, 1), (bh, ML_HEAD_DIM, ML_HEAD_DIM))],
        axis=2)
    x1, xm, idx, gates, counts, hr, hi, cx, m = _mixer(
        x_prompt, x_sample,
        state_s5_re[0].reshape(batch, S5_CH), state_s5_im[0].reshape(batch, S5_CH), cx0,
        jnp.broadcast_to(state_ml_m[0].reshape(bh, 1), (bh, LANES)), w)
    c = jnp.swapaxes(cx[:, :, :, :ML_HEAD_DIM], 2, 3)
    n = cx[:, :, :, ML_HEAD_DIM]

    tile = batch * CHUNK
    t_all = idx.shape[0]
    n_blocks = -(-(t_all * TOP_K + N_EXPERTS * (MOE_BLOCK - 1)) // MOE_BLOCK)
    n_blocks_pad = -(-n_blocks // SUBLANES) * SUBLANES
    dest, meta = _plan(counts, idx, tile, n_blocks_pad)
    dest_flat = dest.reshape(t_all * TOP_K)
    buf = _dispatch(meta[:n_blocks, 1], dest_flat, xm, tile, n_blocks)
    out_rows = _experts(meta, buf, w_gate_up[0], b_gate_up[0][:, None, :], w_down[0],
                        b_down[0][:, None, :], n_blocks)
    y_prompt, y_sample = _combine(dest_flat, gates, x1, out_rows, norm_final[None], batch, lp, ls)

    def states(p):
        return (hr[p].reshape(1, batch, S5_GROUPS, S5_STATE), hi[p].reshape(1, batch, S5_GROUPS, S5_STATE),
                c[p].reshape(1, batch, ML_HEADS, ML_HEAD_DIM, ML_HEAD_DIM),
                n[p].reshape(1, batch, ML_HEADS, ML_HEAD_DIM), m[p, :, 0].reshape(1, batch, ML_HEADS))

    return (y_prompt, y_sample) + states(0) + states(1)
```

```python
import functools

import jax
import jax.numpy as jnp
from jax import lax
from jax.experimental import pallas as pl
from jax.experimental.pallas import tpu as pltpu

F32 = jnp.float32
BF16 = jnp.bfloat16

D_MODEL = 1024
CHUNK = 64
EPS = 1e-6
D_S5 = 512
S5_GROUP = 16
S5_GROUPS = D_S5 // S5_GROUP
S5_STATE = 64
S5_CH = S5_GROUPS * S5_STATE
ML_HEADS = 4
ML_HEAD_DIM = 128
D_ML = ML_HEADS * ML_HEAD_DIM
N_EXPERTS = 32
TOP_K = 4
D_FF = 1024
SWIGLU_LIMIT = 7.0
SWIGLU_ALPHA = 1.702

LANES = 128
SUBLANES = 8
TILE_ROWS = D_MODEL // LANES

OFF_U = 0
OFF_Q = OFF_U + D_S5
OFF_K = OFF_Q + D_ML
OFF_V = OFF_K + D_ML
OFF_O = OFF_V + D_ML
OFF_GS5 = OFF_O + D_ML
OFF_GML = OFF_GS5 + D_MODEL
OFF_GATE = OFF_GML + D_MODEL
W_COLS = OFF_GATE + LANES

S5_HALF_IN = D_S5 // 2
S5_HALF_CH = S5_CH // 2
S5_HALF_TILES = S5_HALF_CH // LANES
S5_SCAN_TILES = 4
ML_GROUP = 2
MOE_BLOCK = 512
VMEM_LIMIT = 60 * 1024 * 1024


def _rmsnorm(x, w):
    return x * lax.rsqrt(jnp.mean(x * x, axis=-1, keepdims=True) + EPS) * w


def _const_spec(shape):
    nd = len(shape)
    return pl.BlockSpec(shape, lambda *_: (0,) * nd, pipeline_mode=pl.Buffered(1))


def _mixer_kernel(xp_ref, xs_ref, h0r_ref, h0i_ref, cx0_ref, m0_ref,
                  nmix_ref, wina_ref, winb_ref, wing_ref, gbias_ref, abr_ref, abi_ref, bdr_ref, bdi_ref, cb_ref,
                  dskip_ref, glu_ref, gn_ref, wml_ref, wout_ref, nmoe_ref, wr_ref, br_ref,
                  x1_ref, xm_ref, idx_ref, gate_ref, cnt_ref, hr_ref, hi_ref, cx_ref, m_ref,
                  s_bur, s_bui, s_tm, s_q, s_k, s_v, s_o, s_col, s_gated, s_cx, sem, *, batch, n_prompt):
    i = pl.program_id(0)
    rows = batch * CHUNK

    @pl.when(i == 0)
    def _():
        hr_ref[...] = jnp.zeros_like(hr_ref)
        hi_ref[...] = jnp.zeros_like(hi_ref)
        m_ref[...] = jnp.zeros_like(m_ref)
        s_cx[...] = jnp.zeros_like(s_cx)
        cnt_ref[...] = jnp.zeros_like(cnt_ref)

    @pl.when(i == n_prompt)
    def _():
        hr_ref[0] = h0r_ref[...]
        hi_ref[0] = h0i_ref[...]
        m_ref[0] = m0_ref[...]
        load = pltpu.make_async_copy(cx0_ref, s_cx, sem)
        load.start()
        load.wait()

    @pl.when(i < n_prompt)
    def _():
        x1_ref[...] = xp_ref[...].reshape(rows, D_MODEL)

    @pl.when(i >= n_prompt)
    def _():
        x1_ref[...] = xs_ref[...].reshape(rows, D_MODEL)

    xn = _rmsnorm(x1_ref[...], nmix_ref[...]).astype(BF16)

    def proj(off, width):
        if off >= OFF_GATE:
            w_cols = wing_ref[...]
        elif off >= OFF_GS5:
            w_cols = winb_ref[:, off - OFF_GS5:off - OFF_GS5 + width]
        else:
            w_cols = wina_ref[:, off:off + width]
        return jnp.dot(xn, w_cols, preferred_element_type=F32)

    gates = proj(OFF_GATE, LANES) + gbias_ref[...]
    lane_g = lax.broadcasted_iota(jnp.int32, (rows, LANES), 1)
    gg = jnp.where(lane_g < ML_HEADS, gates, jax.nn.log_sigmoid(gates))
    gt8 = gg.T[0:SUBLANES, :]
    pos = lax.broadcasted_iota(jnp.int32, (SUBLANES, rows), 1) % CHUNK
    cum = gt8
    shift = 1
    while shift < CHUNK:
        cum = cum + jnp.where(pos >= shift, pltpu.roll(cum, shift, axis=1), 0.0)
        shift *= 2
    g8 = gt8 - pltpu.roll(cum, ML_HEADS, axis=0)
    mx8 = g8
    shift = 1
    while shift < CHUNK:
        mx8 = jnp.maximum(mx8, jnp.where(pos >= shift, pltpu.roll(mx8, shift, axis=1), -jnp.inf))
        shift *= 2
    sub = lax.broadcasted_iota(jnp.int32, (SUBLANES, rows), 0)
    top8 = jnp.where(sub < ML_HEADS, g8, cum)
    s_col[...] = jnp.concatenate(
        [top8, mx8, jnp.zeros((LANES - 2 * SUBLANES, rows), F32)], axis=0).T
    s_q[...] = (proj(OFF_Q, D_ML) * (ML_HEAD_DIM ** -0.5)).astype(BF16)
    s_k[...] = proj(OFF_K, D_ML).astype(BF16)
    s_v[...] = proj(OFF_V, D_ML).astype(BF16)
    s_o[...] = proj(OFF_O, D_ML)

    tri = (lax.broadcasted_iota(jnp.int32, (CHUNK, CHUNK), 0)
           >= lax.broadcasted_iota(jnp.int32, (CHUNK, CHUNK), 1))[None]
    bdims = ((0,), (0,))

    def rowsl(b):
        return slice(b * CHUNK, (b + 1) * CHUNK)

    def headsl(h):
        return slice(h * ML_HEAD_DIM, (h + 1) * ML_HEAD_DIM)

    for g0 in range(0, batch, ML_GROUP):
        items = [(b, h) for b in range(g0, g0 + ML_GROUP) for h in range(ML_HEADS)]
        def col(j, items=items):
            return jnp.stack([jnp.broadcast_to(s_col[rowsl(b), j + h:j + h + 1], (CHUNK, LANES))
                              for b, h in items])

        g_c, b_c, mx_c = col(0), col(ML_HEADS), col(2 * ML_HEADS)
        g_r = jnp.stack([g8[h:h + 1, rowsl(b)] for b, h in items])
        m_prev = jnp.stack([m_ref[0, b * ML_HEADS + h:b * ML_HEADS + h + 1, :] for b, h in items])
        q3 = jnp.stack([s_q[rowsl(b), headsl(h)] for b, h in items])
        k3 = jnp.stack([s_k[rowsl(b), headsl(h)] for b, h in items])
        v3 = jnp.stack([s_v[rowsl(b), headsl(h)] for b, h in items])
        cx = jnp.stack([s_cx[b * ML_HEADS + h] for b, h in items])

        big_m = jnp.maximum(m_prev, mx_c)
        p = jnp.exp(jnp.where(tri, g_r - big_m[:, :, :CHUNK], -jnp.inf))
        w_inter = jnp.exp(m_prev - big_m)
        s = lax.dot_general(q3, k3, (((2,), (2,)), bdims), preferred_element_type=F32) * p
        cqx = lax.dot_general(q3, cx.astype(BF16), (((2,), (1,)), bdims), preferred_element_type=F32)
        num = (lax.dot_general(s.astype(BF16), v3, (((2,), (1,)), bdims), preferred_element_type=F32)
               + w_inter * cqx[:, :, :ML_HEAD_DIM])
        den_dot = jnp.sum(s, axis=2, keepdims=True) + w_inter * cqx[:, :, ML_HEAD_DIM:]
        m_t = b_c + big_m
        hout = num / jnp.maximum(jnp.abs(den_dot), jnp.exp(-m_t))

        m_last = big_m[:, CHUNK - 1:CHUNK, :]
        w_end = jnp.exp(g_c - m_last)
        decay = jnp.exp(m_prev - m_last)
        wvx = jnp.concatenate([w_end * v3.astype(F32), w_end], axis=2).astype(BF16)
        k_t = jnp.stack([s_k[rowsl(b), headsl(h)].astype(F32).T.astype(BF16) for b, h in items])
        cx_new = (jnp.concatenate([decay, decay], axis=2) * cx
                  + lax.dot_general(k_t, wvx, (((2,), (1,)), bdims), preferred_element_type=F32))
        m_new = m_t[:, CHUNK - 1:CHUNK, :]

        hc = hout - jnp.mean(hout, axis=2, keepdims=True)
        hn = hc * lax.rsqrt(jnp.mean(hc * hc, axis=2, keepdims=True) + EPS)
        for n, (b, h) in enumerate(items):
            bh = b * ML_HEADS + h
            s_cx[bh] = cx_new[n]
            m_ref[0, bh:bh + 1, :] = m_new[n]
            s_gated[rowsl(b), headsl(h)] = (jax.nn.sigmoid(s_o[rowsl(b), headsl(h)])
                                            * (hn[n] * gn_ref[:, headsl(h)])).astype(BF16)

    u = proj(OFF_U, D_S5)
    in_tiles = D_S5 // LANES
    for j in range(in_tiles):
        for b in range(batch):
            s_tm[j, pl.ds(b, CHUNK, stride=batch), :] = u[b * CHUNK:(b + 1) * CHUNK,
                                                          j * LANES:(j + 1) * LANES]
    ub = jnp.concatenate([s_tm[j] for j in range(in_tiles)], axis=1).astype(BF16)
    ys = []
    merge = []
    for k in range(2):
        uk = ub[:, k * S5_HALF_IN:(k + 1) * S5_HALF_IN]
        bur = jnp.dot(uk, bdr_ref[k], preferred_element_type=F32)
        bui = jnp.dot(uk, bdi_ref[k], preferred_element_type=F32)
        for j in range(S5_HALF_TILES):
            s_bur[j] = bur[:, j * LANES:(j + 1) * LANES]
            s_bui[j] = bui[:, j * LANES:(j + 1) * LANES]

        if k == 0:
            merge.append(jax.nn.sigmoid(proj(OFF_GML, D_MODEL)) * jnp.dot(
                s_gated[...], wml_ref[...], preferred_element_type=F32))
        else:
            merge.append(jax.nn.sigmoid(proj(OFF_GS5, D_MODEL)))

        for c in range(S5_HALF_TILES // S5_SCAN_TILES):
            tiles = range(c * S5_SCAN_TILES, (c + 1) * S5_SCAN_TILES)
            lanes = [slice(k * S5_HALF_CH + j * LANES, k * S5_HALF_CH + (j + 1) * LANES) for j in tiles]
            ar = [jnp.broadcast_to(abr_ref[:, ls], (batch, LANES)) for ls in lanes]
            ai = [jnp.broadcast_to(abi_ref[:, ls], (batch, LANES)) for ls in lanes]

            def step(t, carry, tiles=tiles, ar=ar, ai=ai):
                slab = pl.ds(pl.multiple_of(t * batch, batch), batch)
                out = []
                for n, j in enumerate(tiles):
                    hr, hi = carry[n]
                    nr = ar[n] * hr - ai[n] * hi + s_bur[j, slab, :]
                    ni = ar[n] * hi + ai[n] * hr + s_bui[j, slab, :]
                    s_bur[j, slab, :] = nr
                    s_bui[j, slab, :] = ni
                    out.append((nr, ni))
                return tuple(out)

            init = tuple((hr_ref[0, :, ls], hi_ref[0, :, ls]) for ls in lanes)
            fin = lax.fori_loop(0, CHUNK, step, init, unroll=True)
            for n, ls in enumerate(lanes):
                hr_ref[0, :, ls] = fin[n][0]
                hi_ref[0, :, ls] = fin[n][1]

        def states(ref):
            return jnp.concatenate([ref[j] for j in range(S5_HALF_TILES)], axis=1).astype(BF16)

        yk = jnp.dot(states(s_bur), cb_ref[k, :S5_HALF_CH, :], preferred_element_type=F32)
        yk = yk + jnp.dot(states(s_bui), cb_ref[k, S5_HALF_CH:, :], preferred_element_type=F32)
        ys.append(yk)
    y_tm = jnp.concatenate(ys, axis=1)
    for j in range(in_tiles):
        s_tm[j] = y_tm[:, j * LANES:(j + 1) * LANES]
    y = jnp.concatenate(
        [jnp.concatenate([s_tm[j, pl.ds(b, CHUNK, stride=batch), :] for j in range(in_tiles)], axis=1)
         for b in range(batch)], axis=0) + dskip_ref[...] * u
    glu = jnp.dot(jax.nn.gelu(y).astype(BF16), glu_ref[...], preferred_element_type=F32)
    y_s5 = glu[:, :D_MODEL] * jax.nn.sigmoid(glu[:, D_MODEL:])
    mix = merge[0] + merge[1] * y_s5

    x1 = x1_ref[...] + jnp.dot(mix.astype(BF16), wout_ref[...], preferred_element_type=F32)
    x1_ref[...] = x1

    xm = _rmsnorm(x1, nmoe_ref[...])
    logits = jnp.dot(xm.astype(BF16), wr_ref[...], preferred_element_type=F32) + br_ref[...]
    lane = lax.broadcasted_iota(jnp.int32, (rows, LANES), 1)
    logits = jnp.where(lane < N_EXPERTS, logits, -jnp.inf)
    vals, idxs = [], []
    for _ in range(TOP_K):
        mx = jnp.max(logits, axis=1, keepdims=True)
        am = jnp.min(jnp.where(logits == mx, lane, LANES), axis=1, keepdims=True)
        vals.append(mx)
        idxs.append(am)
        logits = jnp.where(lane == am, -jnp.inf, logits)
    exps = [jnp.exp(v - vals[0]) for v in vals]
    esum = exps[0] + exps[1] + exps[2] + exps[3]
    idx_w = jnp.zeros((rows, LANES), jnp.int32)
    gate_w = jnp.zeros((rows, LANES), F32)
    for k in range(TOP_K):
        idx_w = jnp.where(lane == k, idxs[k], idx_w)
        gate_w = jnp.where(lane == k, exps[k] / esum, gate_w)
    idx_ref[...] = idx_w[:, :TOP_K]
    gate_ref[...] = gate_w[:, :TOP_K]
    chosen = jnp.zeros((rows, LANES), F32)
    for k in range(TOP_K):
        chosen = chosen + (lane == idxs[k]).astype(F32)
    cnt_ref[...] += jnp.sum(chosen, axis=0, keepdims=True)

    for s in range(TILE_ROWS):
        xm_ref[pl.ds(s, rows, stride=TILE_ROWS), :] = xm[:, s * LANES:(s + 1) * LANES]

    for phase_id, final_step in ((0, n_prompt - 1), (1, pl.num_programs(0) - 1)):
        @pl.when(i == final_step)
        def _(phase_id=phase_id):
            store = pltpu.make_async_copy(s_cx, cx_ref.at[phase_id], sem)
            store.start()
            store.wait()


def _mixer(x_prompt, x_sample, h0r, h0i, cx0, m0, w):
    batch, lp, _ = x_prompt.shape
    assert x_sample.shape[0] == batch and lp % CHUNK == 0 and x_sample.shape[1] % CHUNK == 0
    assert batch % ML_GROUP == 0
    n_prompt = lp // CHUNK
    n_sample = x_sample.shape[1] // CHUNK
    nblk = n_prompt + n_sample
    rows = batch * CHUNK
    t_all = nblk * rows
    bh = batch * ML_HEADS

    def phase(i):
        return jnp.minimum(i // n_prompt, 1)

    state_specs = [
        _const_spec((batch, S5_CH)), _const_spec((batch, S5_CH)),
        pl.BlockSpec(memory_space=pl.ANY),
        _const_spec((bh, LANES)),
    ]
    weights = (w['norm_mix'], w['w_in_a'], w['w_in_b'], w['w_in_g'], w['gate_bias'], w['abar_re'], w['abar_im'], w['bd_re'],
               w['bd_im'], w['cb'], w['d_skip'], w['w_glu'], w['ml_gn'], w['w_ml_out'], w['w_out'],
               w['norm_moe'], w['w_router'], w['b_router'])
    in_specs = ([pl.BlockSpec((batch, CHUNK, D_MODEL), lambda i: (0, jnp.minimum(i, n_prompt - 1), 0)),
                 pl.BlockSpec((batch, CHUNK, D_MODEL), lambda i: (0, jnp.maximum(i - n_prompt, 0), 0))]
                + state_specs + [_const_spec(a.shape) for a in weights])
    out_shape = (
        jax.ShapeDtypeStruct((t_all, D_MODEL), F32),
        jax.ShapeDtypeStruct((t_all * TILE_ROWS, LANES), F32),
        jax.ShapeDtypeStruct((t_all, TOP_K), jnp.int32),
        jax.ShapeDtypeStruct((t_all, TOP_K), F32),
        jax.ShapeDtypeStruct((1, LANES), F32),
        jax.ShapeDtypeStruct((2, batch, S5_CH), F32),
        jax.ShapeDtypeStruct((2, batch, S5_CH), F32),
        jax.ShapeDtypeStruct((2, bh, ML_HEAD_DIM, 2 * ML_HEAD_DIM), F32),
        jax.ShapeDtypeStruct((2, bh, LANES), F32),
    )
    out_specs = (
        pl.BlockSpec((rows, D_MODEL), lambda i: (i, 0)),
        pl.BlockSpec((rows * TILE_ROWS, LANES), lambda i: (i, 0)),
        pl.BlockSpec((rows, TOP_K), lambda i: (i, 0)),
        pl.BlockSpec((rows, TOP_K), lambda i: (i, 0)),
        pl.BlockSpec((1, LANES), lambda i: (0, 0)),
        pl.BlockSpec((1, batch, S5_CH), lambda i: (phase(i), 0, 0)),
        pl.BlockSpec((1, batch, S5_CH), lambda i: (phase(i), 0, 0)),
        pl.BlockSpec(memory_space=pl.ANY),
        pl.BlockSpec((1, bh, LANES), lambda i: (phase(i), 0, 0)),
    )
    scratch = [
        pltpu.VMEM((S5_HALF_TILES, rows, LANES), F32), pltpu.VMEM((S5_HALF_TILES, rows, LANES), F32),
        pltpu.VMEM((D_S5 // LANES, rows, LANES), F32),
        pltpu.VMEM((rows, D_ML), BF16), pltpu.VMEM((rows, D_ML), BF16), pltpu.VMEM((rows, D_ML), BF16),
        pltpu.VMEM((rows, D_ML), F32),
        pltpu.VMEM((rows, LANES), F32),
        pltpu.VMEM((rows, D_ML), BF16),
        pltpu.VMEM((bh, ML_HEAD_DIM, 2 * ML_HEAD_DIM), F32),
        pltpu.SemaphoreType.DMA(()),
    ]
    return pl.pallas_call(
        functools.partial(_mixer_kernel, batch=batch, n_prompt=n_prompt),
        grid=(nblk,),
        in_specs=in_specs,
        out_specs=out_specs,
        out_shape=out_shape,
        scratch_shapes=scratch,
        compiler_params=pltpu.CompilerParams(dimension_semantics=("arbitrary",),
                                             vmem_limit_bytes=VMEM_LIMIT),
        name="mixer",
    )(x_prompt, x_sample, h0r, h0i, cx0, m0, *weights)


def _plan_kernel(cnt_ref, idx_ref, dest_ref, meta_ref, s_run, s_pstart, *, tile, n_blocks_pad):
    i = pl.program_id(0)
    lane = lax.broadcasted_iota(jnp.int32, (tile, LANES), 1)
    idx = idx_ref[...]
    onehots = [(lane == idx[:, k:k + 1]).astype(F32) for k in range(TOP_K)]
    mask = onehots[0] + onehots[1] + onehots[2] + onehots[3]

    @pl.when(i == 0)
    def _():
        cnt = cnt_ref[...]
        nblk = jnp.floor((cnt + (MOE_BLOCK - 1)) * (1.0 / MOE_BLOCK))
        lane1 = lax.broadcasted_iota(jnp.int32, (1, LANES), 1)
        pend = nblk
        shift = 1
        while shift < LANES:
            pend = pend + jnp.where(lane1 >= shift, pltpu.roll(pend, shift, axis=1), 0.0)
            shift *= 2
        s_pstart[...] = (pend - nblk) * MOE_BLOCK
        s_run[...] = jnp.zeros_like(s_run)
        blk = lax.broadcasted_iota(jnp.int32, (n_blocks_pad, LANES), 0).astype(F32)
        lane_b = lax.broadcasted_iota(jnp.int32, (n_blocks_pad, LANES), 1)
        lane_f = lane_b.astype(F32)
        is_e = lane_b < N_EXPERTS
        done = jnp.logical_and(is_e, pend <= blk)
        e_of = jnp.minimum(jnp.sum(done.astype(F32), axis=1, keepdims=True), N_EXPERTS - 1.0)
        mine = lane_f == e_of
        blk_in_e = jnp.sum(jnp.where(mine, blk - (pend - nblk), 0.0), axis=1, keepdims=True)
        cnt_e = jnp.sum(jnp.where(mine, cnt, 0.0), axis=1, keepdims=True)
        valid = jnp.clip(cnt_e - blk_in_e * MOE_BLOCK, 0.0, float(MOE_BLOCK))
        used = jnp.sum(jnp.where(lane1 == N_EXPERTS - 1, pend, 0.0), axis=1, keepdims=True)
        owns = jnp.logical_and(is_e, nblk > 0.0)
        first = jnp.logical_and(blk_in_e == 0.0, blk[:, :1] < used).astype(F32)
        later = jnp.logical_and(owns, lane_f > e_of)
        nxt = jnp.min(jnp.where(later, lane_f, float(LANES)), axis=1, keepdims=True)
        nxt = jnp.where(nxt < float(LANES), nxt, -1.0)
        run = jnp.sum(jnp.logical_and(owns, lane_f < e_of).astype(F32), axis=1, keepdims=True)
        parity = run - 2.0 * jnp.floor(run * 0.5)
        meta = jnp.zeros((n_blocks_pad, LANES), F32)
        for col, val in enumerate((e_of, valid, used, first, nxt, parity)):
            meta = jnp.where(lane_b == col, val, meta)
        meta_ref[...] = meta.astype(jnp.int32)

    r = lax.broadcasted_iota(jnp.int32, (tile, tile), 0)
    c = lax.broadcasted_iota(jnp.int32, (tile, tile), 1)
    lower = (c < r).astype(BF16)
    before = jnp.dot(lower, mask.astype(BF16), preferred_element_type=F32)
    base = before + s_run[...] + s_pstart[...]
    dest = jnp.zeros((tile, LANES), F32)
    for k in range(TOP_K):
        dk = jnp.sum(onehots[k] * base, axis=1, keepdims=True)
        dest = jnp.where(lane == k, dk, dest)
    dest_ref[...] = dest[:, :TOP_K].astype(jnp.int32)
    s_run[...] += jnp.sum(mask, axis=0, keepdims=True)


def _plan(counts, idx_all, tile, n_blocks_pad):
    t_all = idx_all.shape[0]
    return pl.pallas_call(
        functools.partial(_plan_kernel, tile=tile, n_blocks_pad=n_blocks_pad),
        grid=(t_all // tile,),
        in_specs=[pl.BlockSpec((1, LANES), lambda i: (0, 0)),
                  pl.BlockSpec((tile, TOP_K), lambda i: (i, 0))],
        out_specs=(pl.BlockSpec((tile, TOP_K), lambda i: (i, 0)),
                   pl.BlockSpec((n_blocks_pad, LANES), lambda i: (0, 0))),
        out_shape=(jax.ShapeDtypeStruct((t_all, TOP_K), jnp.int32),
                   jax.ShapeDtypeStruct((n_blocks_pad, LANES), jnp.int32)),
        scratch_shapes=[pltpu.VMEM((1, LANES), F32), pltpu.VMEM((1, LANES), F32)],
        compiler_params=pltpu.CompilerParams(dimension_semantics=("arbitrary",)),
        name="moe_plan",
    )(counts, idx_all)


def _token_tile(ref, t):
    return ref.at[pl.ds(pl.multiple_of(t * TILE_ROWS, TILE_ROWS), TILE_ROWS)]


def _dispatch_kernel(valid_ref, dest_ref, xm_ref, buf_ref, zeros, sem, zsem, *, tile, n_blocks):
    blk_rows = MOE_BLOCK * TILE_ROWS

    @pl.when(pl.program_id(0) == 0)
    def _():
        zeros[...] = jnp.zeros_like(zeros)

        def block_copy(i):
            return pltpu.make_async_copy(
                zeros, buf_ref.at[pl.ds(pl.multiple_of(i * blk_rows, blk_rows), blk_rows)], zsem)

        def fill(i, carry):
            @pl.when(valid_ref[i] < MOE_BLOCK)
            def _():
                block_copy(i).start()
            return carry

        def drain(i, carry):
            @pl.when(valid_ref[i] < MOE_BLOCK)
            def _():
                block_copy(i).wait()
            return carry

        lax.fori_loop(0, n_blocks, fill, 0)
        lax.fori_loop(0, n_blocks, drain, 0)

    def issue(j, carry):
        src = _token_tile(xm_ref, j)
        for k in range(TOP_K):
            pltpu.make_async_copy(src, _token_tile(buf_ref, dest_ref[j * TOP_K + k]), sem).start(
                priority=k % 2)
        return carry

    lax.fori_loop(0, tile, issue, 0, unroll=4)
    n_rows = tile * TOP_K * TILE_ROWS
    pltpu.make_async_copy(buf_ref.at[pl.ds(0, n_rows)], buf_ref.at[pl.ds(0, n_rows)], sem).wait()


def _dispatch(block_valid, dest_flat, xm_tiles, tile, n_blocks):
    t_all = xm_tiles.shape[0] // TILE_ROWS
    grid_spec = pltpu.PrefetchScalarGridSpec(
        num_scalar_prefetch=1,
        grid=(t_all // tile,),
        in_specs=[pl.BlockSpec((tile * TOP_K,), lambda i, va: (i,), memory_space=pltpu.SMEM),
                  pl.BlockSpec((tile * TILE_ROWS, LANES), lambda i, va: (i, 0))],
        out_specs=pl.BlockSpec(memory_space=pl.ANY),
        scratch_shapes=[pltpu.VMEM((MOE_BLOCK * TILE_ROWS, LANES), F32),
                        pltpu.SemaphoreType.DMA(()), pltpu.SemaphoreType.DMA(())],
    )
    return pl.pallas_call(
        functools.partial(_dispatch_kernel, tile=tile, n_blocks=n_blocks),
        grid_spec=grid_spec,
        out_shape=jax.ShapeDtypeStruct((n_blocks * MOE_BLOCK * TILE_ROWS, LANES), F32),
        compiler_params=pltpu.CompilerParams(dimension_semantics=("arbitrary",)),
        name="moe_dispatch",
    )(block_valid, dest_flat, xm_tiles)


def _expert_kernel(be_ref, used_ref, first_ref, next_ref, slot_ref, valid_ref, x_ref, wgu_ref, bgu_ref,
                   wd_ref, bd_ref, o_ref, wbuf_gu, wbuf_d, s_wgu, s_wd, sem):
    i = pl.program_id(0)

    @pl.when(i >= used_ref[0])
    def _():
        o_ref[...] = jnp.zeros_like(o_ref)

    def weight_copies(e, slot):
        return (pltpu.make_async_copy(wgu_ref.at[e], wbuf_gu.at[slot], sem.at[0, slot]),
                pltpu.make_async_copy(wd_ref.at[e], wbuf_d.at[slot], sem.at[1, slot]))

    @pl.when(i == 0)
    def _():
        for cp in weight_copies(be_ref[0], slot_ref[0]):
            cp.start()

    @pl.when(first_ref[i] == 1)
    def _():
        slot = slot_ref[i]
        for cp in weight_copies(be_ref[i], slot):
            cp.wait()

        @pl.when(next_ref[i] >= 0)
        def _():
            for cp in weight_copies(next_ref[i], 1 - slot):
                cp.start()

        s_wgu[...] = wbuf_gu[slot].astype(BF16)
        s_wd[...] = wbuf_d[slot].astype(BF16)

    def ffn(nrows):
        pieces = [x_ref[pl.ds(s, nrows, stride=TILE_ROWS), :] for s in range(TILE_ROWS)]
        x = jnp.concatenate(pieces, axis=1).astype(BF16)
        gu = jnp.dot(x, s_wgu[...], preferred_element_type=F32) + bgu_ref[0]
        g = jnp.minimum(gu[:, :D_FF], SWIGLU_LIMIT)
        up = jnp.clip(gu[:, D_FF:], -SWIGLU_LIMIT, SWIGLU_LIMIT)
        hdn = (up + 1.0) * (g * jax.nn.sigmoid(SWIGLU_ALPHA * g))
        out = jnp.dot(hdn.astype(BF16), s_wd[...], preferred_element_type=F32) + bd_ref[0]
        for s in range(TILE_ROWS):
            o_ref[pl.ds(s, nrows, stride=TILE_ROWS), :] = out[:, s * LANES:(s + 1) * LANES]
        if nrows < MOE_BLOCK:
            o_ref[nrows * TILE_ROWS:, :] = jnp.zeros(((MOE_BLOCK - nrows) * TILE_ROWS, LANES), F32)

    in_use = i < used_ref[0]
    half_empty = valid_ref[i] <= MOE_BLOCK // 2

    @pl.when(jnp.logical_and(in_use, jnp.logical_not(half_empty)))
    def _():
        ffn(MOE_BLOCK)

    @pl.when(jnp.logical_and(in_use, half_empty))
    def _():
        ffn(MOE_BLOCK // 2)


def _experts(meta, buf, wgu, bgu, wd, bd, n_blocks):
    blk_rows = MOE_BLOCK * TILE_ROWS
    block_e, n_used = meta[:n_blocks, 0], meta[:1, 2]
    first, nxt, slot = meta[:n_blocks, 3], meta[:n_blocks, 4], meta[:n_blocks, 5]
    valid = meta[:n_blocks, 1]

    def in_row_map(i, be, nu, *_):
        return (jnp.minimum(i, nu[0] - 1), 0)

    def b_map(i, be, *_):
        return (be[i], 0, 0)

    grid_spec = pltpu.PrefetchScalarGridSpec(
        num_scalar_prefetch=6,
        grid=(n_blocks,),
        in_specs=[pl.BlockSpec((blk_rows, LANES), in_row_map),
                  pl.BlockSpec(memory_space=pl.ANY),
                  pl.BlockSpec((1, 1, 2 * D_FF), b_map),
                  pl.BlockSpec(memory_space=pl.ANY),
                  pl.BlockSpec((1, 1, D_MODEL), b_map)],
        out_specs=pl.BlockSpec((blk_rows, LANES), lambda i, *_: (i, 0)),
        scratch_shapes=[pltpu.VMEM((2, D_MODEL, 2 * D_FF), F32), pltpu.VMEM((2, D_FF, D_MODEL), F32),
                        pltpu.VMEM((D_MODEL, 2 * D_FF), BF16), pltpu.VMEM((D_FF, D_MODEL), BF16),
                        pltpu.SemaphoreType.DMA((2, 2))],
    )
    return pl.pallas_call(
        _expert_kernel,
        grid_spec=grid_spec,
        out_shape=jax.ShapeDtypeStruct(buf.shape, F32),
        compiler_params=pltpu.CompilerParams(dimension_semantics=("arbitrary",),
                                             vmem_limit_bytes=VMEM_LIMIT),
        name="moe_experts",
    )(block_e, n_used, first, nxt, slot, valid, buf, wgu, bgu, wd, bd)


def _combine_kernel(dest_ref, next_dest_ref, gate_ref, x1_ref, rows_ref, nf_ref, yp_ref, ys_ref,
                    gbuf, sem, *, batch, n_prompt):
    i = pl.program_id(0)
    tile = batch * CHUNK
    tile_rows = tile * TILE_ROWS

    slot = i % 2

    def gather(dref, slot_id):
        def issue(j, carry):
            for k in range(TOP_K):
                pltpu.make_async_copy(_token_tile(rows_ref, dref[j * TOP_K + k]),
                                      _token_tile(gbuf.at[slot_id, k], j),
                                      sem.at[slot_id]).start(priority=k % 2)
            return carry

        lax.fori_loop(0, tile, issue, 0, unroll=4)

    @pl.when(i == 0)
    def _():
        gather(dest_ref, 0)

    @pl.when(i + 1 < pl.num_programs(0))
    def _():
        gather(next_dest_ref, 1 - slot)

    for k in range(TOP_K):
        pltpu.make_async_copy(rows_ref.at[pl.ds(0, tile_rows)], gbuf.at[slot, k], sem.at[slot]).wait()

    gates = gate_ref[...]
    pieces = []
    for s in range(TILE_ROWS):
        piece = x1_ref[:, s * LANES:(s + 1) * LANES]
        for k in range(TOP_K):
            piece = piece + gates[:, k:k + 1] * gbuf[slot, k, pl.ds(s, tile, stride=TILE_ROWS), :]
        pieces.append(piece)
    y = _rmsnorm(jnp.concatenate(pieces, axis=1), nf_ref[...]).reshape(batch, CHUNK, D_MODEL)

    @pl.when(i < n_prompt)
    def _():
        yp_ref[...] = y

    @pl.when(i >= n_prompt)
    def _():
        ys_ref[...] = y


def _combine(dest_flat, gates, x1, out_rows, norm_final, batch, lp, ls):
    tile = batch * CHUNK
    n_prompt = lp // CHUNK
    n_sample = ls // CHUNK
    last = n_prompt + n_sample - 1
    return pl.pallas_call(
        functools.partial(_combine_kernel, batch=batch, n_prompt=n_prompt),
        grid=(n_prompt + n_sample,),
        in_specs=[pl.BlockSpec((tile * TOP_K,), lambda i: (i,), memory_space=pltpu.SMEM),
                  pl.BlockSpec((tile * TOP_K,), lambda i: (jnp.minimum(i + 1, last),),
                               memory_space=pltpu.SMEM),
                  pl.BlockSpec((tile, TOP_K), lambda i: (i, 0)),
                  pl.BlockSpec((tile, D_MODEL), lambda i: (i, 0)),
                  pl.BlockSpec(memory_space=pl.ANY),
                  pl.BlockSpec((1, D_MODEL), lambda i: (0, 0))],
        out_specs=(pl.BlockSpec((batch, CHUNK, D_MODEL), lambda i: (0, jnp.minimum(i, n_prompt - 1), 0)),
                   pl.BlockSpec((batch, CHUNK, D_MODEL), lambda i: (0, jnp.maximum(i - n_prompt, 0), 0))),
        out_shape=(jax.ShapeDtypeStruct((batch, lp, D_MODEL), F32),
                   jax.ShapeDtypeStruct((batch, ls, D_MODEL), F32)),
        scratch_shapes=[pltpu.VMEM((2, TOP_K, tile * TILE_ROWS, LANES), F32),
                        pltpu.SemaphoreType.DMA((2,))],
        compiler_params=pltpu.CompilerParams(dimension_semantics=("arbitrary",),
                                             vmem_limit_bytes=VMEM_LIMIT),
        name="moe_combine",
    )(dest_flat, dest_flat, gates, x1, out_rows, norm_final)


def _prep_weights(norm_mix, w_in, b_ig, b_fg, s5_a_re, s5_a_im, s5_log_dt, s5_b_re, s5_b_im,
                  s5_c_re, s5_c_im, s5_d, w_s5_glu, ml_gn, w_ml_out, w_out, norm_moe, w_router,
                  b_router):
    w = w_in[0]
    o_ig = D_S5 + 4 * D_ML
    o_gs5 = o_ig + 2 * ML_HEADS
    w_a = w[:, :o_ig].astype(BF16)
    w_b = w[:, o_gs5:].astype(BF16)
    w_g = jnp.concatenate([w[:, o_ig:o_gs5], jnp.zeros((D_MODEL, LANES - 2 * ML_HEADS), F32)],
                          axis=1).astype(BF16)
    gate_bias = jnp.concatenate([b_ig[0], b_fg[0], jnp.zeros((LANES - 2 * ML_HEADS,), F32)])[None]

    ar, ai = s5_a_re[0], s5_a_im[0]
    dt = jnp.exp(s5_log_dt[0])[:, None]
    mag = jnp.exp(dt * ar)
    abar_re = mag * jnp.cos(dt * ai)
    abar_im = mag * jnp.sin(dt * ai)
    den = ar * ar + ai * ai
    fr = ((abar_re - 1.0) * ar + abar_im * ai) / den
    fi = (abar_im * ar - (abar_re - 1.0) * ai) / den
    br, bi = s5_b_re[0], s5_b_im[0]
    bbar_re = fr[..., None] * br - fi[..., None] * bi
    bbar_im = fr[..., None] * bi + fi[..., None] * br
    gh = S5_GROUPS // 2
    same_group = (jnp.arange(gh)[:, None, None, None] == jnp.arange(gh)[None, None, :, None])

    def blockdiag_in(bb):
        t = bb.reshape(2, gh, S5_STATE, S5_GROUP).transpose(0, 1, 3, 2)
        full = jnp.where(same_group[None], t[:, :, :, None, :], 0.0)
        return full.reshape(2, gh * S5_GROUP, gh * S5_STATE).astype(BF16)

    def blockdiag_out(cc):
        t = cc.reshape(2, gh, S5_GROUP, S5_STATE).transpose(0, 1, 3, 2)
        full = jnp.where(same_group[None], t[:, :, :, None, :], 0.0)
        return full.reshape(2, gh * S5_STATE, gh * S5_GROUP)

    cb = jnp.concatenate([blockdiag_out(s5_c_re[0]), -blockdiag_out(s5_c_im[0])], axis=1).astype(BF16)
    w_router_p = jnp.concatenate(
        [w_router[0], jnp.zeros((D_MODEL, LANES - N_EXPERTS), F32)], axis=1).astype(BF16)
    b_router_p = jnp.concatenate([b_router[0], jnp.zeros((LANES - N_EXPERTS,), F32)])[None]
    return dict(
        norm_mix=norm_mix[0][None], w_in_a=w_a, w_in_b=w_b, w_in_g=w_g, gate_bias=gate_bias,
        abar_re=abar_re.reshape(1, S5_CH), abar_im=abar_im.reshape(1, S5_CH),
        bd_re=blockdiag_in(bbar_re), bd_im=blockdiag_in(bbar_im), cb=cb,
        d_skip=s5_d[0][None], w_glu=w_s5_glu[0].astype(BF16), ml_gn=ml_gn[0][None],
        w_ml_out=w_ml_out[0].astype(BF16), w_out=w_out[0].astype(BF16),
        norm_moe=norm_moe[0][None], w_router=w_router_p, b_router=b_router_p)


def kernel(x_prompt, x_sample, state_s5_re, state_s5_im, state_ml_C, state_ml_n, state_ml_m, norm_mix, w_in, b_ig, b_fg, s5_a_re, s5_a_im, s5_log_dt, s5_b_re, s5_b_im, s5_c_re, s5_c_im, s5_d, w_s5_glu, ml_gn, w_ml_out, w_out, norm_moe, w_router, b_router, w_gate_up, b_gate_up, w_down, b_down, norm_final):
    w = _prep_weights(norm_mix, w_in, b_ig, b_fg, s5_a_re, s5_a_im, s5_log_dt, s5_b_re, s5_b_im,
                      s5_c_re, s5_c_im, s5_d, w_s5_glu, ml_gn, w_ml_out, w_out, norm_moe,
                      w_router, b_router)
    batch, lp, _ = x_prompt.shape
    ls = x_sample.shape[1]
    bh = batch * ML_HEADS
    cx0 = jnp.concatenate(
        [jnp.swapaxes(state_ml_C[0].reshape(bh, ML_HEAD_DIM, ML_HEAD_DIM), 1, 2),
         jnp.broadcast_to(state_ml_n[0].reshape(bh, ML_HEAD_DIM, 1), (bh, ML_HEAD_DIM, ML_HEAD_DIM))],
        axis=2)
    x1, xm, idx, gates, counts, hr, hi, cx, m = _mixer(
        x_prompt, x_sample,
        state_s5_re[0].reshape(batch, S5_CH), state_s5_im[0].reshape(batch, S5_CH), cx0,
        jnp.broadcast_to(state_ml_m[0].reshape(bh, 1), (bh, LANES)), w)
    c = jnp.swapaxes(cx[:, :, :, :ML_HEAD_DIM], 2, 3)
    n = cx[:, :, :, ML_HEAD_DIM]

    tile = batch * CHUNK
    t_all = idx.shape[0]
    n_blocks = -(-(t_all * TOP_K + N_EXPERTS * (MOE_BLOCK - 1)) // MOE_BLOCK)
    n_blocks_pad = -(-n_blocks // SUBLANES) * SUBLANES
    dest, meta = _plan(counts, idx, tile, n_blocks_pad)
    dest_flat = dest.reshape(t_all * TOP_K)
    buf = _dispatch(meta[:n_blocks, 1], dest_flat, xm, tile, n_blocks)
    out_rows = _experts(meta, buf, w_gate_up[0], b_gate_up[0][:, None, :], w_down[0],
                        b_down[0][:, None, :], n_blocks)
    y_prompt, y_sample = _combine(dest_flat, gates, x1, out_rows, norm_final[None], batch, lp, ls)

    def states(p):
        return (hr[p].reshape(1, batch, S5_GROUPS, S5_STATE), hi[p].reshape(1, batch, S5_GROUPS, S5_STATE),
                c[p].reshape(1, batch, ML_HEADS, ML_HEAD_DIM, ML_HEAD_DIM),
                n[p].reshape(1, batch, ML_HEADS, ML_HEAD_DIM), m[p, :, 0].reshape(1, batch, ML_HEADS))

    return (y_prompt, y_sample) + states(0) + states(1)
```

```python
import functools

import jax
import jax.numpy as jnp
from jax import lax
from jax.experimental import pallas as pl
from jax.experimental.pallas import tpu as pltpu

F32 = jnp.float32
BF16 = jnp.bfloat16

D_MODEL = 1024
CHUNK = 64
EPS = 1e-6
D_S5 = 512
S5_GROUP = 16
S5_GROUPS = D_S5 // S5_GROUP
S5_STATE = 64
S5_CH = S5_GROUPS * S5_STATE
ML_HEADS = 4
ML_HEAD_DIM = 128
D_ML = ML_HEADS * ML_HEAD_DIM
N_EXPERTS = 32
TOP_K = 4
D_FF = 1024
SWIGLU_LIMIT = 7.0
SWIGLU_ALPHA = 1.702

LANES = 128
SUBLANES = 8
TILE_ROWS = D_MODEL // LANES

OFF_U = 0
OFF_Q = OFF_U + D_S5
OFF_K = OFF_Q + D_ML
OFF_V = OFF_K + D_ML
OFF_O = OFF_V + D_ML
OFF_GS5 = OFF_O + D_ML
OFF_GML = OFF_GS5 + D_MODEL
OFF_GATE = OFF_GML + D_MODEL
W_COLS = OFF_GATE + LANES

S5_HALF_IN = D_S5 // 2
S5_HALF_CH = S5_CH // 2
S5_HALF_TILES = S5_HALF_CH // LANES
S5_SCAN_TILES = 4
ML_GROUP = 2
MOE_BLOCK = 512
VMEM_LIMIT = 60 * 1024 * 1024


def _rmsnorm(x, w):
    return x * lax.rsqrt(jnp.mean(x * x, axis=-1, keepdims=True) + EPS) * w


def _const_spec(shape):
    nd = len(shape)
    return pl.BlockSpec(shape, lambda *_: (0,) * nd, pipeline_mode=pl.Buffered(1))


def _mixer_kernel(xp_ref, xs_ref, h0r_ref, h0i_ref, cx0_ref, m0_ref,
                  nmix_ref, wina_ref, winb_ref, wing_ref, gbias_ref, abr_ref, abi_ref, bdr_ref, bdi_ref, cb_ref,
                  dskip_ref, glu_ref, gn_ref, wml_ref, wout_ref, nmoe_ref, wr_ref, br_ref,
                  x1_ref, xm_ref, idx_ref, gate_ref, cnt_ref, hr_ref, hi_ref, cx_ref, m_ref,
                  s_bur, s_bui, s_tm, s_q, s_k, s_v, s_o, s_col, s_gated, s_cx, sem, *, batch, n_prompt):
    i = pl.program_id(0)
    rows = batch * CHUNK

    @pl.when(i == 0)
    def _():
        hr_ref[...] = jnp.zeros_like(hr_ref)
        hi_ref[...] = jnp.zeros_like(hi_ref)
        m_ref[...] = jnp.zeros_like(m_ref)
        s_cx[...] = jnp.zeros_like(s_cx)
        cnt_ref[...] = jnp.zeros_like(cnt_ref)

    @pl.when(i == n_prompt)
    def _():
        hr_ref[0] = h0r_ref[...]
        hi_ref[0] = h0i_ref[...]
        m_ref[0] = m0_ref[...]
        load = pltpu.make_async_copy(cx0_ref, s_cx, sem)
        load.start()
        load.wait()

    @pl.when(i < n_prompt)
    def _():
        x1_ref[...] = xp_ref[...].reshape(rows, D_MODEL)

    @pl.when(i >= n_prompt)
    def _():
        x1_ref[...] = xs_ref[...].reshape(rows, D_MODEL)

    xn = _rmsnorm(x1_ref[...], nmix_ref[...]).astype(BF16)

    def proj(off, width):
        if off >= OFF_GATE:
            w_cols = wing_ref[...]
        elif off >= OFF_GS5:
            w_cols = winb_ref[:, off - OFF_GS5:off - OFF_GS5 + width]
        else:
            w_cols = wina_ref[:, off:off + width]
        return jnp.dot(xn, w_cols, preferred_element_type=F32)

    gates = proj(OFF_GATE, LANES) + gbias_ref[...]
    lane_g = lax.broadcasted_iota(jnp.int32, (rows, LANES), 1)
    gg = jnp.where(lane_g < ML_HEADS, gates, jax.nn.log_sigmoid(gates))
    gt8 = gg.T[0:SUBLANES, :]
    pos = lax.broadcasted_iota(jnp.int32, (SUBLANES, rows), 1) % CHUNK
    cum = gt8
    shift = 1
    while shift < CHUNK:
        cum = cum + jnp.where(pos >= shift, pltpu.roll(cum, shift, axis=1), 0.0)
        shift *= 2
    g8 = gt8 - pltpu.roll(cum, ML_HEADS, axis=0)
    mx8 = g8
    shift = 1
    while shift < CHUNK:
        mx8 = jnp.maximum(mx8, jnp.where(pos >= shift, pltpu.roll(mx8, shift, axis=1), -jnp.inf))
        shift *= 2
    sub = lax.broadcasted_iota(jnp.int32, (SUBLANES, rows), 0)
    top8 = jnp.where(sub < ML_HEADS, g8, cum)
    s_col[...] = jnp.concatenate(
        [top8, mx8, jnp.zeros((LANES - 2 * SUBLANES, rows), F32)], axis=0).T
    s_q[...] = (proj(OFF_Q, D_ML) * (ML_HEAD_DIM ** -0.5)).astype(BF16)
    s_k[...] = proj(OFF_K, D_ML).astype(BF16)
    s_v[...] = proj(OFF_V, D_ML).astype(BF16)
    s_o[...] = proj(OFF_O, D_ML)

    tri = (lax.broadcasted_iota(jnp.int32, (CHUNK, CHUNK), 0)
           >= lax.broadcasted_iota(jnp.int32, (CHUNK, CHUNK), 1))[None]
    bdims = ((0,), (0,))

    def rowsl(b):
        return slice(b * CHUNK, (b + 1) * CHUNK)

    def headsl(h):
        return slice(h * ML_HEAD_DIM, (h + 1) * ML_HEAD_DIM)

    for g0 in range(0, batch, ML_GROUP):
        items = [(b, h) for b in range(g0, g0 + ML_GROUP) for h in range(ML_HEADS)]
        def col(j, items=items):
            return jnp.stack([jnp.broadcast_to(s_col[rowsl(b), j + h:j + h + 1], (CHUNK, LANES))
                              for b, h in items])

        g_c, b_c, mx_c = col(0), col(ML_HEADS), col(2 * ML_HEADS)
        g_r = jnp.stack([g8[h:h + 1, rowsl(b)] for b, h in items])
        m_prev = jnp.stack([m_ref[0, b * ML_HEADS + h:b * ML_HEADS + h + 1, :] for b, h in items])
        q3 = jnp.stack([s_q[rowsl(b), headsl(h)] for b, h in items])
        k3 = jnp.stack([s_k[rowsl(b), headsl(h)] for b, h in items])
        v3 = jnp.stack([s_v[rowsl(b), headsl(h)] for b, h in items])
        cx = jnp.stack([s_cx[b * ML_HEADS + h] for b, h in items])

        big_m = jnp.maximum(m_prev, mx_c)
        p = jnp.exp(jnp.where(tri, g_r - big_m[:, :, :CHUNK], -jnp.inf))
        w_inter = jnp.exp(m_prev - big_m)
        s = lax.dot_general(q3, k3, (((2,), (2,)), bdims), preferred_element_type=F32) * p
        cqx = lax.dot_general(q3, cx.astype(BF16), (((2,), (1,)), bdims), preferred_element_type=F32)
        num = (lax.dot_general(s.astype(BF16), v3, (((2,), (1,)), bdims), preferred_element_type=F32)
               + w_inter * cqx[:, :, :ML_HEAD_DIM])
        den_dot = jnp.sum(s, axis=2, keepdims=True) + w_inter * cqx[:, :, ML_HEAD_DIM:]
        m_t = b_c + big_m
        hout = num / jnp.maximum(jnp.abs(den_dot), jnp.exp(-m_t))

        m_last = big_m[:, CHUNK - 1:CHUNK, :]
        w_end = jnp.exp(g_c - m_last)
        decay = jnp.exp(m_prev - m_last)
        wvx = jnp.concatenate([w_end * v3.astype(F32), w_end], axis=2).astype(BF16)
        k_t = jnp.stack([s_k[rowsl(b), headsl(h)].astype(F32).T.astype(BF16) for b, h in items])
        cx_new = (jnp.concatenate([decay, decay], axis=2) * cx
                  + lax.dot_general(k_t, wvx, (((2,), (1,)), bdims), preferred_element_type=F32))
        m_new = m_t[:, CHUNK - 1:CHUNK, :]

        hc = hout - jnp.mean(hout, axis=2, keepdims=True)
        hn = hc * lax.rsqrt(jnp.mean(hc * hc, axis=2, keepdims=True) + EPS)
        for n, (b, h) in enumerate(items):
            bh = b * ML_HEADS + h
            s_cx[bh] = cx_new[n]
            m_ref[0, bh:bh + 1, :] = m_new[n]
            s_gated[rowsl(b), headsl(h)] = (jax.nn.sigmoid(s_o[rowsl(b), headsl(h)])
                                            * (hn[n] * gn_ref[:, headsl(h)])).astype(BF16)

    u = proj(OFF_U, D_S5)
    in_tiles = D_S5 // LANES
    for j in range(in_tiles):
        for b in range(batch):
            s_tm[j, pl.ds(b, CHUNK, stride=batch), :] = u[b * CHUNK:(b + 1) * CHUNK,
                                                          j * LANES:(j + 1) * LANES]
    ub = jnp.concatenate([s_tm[j] for j in range(in_tiles)], axis=1).astype(BF16)
    ys = []
    merge = []
    for k in range(2):
        uk = ub[:, k * S5_HALF_IN:(k + 1) * S5_HALF_IN]
        bur = jnp.dot(uk, bdr_ref[k], preferred_element_type=F32)
        bui = jnp.dot(uk, bdi_ref[k], preferred_element_type=F32)
        for j in range(S5_HALF_TILES):
            s_bur[j] = bur[:, j * LANES:(j + 1) * LANES]
            s_bui[j] = bui[:, j * LANES:(j + 1) * LANES]

        if k == 0:
            merge.append(jax.nn.sigmoid(proj(OFF_GML, D_MODEL)) * jnp.dot(
                s_gated[...], wml_ref[...], preferred_element_type=F32))
        else:
            merge.append(jax.nn.sigmoid(proj(OFF_GS5, D_MODEL)))

        for c in range(S5_HALF_TILES // S5_SCAN_TILES):
            tiles = range(c * S5_SCAN_TILES, (c + 1) * S5_SCAN_TILES)
            lanes = [slice(k * S5_HALF_CH + j * LANES, k * S5_HALF_CH + (j + 1) * LANES) for j in tiles]
            ar = [jnp.broadcast_to(abr_ref[:, ls], (batch, LANES)) for ls in lanes]
            ai = [jnp.broadcast_to(abi_ref[:, ls], (batch, LANES)) for ls in lanes]

            def step(t, carry, tiles=tiles, ar=ar, ai=ai):
                slab = pl.ds(pl.multiple_of(t * batch, batch), batch)
                out = []
                for n, j in enumerate(tiles):
                    hr, hi = carry[n]
                    nr = ar[n] * hr - ai[n] * hi + s_bur[j, slab, :]
                    ni = ar[n] * hi + ai[n] * hr + s_bui[j, slab, :]
                    s_bur[j, slab, :] = nr
                    s_bui[j, slab, :] = ni
                    out.append((nr, ni))
                return tuple(out)

            init = tuple((hr_ref[0, :, ls], hi_ref[0, :, ls]) for ls in lanes)
            fin = lax.fori_loop(0, CHUNK, step, init, unroll=True)
            for n, ls in enumerate(lanes):
                hr_ref[0, :, ls] = fin[n][0]
                hi_ref[0, :, ls] = fin[n][1]

        def states(ref):
            return jnp.concatenate([ref[j] for j in range(S5_HALF_TILES)], axis=1).astype(BF16)

        yk = jnp.dot(states(s_bur), cb_ref[k, :S5_HALF_CH, :], preferred_element_type=F32)
        yk = yk + jnp.dot(states(s_bui), cb_ref[k, S5_HALF_CH:, :], preferred_element_type=F32)
        ys.append(yk)
    y_tm = jnp.concatenate(ys, axis=1)
    for j in range(in_tiles):
        s_tm[j] = y_tm[:, j * LANES:(j + 1) * LANES]
    y = jnp.concatenate(
        [jnp.concatenate([s_tm[j, pl.ds(b, CHUNK, stride=batch), :] for j in range(in_tiles)], axis=1)
         for b in range(batch)], axis=0) + dskip_ref[...] * u
    glu = jnp.dot(jax.nn.gelu(y).astype(BF16), glu_ref[...], preferred_element_type=F32)
    y_s5 = glu[:, :D_MODEL] * jax.nn.sigmoid(glu[:, D_MODEL:])
    mix = merge[0] + merge[1] * y_s5

    x1 = x1_ref[...] + jnp.dot(mix.astype(BF16), wout_ref[...], preferred_element_type=F32)
    x1_ref[...] = x1

    xm = _rmsnorm(x1, nmoe_ref[...])
    logits = jnp.dot(xm.astype(BF16), wr_ref[...], preferred_element_type=F32) + br_ref[...]
    lane = lax.broadcasted_iota(jnp.int32, (rows, LANES), 1)
    lane_f = lane.astype(F32)
    logits = jnp.where(lane < N_EXPERTS, logits, -jnp.inf)
    vals, idxs = [], []
    for _ in range(TOP_K):
        mx = jnp.max(logits, axis=1, keepdims=True)
        am = jnp.min(jnp.where(logits == mx, lane_f, float(LANES)), axis=1, keepdims=True)
        vals.append(mx)
        idxs.append(am)
        logits = jnp.where(lane_f == am, -jnp.inf, logits)
    exps = [jnp.exp(v - vals[0]) for v in vals]
    esum = exps[0] + exps[1] + exps[2] + exps[3]
    idx_w = jnp.zeros((rows, LANES), F32)
    gate_w = jnp.zeros((rows, LANES), F32)
    for k in range(TOP_K):
        idx_w = jnp.where(lane == k, idxs[k], idx_w)
        gate_w = jnp.where(lane == k, exps[k] / esum, gate_w)
    idx_ref[...] = idx_w[:, :TOP_K].astype(jnp.int32)
    gate_ref[...] = gate_w[:, :TOP_K]
    chosen = jnp.zeros((rows, LANES), F32)
    for k in range(TOP_K):
        chosen = chosen + (lane_f == idxs[k]).astype(F32)
    cnt_ref[...] += jnp.sum(chosen, axis=0, keepdims=True)

    for s in range(TILE_ROWS):
        xm_ref[pl.ds(s, rows, stride=TILE_ROWS), :] = xm[:, s * LANES:(s + 1) * LANES]

    for phase_id, final_step in ((0, n_prompt - 1), (1, pl.num_programs(0) - 1)):
        @pl.when(i == final_step)
        def _(phase_id=phase_id):
            store = pltpu.make_async_copy(s_cx, cx_ref.at[phase_id], sem)
            store.start()
            store.wait()


def _mixer(x_prompt, x_sample, h0r, h0i, cx0, m0, w):
    batch, lp, _ = x_prompt.shape
    assert x_sample.shape[0] == batch and lp % CHUNK == 0 and x_sample.shape[1] % CHUNK == 0
    assert batch % ML_GROUP == 0
    n_prompt = lp // CHUNK
    n_sample = x_sample.shape[1] // CHUNK
    nblk = n_prompt + n_sample
    rows = batch * CHUNK
    t_all = nblk * rows
    bh = batch * ML_HEADS

    def phase(i):
        return jnp.minimum(i // n_prompt, 1)

    state_specs = [
        _const_spec((batch, S5_CH)), _const_spec((batch, S5_CH)),
        pl.BlockSpec(memory_space=pl.ANY),
        _const_spec((bh, LANES)),
    ]
    weights = (w['norm_mix'], w['w_in_a'], w['w_in_b'], w['w_in_g'], w['gate_bias'], w['abar_re'], w['abar_im'], w['bd_re'],
               w['bd_im'], w['cb'], w['d_skip'], w['w_glu'], w['ml_gn'], w['w_ml_out'], w['w_out'],
               w['norm_moe'], w['w_router'], w['b_router'])
    in_specs = ([pl.BlockSpec((batch, CHUNK, D_MODEL), lambda i: (0, jnp.minimum(i, n_prompt - 1), 0)),
                 pl.BlockSpec((batch, CHUNK, D_MODEL), lambda i: (0, jnp.maximum(i - n_prompt, 0), 0))]
                + state_specs + [_const_spec(a.shape) for a in weights])
    out_shape = (
        jax.ShapeDtypeStruct((t_all, D_MODEL), F32),
        jax.ShapeDtypeStruct((t_all * TILE_ROWS, LANES), F32),
        jax.ShapeDtypeStruct((t_all, TOP_K), jnp.int32),
        jax.ShapeDtypeStruct((t_all, TOP_K), F32),
        jax.ShapeDtypeStruct((1, LANES), F32),
        jax.ShapeDtypeStruct((2, batch, S5_CH), F32),
        jax.ShapeDtypeStruct((2, batch, S5_CH), F32),
        jax.ShapeDtypeStruct((2, bh, ML_HEAD_DIM, 2 * ML_HEAD_DIM), F32),
        jax.ShapeDtypeStruct((2, bh, LANES), F32),
    )
    out_specs = (
        pl.BlockSpec((rows, D_MODEL), lambda i: (i, 0)),
        pl.BlockSpec((rows * TILE_ROWS, LANES), lambda i: (i, 0)),
        pl.BlockSpec((rows, TOP_K), lambda i: (i, 0)),
        pl.BlockSpec((rows, TOP_K), lambda i: (i, 0)),
        pl.BlockSpec((1, LANES), lambda i: (0, 0)),
        pl.BlockSpec((1, batch, S5_CH), lambda i: (phase(i), 0, 0)),
        pl.BlockSpec((1, batch, S5_CH), lambda i: (phase(i), 0, 0)),
        pl.BlockSpec(memory_space=pl.ANY),
        pl.BlockSpec((1, bh, LANES), lambda i: (phase(i), 0, 0)),
    )
    scratch = [
        pltpu.VMEM((S5_HALF_TILES, rows, LANES), F32), pltpu.VMEM((S5_HALF_TILES, rows, LANES), F32),
        pltpu.VMEM((D_S5 // LANES, rows, LANES), F32),
        pltpu.VMEM((rows, D_ML), BF16), pltpu.VMEM((rows, D_ML), BF16), pltpu.VMEM((rows, D_ML), BF16),
        pltpu.VMEM((rows, D_ML), F32),
        pltpu.VMEM((rows, LANES), F32),
        pltpu.VMEM((rows, D_ML), BF16),
        pltpu.VMEM((bh, ML_HEAD_DIM, 2 * ML_HEAD_DIM), F32),
        pltpu.SemaphoreType.DMA(()),
    ]
    return pl.pallas_call(
        functools.partial(_mixer_kernel, batch=batch, n_prompt=n_prompt),
        grid=(nblk,),
        in_specs=in_specs,
        out_specs=out_specs,
        out_shape=out_shape,
        scratch_shapes=scratch,
        compiler_params=pltpu.CompilerParams(dimension_semantics=("arbitrary",),
                                             vmem_limit_bytes=VMEM_LIMIT),
        name="mixer",
    )(x_prompt, x_sample, h0r, h0i, cx0, m0, *weights)


def _plan_kernel(cnt_ref, idx_ref, dest_ref, meta_ref, s_run, s_pstart, *, tile, n_blocks_pad):
    i = pl.program_id(0)
    lane = lax.broadcasted_iota(jnp.int32, (tile, LANES), 1)
    idx = idx_ref[...]
    onehots = [(lane == idx[:, k:k + 1]).astype(F32) for k in range(TOP_K)]
    mask = onehots[0] + onehots[1] + onehots[2] + onehots[3]

    @pl.when(i == 0)
    def _():
        cnt = cnt_ref[...]
        nblk = jnp.floor((cnt + (MOE_BLOCK - 1)) * (1.0 / MOE_BLOCK))
        lane1 = lax.broadcasted_iota(jnp.int32, (1, LANES), 1)
        pend = nblk
        shift = 1
        while shift < LANES:
            pend = pend + jnp.where(lane1 >= shift, pltpu.roll(pend, shift, axis=1), 0.0)
            shift *= 2
        s_pstart[...] = (pend - nblk) * MOE_BLOCK
        s_run[...] = jnp.zeros_like(s_run)
        blk = lax.broadcasted_iota(jnp.int32, (n_blocks_pad, LANES), 0).astype(F32)
        lane_b = lax.broadcasted_iota(jnp.int32, (n_blocks_pad, LANES), 1)
        lane_f = lane_b.astype(F32)
        is_e = lane_b < N_EXPERTS
        done = jnp.logical_and(is_e, pend <= blk)
        e_of = jnp.minimum(jnp.sum(done.astype(F32), axis=1, keepdims=True), N_EXPERTS - 1.0)
        mine = lane_f == e_of
        blk_in_e = jnp.sum(jnp.where(mine, blk - (pend - nblk), 0.0), axis=1, keepdims=True)
        cnt_e = jnp.sum(jnp.where(mine, cnt, 0.0), axis=1, keepdims=True)
        valid = jnp.clip(cnt_e - blk_in_e * MOE_BLOCK, 0.0, float(MOE_BLOCK))
        used = jnp.sum(jnp.where(lane1 == N_EXPERTS - 1, pend, 0.0), axis=1, keepdims=True)
        owns = jnp.logical_and(is_e, nblk > 0.0)
        first = jnp.logical_and(blk_in_e == 0.0, blk[:, :1] < used).astype(F32)
        later = jnp.logical_and(owns, lane_f > e_of)
        nxt = jnp.min(jnp.where(later, lane_f, float(LANES)), axis=1, keepdims=True)
        nxt = jnp.where(nxt < float(LANES), nxt, -1.0)
        run = jnp.sum(jnp.logical_and(owns, lane_f < e_of).astype(F32), axis=1, keepdims=True)
        parity = run - 2.0 * jnp.floor(run * 0.5)
        meta = jnp.zeros((n_blocks_pad, LANES), F32)
        for col, val in enumerate((e_of, valid, used, first, nxt, parity)):
            meta = jnp.where(lane_b == col, val, meta)
        meta_ref[...] = meta.astype(jnp.int32)

    r = lax.broadcasted_iota(jnp.int32, (tile, tile), 0)
    c = lax.broadcasted_iota(jnp.int32, (tile, tile), 1)
    lower = (c < r).astype(BF16)
    before = jnp.dot(lower, mask.astype(BF16), preferred_element_type=F32)
    base = before + s_run[...] + s_pstart[...]
    dest = jnp.zeros((tile, LANES), F32)
    for k in range(TOP_K):
        dk = jnp.sum(onehots[k] * base, axis=1, keepdims=True)
        dest = jnp.where(lane == k, dk, dest)
    dest_ref[...] = dest[:, :TOP_K].astype(jnp.int32)
    s_run[...] += jnp.sum(mask, axis=0, keepdims=True)


def _plan(counts, idx_all, tile, n_blocks_pad):
    t_all = idx_all.shape[0]
    return pl.pallas_call(
        functools.partial(_plan_kernel, tile=tile, n_blocks_pad=n_blocks_pad),
        grid=(t_all // tile,),
        in_specs=[pl.BlockSpec((1, LANES), lambda i: (0, 0)),
                  pl.BlockSpec((tile, TOP_K), lambda i: (i, 0))],
        out_specs=(pl.BlockSpec((tile, TOP_K), lambda i: (i, 0)),
                   pl.BlockSpec((n_blocks_pad, LANES), lambda i: (0, 0))),
        out_shape=(jax.ShapeDtypeStruct((t_all, TOP_K), jnp.int32),
                   jax.ShapeDtypeStruct((n_blocks_pad, LANES), jnp.int32)),
        scratch_shapes=[pltpu.VMEM((1, LANES), F32), pltpu.VMEM((1, LANES), F32)],
        compiler_params=pltpu.CompilerParams(dimension_semantics=("arbitrary",)),
        name="moe_plan",
    )(counts, idx_all)


def _token_tile(ref, t):
    return ref.at[pl.ds(pl.multiple_of(t * TILE_ROWS, TILE_ROWS), TILE_ROWS)]


def _dispatch_kernel(valid_ref, dest_ref, xm_ref, buf_ref, zeros, sem, zsem, *, tile, n_blocks):
    blk_rows = MOE_BLOCK * TILE_ROWS

    @pl.when(pl.program_id(0) == 0)
    def _():
        zeros[...] = jnp.zeros_like(zeros)

        def block_copy(i):
            return pltpu.make_async_copy(
                zeros, buf_ref.at[pl.ds(pl.multiple_of(i * blk_rows, blk_rows), blk_rows)], zsem)

        def fill(i, carry):
            @pl.when(valid_ref[i] < MOE_BLOCK)
            def _():
                block_copy(i).start()
            return carry

        def drain(i, carry):
            @pl.when(valid_ref[i] < MOE_BLOCK)
            def _():
                block_copy(i).wait()
            return carry

        lax.fori_loop(0, n_blocks, fill, 0)
        lax.fori_loop(0, n_blocks, drain, 0)

    def issue(j, carry):
        src = _token_tile(xm_ref, j)
        for k in range(TOP_K):
            pltpu.make_async_copy(src, _token_tile(buf_ref, dest_ref[j * TOP_K + k]), sem).start(
                priority=k % 2)
        return carry

    lax.fori_loop(0, tile, issue, 0, unroll=4)
    n_rows = tile * TOP_K * TILE_ROWS
    pltpu.make_async_copy(buf_ref.at[pl.ds(0, n_rows)], buf_ref.at[pl.ds(0, n_rows)], sem).wait()


def _dispatch(block_valid, dest_flat, xm_tiles, tile, n_blocks):
    t_all = xm_tiles.shape[0] // TILE_ROWS
    grid_spec = pltpu.PrefetchScalarGridSpec(
        num_scalar_prefetch=1,
        grid=(t_all // tile,),
        in_specs=[pl.BlockSpec((tile * TOP_K,), lambda i, va: (i,), memory_space=pltpu.SMEM),
                  pl.BlockSpec((tile * TILE_ROWS, LANES), lambda i, va: (i, 0))],
        out_specs=pl.BlockSpec(memory_space=pl.ANY),
        scratch_shapes=[pltpu.VMEM((MOE_BLOCK * TILE_ROWS, LANES), F32),
                        pltpu.SemaphoreType.DMA(()), pltpu.SemaphoreType.DMA(())],
    )
    return pl.pallas_call(
        functools.partial(_dispatch_kernel, tile=tile, n_blocks=n_blocks),
        grid_spec=grid_spec,
        out_shape=jax.ShapeDtypeStruct((n_blocks * MOE_BLOCK * TILE_ROWS, LANES), F32),
        compiler_params=pltpu.CompilerParams(dimension_semantics=("arbitrary",)),
        name="moe_dispatch",
    )(block_valid, dest_flat, xm_tiles)


def _expert_kernel(be_ref, used_ref, first_ref, next_ref, slot_ref, valid_ref, x_ref, wgu_ref, bgu_ref,
                   wd_ref, bd_ref, o_ref, wbuf_gu, wbuf_d, s_wgu, s_wd, sem):
    i = pl.program_id(0)

    @pl.when(i >= used_ref[0])
    def _():
        o_ref[...] = jnp.zeros_like(o_ref)

    def weight_copies(e, slot):
        return (pltpu.make_async_copy(wgu_ref.at[e], wbuf_gu.at[slot], sem.at[0, slot]),
                pltpu.make_async_copy(wd_ref.at[e], wbuf_d.at[slot], sem.at[1, slot]))

    @pl.when(i == 0)
    def _():
        for cp in weight_copies(be_ref[0], slot_ref[0]):
            cp.start()

    @pl.when(first_ref[i] == 1)
    def _():
        slot = slot_ref[i]
        for cp in weight_copies(be_ref[i], slot):
            cp.wait()

        @pl.when(next_ref[i] >= 0)
        def _():
            for cp in weight_copies(next_ref[i], 1 - slot):
                cp.start()

        s_wgu[...] = wbuf_gu[slot].astype(BF16)
        s_wd[...] = wbuf_d[slot].astype(BF16)

    def ffn(nrows):
        pieces = [x_ref[pl.ds(s, nrows, stride=TILE_ROWS), :] for s in range(TILE_ROWS)]
        x = jnp.concatenate(pieces, axis=1).astype(BF16)
        gu = jnp.dot(x, s_wgu[...], preferred_element_type=F32) + bgu_ref[0]
        g = jnp.minimum(gu[:, :D_FF], SWIGLU_LIMIT)
        up = jnp.clip(gu[:, D_FF:], -SWIGLU_LIMIT, SWIGLU_LIMIT)
        hdn = (up + 1.0) * (g * jax.nn.sigmoid(SWIGLU_ALPHA * g))
        out = jnp.dot(hdn.astype(BF16), s_wd[...], preferred_element_type=F32) + bd_ref[0]
        for s in range(TILE_ROWS):
            o_ref[pl.ds(s, nrows, stride=TILE_ROWS), :] = out[:, s * LANES:(s + 1) * LANES]
        if nrows < MOE_BLOCK:
            o_ref[nrows * TILE_ROWS:, :] = jnp.zeros(((MOE_BLOCK - nrows) * TILE_ROWS, LANES), F32)

    in_use = i < used_ref[0]
    half_empty = valid_ref[i] <= MOE_BLOCK // 2

    @pl.when(jnp.logical_and(in_use, jnp.logical_not(half_empty)))
    def _():
        ffn(MOE_BLOCK)

    @pl.when(jnp.logical_and(in_use, half_empty))
    def _():
        ffn(MOE_BLOCK // 2)


def _experts(meta, buf, wgu, bgu, wd, bd, n_blocks):
    blk_rows = MOE_BLOCK * TILE_ROWS
    block_e, n_used = meta[:n_blocks, 0], meta[:1, 2]
    first, nxt, slot = meta[:n_blocks, 3], meta[:n_blocks, 4], meta[:n_blocks, 5]
    valid = meta[:n_blocks, 1]

    def in_row_map(i, be, nu, *_):
        return (jnp.minimum(i, nu[0] - 1), 0)

    def b_map(i, be, *_):
        return (be[i], 0, 0)

    grid_spec = pltpu.PrefetchScalarGridSpec(
        num_scalar_prefetch=6,
        grid=(n_blocks,),
        in_specs=[pl.BlockSpec((blk_rows, LANES), in_row_map),
                  pl.BlockSpec(memory_space=pl.ANY),
                  pl.BlockSpec((1, 1, 2 * D_FF), b_map),
                  pl.BlockSpec(memory_space=pl.ANY),
                  pl.BlockSpec((1, 1, D_MODEL), b_map)],
        out_specs=pl.BlockSpec((blk_rows, LANES), lambda i, *_: (i, 0)),
        scratch_shapes=[pltpu.VMEM((2, D_MODEL, 2 * D_FF), F32), pltpu.VMEM((2, D_FF, D_MODEL), F32),
                        pltpu.VMEM((D_MODEL, 2 * D_FF), BF16), pltpu.VMEM((D_FF, D_MODEL), BF16),
                        pltpu.SemaphoreType.DMA((2, 2))],
    )
    return pl.pallas_call(
        _expert_kernel,
        grid_spec=grid_spec,
        out_shape=jax.ShapeDtypeStruct(buf.shape, F32),
        compiler_params=pltpu.CompilerParams(dimension_semantics=("arbitrary",),
                                             vmem_limit_bytes=VMEM_LIMIT),
        name="moe_experts",
    )(block_e, n_used, first, nxt, slot, valid, buf, wgu, bgu, wd, bd)


def _combine_kernel(dest_ref, next_dest_ref, gate_ref, x1_ref, rows_ref, nf_ref, yp_ref, ys_ref,
                    gbuf, sem, *, batch, n_prompt):
    i = pl.program_id(0)
    tile = batch * CHUNK
    tile_rows = tile * TILE_ROWS

    slot = i % 2

    def gather(dref, slot_id):
        def issue(j, carry):
            for k in range(TOP_K):
                pltpu.make_async_copy(_token_tile(rows_ref, dref[j * TOP_K + k]),
                                      _token_tile(gbuf.at[slot_id, k], j),
                                      sem.at[slot_id]).start(priority=k % 2)
            return carry

        lax.fori_loop(0, tile, issue, 0, unroll=4)

    @pl.when(i == 0)
    def _():
        gather(dest_ref, 0)

    @pl.when(i + 1 < pl.num_programs(0))
    def _():
        gather(next_dest_ref, 1 - slot)

    for k in range(TOP_K):
        pltpu.make_async_copy(rows_ref.at[pl.ds(0, tile_rows)], gbuf.at[slot, k], sem.at[slot]).wait()

    gates = gate_ref[...]
    pieces = []
    for s in range(TILE_ROWS):
        piece = x1_ref[:, s * LANES:(s + 1) * LANES]
        for k in range(TOP_K):
            piece = piece + gates[:, k:k + 1] * gbuf[slot, k, pl.ds(s, tile, stride=TILE_ROWS), :]
        pieces.append(piece)
    y = _rmsnorm(jnp.concatenate(pieces, axis=1), nf_ref[...]).reshape(batch, CHUNK, D_MODEL)

    @pl.when(i < n_prompt)
    def _():
        yp_ref[...] = y

    @pl.when(i >= n_prompt)
    def _():
        ys_ref[...] = y


def _combine(dest_flat, gates, x1, out_rows, norm_final, batch, lp, ls):
    tile = batch * CHUNK
    n_prompt = lp // CHUNK
    n_sample = ls // CHUNK
    last = n_prompt + n_sample - 1
    return pl.pallas_call(
        functools.partial(_combine_kernel, batch=batch, n_prompt=n_prompt),
        grid=(n_prompt + n_sample,),
        in_specs=[pl.BlockSpec((tile * TOP_K,), lambda i: (i,), memory_space=pltpu.SMEM),
                  pl.BlockSpec((tile * TOP_K,), lambda i: (jnp.minimum(i + 1, last),),
                               memory_space=pltpu.SMEM),
                  pl.BlockSpec((tile, TOP_K), lambda i: (i, 0)),
                  pl.BlockSpec((tile, D_MODEL), lambda i: (i, 0)),
                  pl.BlockSpec(memory_space=pl.ANY),
                  pl.BlockSpec((1, D_MODEL), lambda i: (0, 0))],
        out_specs=(pl.BlockSpec((batch, CHUNK, D_MODEL), lambda i: (0, jnp.minimum(i, n_prompt - 1), 0)),
                   pl.BlockSpec((batch, CHUNK, D_MODEL), lambda i: (0, jnp.maximum(i - n_prompt, 0), 0))),
        out_shape=(jax.ShapeDtypeStruct((batch, lp, D_MODEL), F32),
                   jax.ShapeDtypeStruct((batch, ls, D_MODEL), F32)),
        scratch_shapes=[pltpu.VMEM((2, TOP_K, tile * TILE_ROWS, LANES), F32),
                        pltpu.SemaphoreType.DMA((2,))],
        compiler_params=pltpu.CompilerParams(dimension_semantics=("arbitrary",),
                                             vmem_limit_bytes=VMEM_LIMIT),
        name="moe_combine",
    )(dest_flat, dest_flat, gates, x1, out_rows, norm_final)


def _prep_weights(norm_mix, w_in, b_ig, b_fg, s5_a_re, s5_a_im, s5_log_dt, s5_b_re, s5_b_im,
                  s5_c_re, s5_c_im, s5_d, w_s5_glu, ml_gn, w_ml_out, w_out, norm_moe, w_router,
                  b_router):
    w = w_in[0]
    o_ig = D_S5 + 4 * D_ML
    o_gs5 = o_ig + 2 * ML_HEADS
    w_a = w[:, :o_ig].astype(BF16)
    w_b = w[:, o_gs5:].astype(BF16)
    w_g = jnp.concatenate([w[:, o_ig:o_gs5], jnp.zeros((D_MODEL, LANES - 2 * ML_HEADS), F32)],
                          axis=1).astype(BF16)
    gate_bias = jnp.concatenate([b_ig[0], b_fg[0], jnp.zeros((LANES - 2 * ML_HEADS,), F32)])[None]

    ar, ai = s5_a_re[0], s5_a_im[0]
    dt = jnp.exp(s5_log_dt[0])[:, None]
    mag = jnp.exp(dt * ar)
    abar_re = mag * jnp.cos(dt * ai)
    abar_im = mag * jnp.sin(dt * ai)
    den = ar * ar + ai * ai
    fr = ((abar_re - 1.0) * ar + abar_im * ai) / den
    fi = (abar_im * ar - (abar_re - 1.0) * ai) / den
    br, bi = s5_b_re[0], s5_b_im[0]
    bbar_re = fr[..., None] * br - fi[..., None] * bi
    bbar_im = fr[..., None] * bi + fi[..., None] * br
    gh = S5_GROUPS // 2
    same_group = (jnp.arange(gh)[:, None, None, None] == jnp.arange(gh)[None, None, :, None])

    def blockdiag_in(bb):
        t = bb.reshape(2, gh, S5_STATE, S5_GROUP).transpose(0, 1, 3, 2)
        full = jnp.where(same_group[None], t[:, :, :, None, :], 0.0)
        return full.reshape(2, gh * S5_GROUP, gh * S5_STATE).astype(BF16)

    def blockdiag_out(cc):
        t = cc.reshape(2, gh, S5_GROUP, S5_STATE).transpose(0, 1, 3, 2)
        full = jnp.where(same_group[None], t[:, :, :, None, :], 0.0)
        return full.reshape(2, gh * S5_STATE, gh * S5_GROUP)

    cb = jnp.concatenate([blockdiag_out(s5_c_re[0]), -blockdiag_out(s5_c_im[0])], axis=1).astype(BF16)
    w_router_p = jnp.concatenate(
        [w_router[0], jnp.zeros((D_MODEL, LANES - N_EXPERTS), F32)], axis=1).astype(BF16)
    b_router_p = jnp.concatenate([b_router[0], jnp.zeros((LANES - N_EXPERTS,), F32)])[None]
    return dict(
        norm_mix=norm_mix[0][None], w_in_a=w_a, w_in_b=w_b, w_in_g=w_g, gate_bias=gate_bias,
        abar_re=abar_re.reshape(1, S5_CH), abar_im=abar_im.reshape(1, S5_CH),
        bd_re=blockdiag_in(bbar_re), bd_im=blockdiag_in(bbar_im), cb=cb,
        d_skip=s5_d[0][None], w_glu=w_s5_glu[0].astype(BF16), ml_gn=ml_gn[0][None],
        w_ml_out=w_ml_out[0].astype(BF16), w_out=w_out[0].astype(BF16),
        norm_moe=norm_moe[0][None], w_router=w_router_p, b_router=b_router_p)


def kernel(x_prompt, x_sample, state_s5_re, state_s5_im, state_ml_C, state_ml_n, state_ml_m, norm_mix, w_in, b_ig, b_fg, s5_a_re, s5_a_im, s5_log_dt, s5_b_re, s5_b_im, s5_c_re, s5_c_im, s5_d, w_s5_glu, ml_gn, w_ml_out, w_out, norm_moe, w_router, b_router, w_gate_up, b_gate_up, w_down, b_down, norm_final):
    w = _prep_weights(norm_mix, w_in, b_ig, b_fg, s5_a_re, s5_a_im, s5_log_dt, s5_b_re, s5_b_im,
                      s5_c_re, s5_c_im, s5_d, w_s5_glu, ml_gn, w_ml_out, w_out, norm_moe,
                      w_router, b_router)
    batch, lp, _ = x_prompt.shape
    ls = x_sample.shape[1]
    bh = batch * ML_HEADS
    cx0 = jnp.concatenate(
        [jnp.swapaxes(state_ml_C[0].reshape(bh, ML_HEAD_DIM, ML_HEAD_DIM), 1, 2),
         jnp.broadcast_to(state_ml_n[0].reshape(bh, ML_HEAD_DIM, 1), (bh, ML_HEAD_DIM, ML_HEAD_DIM))],
        axis=2)
    x1, xm, idx, gates, counts, hr, hi, cx, m = _mixer(
        x_prompt, x_sample,
        state_s5_re[0].reshape(batch, S5_CH), state_s5_im[0].reshape(batch, S5_CH), cx0,
        jnp.broadcast_to(state_ml_m[0].reshape(bh, 1), (bh, LANES)), w)
    c = jnp.swapaxes(cx[:, :, :, :ML_HEAD_DIM], 2, 3)
    n = cx[:, :, :, ML_HEAD_DIM]

    tile = batch * CHUNK
    t_all = idx.shape[0]
    n_blocks = -(-(t_all * TOP_K + N_EXPERTS * (MOE_BLOCK - 1)) // MOE_BLOCK)
    n_blocks_pad = -(-n_blocks // SUBLANES) * SUBLANES
    dest, meta = _plan(counts, idx, tile, n_blocks_pad)
    dest_flat = dest.reshape(t_all * TOP_K)
    buf = _dispatch(meta[:n_blocks, 1], dest_flat, xm, tile, n_blocks)
    out_rows = _experts(meta, buf, w_gate_up[0], b_gate_up[0][:, None, :], w_down[0],
                        b_down[0][:, None, :], n_blocks)
    y_prompt, y_sample = _combine(dest_flat, gates, x1, out_rows, norm_final[None], batch, lp, ls)

    def states(p):
        return (hr[p].reshape(1, batch, S5_GROUPS, S5_STATE), hi[p].reshape(1, batch, S5_GROUPS, S5_STATE),
                c[p].reshape(1, batch, ML_HEADS, ML_HEAD_DIM, ML_HEAD_DIM),
                n[p].reshape(1, batch, ML_HEADS, ML_HEAD_DIM), m[p, :, 0].reshape(1, batch, ML_HEADS))

    return (y_prompt, y_sample) + states(0) + states(1)
```

```python
import functools

import jax
import jax.numpy as jnp
from jax import lax
from jax.experimental import pallas as pl
from jax.experimental.pallas import tpu as pltpu

F32 = jnp.float32
BF16 = jnp.bfloat16

D_MODEL = 1024
CHUNK = 64
EPS = 1e-6
D_S5 = 512
S5_GROUP = 16
S5_GROUPS = D_S5 // S5_GROUP
S5_STATE = 64
S5_CH = S5_GROUPS * S5_STATE
ML_HEADS = 4
ML_HEAD_DIM = 128
D_ML = ML_HEADS * ML_HEAD_DIM
N_EXPERTS = 32
TOP_K = 4
D_FF = 1024
SWIGLU_LIMIT = 7.0
SWIGLU_ALPHA = 1.702

LANES = 128
SUBLANES = 8
TILE_ROWS = D_MODEL // LANES

OFF_U = 0
OFF_Q = OFF_U + D_S5
OFF_K = OFF_Q + D_ML
OFF_V = OFF_K + D_ML
OFF_O = OFF_V + D_ML
OFF_GS5 = OFF_O + D_ML
OFF_GML = OFF_GS5 + D_MODEL
OFF_GATE = OFF_GML + D_MODEL
W_COLS = OFF_GATE + LANES

S5_HALF_IN = D_S5 // 2
S5_HALF_CH = S5_CH // 2
S5_HALF_TILES = S5_HALF_CH // LANES
S5_SCAN_TILES = 4
ML_GROUP = 2
MOE_BLOCK = 512
VMEM_LIMIT = 60 * 1024 * 1024


def _rmsnorm(x, w):
    return x * lax.rsqrt(jnp.mean(x * x, axis=-1, keepdims=True) + EPS) * w


def _const_spec(shape):
    nd = len(shape)
    return pl.BlockSpec(shape, lambda *_: (0,) * nd, pipeline_mode=pl.Buffered(1))


def _mixer_kernel(xp_ref, xs_ref, h0r_ref, h0i_ref, cx0_ref, m0_ref,
                  nmix_ref, wina_ref, winb_ref, wing_ref, gbias_ref, abr_ref, abi_ref, bdr_ref, bdi_ref, cb_ref,
                  dskip_ref, glu_ref, gn_ref, wml_ref, wout_ref, nmoe_ref, wr_ref, br_ref,
                  x1_ref, xm_ref, idx_ref, gate_ref, cnt_ref, hr_ref, hi_ref, cx_ref, m_ref,
                  s_bur, s_bui, s_tm, s_q, s_k, s_v, s_o, s_col, s_gated, s_cx, sem, *, batch, n_prompt):
    i = pl.program_id(0)
    rows = batch * CHUNK

    @pl.when(i == 0)
    def _():
        hr_ref[...] = jnp.zeros_like(hr_ref)
        hi_ref[...] = jnp.zeros_like(hi_ref)
        m_ref[...] = jnp.zeros_like(m_ref)
        s_cx[...] = jnp.zeros_like(s_cx)
        cnt_ref[...] = jnp.zeros_like(cnt_ref)

    @pl.when(i == n_prompt)
    def _():
        hr_ref[0] = h0r_ref[...]
        hi_ref[0] = h0i_ref[...]
        m_ref[0] = m0_ref[...]
        load = pltpu.make_async_copy(cx0_ref, s_cx, sem)
        load.start()
        load.wait()

    @pl.when(i < n_prompt)
    def _():
        x1_ref[...] = xp_ref[...].reshape(rows, D_MODEL)

    @pl.when(i >= n_prompt)
    def _():
        x1_ref[...] = xs_ref[...].reshape(rows, D_MODEL)

    xn = _rmsnorm(x1_ref[...], nmix_ref[...]).astype(BF16)

    def proj(off, width):
        if off >= OFF_GATE:
            w_cols = wing_ref[...]
        elif off >= OFF_GS5:
            w_cols = winb_ref[:, off - OFF_GS5:off - OFF_GS5 + width]
        else:
            w_cols = wina_ref[:, off:off + width]
        return jnp.dot(xn, w_cols, preferred_element_type=F32)

    gates = proj(OFF_GATE, LANES) + gbias_ref[...]
    lane_g = lax.broadcasted_iota(jnp.int32, (rows, LANES), 1)
    gg = jnp.where(lane_g < ML_HEADS, gates, jax.nn.log_sigmoid(gates))
    gt8 = gg.T[0:SUBLANES, :]
    pos = lax.broadcasted_iota(jnp.int32, (SUBLANES, rows), 1) % CHUNK
    cum = gt8
    shift = 1
    while shift < CHUNK:
        cum = cum + jnp.where(pos >= shift, pltpu.roll(cum, shift, axis=1), 0.0)
        shift *= 2
    g8 = gt8 - pltpu.roll(cum, ML_HEADS, axis=0)
    mx8 = g8
    shift = 1
    while shift < CHUNK:
        mx8 = jnp.maximum(mx8, jnp.where(pos >= shift, pltpu.roll(mx8, shift, axis=1), -jnp.inf))
        shift *= 2
    sub = lax.broadcasted_iota(jnp.int32, (SUBLANES, rows), 0)
    top8 = jnp.where(sub < ML_HEADS, g8, cum)
    s_col[...] = jnp.concatenate(
        [top8, mx8, jnp.zeros((LANES - 2 * SUBLANES, rows), F32)], axis=0).T
    s_q[...] = (proj(OFF_Q, D_ML) * (ML_HEAD_DIM ** -0.5)).astype(BF16)
    s_k[...] = proj(OFF_K, D_ML).astype(BF16)
    s_v[...] = proj(OFF_V, D_ML).astype(BF16)
    s_o[...] = proj(OFF_O, D_ML)

    tri = (lax.broadcasted_iota(jnp.int32, (CHUNK, CHUNK), 0)
           >= lax.broadcasted_iota(jnp.int32, (CHUNK, CHUNK), 1))[None]
    bdims = ((0,), (0,))

    def rowsl(b):
        return slice(b * CHUNK, (b + 1) * CHUNK)

    def headsl(h):
        return slice(h * ML_HEAD_DIM, (h + 1) * ML_HEAD_DIM)

    for g0 in range(0, batch, ML_GROUP):
        items = [(b, h) for b in range(g0, g0 + ML_GROUP) for h in range(ML_HEADS)]
        def col(j, items=items):
            return jnp.stack([jnp.broadcast_to(s_col[rowsl(b), j + h:j + h + 1], (CHUNK, LANES))
                              for b, h in items])

        g_c, b_c, mx_c = col(0), col(ML_HEADS), col(2 * ML_HEADS)
        g_r = jnp.stack([g8[h:h + 1, rowsl(b)] for b, h in items])
        m_prev = jnp.stack([m_ref[0, b * ML_HEADS + h:b * ML_HEADS + h + 1, :] for b, h in items])
        q3 = jnp.stack([s_q[rowsl(b), headsl(h)] for b, h in items])
        k3 = jnp.stack([s_k[rowsl(b), headsl(h)] for b, h in items])
        v3 = jnp.stack([s_v[rowsl(b), headsl(h)] for b, h in items])
        cx = jnp.stack([s_cx[b * ML_HEADS + h] for b, h in items])

        big_m = jnp.maximum(m_prev, mx_c)
        p = jnp.exp(jnp.where(tri, g_r - big_m[:, :, :CHUNK], -jnp.inf))
        w_inter = jnp.exp(m_prev - big_m)
        s = lax.dot_general(q3, k3, (((2,), (2,)), bdims), preferred_element_type=F32) * p
        cqx = lax.dot_general(q3, cx.astype(BF16), (((2,), (1,)), bdims), preferred_element_type=F32)
        num = (lax.dot_general(s.astype(BF16), v3, (((2,), (1,)), bdims), preferred_element_type=F32)
               + w_inter * cqx[:, :, :ML_HEAD_DIM])
        den_dot = jnp.sum(s, axis=2, keepdims=True) + w_inter * cqx[:, :, ML_HEAD_DIM:]
        m_t = b_c + big_m
        hout = num / jnp.maximum(jnp.abs(den_dot), jnp.exp(-m_t))

        m_last = big_m[:, CHUNK - 1:CHUNK, :]
        w_end = jnp.exp(g_c - m_last)
        decay = jnp.exp(m_prev - m_last)
        wvx = jnp.concatenate([w_end * v3.astype(F32), w_end], axis=2).astype(BF16)
        k_t = jnp.stack([s_k[rowsl(b), headsl(h)].astype(F32).T.astype(BF16) for b, h in items])
        cx_new = (jnp.concatenate([decay, decay], axis=2) * cx
                  + lax.dot_general(k_t, wvx, (((2,), (1,)), bdims), preferred_element_type=F32))
        m_new = m_t[:, CHUNK - 1:CHUNK, :]

        hc = hout - jnp.mean(hout, axis=2, keepdims=True)
        hn = hc * lax.rsqrt(jnp.mean(hc * hc, axis=2, keepdims=True) + EPS)
        for n, (b, h) in enumerate(items):
            bh = b * ML_HEADS + h
            s_cx[bh] = cx_new[n]
            m_ref[0, bh:bh + 1, :] = m_new[n]
            s_gated[rowsl(b), headsl(h)] = (jax.nn.sigmoid(s_o[rowsl(b), headsl(h)])
                                            * (hn[n] * gn_ref[:, headsl(h)])).astype(BF16)

    u = proj(OFF_U, D_S5)
    in_tiles = D_S5 // LANES
    for j in range(in_tiles):
        for b in range(batch):
            s_tm[j, pl.ds(b, CHUNK, stride=batch), :] = u[b * CHUNK:(b + 1) * CHUNK,
                                                          j * LANES:(j + 1) * LANES]
    ub = jnp.concatenate([s_tm[j] for j in range(in_tiles)], axis=1).astype(BF16)
    ys = []
    merge = []
    for k in range(2):
        uk = ub[:, k * S5_HALF_IN:(k + 1) * S5_HALF_IN]
        bur = jnp.dot(uk, bdr_ref[k], preferred_element_type=F32)
        bui = jnp.dot(uk, bdi_ref[k], preferred_element_type=F32)
        for j in range(S5_HALF_TILES):
            s_bur[j] = bur[:, j * LANES:(j + 1) * LANES]
            s_bui[j] = bui[:, j * LANES:(j + 1) * LANES]

        if k == 0:
            merge.append(jax.nn.sigmoid(proj(OFF_GML, D_MODEL)) * jnp.dot(
                s_gated[...], wml_ref[...], preferred_element_type=F32))
        else:
            merge.append(jax.nn.sigmoid(proj(OFF_GS5, D_MODEL)))

        for c in range(S5_HALF_TILES // S5_SCAN_TILES):
            tiles = range(c * S5_SCAN_TILES, (c + 1) * S5_SCAN_TILES)
            lanes = [slice(k * S5_HALF_CH + j * LANES, k * S5_HALF_CH + (j + 1) * LANES) for j in tiles]
            ar = [jnp.broadcast_to(abr_ref[:, ls], (batch, LANES)) for ls in lanes]
            ai = [jnp.broadcast_to(abi_ref[:, ls], (batch, LANES)) for ls in lanes]

            def step(t, carry, tiles=tiles, ar=ar, ai=ai):
                slab = pl.ds(pl.multiple_of(t * batch, batch), batch)
                out = []
                for n, j in enumerate(tiles):
                    hr, hi = carry[n]
                    nr = ar[n] * hr - ai[n] * hi + s_bur[j, slab, :]
                    ni = ar[n] * hi + ai[n] * hr + s_bui[j, slab, :]
                    s_bur[j, slab, :] = nr
                    s_bui[j, slab, :] = ni
                    out.append((nr, ni))
                return tuple(out)

            init = tuple((hr_ref[0, :, ls], hi_ref[0, :, ls]) for ls in lanes)
            fin = lax.fori_loop(0, CHUNK, step, init, unroll=True)
            for n, ls in enumerate(lanes):
                hr_ref[0, :, ls] = fin[n][0]
                hi_ref[0, :, ls] = fin[n][1]

        def states(ref):
            return jnp.concatenate([ref[j] for j in range(S5_HALF_TILES)], axis=1).astype(BF16)

        yk = jnp.dot(states(s_bur), cb_ref[k, :S5_HALF_CH, :], preferred_element_type=F32)
        yk = yk + jnp.dot(states(s_bui), cb_ref[k, S5_HALF_CH:, :], preferred_element_type=F32)
        ys.append(yk)
    y_tm = jnp.concatenate(ys, axis=1)
    for j in range(in_tiles):
        s_tm[j] = y_tm[:, j * LANES:(j + 1) * LANES]
    y = jnp.concatenate(
        [jnp.concatenate([s_tm[j, pl.ds(b, CHUNK, stride=batch), :] for j in range(in_tiles)], axis=1)
         for b in range(batch)], axis=0) + dskip_ref[...] * u
    glu = jnp.dot(jax.nn.gelu(y).astype(BF16), glu_ref[...], preferred_element_type=F32)
    y_s5 = glu[:, :D_MODEL] * jax.nn.sigmoid(glu[:, D_MODEL:])
    mix = merge[0] + merge[1] * y_s5

    x1 = x1_ref[...] + jnp.dot(mix.astype(BF16), wout_ref[...], preferred_element_type=F32)
    x1_ref[...] = x1

    xm = _rmsnorm(x1, nmoe_ref[...])
    logits = jnp.dot(xm.astype(BF16), wr_ref[...], preferred_element_type=F32) + br_ref[...]
    lane = lax.broadcasted_iota(jnp.int32, (rows, LANES), 1)
    lane_f = lane.astype(F32)
    logits = jnp.where(lane < N_EXPERTS, logits, -jnp.inf)
    vals, idxs = [], []
    for _ in range(TOP_K):
        mx = jnp.max(logits, axis=1, keepdims=True)
        am = jnp.min(jnp.where(logits == mx, lane_f, float(LANES)), axis=1, keepdims=True)
        vals.append(mx)
        idxs.append(am)
        logits = jnp.where(lane_f == am, -jnp.inf, logits)
    exps = [jnp.exp(v - vals[0]) for v in vals]
    esum = exps[0] + exps[1] + exps[2] + exps[3]
    idx_w = jnp.zeros((rows, LANES), F32)
    gate_w = jnp.zeros((rows, LANES), F32)
    for k in range(TOP_K):
        idx_w = jnp.where(lane == k, idxs[k], idx_w)
        gate_w = jnp.where(lane == k, exps[k] / esum, gate_w)
    idx_ref[...] = idx_w[:, :TOP_K].astype(jnp.int32)
    gate_ref[...] = gate_w[:, :TOP_K]
    chosen = jnp.zeros((rows, LANES), F32)
    for k in range(TOP_K):
        chosen = chosen + (lane_f == idxs[k]).astype(F32)
    cnt_ref[...] += jnp.sum(chosen, axis=0, keepdims=True)

    for s in range(TILE_ROWS):
        xm_ref[pl.ds(s, rows, stride=TILE_ROWS), :] = xm[:, s * LANES:(s + 1) * LANES]

    for phase_id, final_step in ((0, n_prompt - 1), (1, pl.num_programs(0) - 1)):
        @pl.when(i == final_step)
        def _(phase_id=phase_id):
            store = pltpu.make_async_copy(s_cx, cx_ref.at[phase_id], sem)
            store.start()
            store.wait()


def _mixer(x_prompt, x_sample, h0r, h0i, cx0, m0, w):
    batch, lp, _ = x_prompt.shape
    assert x_sample.shape[0] == batch and lp % CHUNK == 0 and x_sample.shape[1] % CHUNK == 0
    assert batch % ML_GROUP == 0
    n_prompt = lp // CHUNK
    n_sample = x_sample.shape[1] // CHUNK
    nblk = n_prompt + n_sample
    rows = batch * CHUNK
    t_all = nblk * rows
    bh = batch * ML_HEADS

    def phase(i):
        return jnp.minimum(i // n_prompt, 1)

    state_specs = [
        _const_spec((batch, S5_CH)), _const_spec((batch, S5_CH)),
        pl.BlockSpec(memory_space=pl.ANY),
        _const_spec((bh, LANES)),
    ]
    weights = (w['norm_mix'], w['w_in_a'], w['w_in_b'], w['w_in_g'], w['gate_bias'], w['abar_re'], w['abar_im'], w['bd_re'],
               w['bd_im'], w['cb'], w['d_skip'], w['w_glu'], w['ml_gn'], w['w_ml_out'], w['w_out'],
               w['norm_moe'], w['w_router'], w['b_router'])
    in_specs = ([pl.BlockSpec((batch, CHUNK, D_MODEL), lambda i: (0, jnp.minimum(i, n_prompt - 1), 0)),
                 pl.BlockSpec((batch, CHUNK, D_MODEL), lambda i: (0, jnp.maximum(i - n_prompt, 0), 0))]
                + state_specs + [_const_spec(a.shape) for a in weights])
    out_shape = (
        jax.ShapeDtypeStruct((t_all, D_MODEL), F32),
        jax.ShapeDtypeStruct((t_all * TILE_ROWS, LANES), F32),
        jax.ShapeDtypeStruct((t_all, TOP_K), jnp.int32),
        jax.ShapeDtypeStruct((t_all, TOP_K), F32),
        jax.ShapeDtypeStruct((1, LANES), F32),
        jax.ShapeDtypeStruct((2, batch, S5_CH), F32),
        jax.ShapeDtypeStruct((2, batch, S5_CH), F32),
        jax.ShapeDtypeStruct((2, bh, ML_HEAD_DIM, 2 * ML_HEAD_DIM), F32),
        jax.ShapeDtypeStruct((2, bh, LANES), F32),
    )
    out_specs = (
        pl.BlockSpec((rows, D_MODEL), lambda i: (i, 0)),
        pl.BlockSpec((rows * TILE_ROWS, LANES), lambda i: (i, 0)),
        pl.BlockSpec((rows, TOP_K), lambda i: (i, 0)),
        pl.BlockSpec((rows, TOP_K), lambda i: (i, 0)),
        pl.BlockSpec((1, LANES), lambda i: (0, 0)),
        pl.BlockSpec((1, batch, S5_CH), lambda i: (phase(i), 0, 0)),
        pl.BlockSpec((1, batch, S5_CH), lambda i: (phase(i), 0, 0)),
        pl.BlockSpec(memory_space=pl.ANY),
        pl.BlockSpec((1, bh, LANES), lambda i: (phase(i), 0, 0)),
    )
    scratch = [
        pltpu.VMEM((S5_HALF_TILES, rows, LANES), F32), pltpu.VMEM((S5_HALF_TILES, rows, LANES), F32),
        pltpu.VMEM((D_S5 // LANES, rows, LANES), F32),
        pltpu.VMEM((rows, D_ML), BF16), pltpu.VMEM((rows, D_ML), BF16), pltpu.VMEM((rows, D_ML), BF16),
        pltpu.VMEM((rows, D_ML), F32),
        pltpu.VMEM((rows, LANES), F32),
        pltpu.VMEM((rows, D_ML), BF16),
        pltpu.VMEM((bh, ML_HEAD_DIM, 2 * ML_HEAD_DIM), F32),
        pltpu.SemaphoreType.DMA(()),
    ]
    return pl.pallas_call(
        functools.partial(_mixer_kernel, batch=batch, n_prompt=n_prompt),
        grid=(nblk,),
        in_specs=in_specs,
        out_specs=out_specs,
        out_shape=out_shape,
        scratch_shapes=scratch,
        compiler_params=pltpu.CompilerParams(dimension_semantics=("arbitrary",),
                                             vmem_limit_bytes=VMEM_LIMIT),
        name="mixer",
    )(x_prompt, x_sample, h0r, h0i, cx0, m0, *weights)


def _plan_kernel(cnt_ref, idx_ref, spos_ref, runs_ref, meta_ref, s_run, s_pstart, *, tile,
                 n_blocks_pad):
    i = pl.program_id(0)
    lane = lax.broadcasted_iota(jnp.int32, (tile, LANES), 1)
    idx = idx_ref[...]
    onehots = [(lane == idx[:, k:k + 1]).astype(F32) for k in range(TOP_K)]
    mask = onehots[0] + onehots[1] + onehots[2] + onehots[3]

    @pl.when(i == 0)
    def _():
        cnt = cnt_ref[...]
        nblk = jnp.floor((cnt + (MOE_BLOCK - 1)) * (1.0 / MOE_BLOCK))
        lane1 = lax.broadcasted_iota(jnp.int32, (1, LANES), 1)
        pend = nblk
        shift = 1
        while shift < LANES:
            pend = pend + jnp.where(lane1 >= shift, pltpu.roll(pend, shift, axis=1), 0.0)
            shift *= 2
        s_pstart[...] = (pend - nblk) * MOE_BLOCK
        s_run[...] = jnp.zeros_like(s_run)
        blk = lax.broadcasted_iota(jnp.int32, (n_blocks_pad, LANES), 0).astype(F32)
        lane_b = lax.broadcasted_iota(jnp.int32, (n_blocks_pad, LANES), 1)
        lane_f = lane_b.astype(F32)
        is_e = lane_b < N_EXPERTS
        done = jnp.logical_and(is_e, pend <= blk)
        e_of = jnp.minimum(jnp.sum(done.astype(F32), axis=1, keepdims=True), N_EXPERTS - 1.0)
        mine = lane_f == e_of
        blk_in_e = jnp.sum(jnp.where(mine, blk - (pend - nblk), 0.0), axis=1, keepdims=True)
        cnt_e = jnp.sum(jnp.where(mine, cnt, 0.0), axis=1, keepdims=True)
        valid = jnp.clip(cnt_e - blk_in_e * MOE_BLOCK, 0.0, float(MOE_BLOCK))
        used = jnp.sum(jnp.where(lane1 == N_EXPERTS - 1, pend, 0.0), axis=1, keepdims=True)
        owns = jnp.logical_and(is_e, nblk > 0.0)
        first = jnp.logical_and(blk_in_e == 0.0, blk[:, :1] < used).astype(F32)
        later = jnp.logical_and(owns, lane_f > e_of)
        nxt = jnp.min(jnp.where(later, lane_f, float(LANES)), axis=1, keepdims=True)
        nxt = jnp.where(nxt < float(LANES), nxt, -1.0)
        run = jnp.sum(jnp.logical_and(owns, lane_f < e_of).astype(F32), axis=1, keepdims=True)
        parity = run - 2.0 * jnp.floor(run * 0.5)
        meta = jnp.zeros((n_blocks_pad, LANES), F32)
        for col, val in enumerate((e_of, valid, used, first, nxt, parity)):
            meta = jnp.where(lane_b == col, val, meta)
        meta_ref[...] = meta.astype(jnp.int32)

    r = lax.broadcasted_iota(jnp.int32, (tile, tile), 0)
    c = lax.broadcasted_iota(jnp.int32, (tile, tile), 1)
    lower = (c < r).astype(BF16)
    before = jnp.dot(lower, mask.astype(BF16), preferred_element_type=F32)
    tile_cnt = jnp.sum(mask, axis=0, keepdims=True)
    lane1 = lax.broadcasted_iota(jnp.int32, (1, LANES), 1)
    incl = tile_cnt
    shift = 1
    while shift < LANES:
        incl = incl + jnp.where(lane1 >= shift, pltpu.roll(incl, shift, axis=1), 0.0)
        shift *= 2
    tile_start = incl - tile_cnt
    base = before + tile_start
    spos = jnp.zeros((tile, LANES), F32)
    for k in range(TOP_K):
        pk = jnp.sum(onehots[k] * base, axis=1, keepdims=True)
        spos = jnp.where(lane == k, pk, spos)
    spos_ref[...] = spos[:, :TOP_K].astype(jnp.int32)
    sub = lax.broadcasted_iota(jnp.int32, (SUBLANES, LANES), 0)
    runs = jnp.where(sub == 0, tile_cnt,
                     jnp.where(sub == 1, s_pstart[...] + s_run[...],
                               jnp.where(sub == 2, tile_start, 0.0)))
    runs_ref[0] = runs.astype(jnp.int32)
    s_run[...] += tile_cnt


def _plan(counts, idx_all, tile, n_blocks_pad):
    t_all = idx_all.shape[0]
    return pl.pallas_call(
        functools.partial(_plan_kernel, tile=tile, n_blocks_pad=n_blocks_pad),
        grid=(t_all // tile,),
        in_specs=[pl.BlockSpec((1, LANES), lambda i: (0, 0)),
                  pl.BlockSpec((tile, TOP_K), lambda i: (i, 0))],
        out_specs=(pl.BlockSpec((tile, TOP_K), lambda i: (i, 0)),
                   pl.BlockSpec((1, SUBLANES, LANES), lambda i: (i, 0, 0)),
                   pl.BlockSpec((n_blocks_pad, LANES), lambda i: (0, 0))),
        out_shape=(jax.ShapeDtypeStruct((t_all, TOP_K), jnp.int32),
                   jax.ShapeDtypeStruct((t_all // tile, SUBLANES, LANES), jnp.int32),
                   jax.ShapeDtypeStruct((n_blocks_pad, LANES), jnp.int32)),
        scratch_shapes=[pltpu.VMEM((1, LANES), F32), pltpu.VMEM((1, LANES), F32)],
        compiler_params=pltpu.CompilerParams(dimension_semantics=("arbitrary",)),
        name="moe_plan",
    )(counts, idx_all)


RUN_BITS = 10


def _run_copies(runs_ref, sorted_hbm, stage, sem, to_hbm):
    def expert(e, carry):
        n = runs_ref[e]
        first_sorted = runs_ref[LANES + e]
        first_staged = runs_ref[2 * LANES + e]
        for bit in reversed(range(RUN_BITS)):
            size = (1 << bit) * TILE_ROWS
            done = lax.shift_left(lax.shift_right_logical(n, bit + 1), bit + 1)

            @pl.when(lax.bitwise_and(lax.shift_right_logical(n, bit), 1) == 1)
            def _(size=size, done=done, bit=bit):
                hbm = sorted_hbm.at[pl.ds(pl.multiple_of((first_sorted + done) * TILE_ROWS, TILE_ROWS), size)]
                vmem = stage.at[pl.ds(pl.multiple_of((first_staged + done) * TILE_ROWS, TILE_ROWS), size)]
                src, dst = (vmem, hbm) if to_hbm else (hbm, vmem)
                pltpu.make_async_copy(src, dst, sem).start(priority=bit % 2)
        return carry

    lax.fori_loop(0, N_EXPERTS, expert, 0)


def _dispatch_kernel(valid_ref, spos_ref, runs_ref, xm_ref, buf_ref, zeros, stage, sem, zsem, *, tile,
                     n_blocks):
    i = pl.program_id(0)
    slot = i % 2
    blk_rows = MOE_BLOCK * TILE_ROWS
    n_rows = tile * TOP_K * TILE_ROWS

    def drain(s):
        pltpu.make_async_copy(buf_ref.at[pl.ds(0, n_rows)], buf_ref.at[pl.ds(0, n_rows)], sem.at[s]).wait()

    @pl.when(pl.program_id(0) == 0)
    def _():
        zeros[...] = jnp.zeros_like(zeros)

        def block_copy(i):
            return pltpu.make_async_copy(
                zeros, buf_ref.at[pl.ds(pl.multiple_of(i * blk_rows, blk_rows), blk_rows)], zsem)

        def fill(i, carry):
            @pl.when(valid_ref[i] < MOE_BLOCK)
            def _():
                block_copy(i).start()
            return carry

        def fill_wait(i, carry):
            @pl.when(valid_ref[i] < MOE_BLOCK)
            def _():
                block_copy(i).wait()
            return carry

        lax.fori_loop(0, n_blocks, fill, 0)
        lax.fori_loop(0, n_blocks, fill_wait, 0)

    @pl.when(i >= 2)
    def _():
        drain(slot)

    def place(j, carry):
        tok = xm_ref[pl.ds(pl.multiple_of(j * TILE_ROWS, TILE_ROWS), TILE_ROWS), :]
        for k in range(TOP_K):
            p = spos_ref[j * TOP_K + k]
            stage[slot, pl.ds(pl.multiple_of(p * TILE_ROWS, TILE_ROWS), TILE_ROWS), :] = tok
        return carry

    lax.fori_loop(0, tile, place, 0, unroll=8)
    _run_copies(runs_ref, buf_ref, stage.at[slot], sem.at[slot], to_hbm=True)

    last = pl.num_programs(0) - 1

    @pl.when(i == last)
    def _():
        drain(slot)

    @pl.when(jnp.logical_and(i == last, i >= 1))
    def _():
        drain(1 - slot)


def _dispatch(block_valid, spos_flat, runs_flat, xm_tiles, tile, n_blocks):
    t_all = xm_tiles.shape[0] // TILE_ROWS
    grid_spec = pltpu.PrefetchScalarGridSpec(
        num_scalar_prefetch=1,
        grid=(t_all // tile,),
        in_specs=[pl.BlockSpec((tile * TOP_K,), lambda i, va: (i,), memory_space=pltpu.SMEM),
                  pl.BlockSpec((SUBLANES * LANES,), lambda i, va: (i,), memory_space=pltpu.SMEM),
                  pl.BlockSpec((tile * TILE_ROWS, LANES), lambda i, va: (i, 0))],
        out_specs=pl.BlockSpec(memory_space=pl.ANY),
        scratch_shapes=[pltpu.VMEM((MOE_BLOCK * TILE_ROWS, LANES), F32),
                        pltpu.VMEM((2, tile * TOP_K * TILE_ROWS, LANES), F32),
                        pltpu.SemaphoreType.DMA((2,)), pltpu.SemaphoreType.DMA(())],
    )
    return pl.pallas_call(
        functools.partial(_dispatch_kernel, tile=tile, n_blocks=n_blocks),
        grid_spec=grid_spec,
        out_shape=jax.ShapeDtypeStruct((n_blocks * MOE_BLOCK * TILE_ROWS, LANES), F32),
        compiler_params=pltpu.CompilerParams(dimension_semantics=("arbitrary",),
                                             vmem_limit_bytes=VMEM_LIMIT),
        name="moe_dispatch",
    )(block_valid, spos_flat, runs_flat, xm_tiles)


def _expert_kernel(be_ref, used_ref, first_ref, next_ref, slot_ref, valid_ref, x_ref, wgu_ref, bgu_ref,
                   wd_ref, bd_ref, o_ref, wbuf_gu, wbuf_d, s_wgu, s_wd, sem):
    i = pl.program_id(0)

    @pl.when(i >= used_ref[0])
    def _():
        o_ref[...] = jnp.zeros_like(o_ref)

    def weight_copies(e, slot):
        return (pltpu.make_async_copy(wgu_ref.at[e], wbuf_gu.at[slot], sem.at[0, slot]),
                pltpu.make_async_copy(wd_ref.at[e], wbuf_d.at[slot], sem.at[1, slot]))

    @pl.when(i == 0)
    def _():
        for cp in weight_copies(be_ref[0], slot_ref[0]):
            cp.start()

    @pl.when(first_ref[i] == 1)
    def _():
        slot = slot_ref[i]
        for cp in weight_copies(be_ref[i], slot):
            cp.wait()

        @pl.when(next_ref[i] >= 0)
        def _():
            for cp in weight_copies(next_ref[i], 1 - slot):
                cp.start()

        s_wgu[...] = wbuf_gu[slot].astype(BF16)
        s_wd[...] = wbuf_d[slot].astype(BF16)

    def ffn(nrows):
        pieces = [x_ref[pl.ds(s, nrows, stride=TILE_ROWS), :] for s in range(TILE_ROWS)]
        x = jnp.concatenate(pieces, axis=1).astype(BF16)
        gu = jnp.dot(x, s_wgu[...], preferred_element_type=F32) + bgu_ref[0]
        g = jnp.minimum(gu[:, :D_FF], SWIGLU_LIMIT)
        up = jnp.clip(gu[:, D_FF:], -SWIGLU_LIMIT, SWIGLU_LIMIT)
        hdn = (up + 1.0) * (g * jax.nn.sigmoid(SWIGLU_ALPHA * g))
        out = jnp.dot(hdn.astype(BF16), s_wd[...], preferred_element_type=F32) + bd_ref[0]
        for s in range(TILE_ROWS):
            o_ref[pl.ds(s, nrows, stride=TILE_ROWS), :] = out[:, s * LANES:(s + 1) * LANES]
        if nrows < MOE_BLOCK:
            o_ref[nrows * TILE_ROWS:, :] = jnp.zeros(((MOE_BLOCK - nrows) * TILE_ROWS, LANES), F32)

    in_use = i < used_ref[0]
    half_empty = valid_ref[i] <= MOE_BLOCK // 2

    @pl.when(jnp.logical_and(in_use, jnp.logical_not(half_empty)))
    def _():
        ffn(MOE_BLOCK)

    @pl.when(jnp.logical_and(in_use, half_empty))
    def _():
        ffn(MOE_BLOCK // 2)


def _experts(meta, buf, wgu, bgu, wd, bd, n_blocks):
    blk_rows = MOE_BLOCK * TILE_ROWS
    block_e, n_used = meta[:n_blocks, 0], meta[:1, 2]
    first, nxt, slot = meta[:n_blocks, 3], meta[:n_blocks, 4], meta[:n_blocks, 5]
    valid = meta[:n_blocks, 1]

    def in_row_map(i, be, nu, *_):
        return (jnp.minimum(i, nu[0] - 1), 0)

    def b_map(i, be, *_):
        return (be[i], 0, 0)

    grid_spec = pltpu.PrefetchScalarGridSpec(
        num_scalar_prefetch=6,
        grid=(n_blocks,),
        in_specs=[pl.BlockSpec((blk_rows, LANES), in_row_map),
                  pl.BlockSpec(memory_space=pl.ANY),
                  pl.BlockSpec((1, 1, 2 * D_FF), b_map),
                  pl.BlockSpec(memory_space=pl.ANY),
                  pl.BlockSpec((1, 1, D_MODEL), b_map)],
        out_specs=pl.BlockSpec((blk_rows, LANES), lambda i, *_: (i, 0)),
        scratch_shapes=[pltpu.VMEM((2, D_MODEL, 2 * D_FF), F32), pltpu.VMEM((2, D_FF, D_MODEL), F32),
                        pltpu.VMEM((D_MODEL, 2 * D_FF), BF16), pltpu.VMEM((D_FF, D_MODEL), BF16),
                        pltpu.SemaphoreType.DMA((2, 2))],
    )
    return pl.pallas_call(
        _expert_kernel,
        grid_spec=grid_spec,
        out_shape=jax.ShapeDtypeStruct(buf.shape, F32),
        compiler_params=pltpu.CompilerParams(dimension_semantics=("arbitrary",),
                                             vmem_limit_bytes=VMEM_LIMIT),
        name="moe_experts",
    )(block_e, n_used, first, nxt, slot, valid, buf, wgu, bgu, wd, bd)


def _combine_kernel(spos_ref, runs_ref, next_runs_ref, gate_ref, x1_ref, rows_ref, nf_ref, yp_ref,
                    ys_ref, stage, gbuf, sem, *, batch, n_prompt):
    i = pl.program_id(0)
    tile = batch * CHUNK
    n_rows = tile * TOP_K * TILE_ROWS
    slot = i % 2

    @pl.when(i == 0)
    def _():
        _run_copies(runs_ref, rows_ref, stage.at[0], sem.at[0], to_hbm=False)

    @pl.when(i + 1 < pl.num_programs(0))
    def _():
        _run_copies(next_runs_ref, rows_ref, stage.at[1 - slot], sem.at[1 - slot], to_hbm=False)

    pltpu.make_async_copy(rows_ref.at[pl.ds(0, n_rows)], stage.at[slot], sem.at[slot]).wait()

    def pick(j, carry):
        for k in range(TOP_K):
            p = spos_ref[j * TOP_K + k]
            gbuf[k, pl.ds(pl.multiple_of(j * TILE_ROWS, TILE_ROWS), TILE_ROWS), :] = (
                stage[slot, pl.ds(pl.multiple_of(p * TILE_ROWS, TILE_ROWS), TILE_ROWS), :])
        return carry

    lax.fori_loop(0, tile, pick, 0, unroll=8)

    gates = gate_ref[...]
    pieces = []
    for s in range(TILE_ROWS):
        piece = x1_ref[:, s * LANES:(s + 1) * LANES]
        for k in range(TOP_K):
            piece = piece + gates[:, k:k + 1] * gbuf[k, pl.ds(s, tile, stride=TILE_ROWS), :]
        pieces.append(piece)
    y = _rmsnorm(jnp.concatenate(pieces, axis=1), nf_ref[...]).reshape(batch, CHUNK, D_MODEL)

    @pl.when(i < n_prompt)
    def _():
        yp_ref[...] = y

    @pl.when(i >= n_prompt)
    def _():
        ys_ref[...] = y


def _combine(spos_flat, runs_flat, gates, x1, out_rows, norm_final, batch, lp, ls):
    tile = batch * CHUNK
    n_prompt = lp // CHUNK
    n_sample = ls // CHUNK
    last = n_prompt + n_sample - 1
    return pl.pallas_call(
        functools.partial(_combine_kernel, batch=batch, n_prompt=n_prompt),
        grid=(n_prompt + n_sample,),
        in_specs=[pl.BlockSpec((tile * TOP_K,), lambda i: (i,), memory_space=pltpu.SMEM),
                  pl.BlockSpec((SUBLANES * LANES,), lambda i: (i,), memory_space=pltpu.SMEM),
                  pl.BlockSpec((SUBLANES * LANES,), lambda i: (jnp.minimum(i + 1, last),),
                               memory_space=pltpu.SMEM),
                  pl.BlockSpec((tile, TOP_K), lambda i: (i, 0)),
                  pl.BlockSpec((tile, D_MODEL), lambda i: (i, 0)),
                  pl.BlockSpec(memory_space=pl.ANY),
                  pl.BlockSpec((1, D_MODEL), lambda i: (0, 0))],
        out_specs=(pl.BlockSpec((batch, CHUNK, D_MODEL), lambda i: (0, jnp.minimum(i, n_prompt - 1), 0)),
                   pl.BlockSpec((batch, CHUNK, D_MODEL), lambda i: (0, jnp.maximum(i - n_prompt, 0), 0))),
        out_shape=(jax.ShapeDtypeStruct((batch, lp, D_MODEL), F32),
                   jax.ShapeDtypeStruct((batch, ls, D_MODEL), F32)),
        scratch_shapes=[pltpu.VMEM((2, tile * TOP_K * TILE_ROWS, LANES), F32),
                        pltpu.VMEM((TOP_K, tile * TILE_ROWS, LANES), F32),
                        pltpu.SemaphoreType.DMA((2,))],
        compiler_params=pltpu.CompilerParams(dimension_semantics=("arbitrary",),
                                             vmem_limit_bytes=VMEM_LIMIT),
        name="moe_combine",
    )(spos_flat, runs_flat, runs_flat, gates, x1, out_rows, norm_final)


def _prep_weights(norm_mix, w_in, b_ig, b_fg, s5_a_re, s5_a_im, s5_log_dt, s5_b_re, s5_b_im,
                  s5_c_re, s5_c_im, s5_d, w_s5_glu, ml_gn, w_ml_out, w_out, norm_moe, w_router,
                  b_router):
    w = w_in[0]
    o_ig = D_S5 + 4 * D_ML
    o_gs5 = o_ig + 2 * ML_HEADS
    w_a = w[:, :o_ig].astype(BF16)
    w_b = w[:, o_gs5:].astype(BF16)
    w_g = jnp.concatenate([w[:, o_ig:o_gs5], jnp.zeros((D_MODEL, LANES - 2 * ML_HEADS), F32)],
                          axis=1).astype(BF16)
    gate_bias = jnp.concatenate([b_ig[0], b_fg[0], jnp.zeros((LANES - 2 * ML_HEADS,), F32)])[None]

    ar, ai = s5_a_re[0], s5_a_im[0]
    dt = jnp.exp(s5_log_dt[0])[:, None]
    mag = jnp.exp(dt * ar)
    abar_re = mag * jnp.cos(dt * ai)
    abar_im = mag * jnp.sin(dt * ai)
    den = ar * ar + ai * ai
    fr = ((abar_re - 1.0) * ar + abar_im * ai) / den
    fi = (abar_im * ar - (abar_re - 1.0) * ai) / den
    br, bi = s5_b_re[0], s5_b_im[0]
    bbar_re = fr[..., None] * br - fi[..., None] * bi
    bbar_im = fr[..., None] * bi + fi[..., None] * br
    gh = S5_GROUPS // 2
    same_group = (jnp.arange(gh)[:, None, None, None] == jnp.arange(gh)[None, None, :, None])

    def blockdiag_in(bb):
        t = bb.reshape(2, gh, S5_STATE, S5_GROUP).transpose(0, 1, 3, 2)
        full = jnp.where(same_group[None], t[:, :, :, None, :], 0.0)
        return full.reshape(2, gh * S5_GROUP, gh * S5_STATE).astype(BF16)

    def blockdiag_out(cc):
        t = cc.reshape(2, gh, S5_GROUP, S5_STATE).transpose(0, 1, 3, 2)
        full = jnp.where(same_group[None], t[:, :, :, None, :], 0.0)
        return full.reshape(2, gh * S5_STATE, gh * S5_GROUP)

    cb = jnp.concatenate([blockdiag_out(s5_c_re[0]), -blockdiag_out(s5_c_im[0])], axis=1).astype(BF16)
    w_router_p = jnp.concatenate(
        [w_router[0], jnp.zeros((D_MODEL, LANES - N_EXPERTS), F32)], axis=1).astype(BF16)
    b_router_p = jnp.concatenate([b_router[0], jnp.zeros((LANES - N_EXPERTS,), F32)])[None]
    return dict(
        norm_mix=norm_mix[0][None], w_in_a=w_a, w_in_b=w_b, w_in_g=w_g, gate_bias=gate_bias,
        abar_re=abar_re.reshape(1, S5_CH), abar_im=abar_im.reshape(1, S5_CH),
        bd_re=blockdiag_in(bbar_re), bd_im=blockdiag_in(bbar_im), cb=cb,
        d_skip=s5_d[0][None], w_glu=w_s5_glu[0].astype(BF16), ml_gn=ml_gn[0][None],
        w_ml_out=w_ml_out[0].astype(BF16), w_out=w_out[0].astype(BF16),
        norm_moe=norm_moe[0][None], w_router=w_router_p, b_router=b_router_p)


def kernel(x_prompt, x_sample, state_s5_re, state_s5_im, state_ml_C, state_ml_n, state_ml_m, norm_mix, w_in, b_ig, b_fg, s5_a_re, s5_a_im, s5_log_dt, s5_b_re, s5_b_im, s5_c_re, s5_c_im, s5_d, w_s5_glu, ml_gn, w_ml_out, w_out, norm_moe, w_router, b_router, w_gate_up, b_gate_up, w_down, b_down, norm_final):
    w = _prep_weights(norm_mix, w_in, b_ig, b_fg, s5_a_re, s5_a_im, s5_log_dt, s5_b_re, s5_b_im,
                      s5_c_re, s5_c_im, s5_d, w_s5_glu, ml_gn, w_ml_out, w_out, norm_moe,
                      w_router, b_router)
    batch, lp, _ = x_prompt.shape
    ls = x_sample.shape[1]
    bh = batch * ML_HEADS
    cx0 = jnp.concatenate(
        [jnp.swapaxes(state_ml_C[0].reshape(bh, ML_HEAD_DIM, ML_HEAD_DIM), 1, 2),
         jnp.broadcast_to(state_ml_n[0].reshape(bh, ML_HEAD_DIM, 1), (bh, ML_HEAD_DIM, ML_HEAD_DIM))],
        axis=2)
    x1, xm, idx, gates, counts, hr, hi, cx, m = _mixer(
        x_prompt, x_sample,
        state_s5_re[0].reshape(batch, S5_CH), state_s5_im[0].reshape(batch, S5_CH), cx0,
        jnp.broadcast_to(state_ml_m[0].reshape(bh, 1), (bh, LANES)), w)
    c = jnp.swapaxes(cx[:, :, :, :ML_HEAD_DIM], 2, 3)
    n = cx[:, :, :, ML_HEAD_DIM]

    tile = batch * CHUNK
    t_all = idx.shape[0]
    n_blocks = -(-(t_all * TOP_K + N_EXPERTS * (MOE_BLOCK - 1)) // MOE_BLOCK)
    n_blocks_pad = -(-n_blocks // SUBLANES) * SUBLANES
    spos, runs, meta = _plan(counts, idx, tile, n_blocks_pad)
    spos_flat = spos.reshape(t_all * TOP_K)
    runs_flat = runs.reshape(-1)
    buf = _dispatch(meta[:n_blocks, 1], spos_flat, runs_flat, xm, tile, n_blocks)
    out_rows = _experts(meta, buf, w_gate_up[0], b_gate_up[0][:, None, :], w_down[0],
                        b_down[0][:, None, :], n_blocks)
    y_prompt, y_sample = _combine(spos_flat, runs_flat, gates, x1, out_rows, norm_final[None], batch,
                                  lp, ls)

    def states(p):
        return (hr[p].reshape(1, batch, S5_GROUPS, S5_STATE), hi[p].reshape(1, batch, S5_GROUPS, S5_STATE),
                c[p].reshape(1, batch, ML_HEADS, ML_HEAD_DIM, ML_HEAD_DIM),
                n[p].reshape(1, batch, ML_HEADS, ML_HEAD_DIM), m[p, :, 0].reshape(1, batch, ML_HEADS))

    return (y_prompt, y_sample) + states(0) + states(1)
```

```python
import functools

import jax
import jax.numpy as jnp
from jax import lax
from jax.experimental import pallas as pl
from jax.experimental.pallas import tpu as pltpu

F32 = jnp.float32
BF16 = jnp.bfloat16

D_MODEL = 1024
CHUNK = 64
EPS = 1e-6
D_S5 = 512
S5_GROUP = 16
S5_GROUPS = D_S5 // S5_GROUP
S5_STATE = 64
S5_CH = S5_GROUPS * S5_STATE
ML_HEADS = 4
ML_HEAD_DIM = 128
D_ML = ML_HEADS * ML_HEAD_DIM
N_EXPERTS = 32
TOP_K = 4
D_FF = 1024
SWIGLU_LIMIT = 7.0
SWIGLU_ALPHA = 1.702

LANES = 128
SUBLANES = 8
TILE_ROWS = D_MODEL // LANES

OFF_U = 0
OFF_Q = OFF_U + D_S5
OFF_K = OFF_Q + D_ML
OFF_V = OFF_K + D_ML
OFF_O = OFF_V + D_ML
OFF_GS5 = OFF_O + D_ML
OFF_GML = OFF_GS5 + D_MODEL
OFF_GATE = OFF_GML + D_MODEL
W_COLS = OFF_GATE + LANES

S5_HALF_IN = D_S5 // 2
S5_HALF_CH = S5_CH // 2
S5_HALF_TILES = S5_HALF_CH // LANES
S5_SCAN_TILES = 4
ML_GROUP = 2
MOE_BLOCK = 512
VMEM_LIMIT = 60 * 1024 * 1024


def _rmsnorm(x, w):
    return x * lax.rsqrt(jnp.mean(x * x, axis=-1, keepdims=True) + EPS) * w


def _const_spec(shape):
    nd = len(shape)
    return pl.BlockSpec(shape, lambda *_: (0,) * nd, pipeline_mode=pl.Buffered(1))


def _mixer_kernel(xp_ref, xs_ref, h0r_ref, h0i_ref, cx0_ref, m0_ref,
                  nmix_ref, wina_ref, winb_ref, wing_ref, gbias_ref, abr_ref, abi_ref, bdr_ref, bdi_ref, cb_ref,
                  dskip_ref, glu_ref, gn_ref, wml_ref, wout_ref, nmoe_ref, wr_ref, br_ref,
                  x1_ref, xm_ref, idx_ref, gate_ref, cnt_ref, hr_ref, hi_ref, cx_ref, m_ref,
                  s_bur, s_bui, s_tm, s_q, s_k, s_v, s_o, s_col, s_gated, s_cx, sem, *, batch, n_prompt):
    i = pl.program_id(0)
    rows = batch * CHUNK

    @pl.when(i == 0)
    def _():
        hr_ref[...] = jnp.zeros_like(hr_ref)
        hi_ref[...] = jnp.zeros_like(hi_ref)
        m_ref[...] = jnp.zeros_like(m_ref)
        s_cx[...] = jnp.zeros_like(s_cx)
        cnt_ref[...] = jnp.zeros_like(cnt_ref)

    @pl.when(i == n_prompt)
    def _():
        hr_ref[0] = h0r_ref[...]
        hi_ref[0] = h0i_ref[...]
        m_ref[0] = m0_ref[...]
        load = pltpu.make_async_copy(cx0_ref, s_cx, sem)
        load.start()
        load.wait()

    @pl.when(i < n_prompt)
    def _():
        x1_ref[...] = xp_ref[...].reshape(rows, D_MODEL)

    @pl.when(i >= n_prompt)
    def _():
        x1_ref[...] = xs_ref[...].reshape(rows, D_MODEL)

    xn = _rmsnorm(x1_ref[...], nmix_ref[...]).astype(BF16)

    def proj(off, width):
        if off >= OFF_GATE:
            w_cols = wing_ref[...]
        elif off >= OFF_GS5:
            w_cols = winb_ref[:, off - OFF_GS5:off - OFF_GS5 + width]
        else:
            w_cols = wina_ref[:, off:off + width]
        return jnp.dot(xn, w_cols, preferred_element_type=F32)

    gates = proj(OFF_GATE, LANES) + gbias_ref[...]
    lane_g = lax.broadcasted_iota(jnp.int32, (rows, LANES), 1)
    gg = jnp.where(lane_g < ML_HEADS, gates, jax.nn.log_sigmoid(gates))
    gt8 = gg.T[0:SUBLANES, :]
    pos = lax.broadcasted_iota(jnp.int32, (SUBLANES, rows), 1) % CHUNK
    cum = gt8
    shift = 1
    while shift < CHUNK:
        cum = cum + jnp.where(pos >= shift, pltpu.roll(cum, shift, axis=1), 0.0)
        shift *= 2
    g8 = gt8 - pltpu.roll(cum, ML_HEADS, axis=0)
    mx8 = g8
    shift = 1
    while shift < CHUNK:
        mx8 = jnp.maximum(mx8, jnp.where(pos >= shift, pltpu.roll(mx8, shift, axis=1), -jnp.inf))
        shift *= 2
    sub = lax.broadcasted_iota(jnp.int32, (SUBLANES, rows), 0)
    top8 = jnp.where(sub < ML_HEADS, g8, cum)
    s_col[...] = jnp.concatenate(
        [top8, mx8, jnp.zeros((LANES - 2 * SUBLANES, rows), F32)], axis=0).T
    s_q[...] = (proj(OFF_Q, D_ML) * (ML_HEAD_DIM ** -0.5)).astype(BF16)
    s_k[...] = proj(OFF_K, D_ML).astype(BF16)
    s_v[...] = proj(OFF_V, D_ML).astype(BF16)
    s_o[...] = proj(OFF_O, D_ML)

    tri = (lax.broadcasted_iota(jnp.int32, (CHUNK, CHUNK), 0)
           >= lax.broadcasted_iota(jnp.int32, (CHUNK, CHUNK), 1))[None]
    bdims = ((0,), (0,))

    def rowsl(b):
        return slice(b * CHUNK, (b + 1) * CHUNK)

    def headsl(h):
        return slice(h * ML_HEAD_DIM, (h + 1) * ML_HEAD_DIM)

    for g0 in range(0, batch, ML_GROUP):
        items = [(b, h) for b in range(g0, g0 + ML_GROUP) for h in range(ML_HEADS)]
        def col(j, items=items):
            return jnp.stack([jnp.broadcast_to(s_col[rowsl(b), j + h:j + h + 1], (CHUNK, LANES))
                              for b, h in items])

        g_c, b_c, mx_c = col(0), col(ML_HEADS), col(2 * ML_HEADS)
        g_r = jnp.stack([g8[h:h + 1, rowsl(b)] for b, h in items])
        m_prev = jnp.stack([m_ref[0, b * ML_HEADS + h:b * ML_HEADS + h + 1, :] for b, h in items])
        q3 = jnp.stack([s_q[rowsl(b), headsl(h)] for b, h in items])
        k3 = jnp.stack([s_k[rowsl(b), headsl(h)] for b, h in items])
        v3 = jnp.stack([s_v[rowsl(b), headsl(h)] for b, h in items])
        cx = jnp.stack([s_cx[b * ML_HEADS + h] for b, h in items])

        big_m = jnp.maximum(m_prev, mx_c)
        p = jnp.exp(jnp.where(tri, g_r - big_m[:, :, :CHUNK], -jnp.inf))
        w_inter = jnp.exp(m_prev - big_m)
        s = lax.dot_general(q3, k3, (((2,), (2,)), bdims), preferred_element_type=F32) * p
        cqx = lax.dot_general(q3, cx.astype(BF16), (((2,), (1,)), bdims), preferred_element_type=F32)
        num = (lax.dot_general(s.astype(BF16), v3, (((2,), (1,)), bdims), preferred_element_type=F32)
               + w_inter * cqx[:, :, :ML_HEAD_DIM])
        den_dot = jnp.sum(s, axis=2, keepdims=True) + w_inter * cqx[:, :, ML_HEAD_DIM:]
        m_t = b_c + big_m
        hout = num / jnp.maximum(jnp.abs(den_dot), jnp.exp(-m_t))

        m_last = big_m[:, CHUNK - 1:CHUNK, :]
        w_end = jnp.exp(g_c - m_last)
        decay = jnp.exp(m_prev - m_last)
        wvx = jnp.concatenate([w_end * v3.astype(F32), w_end], axis=2).astype(BF16)
        k_t = jnp.stack([s_k[rowsl(b), headsl(h)].astype(F32).T.astype(BF16) for b, h in items])
        cx_new = (jnp.concatenate([decay, decay], axis=2) * cx
                  + lax.dot_general(k_t, wvx, (((2,), (1,)), bdims), preferred_element_type=F32))
        m_new = m_t[:, CHUNK - 1:CHUNK, :]

        hc = hout - jnp.mean(hout, axis=2, keepdims=True)
        hn = hc * lax.rsqrt(jnp.mean(hc * hc, axis=2, keepdims=True) + EPS)
        for n, (b, h) in enumerate(items):
            bh = b * ML_HEADS + h
            s_cx[bh] = cx_new[n]
            m_ref[0, bh:bh + 1, :] = m_new[n]
            s_gated[rowsl(b), headsl(h)] = (jax.nn.sigmoid(s_o[rowsl(b), headsl(h)])
                                            * (hn[n] * gn_ref[:, headsl(h)])).astype(BF16)

    u = proj(OFF_U, D_S5)
    in_tiles = D_S5 // LANES
    for j in range(in_tiles):
        for b in range(batch):
            s_tm[j, pl.ds(b, CHUNK, stride=batch), :] = u[b * CHUNK:(b + 1) * CHUNK,
                                                          j * LANES:(j + 1) * LANES]
    ub = jnp.concatenate([s_tm[j] for j in range(in_tiles)], axis=1).astype(BF16)
    ys = []
    merge = []
    for k in range(2):
        uk = ub[:, k * S5_HALF_IN:(k + 1) * S5_HALF_IN]
        bur = jnp.dot(uk, bdr_ref[k], preferred_element_type=F32)
        bui = jnp.dot(uk, bdi_ref[k], preferred_element_type=F32)
        for j in range(S5_HALF_TILES):
            s_bur[j] = bur[:, j * LANES:(j + 1) * LANES]
            s_bui[j] = bui[:, j * LANES:(j + 1) * LANES]

        if k == 0:
            merge.append(jax.nn.sigmoid(proj(OFF_GML, D_MODEL)) * jnp.dot(
                s_gated[...], wml_ref[...], preferred_element_type=F32))
        else:
            merge.append(jax.nn.sigmoid(proj(OFF_GS5, D_MODEL)))

        for c in range(S5_HALF_TILES // S5_SCAN_TILES):
            tiles = range(c * S5_SCAN_TILES, (c + 1) * S5_SCAN_TILES)
            lanes = [slice(k * S5_HALF_CH + j * LANES, k * S5_HALF_CH + (j + 1) * LANES) for j in tiles]
            ar = [jnp.broadcast_to(abr_ref[:, ls], (batch, LANES)) for ls in lanes]
            ai = [jnp.broadcast_to(abi_ref[:, ls], (batch, LANES)) for ls in lanes]

            def step(t, carry, tiles=tiles, ar=ar, ai=ai):
                slab = pl.ds(pl.multiple_of(t * batch, batch), batch)
                out = []
                for n, j in enumerate(tiles):
                    hr, hi = carry[n]
                    nr = ar[n] * hr - ai[n] * hi + s_bur[j, slab, :]
                    ni = ar[n] * hi + ai[n] * hr + s_bui[j, slab, :]
                    s_bur[j, slab, :] = nr
                    s_bui[j, slab, :] = ni
                    out.append((nr, ni))
                return tuple(out)

            init = tuple((hr_ref[0, :, ls], hi_ref[0, :, ls]) for ls in lanes)
            fin = lax.fori_loop(0, CHUNK, step, init, unroll=True)
            for n, ls in enumerate(lanes):
                hr_ref[0, :, ls] = fin[n][0]
                hi_ref[0, :, ls] = fin[n][1]

        def states(ref):
            return jnp.concatenate([ref[j] for j in range(S5_HALF_TILES)], axis=1).astype(BF16)

        yk = jnp.dot(states(s_bur), cb_ref[k, :S5_HALF_CH, :], preferred_element_type=F32)
        yk = yk + jnp.dot(states(s_bui), cb_ref[k, S5_HALF_CH:, :], preferred_element_type=F32)
        ys.append(yk)
    y_tm = jnp.concatenate(ys, axis=1)
    for j in range(in_tiles):
        s_tm[j] = y_tm[:, j * LANES:(j + 1) * LANES]
    y = jnp.concatenate(
        [jnp.concatenate([s_tm[j, pl.ds(b, CHUNK, stride=batch), :] for j in range(in_tiles)], axis=1)
         for b in range(batch)], axis=0) + dskip_ref[...] * u
    glu = jnp.dot(jax.nn.gelu(y).astype(BF16), glu_ref[...], preferred_element_type=F32)
    y_s5 = glu[:, :D_MODEL] * jax.nn.sigmoid(glu[:, D_MODEL:])
    mix = merge[0] + merge[1] * y_s5

    x1 = x1_ref[...] + jnp.dot(mix.astype(BF16), wout_ref[...], preferred_element_type=F32)
    x1_ref[...] = x1

    xm = _rmsnorm(x1, nmoe_ref[...])
    logits = jnp.dot(xm.astype(BF16), wr_ref[...], preferred_element_type=F32) + br_ref[...]
    lane = lax.broadcasted_iota(jnp.int32, (rows, LANES), 1)
    lane_f = lane.astype(F32)
    logits = jnp.where(lane < N_EXPERTS, logits, -jnp.inf)
    vals, idxs = [], []
    for _ in range(TOP_K):
        mx = jnp.max(logits, axis=1, keepdims=True)
        am = jnp.min(jnp.where(logits == mx, lane_f, float(LANES)), axis=1, keepdims=True)
        vals.append(mx)
        idxs.append(am)
        logits = jnp.where(lane_f == am, -jnp.inf, logits)
    exps = [jnp.exp(v - vals[0]) for v in vals]
    esum = exps[0] + exps[1] + exps[2] + exps[3]
    idx_w = jnp.zeros((rows, LANES), F32)
    gate_w = jnp.zeros((rows, LANES), F32)
    for k in range(TOP_K):
        idx_w = jnp.where(lane == k, idxs[k], idx_w)
        gate_w = jnp.where(lane == k, exps[k] / esum, gate_w)
    idx_ref[...] = idx_w[:, :TOP_K].astype(jnp.int32)
    gate_ref[...] = gate_w[:, :TOP_K]
    chosen = jnp.zeros((rows, LANES), F32)
    for k in range(TOP_K):
        chosen = chosen + (lane_f == idxs[k]).astype(F32)
    cnt_ref[...] += jnp.sum(chosen, axis=0, keepdims=True)

    for s in range(TILE_ROWS):
        xm_ref[pl.ds(s, rows, stride=TILE_ROWS), :] = xm[:, s * LANES:(s + 1) * LANES]

    for phase_id, final_step in ((0, n_prompt - 1), (1, pl.num_programs(0) - 1)):
        @pl.when(i == final_step)
        def _(phase_id=phase_id):
            store = pltpu.make_async_copy(s_cx, cx_ref.at[phase_id], sem)
            store.start()
            store.wait()


def _mixer(x_prompt, x_sample, h0r, h0i, cx0, m0, w):
    batch, lp, _ = x_prompt.shape
    assert x_sample.shape[0] == batch and lp % CHUNK == 0 and x_sample.shape[1] % CHUNK == 0
    assert batch % ML_GROUP == 0
    n_prompt = lp // CHUNK
    n_sample = x_sample.shape[1] // CHUNK
    nblk = n_prompt + n_sample
    rows = batch * CHUNK
    t_all = nblk * rows
    bh = batch * ML_HEADS

    def phase(i):
        return jnp.minimum(i // n_prompt, 1)

    state_specs = [
        _const_spec((batch, S5_CH)), _const_spec((batch, S5_CH)),
        pl.BlockSpec(memory_space=pl.ANY),
        _const_spec((bh, LANES)),
    ]
    weights = (w['norm_mix'], w['w_in_a'], w['w_in_b'], w['w_in_g'], w['gate_bias'], w['abar_re'], w['abar_im'], w['bd_re'],
               w['bd_im'], w['cb'], w['d_skip'], w['w_glu'], w['ml_gn'], w['w_ml_out'], w['w_out'],
               w['norm_moe'], w['w_router'], w['b_router'])
    in_specs = ([pl.BlockSpec((batch, CHUNK, D_MODEL), lambda i: (0, jnp.minimum(i, n_prompt - 1), 0)),
                 pl.BlockSpec((batch, CHUNK, D_MODEL), lambda i: (0, jnp.maximum(i - n_prompt, 0), 0))]
                + state_specs + [_const_spec(a.shape) for a in weights])
    out_shape = (
        jax.ShapeDtypeStruct((t_all, D_MODEL), F32),
        jax.ShapeDtypeStruct((t_all * TILE_ROWS, LANES), F32),
        jax.ShapeDtypeStruct((t_all, TOP_K), jnp.int32),
        jax.ShapeDtypeStruct((t_all, TOP_K), F32),
        jax.ShapeDtypeStruct((1, LANES), F32),
        jax.ShapeDtypeStruct((2, batch, S5_CH), F32),
        jax.ShapeDtypeStruct((2, batch, S5_CH), F32),
        jax.ShapeDtypeStruct((2, bh, ML_HEAD_DIM, 2 * ML_HEAD_DIM), F32),
        jax.ShapeDtypeStruct((2, bh, LANES), F32),
    )
    out_specs = (
        pl.BlockSpec((rows, D_MODEL), lambda i: (i, 0)),
        pl.BlockSpec((rows * TILE_ROWS, LANES), lambda i: (i, 0)),
        pl.BlockSpec((rows, TOP_K), lambda i: (i, 0)),
        pl.BlockSpec((rows, TOP_K), lambda i: (i, 0)),
        pl.BlockSpec((1, LANES), lambda i: (0, 0)),
        pl.BlockSpec((1, batch, S5_CH), lambda i: (phase(i), 0, 0)),
        pl.BlockSpec((1, batch, S5_CH), lambda i: (phase(i), 0, 0)),
        pl.BlockSpec(memory_space=pl.ANY),
        pl.BlockSpec((1, bh, LANES), lambda i: (phase(i), 0, 0)),
    )
    scratch = [
        pltpu.VMEM((S5_HALF_TILES, rows, LANES), F32), pltpu.VMEM((S5_HALF_TILES, rows, LANES), F32),
        pltpu.VMEM((D_S5 // LANES, rows, LANES), F32),
        pltpu.VMEM((rows, D_ML), BF16), pltpu.VMEM((rows, D_ML), BF16), pltpu.VMEM((rows, D_ML), BF16),
        pltpu.VMEM((rows, D_ML), F32),
        pltpu.VMEM((rows, LANES), F32),
        pltpu.VMEM((rows, D_ML), BF16),
        pltpu.VMEM((bh, ML_HEAD_DIM, 2 * ML_HEAD_DIM), F32),
        pltpu.SemaphoreType.DMA(()),
    ]
    return pl.pallas_call(
        functools.partial(_mixer_kernel, batch=batch, n_prompt=n_prompt),
        grid=(nblk,),
        in_specs=in_specs,
        out_specs=out_specs,
        out_shape=out_shape,
        scratch_shapes=scratch,
        compiler_params=pltpu.CompilerParams(dimension_semantics=("arbitrary",),
                                             vmem_limit_bytes=VMEM_LIMIT),
        name="mixer",
    )(x_prompt, x_sample, h0r, h0i, cx0, m0, *weights)


def _plan_kernel(cnt_ref, idx_ref, spos_ref, runs_ref, meta_ref, s_run, s_pstart, *, tile,
                 n_blocks_pad):
    i = pl.program_id(0)
    lane = lax.broadcasted_iota(jnp.int32, (tile, LANES), 1)
    idx = idx_ref[...]
    onehots = [(lane == idx[:, k:k + 1]).astype(F32) for k in range(TOP_K)]
    mask = onehots[0] + onehots[1] + onehots[2] + onehots[3]

    @pl.when(i == 0)
    def _():
        cnt = cnt_ref[...]
        nblk = jnp.floor((cnt + (MOE_BLOCK - 1)) * (1.0 / MOE_BLOCK))
        lane1 = lax.broadcasted_iota(jnp.int32, (1, LANES), 1)
        pend = nblk
        shift = 1
        while shift < LANES:
            pend = pend + jnp.where(lane1 >= shift, pltpu.roll(pend, shift, axis=1), 0.0)
            shift *= 2
        s_pstart[...] = (pend - nblk) * MOE_BLOCK
        s_run[...] = jnp.zeros_like(s_run)
        blk = lax.broadcasted_iota(jnp.int32, (n_blocks_pad, LANES), 0).astype(F32)
        lane_b = lax.broadcasted_iota(jnp.int32, (n_blocks_pad, LANES), 1)
        lane_f = lane_b.astype(F32)
        is_e = lane_b < N_EXPERTS
        done = jnp.logical_and(is_e, pend <= blk)
        e_of = jnp.minimum(jnp.sum(done.astype(F32), axis=1, keepdims=True), N_EXPERTS - 1.0)
        mine = lane_f == e_of
        blk_in_e = jnp.sum(jnp.where(mine, blk - (pend - nblk), 0.0), axis=1, keepdims=True)
        cnt_e = jnp.sum(jnp.where(mine, cnt, 0.0), axis=1, keepdims=True)
        valid = jnp.clip(cnt_e - blk_in_e * MOE_BLOCK, 0.0, float(MOE_BLOCK))
        used = jnp.sum(jnp.where(lane1 == N_EXPERTS - 1, pend, 0.0), axis=1, keepdims=True)
        owns = jnp.logical_and(is_e, nblk > 0.0)
        first = jnp.logical_and(blk_in_e == 0.0, blk[:, :1] < used).astype(F32)
        later = jnp.logical_and(owns, lane_f > e_of)
        nxt = jnp.min(jnp.where(later, lane_f, float(LANES)), axis=1, keepdims=True)
        nxt = jnp.where(nxt < float(LANES), nxt, -1.0)
        run = jnp.sum(jnp.logical_and(owns, lane_f < e_of).astype(F32), axis=1, keepdims=True)
        parity = run - 2.0 * jnp.floor(run * 0.5)
        meta = jnp.zeros((n_blocks_pad, LANES), F32)
        for col, val in enumerate((e_of, valid, used, first, nxt, parity)):
            meta = jnp.where(lane_b == col, val, meta)
        meta_ref[...] = meta.astype(jnp.int32)

    r = lax.broadcasted_iota(jnp.int32, (tile, tile), 0)
    c = lax.broadcasted_iota(jnp.int32, (tile, tile), 1)
    lower = (c < r).astype(BF16)
    before = jnp.dot(lower, mask.astype(BF16), preferred_element_type=F32)
    tile_cnt = jnp.sum(mask, axis=0, keepdims=True)
    lane1 = lax.broadcasted_iota(jnp.int32, (1, LANES), 1)
    incl = tile_cnt
    shift = 1
    while shift < LANES:
        incl = incl + jnp.where(lane1 >= shift, pltpu.roll(incl, shift, axis=1), 0.0)
        shift *= 2
    tile_start = incl - tile_cnt
    parity = (i % 2).astype(F32)
    base = (before + tile_start + parity * (tile * TOP_K)) * TILE_ROWS
    spos = jnp.zeros((tile, LANES), F32)
    for k in range(TOP_K):
        pk = jnp.sum(onehots[k] * base, axis=1, keepdims=True)
        spos = jnp.where(lane == k, pk, spos)
    spos_ref[...] = spos[:, :TOP_K].astype(jnp.int32)
    sub = lax.broadcasted_iota(jnp.int32, (SUBLANES, LANES), 0)
    runs = jnp.where(sub == 0, tile_cnt,
                     jnp.where(sub == 1, s_pstart[...] + s_run[...],
                               jnp.where(sub == 2, tile_start, 0.0)))
    runs_ref[0] = runs.astype(jnp.int32)
    s_run[...] += tile_cnt


def _plan(counts, idx_all, tile, n_blocks_pad):
    t_all = idx_all.shape[0]
    return pl.pallas_call(
        functools.partial(_plan_kernel, tile=tile, n_blocks_pad=n_blocks_pad),
        grid=(t_all // tile,),
        in_specs=[pl.BlockSpec((1, LANES), lambda i: (0, 0)),
                  pl.BlockSpec((tile, TOP_K), lambda i: (i, 0))],
        out_specs=(pl.BlockSpec((tile, TOP_K), lambda i: (i, 0)),
                   pl.BlockSpec((1, SUBLANES, LANES), lambda i: (i, 0, 0)),
                   pl.BlockSpec((n_blocks_pad, LANES), lambda i: (0, 0))),
        out_shape=(jax.ShapeDtypeStruct((t_all, TOP_K), jnp.int32),
                   jax.ShapeDtypeStruct((t_all // tile, SUBLANES, LANES), jnp.int32),
                   jax.ShapeDtypeStruct((n_blocks_pad, LANES), jnp.int32)),
        scratch_shapes=[pltpu.VMEM((1, LANES), F32), pltpu.VMEM((1, LANES), F32)],
        compiler_params=pltpu.CompilerParams(dimension_semantics=("arbitrary",)),
        name="moe_plan",
    )(counts, idx_all)


RUN_BITS = 10


def _run_copies(runs_ref, sorted_hbm, stage, sem, to_hbm):
    def expert(e, carry):
        n = runs_ref[e]
        first_sorted = runs_ref[LANES + e]
        first_staged = runs_ref[2 * LANES + e]
        for bit in reversed(range(RUN_BITS)):
            size = (1 << bit) * TILE_ROWS
            done = lax.shift_left(lax.shift_right_logical(n, bit + 1), bit + 1)

            @pl.when(lax.bitwise_and(lax.shift_right_logical(n, bit), 1) == 1)
            def _(size=size, done=done, bit=bit):
                hbm = sorted_hbm.at[pl.ds(pl.multiple_of((first_sorted + done) * TILE_ROWS, TILE_ROWS), size)]
                vmem = stage.at[pl.ds(pl.multiple_of((first_staged + done) * TILE_ROWS, TILE_ROWS), size)]
                src, dst = (vmem, hbm) if to_hbm else (hbm, vmem)
                pltpu.make_async_copy(src, dst, sem).start(priority=bit % 2)
        return carry

    lax.fori_loop(0, N_EXPERTS, expert, 0)


def _dispatch_kernel(valid_ref, spos_ref, runs_ref, xm_ref, buf_ref, zeros, stage, sem, zsem, *, tile,
                     n_blocks):
    i = pl.program_id(0)
    slot = i % 2
    blk_rows = MOE_BLOCK * TILE_ROWS
    n_rows = tile * TOP_K * TILE_ROWS

    def drain(s):
        pltpu.make_async_copy(buf_ref.at[pl.ds(0, n_rows)], buf_ref.at[pl.ds(0, n_rows)], sem.at[s]).wait()

    @pl.when(pl.program_id(0) == 0)
    def _():
        zeros[...] = jnp.zeros_like(zeros)

        def block_copy(i):
            return pltpu.make_async_copy(
                zeros, buf_ref.at[pl.ds(pl.multiple_of(i * blk_rows, blk_rows), blk_rows)], zsem)

        def fill(i, carry):
            @pl.when(valid_ref[i] < MOE_BLOCK)
            def _():
                block_copy(i).start()
            return carry

        def fill_wait(i, carry):
            @pl.when(valid_ref[i] < MOE_BLOCK)
            def _():
                block_copy(i).wait()
            return carry

        lax.fori_loop(0, n_blocks, fill, 0)
        lax.fori_loop(0, n_blocks, fill_wait, 0)

    @pl.when(i >= 2)
    def _():
        drain(slot)

    def place(j, carry):
        tok = xm_ref[pl.ds(pl.multiple_of(j * TILE_ROWS, TILE_ROWS), TILE_ROWS), :]
        for k in range(TOP_K):
            stage[pl.ds(pl.multiple_of(spos_ref[j * TOP_K + k], TILE_ROWS), TILE_ROWS), :] = tok
        return carry

    lax.fori_loop(0, tile, place, 0, unroll=8)
    slot_rows = stage.at[pl.ds(pl.multiple_of(slot * n_rows, n_rows), n_rows)]
    _run_copies(runs_ref, buf_ref, slot_rows, sem.at[slot], to_hbm=True)

    last = pl.num_programs(0) - 1

    @pl.when(i == last)
    def _():
        drain(slot)

    @pl.when(jnp.logical_and(i == last, i >= 1))
    def _():
        drain(1 - slot)


def _dispatch(block_valid, spos_flat, runs_flat, xm_tiles, tile, n_blocks):
    t_all = xm_tiles.shape[0] // TILE_ROWS
    grid_spec = pltpu.PrefetchScalarGridSpec(
        num_scalar_prefetch=1,
        grid=(t_all // tile,),
        in_specs=[pl.BlockSpec((tile * TOP_K,), lambda i, va: (i,), memory_space=pltpu.SMEM),
                  pl.BlockSpec((SUBLANES * LANES,), lambda i, va: (i,), memory_space=pltpu.SMEM),
                  pl.BlockSpec((tile * TILE_ROWS, LANES), lambda i, va: (i, 0))],
        out_specs=pl.BlockSpec(memory_space=pl.ANY),
        scratch_shapes=[pltpu.VMEM((MOE_BLOCK * TILE_ROWS, LANES), F32),
                        pltpu.VMEM((2 * tile * TOP_K * TILE_ROWS, LANES), F32),
                        pltpu.SemaphoreType.DMA((2,)), pltpu.SemaphoreType.DMA(())],
    )
    return pl.pallas_call(
        functools.partial(_dispatch_kernel, tile=tile, n_blocks=n_blocks),
        grid_spec=grid_spec,
        out_shape=jax.ShapeDtypeStruct((n_blocks * MOE_BLOCK * TILE_ROWS, LANES), F32),
        compiler_params=pltpu.CompilerParams(dimension_semantics=("arbitrary",),
                                             vmem_limit_bytes=VMEM_LIMIT),
        name="moe_dispatch",
    )(block_valid, spos_flat, runs_flat, xm_tiles)


def _expert_kernel(be_ref, used_ref, first_ref, next_ref, slot_ref, valid_ref, x_ref, wgu_ref, bgu_ref,
                   wd_ref, bd_ref, o_ref, wbuf_gu, wbuf_d, s_wgu, s_wd, sem):
    i = pl.program_id(0)

    @pl.when(i >= used_ref[0])
    def _():
        o_ref[...] = jnp.zeros_like(o_ref)

    def weight_copies(e, slot):
        return (pltpu.make_async_copy(wgu_ref.at[e], wbuf_gu.at[slot], sem.at[0, slot]),
                pltpu.make_async_copy(wd_ref.at[e], wbuf_d.at[slot], sem.at[1, slot]))

    @pl.when(i == 0)
    def _():
        for cp in weight_copies(be_ref[0], slot_ref[0]):
            cp.start()

    @pl.when(first_ref[i] == 1)
    def _():
        slot = slot_ref[i]
        for cp in weight_copies(be_ref[i], slot):
            cp.wait()

        @pl.when(next_ref[i] >= 0)
        def _():
            for cp in weight_copies(next_ref[i], 1 - slot):
                cp.start()

        s_wgu[...] = wbuf_gu[slot].astype(BF16)
        s_wd[...] = wbuf_d[slot].astype(BF16)

    def ffn(nrows):
        pieces = [x_ref[pl.ds(s, nrows, stride=TILE_ROWS), :] for s in range(TILE_ROWS)]
        x = jnp.concatenate(pieces, axis=1).astype(BF16)
        gu = jnp.dot(x, s_wgu[...], preferred_element_type=F32) + bgu_ref[0]
        g = jnp.minimum(gu[:, :D_FF], SWIGLU_LIMIT)
        up = jnp.clip(gu[:, D_FF:], -SWIGLU_LIMIT, SWIGLU_LIMIT)
        hdn = (up + 1.0) * (g * jax.nn.sigmoid(SWIGLU_ALPHA * g))
        out = jnp.dot(hdn.astype(BF16), s_wd[...], preferred_element_type=F32) + bd_ref[0]
        for s in range(TILE_ROWS):
            o_ref[pl.ds(s, nrows, stride=TILE_ROWS), :] = out[:, s * LANES:(s + 1) * LANES]
        if nrows < MOE_BLOCK:
            o_ref[nrows * TILE_ROWS:, :] = jnp.zeros(((MOE_BLOCK - nrows) * TILE_ROWS, LANES), F32)

    in_use = i < used_ref[0]
    half_empty = valid_ref[i] <= MOE_BLOCK // 2

    @pl.when(jnp.logical_and(in_use, jnp.logical_not(half_empty)))
    def _():
        ffn(MOE_BLOCK)

    @pl.when(jnp.logical_and(in_use, half_empty))
    def _():
        ffn(MOE_BLOCK // 2)


def _experts(meta, buf, wgu, bgu, wd, bd, n_blocks):
    blk_rows = MOE_BLOCK * TILE_ROWS
    block_e, n_used = meta[:n_blocks, 0], meta[:1, 2]
    first, nxt, slot = meta[:n_blocks, 3], meta[:n_blocks, 4], meta[:n_blocks, 5]
    valid = meta[:n_blocks, 1]

    def in_row_map(i, be, nu, *_):
        return (jnp.minimum(i, nu[0] - 1), 0)

    def b_map(i, be, *_):
        return (be[i], 0, 0)

    grid_spec = pltpu.PrefetchScalarGridSpec(
        num_scalar_prefetch=6,
        grid=(n_blocks,),
        in_specs=[pl.BlockSpec((blk_rows, LANES), in_row_map),
                  pl.BlockSpec(memory_space=pl.ANY),
                  pl.BlockSpec((1, 1, 2 * D_FF), b_map),
                  pl.BlockSpec(memory_space=pl.ANY),
                  pl.BlockSpec((1, 1, D_MODEL), b_map)],
        out_specs=pl.BlockSpec((blk_rows, LANES), lambda i, *_: (i, 0)),
        scratch_shapes=[pltpu.VMEM((2, D_MODEL, 2 * D_FF), F32), pltpu.VMEM((2, D_FF, D_MODEL), F32),
                        pltpu.VMEM((D_MODEL, 2 * D_FF), BF16), pltpu.VMEM((D_FF, D_MODEL), BF16),
                        pltpu.SemaphoreType.DMA((2, 2))],
    )
    return pl.pallas_call(
        _expert_kernel,
        grid_spec=grid_spec,
        out_shape=jax.ShapeDtypeStruct(buf.shape, F32),
        compiler_params=pltpu.CompilerParams(dimension_semantics=("arbitrary",),
                                             vmem_limit_bytes=VMEM_LIMIT),
        name="moe_experts",
    )(block_e, n_used, first, nxt, slot, valid, buf, wgu, bgu, wd, bd)


def _combine_kernel(spos_ref, runs_ref, next_runs_ref, gate_ref, x1_ref, rows_ref, nf_ref, yp_ref,
                    ys_ref, stage, gbuf, sem, *, batch, n_prompt):
    i = pl.program_id(0)
    tile = batch * CHUNK
    n_rows = tile * TOP_K * TILE_ROWS
    slot = i % 2

    def slot_rows(s):
        return stage.at[pl.ds(pl.multiple_of(s * n_rows, n_rows), n_rows)]

    @pl.when(i == 0)
    def _():
        _run_copies(runs_ref, rows_ref, slot_rows(0), sem.at[0], to_hbm=False)

    @pl.when(i + 1 < pl.num_programs(0))
    def _():
        _run_copies(next_runs_ref, rows_ref, slot_rows(1 - slot), sem.at[1 - slot], to_hbm=False)

    pltpu.make_async_copy(rows_ref.at[pl.ds(0, n_rows)], slot_rows(slot), sem.at[slot]).wait()

    def pick(j, carry):
        for k in range(TOP_K):
            gbuf[k, pl.ds(pl.multiple_of(j * TILE_ROWS, TILE_ROWS), TILE_ROWS), :] = (
                stage[pl.ds(pl.multiple_of(spos_ref[j * TOP_K + k], TILE_ROWS), TILE_ROWS), :])
        return carry

    lax.fori_loop(0, tile, pick, 0, unroll=8)

    gates = gate_ref[...]
    pieces = []
    for s in range(TILE_ROWS):
        piece = x1_ref[:, s * LANES:(s + 1) * LANES]
        for k in range(TOP_K):
            piece = piece + gates[:, k:k + 1] * gbuf[k, pl.ds(s, tile, stride=TILE_ROWS), :]
        pieces.append(piece)
    y = _rmsnorm(jnp.concatenate(pieces, axis=1), nf_ref[...]).reshape(batch, CHUNK, D_MODEL)

    @pl.when(i < n_prompt)
    def _():
        yp_ref[...] = y

    @pl.when(i >= n_prompt)
    def _():
        ys_ref[...] = y


def _combine(spos_flat, runs_flat, gates, x1, out_rows, norm_final, batch, lp, ls):
    tile = batch * CHUNK
    n_prompt = lp // CHUNK
    n_sample = ls // CHUNK
    last = n_prompt + n_sample - 1
    return pl.pallas_call(
        functools.partial(_combine_kernel, batch=batch, n_prompt=n_prompt),
        grid=(n_prompt + n_sample,),
        in_specs=[pl.BlockSpec((tile * TOP_K,), lambda i: (i,), memory_space=pltpu.SMEM),
                  pl.BlockSpec((SUBLANES * LANES,), lambda i: (i,), memory_space=pltpu.SMEM),
                  pl.BlockSpec((SUBLANES * LANES,), lambda i: (jnp.minimum(i + 1, last),),
                               memory_space=pltpu.SMEM),
                  pl.BlockSpec((tile, TOP_K), lambda i: (i, 0)),
                  pl.BlockSpec((tile, D_MODEL), lambda i: (i, 0)),
                  pl.BlockSpec(memory_space=pl.ANY),
                  pl.BlockSpec((1, D_MODEL), lambda i: (0, 0))],
        out_specs=(pl.BlockSpec((batch, CHUNK, D_MODEL), lambda i: (0, jnp.minimum(i, n_prompt - 1), 0)),
                   pl.BlockSpec((batch, CHUNK, D_MODEL), lambda i: (0, jnp.maximum(i - n_prompt, 0), 0))),
        out_shape=(jax.ShapeDtypeStruct((batch, lp, D_MODEL), F32),
                   jax.ShapeDtypeStruct((batch, ls, D_MODEL), F32)),
        scratch_shapes=[pltpu.VMEM((2 * tile * TOP_K * TILE_ROWS, LANES), F32),
                        pltpu.VMEM((TOP_K, tile * TILE_ROWS, LANES), F32),
                        pltpu.SemaphoreType.DMA((2,))],
        compiler_params=pltpu.CompilerParams(dimension_semantics=("arbitrary",),
                                             vmem_limit_bytes=VMEM_LIMIT),
        name="moe_combine",
    )(spos_flat, runs_flat, runs_flat, gates, x1, out_rows, norm_final)


def _prep_weights(norm_mix, w_in, b_ig, b_fg, s5_a_re, s5_a_im, s5_log_dt, s5_b_re, s5_b_im,
                  s5_c_re, s5_c_im, s5_d, w_s5_glu, ml_gn, w_ml_out, w_out, norm_moe, w_router,
                  b_router):
    w = w_in[0]
    o_ig = D_S5 + 4 * D_ML
    o_gs5 = o_ig + 2 * ML_HEADS
    w_a = w[:, :o_ig].astype(BF16)
    w_b = w[:, o_gs5:].astype(BF16)
    w_g = jnp.concatenate([w[:, o_ig:o_gs5], jnp.zeros((D_MODEL, LANES - 2 * ML_HEADS), F32)],
                          axis=1).astype(BF16)
    gate_bias = jnp.concatenate([b_ig[0], b_fg[0], jnp.zeros((LANES - 2 * ML_HEADS,), F32)])[None]

    ar, ai = s5_a_re[0], s5_a_im[0]
    dt = jnp.exp(s5_log_dt[0])[:, None]
    mag = jnp.exp(dt * ar)
    abar_re = mag * jnp.cos(dt * ai)
    abar_im = mag * jnp.sin(dt * ai)
    den = ar * ar + ai * ai
    fr = ((abar_re - 1.0) * ar + abar_im * ai) / den
    fi = (abar_im * ar - (abar_re - 1.0) * ai) / den
    br, bi = s5_b_re[0], s5_b_im[0]
    bbar_re = fr[..., None] * br - fi[..., None] * bi
    bbar_im = fr[..., None] * bi + fi[..., None] * br
    gh = S5_GROUPS // 2
    same_group = (jnp.arange(gh)[:, None, None, None] == jnp.arange(gh)[None, None, :, None])

    def blockdiag_in(bb):
        t = bb.reshape(2, gh, S5_STATE, S5_GROUP).transpose(0, 1, 3, 2)
        full = jnp.where(same_group[None], t[:, :, :, None, :], 0.0)
        return full.reshape(2, gh * S5_GROUP, gh * S5_STATE).astype(BF16)

    def blockdiag_out(cc):
        t = cc.reshape(2, gh, S5_GROUP, S5_STATE).transpose(0, 1, 3, 2)
        full = jnp.where(same_group[None], t[:, :, :, None, :], 0.0)
        return full.reshape(2, gh * S5_STATE, gh * S5_GROUP)

    cb = jnp.concatenate([blockdiag_out(s5_c_re[0]), -blockdiag_out(s5_c_im[0])], axis=1).astype(BF16)
    w_router_p = jnp.concatenate(
        [w_router[0], jnp.zeros((D_MODEL, LANES - N_EXPERTS), F32)], axis=1).astype(BF16)
    b_router_p = jnp.concatenate([b_router[0], jnp.zeros((LANES - N_EXPERTS,), F32)])[None]
    return dict(
        norm_mix=norm_mix[0][None], w_in_a=w_a, w_in_b=w_b, w_in_g=w_g, gate_bias=gate_bias,
        abar_re=abar_re.reshape(1, S5_CH), abar_im=abar_im.reshape(1, S5_CH),
        bd_re=blockdiag_in(bbar_re), bd_im=blockdiag_in(bbar_im), cb=cb,
        d_skip=s5_d[0][None], w_glu=w_s5_glu[0].astype(BF16), ml_gn=ml_gn[0][None],
        w_ml_out=w_ml_out[0].astype(BF16), w_out=w_out[0].astype(BF16),
        norm_moe=norm_moe[0][None], w_router=w_router_p, b_router=b_router_p)


def kernel(x_prompt, x_sample, state_s5_re, state_s5_im, state_ml_C, state_ml_n, state_ml_m, norm_mix, w_in, b_ig, b_fg, s5_a_re, s5_a_im, s5_log_dt, s5_b_re, s5_b_im, s5_c_re, s5_c_im, s5_d, w_s5_glu, ml_gn, w_ml_out, w_out, norm_moe, w_router, b_router, w_gate_up, b_gate_up, w_down, b_down, norm_final):
    w = _prep_weights(norm_mix, w_in, b_ig, b_fg, s5_a_re, s5_a_im, s5_log_dt, s5_b_re, s5_b_im,
                      s5_c_re, s5_c_im, s5_d, w_s5_glu, ml_gn, w_ml_out, w_out, norm_moe,
                      w_router, b_router)
    batch, lp, _ = x_prompt.shape
    ls = x_sample.shape[1]
    bh = batch * ML_HEADS
    cx0 = jnp.concatenate(
        [jnp.swapaxes(state_ml_C[0].reshape(bh, ML_HEAD_DIM, ML_HEAD_DIM), 1, 2),
         jnp.broadcast_to(state_ml_n[0].reshape(bh, ML_HEAD_DIM, 1), (bh, ML_HEAD_DIM, ML_HEAD_DIM))],
        axis=2)
    x1, xm, idx, gates, counts, hr, hi, cx, m = _mixer(
        x_prompt, x_sample,
        state_s5_re[0].reshape(batch, S5_CH), state_s5_im[0].reshape(batch, S5_CH), cx0,
        jnp.broadcast_to(state_ml_m[0].reshape(bh, 1), (bh, LANES)), w)
    c = jnp.swapaxes(cx[:, :, :, :ML_HEAD_DIM], 2, 3)
    n = cx[:, :, :, ML_HEAD_DIM]

    tile = batch * CHUNK
    t_all = idx.shape[0]
    n_blocks = -(-(t_all * TOP_K + N_EXPERTS * (MOE_BLOCK - 1)) // MOE_BLOCK)
    n_blocks_pad = -(-n_blocks // SUBLANES) * SUBLANES
    spos, runs, meta = _plan(counts, idx, tile, n_blocks_pad)
    spos_flat = spos.reshape(t_all * TOP_K)
    runs_flat = runs.reshape(-1)
    buf = _dispatch(meta[:n_blocks, 1], spos_flat, runs_flat, xm, tile, n_blocks)
    out_rows = _experts(meta, buf, w_gate_up[0], b_gate_up[0][:, None, :], w_down[0],
                        b_down[0][:, None, :], n_blocks)
    y_prompt, y_sample = _combine(spos_flat, runs_flat, gates, x1, out_rows, norm_final[None], batch,
                                  lp, ls)

    def states(p):
        return (hr[p].reshape(1, batch, S5_GROUPS, S5_STATE), hi[p].reshape(1, batch, S5_GROUPS, S5_STATE),
                c[p].reshape(1, batch, ML_HEADS, ML_HEAD_DIM, ML_HEAD_DIM),
                n[p].reshape(1, batch, ML_HEADS, ML_HEAD_DIM), m[p, :, 0].reshape(1, batch, ML_HEADS))

    return (y_prompt, y_sample) + states(0) + states(1)
```

```python
import functools

import jax
import jax.numpy as jnp
from jax import lax
from jax.experimental import pallas as pl
from jax.experimental.pallas import tpu as pltpu

F32 = jnp.float32
BF16 = jnp.bfloat16

D_MODEL = 1024
CHUNK = 64
EPS = 1e-6
D_S5 = 512
S5_GROUP = 16
S5_GROUPS = D_S5 // S5_GROUP
S5_STATE = 64
S5_CH = S5_GROUPS * S5_STATE
ML_HEADS = 4
ML_HEAD_DIM = 128
D_ML = ML_HEADS * ML_HEAD_DIM
N_EXPERTS = 32
TOP_K = 4
D_FF = 1024
SWIGLU_LIMIT = 7.0
SWIGLU_ALPHA = 1.702

LANES = 128
SUBLANES = 8
TILE_ROWS = D_MODEL // LANES

OFF_U = 0
OFF_Q = OFF_U + D_S5
OFF_K = OFF_Q + D_ML
OFF_V = OFF_K + D_ML
OFF_O = OFF_V + D_ML
OFF_GS5 = OFF_O + D_ML
OFF_GML = OFF_GS5 + D_MODEL
OFF_GATE = OFF_GML + D_MODEL

S5_HALF_IN = D_S5 // 2
S5_HALF_CH = S5_CH // 2
S5_HALF_TILES = S5_HALF_CH // LANES
S5_SCAN_TILES = 4
ML_GROUP = 2
MOE_BLOCK = 512
VMEM_LIMIT = 60 * 1024 * 1024


def _rmsnorm(x, w):
    return x * lax.rsqrt(jnp.mean(x * x, axis=-1, keepdims=True) + EPS) * w


def _const_spec(shape):
    nd = len(shape)
    return pl.BlockSpec(shape, lambda *_: (0,) * nd, pipeline_mode=pl.Buffered(1))


def _mixer_kernel(xp_ref, xs_ref, h0r_ref, h0i_ref, cx0_ref, m0_ref,
                  nmix_ref, wina_ref, winb_ref, wing_ref, gbias_ref, abr_ref, abi_ref, bdr_ref, bdi_ref, cb_ref,
                  dskip_ref, glu_ref, gn_ref, wml_ref, wout_ref, nmoe_ref, wr_ref, br_ref,
                  x1_ref, xm_ref, idx_ref, gate_ref, cnt_ref, hr_ref, hi_ref, cx_ref, m_ref,
                  s_bur, s_bui, s_tm, s_q, s_k, s_v, s_o, s_col, s_gated, s_cx, sem, *, batch, n_prompt):
    i = pl.program_id(0)
    rows = batch * CHUNK

    @pl.when(i == 0)
    def _():
        hr_ref[...] = jnp.zeros_like(hr_ref)
        hi_ref[...] = jnp.zeros_like(hi_ref)
        m_ref[...] = jnp.zeros_like(m_ref)
        s_cx[...] = jnp.zeros_like(s_cx)
        cnt_ref[...] = jnp.zeros_like(cnt_ref)

    @pl.when(i == n_prompt)
    def _():
        hr_ref[0] = h0r_ref[...]
        hi_ref[0] = h0i_ref[...]
        m_ref[0] = m0_ref[...]
        load = pltpu.make_async_copy(cx0_ref, s_cx, sem)
        load.start()
        load.wait()

    @pl.when(i < n_prompt)
    def _():
        x1_ref[...] = xp_ref[...].reshape(rows, D_MODEL)

    @pl.when(i >= n_prompt)
    def _():
        x1_ref[...] = xs_ref[...].reshape(rows, D_MODEL)

    xn = _rmsnorm(x1_ref[...], nmix_ref[...]).astype(BF16)

    def proj(off, width):
        if off >= OFF_GATE:
            w_cols = wing_ref[...]
        elif off >= OFF_GS5:
            w_cols = winb_ref[:, off - OFF_GS5:off - OFF_GS5 + width]
        else:
            w_cols = wina_ref[:, off:off + width]
        return jnp.dot(xn, w_cols, preferred_element_type=F32)

    gates = proj(OFF_GATE, LANES) + gbias_ref[...]
    lane_g = lax.broadcasted_iota(jnp.int32, (rows, LANES), 1)
    gg = jnp.where(lane_g < ML_HEADS, gates, jax.nn.log_sigmoid(gates))
    gt8 = gg.T[0:SUBLANES, :]
    pos = lax.broadcasted_iota(jnp.int32, (SUBLANES, rows), 1) % CHUNK
    cum = gt8
    shift = 1
    while shift < CHUNK:
        cum = cum + jnp.where(pos >= shift, pltpu.roll(cum, shift, axis=1), 0.0)
        shift *= 2
    g8 = gt8 - pltpu.roll(cum, ML_HEADS, axis=0)
    mx8 = g8
    shift = 1
    while shift < CHUNK:
        mx8 = jnp.maximum(mx8, jnp.where(pos >= shift, pltpu.roll(mx8, shift, axis=1), -jnp.inf))
        shift *= 2
    sub = lax.broadcasted_iota(jnp.int32, (SUBLANES, rows), 0)
    top8 = jnp.where(sub < ML_HEADS, g8, cum)
    s_col[...] = jnp.concatenate(
        [top8, mx8, jnp.zeros((LANES - 2 * SUBLANES, rows), F32)], axis=0).T
    s_q[...] = (proj(OFF_Q, D_ML) * (ML_HEAD_DIM ** -0.5)).astype(BF16)
    s_k[...] = proj(OFF_K, D_ML).astype(BF16)
    s_v[...] = proj(OFF_V, D_ML).astype(BF16)
    s_o[...] = proj(OFF_O, D_ML)

    tri = (lax.broadcasted_iota(jnp.int32, (CHUNK, CHUNK), 0)
           >= lax.broadcasted_iota(jnp.int32, (CHUNK, CHUNK), 1))[None]
    bdims = ((0,), (0,))

    def rowsl(b):
        return slice(b * CHUNK, (b + 1) * CHUNK)

    def headsl(h):
        return slice(h * ML_HEAD_DIM, (h + 1) * ML_HEAD_DIM)

    for g0 in range(0, batch, ML_GROUP):
        items = [(b, h) for b in range(g0, g0 + ML_GROUP) for h in range(ML_HEADS)]
        def col(j, items=items):
            return jnp.stack([jnp.broadcast_to(s_col[rowsl(b), j + h:j + h + 1], (CHUNK, LANES))
                              for b, h in items])

        g_c, b_c, mx_c = col(0), col(ML_HEADS), col(2 * ML_HEADS)
        g_r = jnp.stack([g8[h:h + 1, rowsl(b)] for b, h in items])
        m_prev = jnp.stack([m_ref[0, b * ML_HEADS + h:b * ML_HEADS + h + 1, :] for b, h in items])
        q3 = jnp.stack([s_q[rowsl(b), headsl(h)] for b, h in items])
        k3 = jnp.stack([s_k[rowsl(b), headsl(h)] for b, h in items])
        v3 = jnp.stack([s_v[rowsl(b), headsl(h)] for b, h in items])
        cx = jnp.stack([s_cx[b * ML_HEADS + h] for b, h in items])

        big_m = jnp.maximum(m_prev, mx_c)
        p = jnp.exp(jnp.where(tri, g_r - big_m[:, :, :CHUNK], -jnp.inf))
        w_inter = jnp.exp(m_prev - big_m)
        s = lax.dot_general(q3, k3, (((2,), (2,)), bdims), preferred_element_type=F32) * p
        cqx = lax.dot_general(q3, cx.astype(BF16), (((2,), (1,)), bdims), preferred_element_type=F32)
        num = (lax.dot_general(s.astype(BF16), v3, (((2,), (1,)), bdims), preferred_element_type=F32)
               + w_inter * cqx[:, :, :ML_HEAD_DIM])
        den_dot = jnp.sum(s, axis=2, keepdims=True) + w_inter * cqx[:, :, ML_HEAD_DIM:]
        m_t = b_c + big_m
        hout = num / jnp.maximum(jnp.abs(den_dot), jnp.exp(-m_t))

        m_last = big_m[:, CHUNK - 1:CHUNK, :]
        w_end = jnp.exp(g_c - m_last)
        decay = jnp.exp(m_prev - m_last)
        wvx = jnp.concatenate([w_end * v3.astype(F32), w_end], axis=2).astype(BF16)
        k_t = jnp.stack([s_k[rowsl(b), headsl(h)].astype(F32).T.astype(BF16) for b, h in items])
        cx_new = (jnp.concatenate([decay, decay], axis=2) * cx
                  + lax.dot_general(k_t, wvx, (((2,), (1,)), bdims), preferred_element_type=F32))
        m_new = m_t[:, CHUNK - 1:CHUNK, :]

        hc = hout - jnp.mean(hout, axis=2, keepdims=True)
        hn = hc * lax.rsqrt(jnp.mean(hc * hc, axis=2, keepdims=True) + EPS)
        for n, (b, h) in enumerate(items):
            bh = b * ML_HEADS + h
            s_cx[bh] = cx_new[n]
            m_ref[0, bh:bh + 1, :] = m_new[n]
            s_gated[rowsl(b), headsl(h)] = (jax.nn.sigmoid(s_o[rowsl(b), headsl(h)])
                                            * (hn[n] * gn_ref[:, headsl(h)])).astype(BF16)

    u = proj(OFF_U, D_S5)
    in_tiles = D_S5 // LANES
    for j in range(in_tiles):
        for b in range(batch):
            s_tm[j, pl.ds(b, CHUNK, stride=batch), :] = u[b * CHUNK:(b + 1) * CHUNK,
                                                          j * LANES:(j + 1) * LANES]
    ub = jnp.concatenate([s_tm[j] for j in range(in_tiles)], axis=1).astype(BF16)
    ys = []
    merge = []
    for k in range(2):
        uk = ub[:, k * S5_HALF_IN:(k + 1) * S5_HALF_IN]
        bur = jnp.dot(uk, bdr_ref[k], preferred_element_type=F32)
        bui = jnp.dot(uk, bdi_ref[k], preferred_element_type=F32)
        for j in range(S5_HALF_TILES):
            s_bur[j] = bur[:, j * LANES:(j + 1) * LANES]
            s_bui[j] = bui[:, j * LANES:(j + 1) * LANES]

        if k == 0:
            merge.append(jax.nn.sigmoid(proj(OFF_GML, D_MODEL)) * jnp.dot(
                s_gated[...], wml_ref[...], preferred_element_type=F32))
        else:
            merge.append(jax.nn.sigmoid(proj(OFF_GS5, D_MODEL)))

        for c in range(S5_HALF_TILES // S5_SCAN_TILES):
            tiles = range(c * S5_SCAN_TILES, (c + 1) * S5_SCAN_TILES)
            lanes = [slice(k * S5_HALF_CH + j * LANES, k * S5_HALF_CH + (j + 1) * LANES) for j in tiles]
            ar = [jnp.broadcast_to(abr_ref[:, ls], (batch, LANES)) for ls in lanes]
            ai = [jnp.broadcast_to(abi_ref[:, ls], (batch, LANES)) for ls in lanes]

            def step(t, carry, tiles=tiles, ar=ar, ai=ai):
                slab = pl.ds(pl.multiple_of(t * batch, batch), batch)
                out = []
                for n, j in enumerate(tiles):
                    hr, hi = carry[n]
                    nr = ar[n] * hr - ai[n] * hi + s_bur[j, slab, :]
                    ni = ar[n] * hi + ai[n] * hr + s_bui[j, slab, :]
                    s_bur[j, slab, :] = nr
                    s_bui[j, slab, :] = ni
                    out.append((nr, ni))
                return tuple(out)

            init = tuple((hr_ref[0, :, ls], hi_ref[0, :, ls]) for ls in lanes)
            fin = lax.fori_loop(0, CHUNK, step, init, unroll=True)
            for n, ls in enumerate(lanes):
                hr_ref[0, :, ls] = fin[n][0]
                hi_ref[0, :, ls] = fin[n][1]

        def states(ref):
            return jnp.concatenate([ref[j] for j in range(S5_HALF_TILES)], axis=1).astype(BF16)

        yk = jnp.dot(states(s_bur), cb_ref[k, :S5_HALF_CH, :], preferred_element_type=F32)
        yk = yk + jnp.dot(states(s_bui), cb_ref[k, S5_HALF_CH:, :], preferred_element_type=F32)
        ys.append(yk)
    y_tm = jnp.concatenate(ys, axis=1)
    for j in range(in_tiles):
        s_tm[j] = y_tm[:, j * LANES:(j + 1) * LANES]
    y = jnp.concatenate(
        [jnp.concatenate([s_tm[j, pl.ds(b, CHUNK, stride=batch), :] for j in range(in_tiles)], axis=1)
         for b in range(batch)], axis=0) + dskip_ref[...] * u
    glu = jnp.dot(jax.nn.gelu(y).astype(BF16), glu_ref[...], preferred_element_type=F32)
    y_s5 = glu[:, :D_MODEL] * jax.nn.sigmoid(glu[:, D_MODEL:])
    mix = merge[0] + merge[1] * y_s5

    x1 = x1_ref[...] + jnp.dot(mix.astype(BF16), wout_ref[...], preferred_element_type=F32)
    x1_ref[...] = x1

    xm = _rmsnorm(x1, nmoe_ref[...])
    logits = jnp.dot(xm.astype(BF16), wr_ref[...], preferred_element_type=F32) + br_ref[...]
    lane = lax.broadcasted_iota(jnp.int32, (rows, LANES), 1)
    lane_f = lane.astype(F32)
    logits = jnp.where(lane < N_EXPERTS, logits, -jnp.inf)
    vals, idxs = [], []
    for _ in range(TOP_K):
        mx = jnp.max(logits, axis=1, keepdims=True)
        am = jnp.min(jnp.where(logits == mx, lane_f, float(LANES)), axis=1, keepdims=True)
        vals.append(mx)
        idxs.append(am)
        logits = jnp.where(lane_f == am, -jnp.inf, logits)
    exps = [jnp.exp(v - vals[0]) for v in vals]
    esum = exps[0] + exps[1] + exps[2] + exps[3]
    idx_w = jnp.zeros((rows, LANES), F32)
    gate_w = jnp.zeros((rows, LANES), F32)
    for k in range(TOP_K):
        idx_w = jnp.where(lane == k, idxs[k], idx_w)
        gate_w = jnp.where(lane == k, exps[k] / esum, gate_w)
    idx_ref[...] = idx_w[:, :TOP_K].astype(jnp.int32)
    gate_ref[...] = gate_w[:, :TOP_K]
    chosen = jnp.zeros((rows, LANES), F32)
    for k in range(TOP_K):
        chosen = chosen + (lane_f == idxs[k]).astype(F32)
    cnt_ref[...] += jnp.sum(chosen, axis=0, keepdims=True)

    for s in range(TILE_ROWS):
        xm_ref[pl.ds(s, rows, stride=TILE_ROWS), :] = xm[:, s * LANES:(s + 1) * LANES]

    for phase_id, final_step in ((0, n_prompt - 1), (1, pl.num_programs(0) - 1)):
        @pl.when(i == final_step)
        def _(phase_id=phase_id):
            store = pltpu.make_async_copy(s_cx, cx_ref.at[phase_id], sem)
            store.start()
            store.wait()


def _mixer(x_prompt, x_sample, h0r, h0i, cx0, m0, w):
    batch, lp, _ = x_prompt.shape
    assert x_sample.shape[0] == batch and lp % CHUNK == 0 and x_sample.shape[1] % CHUNK == 0
    assert batch % ML_GROUP == 0
    n_prompt = lp // CHUNK
    n_sample = x_sample.shape[1] // CHUNK
    nblk = n_prompt + n_sample
    rows = batch * CHUNK
    t_all = nblk * rows
    bh = batch * ML_HEADS

    def phase(i):
        return jnp.minimum(i // n_prompt, 1)

    state_specs = [
        _const_spec((batch, S5_CH)), _const_spec((batch, S5_CH)),
        pl.BlockSpec(memory_space=pl.ANY),
        _const_spec((bh, LANES)),
    ]
    weights = (w['norm_mix'], w['w_in_a'], w['w_in_b'], w['w_in_g'], w['gate_bias'], w['abar_re'], w['abar_im'], w['bd_re'],
               w['bd_im'], w['cb'], w['d_skip'], w['w_glu'], w['ml_gn'], w['w_ml_out'], w['w_out'],
               w['norm_moe'], w['w_router'], w['b_router'])
    in_specs = ([pl.BlockSpec((batch, CHUNK, D_MODEL), lambda i: (0, jnp.minimum(i, n_prompt - 1), 0)),
                 pl.BlockSpec((batch, CHUNK, D_MODEL), lambda i: (0, jnp.maximum(i - n_prompt, 0), 0))]
                + state_specs + [_const_spec(a.shape) for a in weights])
    out_shape = (
        jax.ShapeDtypeStruct((t_all, D_MODEL), F32),
        jax.ShapeDtypeStruct((t_all * TILE_ROWS, LANES), F32),
        jax.ShapeDtypeStruct((t_all, TOP_K), jnp.int32),
        jax.ShapeDtypeStruct((t_all, TOP_K), F32),
        jax.ShapeDtypeStruct((1, LANES), F32),
        jax.ShapeDtypeStruct((2, batch, S5_CH), F32),
        jax.ShapeDtypeStruct((2, batch, S5_CH), F32),
        jax.ShapeDtypeStruct((2, bh, ML_HEAD_DIM, 2 * ML_HEAD_DIM), F32),
        jax.ShapeDtypeStruct((2, bh, LANES), F32),
    )
    out_specs = (
        pl.BlockSpec((rows, D_MODEL), lambda i: (i, 0)),
        pl.BlockSpec((rows * TILE_ROWS, LANES), lambda i: (i, 0)),
        pl.BlockSpec((rows, TOP_K), lambda i: (i, 0)),
        pl.BlockSpec((rows, TOP_K), lambda i: (i, 0)),
        pl.BlockSpec((1, LANES), lambda i: (0, 0)),
        pl.BlockSpec((1, batch, S5_CH), lambda i: (phase(i), 0, 0)),
        pl.BlockSpec((1, batch, S5_CH), lambda i: (phase(i), 0, 0)),
        pl.BlockSpec(memory_space=pl.ANY),
        pl.BlockSpec((1, bh, LANES), lambda i: (phase(i), 0, 0)),
    )
    scratch = [
        pltpu.VMEM((S5_HALF_TILES, rows, LANES), F32), pltpu.VMEM((S5_HALF_TILES, rows, LANES), F32),
        pltpu.VMEM((D_S5 // LANES, rows, LANES), F32),
        pltpu.VMEM((rows, D_ML), BF16), pltpu.VMEM((rows, D_ML), BF16), pltpu.VMEM((rows, D_ML), BF16),
        pltpu.VMEM((rows, D_ML), F32),
        pltpu.VMEM((rows, LANES), F32),
        pltpu.VMEM((rows, D_ML), BF16),
        pltpu.VMEM((bh, ML_HEAD_DIM, 2 * ML_HEAD_DIM), F32),
        pltpu.SemaphoreType.DMA(()),
    ]
    return pl.pallas_call(
        functools.partial(_mixer_kernel, batch=batch, n_prompt=n_prompt),
        grid=(nblk,),
        in_specs=in_specs,
        out_specs=out_specs,
        out_shape=out_shape,
        scratch_shapes=scratch,
        compiler_params=pltpu.CompilerParams(dimension_semantics=("arbitrary",),
                                             vmem_limit_bytes=VMEM_LIMIT),
        name="mixer",
    )(x_prompt, x_sample, h0r, h0i, cx0, m0, *weights)


def _plan_kernel(cnt_ref, idx_ref, spos_ref, runs_ref, meta_ref, s_run, s_pstart, s_lower, *, tile,
                 n_blocks_pad):
    i = pl.program_id(0)
    lane = lax.broadcasted_iota(jnp.int32, (tile, LANES), 1)
    idx = idx_ref[...]
    onehots = [(lane == idx[:, k:k + 1]).astype(F32) for k in range(TOP_K)]
    mask = onehots[0] + onehots[1] + onehots[2] + onehots[3]

    @pl.when(i == 0)
    def _():
        cnt = cnt_ref[...]
        nblk = jnp.floor((cnt + (MOE_BLOCK - 1)) * (1.0 / MOE_BLOCK))
        lane1 = lax.broadcasted_iota(jnp.int32, (1, LANES), 1)
        pend = nblk
        shift = 1
        while shift < LANES:
            pend = pend + jnp.where(lane1 >= shift, pltpu.roll(pend, shift, axis=1), 0.0)
            shift *= 2
        s_pstart[...] = (pend - nblk) * MOE_BLOCK
        s_run[...] = jnp.zeros_like(s_run)
        r = lax.broadcasted_iota(jnp.int32, (tile, tile), 0)
        c = lax.broadcasted_iota(jnp.int32, (tile, tile), 1)
        s_lower[...] = (c < r).astype(BF16)
        blk = lax.broadcasted_iota(jnp.int32, (n_blocks_pad, LANES), 0).astype(F32)
        lane_b = lax.broadcasted_iota(jnp.int32, (n_blocks_pad, LANES), 1)
        lane_f = lane_b.astype(F32)
        is_e = lane_b < N_EXPERTS
        done = jnp.logical_and(is_e, pend <= blk)
        e_of = jnp.minimum(jnp.sum(done.astype(F32), axis=1, keepdims=True), N_EXPERTS - 1.0)
        mine = lane_f == e_of
        blk_in_e = jnp.sum(jnp.where(mine, blk - (pend - nblk), 0.0), axis=1, keepdims=True)
        cnt_e = jnp.sum(jnp.where(mine, cnt, 0.0), axis=1, keepdims=True)
        valid = jnp.clip(cnt_e - blk_in_e * MOE_BLOCK, 0.0, float(MOE_BLOCK))
        used = jnp.sum(jnp.where(lane1 == N_EXPERTS - 1, pend, 0.0), axis=1, keepdims=True)
        owns = jnp.logical_and(is_e, nblk > 0.0)
        first = jnp.logical_and(blk_in_e == 0.0, blk[:, :1] < used).astype(F32)
        later = jnp.logical_and(owns, lane_f > e_of)
        nxt = jnp.min(jnp.where(later, lane_f, float(LANES)), axis=1, keepdims=True)
        nxt = jnp.where(nxt < float(LANES), nxt, -1.0)
        run = jnp.sum(jnp.logical_and(owns, lane_f < e_of).astype(F32), axis=1, keepdims=True)
        parity = run - 2.0 * jnp.floor(run * 0.5)
        meta = jnp.zeros((n_blocks_pad, LANES), F32)
        for col, val in enumerate((e_of, valid, used, first, nxt, parity)):
            meta = jnp.where(lane_b == col, val, meta)
        meta_ref[...] = meta.astype(jnp.int32)

    before = jnp.dot(s_lower[...], mask.astype(BF16), preferred_element_type=F32)
    tile_cnt = jnp.sum(mask, axis=0, keepdims=True)
    lane1 = lax.broadcasted_iota(jnp.int32, (1, LANES), 1)
    incl = tile_cnt
    shift = 1
    while shift < LANES:
        incl = incl + jnp.where(lane1 >= shift, pltpu.roll(incl, shift, axis=1), 0.0)
        shift *= 2
    tile_start = incl - tile_cnt
    parity = (i % 2).astype(F32)
    base = (before + tile_start + parity * (tile * TOP_K)) * TILE_ROWS
    spos = jnp.zeros((tile, LANES), F32)
    for k in range(TOP_K):
        pk = jnp.sum(onehots[k] * base, axis=1, keepdims=True)
        spos = jnp.where(lane == k, pk, spos)
    spos_ref[...] = spos[:, :TOP_K].astype(jnp.int32)
    sub = lax.broadcasted_iota(jnp.int32, (SUBLANES, LANES), 0)
    runs = jnp.where(sub == 0, tile_cnt,
                     jnp.where(sub == 1, s_pstart[...] + s_run[...],
                               jnp.where(sub == 2, tile_start, 0.0)))
    runs_ref[0] = runs.astype(jnp.int32)
    s_run[...] += tile_cnt


def _plan(counts, idx_all, tile, n_blocks_pad):
    t_all = idx_all.shape[0]
    return pl.pallas_call(
        functools.partial(_plan_kernel, tile=tile, n_blocks_pad=n_blocks_pad),
        grid=(t_all // tile,),
        in_specs=[pl.BlockSpec((1, LANES), lambda i: (0, 0)),
                  pl.BlockSpec((tile, TOP_K), lambda i: (i, 0))],
        out_specs=(pl.BlockSpec((tile, TOP_K), lambda i: (i, 0)),
                   pl.BlockSpec((1, SUBLANES, LANES), lambda i: (i, 0, 0)),
                   pl.BlockSpec((n_blocks_pad, LANES), lambda i: (0, 0))),
        out_shape=(jax.ShapeDtypeStruct((t_all, TOP_K), jnp.int32),
                   jax.ShapeDtypeStruct((t_all // tile, SUBLANES, LANES), jnp.int32),
                   jax.ShapeDtypeStruct((n_blocks_pad, LANES), jnp.int32)),
        scratch_shapes=[pltpu.VMEM((1, LANES), F32), pltpu.VMEM((1, LANES), F32),
                        pltpu.VMEM((tile, tile), BF16)],
        compiler_params=pltpu.CompilerParams(dimension_semantics=("arbitrary",)),
        name="moe_plan",
    )(counts, idx_all)


RUN_BITS = 10
RUN_SPLIT = 7


def _run_copies(runs_ref, sorted_hbm, stage, sem, to_hbm):
    def expert(e, carry):
        n = runs_ref[e]
        first_sorted = runs_ref[LANES + e]
        first_staged = runs_ref[2 * LANES + e]
        def piece(bit):
            size = (1 << bit) * TILE_ROWS
            done = lax.shift_left(lax.shift_right_logical(n, bit + 1), bit + 1)

            @pl.when(lax.bitwise_and(lax.shift_right_logical(n, bit), 1) == 1)
            def _():
                hbm = sorted_hbm.at[pl.ds(pl.multiple_of((first_sorted + done) * TILE_ROWS, TILE_ROWS), size)]
                vmem = stage.at[pl.ds(pl.multiple_of((first_staged + done) * TILE_ROWS, TILE_ROWS), size)]
                src, dst = (vmem, hbm) if to_hbm else (hbm, vmem)
                pltpu.make_async_copy(src, dst, sem).start(priority=bit % 2)

        @pl.when(n >= (1 << RUN_SPLIT))
        def _():
            for bit in reversed(range(RUN_SPLIT, RUN_BITS)):
                piece(bit)

        for bit in reversed(range(RUN_SPLIT)):
            piece(bit)
        return carry

    lax.fori_loop(0, N_EXPERTS, expert, 0)


def _dispatch_kernel(valid_ref, spos_ref, runs_ref, xm_ref, buf_ref, zeros, stage, sem, zsem, *, tile,
                     n_blocks):
    i = pl.program_id(0)
    slot = i % 2
    blk_rows = MOE_BLOCK * TILE_ROWS
    n_rows = tile * TOP_K * TILE_ROWS

    def drain(s):
        pltpu.make_async_copy(buf_ref.at[pl.ds(0, n_rows)], buf_ref.at[pl.ds(0, n_rows)], sem.at[s]).wait()

    @pl.when(pl.program_id(0) == 0)
    def _():
        zeros[...] = jnp.zeros_like(zeros)

        def block_copy(i):
            return pltpu.make_async_copy(
                zeros, buf_ref.at[pl.ds(pl.multiple_of(i * blk_rows, blk_rows), blk_rows)], zsem)

        def fill(i, carry):
            @pl.when(valid_ref[i] < MOE_BLOCK)
            def _():
                block_copy(i).start()
            return carry

        def fill_wait(i, carry):
            @pl.when(valid_ref[i] < MOE_BLOCK)
            def _():
                block_copy(i).wait()
            return carry

        lax.fori_loop(0, n_blocks, fill, 0)
        lax.fori_loop(0, n_blocks, fill_wait, 0)

    @pl.when(i >= 2)
    def _():
        drain(slot)

    def place(j, carry):
        tok = xm_ref[pl.ds(pl.multiple_of(j * TILE_ROWS, TILE_ROWS), TILE_ROWS), :]
        for k in range(TOP_K):
            stage[pl.ds(pl.multiple_of(spos_ref[j * TOP_K + k], TILE_ROWS), TILE_ROWS), :] = tok
        return carry

    lax.fori_loop(0, tile, place, 0, unroll=16)
    slot_rows = stage.at[pl.ds(pl.multiple_of(slot * n_rows, n_rows), n_rows)]
    _run_copies(runs_ref, buf_ref, slot_rows, sem.at[slot], to_hbm=True)

    last = pl.num_programs(0) - 1

    @pl.when(i == last)
    def _():
        drain(slot)

    @pl.when(jnp.logical_and(i == last, i >= 1))
    def _():
        drain(1 - slot)


def _dispatch(block_valid, spos_flat, runs_flat, xm_tiles, tile, n_blocks):
    t_all = xm_tiles.shape[0] // TILE_ROWS
    grid_spec = pltpu.PrefetchScalarGridSpec(
        num_scalar_prefetch=1,
        grid=(t_all // tile,),
        in_specs=[pl.BlockSpec((tile * TOP_K,), lambda i, va: (i,), memory_space=pltpu.SMEM),
                  pl.BlockSpec((SUBLANES * LANES,), lambda i, va: (i,), memory_space=pltpu.SMEM),
                  pl.BlockSpec((tile * TILE_ROWS, LANES), lambda i, va: (i, 0))],
        out_specs=pl.BlockSpec(memory_space=pl.ANY),
        scratch_shapes=[pltpu.VMEM((MOE_BLOCK * TILE_ROWS, LANES), F32),
                        pltpu.VMEM((2 * tile * TOP_K * TILE_ROWS, LANES), F32),
                        pltpu.SemaphoreType.DMA((2,)), pltpu.SemaphoreType.DMA(())],
    )
    return pl.pallas_call(
        functools.partial(_dispatch_kernel, tile=tile, n_blocks=n_blocks),
        grid_spec=grid_spec,
        out_shape=jax.ShapeDtypeStruct((n_blocks * MOE_BLOCK * TILE_ROWS, LANES), F32),
        compiler_params=pltpu.CompilerParams(dimension_semantics=("arbitrary",),
                                             vmem_limit_bytes=VMEM_LIMIT),
        name="moe_dispatch",
    )(block_valid, spos_flat, runs_flat, xm_tiles)


def _expert_kernel(be_ref, used_ref, first_ref, next_ref, slot_ref, valid_ref, x_ref, wgu_ref, bgu_ref,
                   wd_ref, bd_ref, o_ref, wbuf_gu, wbuf_d, s_wgu, s_wd, sem):
    i = pl.program_id(0)

    @pl.when(i >= used_ref[0])
    def _():
        o_ref[...] = jnp.zeros_like(o_ref)

    def weight_copies(e, slot):
        return (pltpu.make_async_copy(wgu_ref.at[e], wbuf_gu.at[slot], sem.at[0, slot]),
                pltpu.make_async_copy(wd_ref.at[e], wbuf_d.at[slot], sem.at[1, slot]))

    @pl.when(i == 0)
    def _():
        for cp in weight_copies(be_ref[0], slot_ref[0]):
            cp.start()

    @pl.when(first_ref[i] == 1)
    def _():
        slot = slot_ref[i]
        for cp in weight_copies(be_ref[i], slot):
            cp.wait()

        @pl.when(next_ref[i] >= 0)
        def _():
            for cp in weight_copies(next_ref[i], 1 - slot):
                cp.start()

        s_wgu[...] = wbuf_gu[slot].astype(BF16)
        s_wd[...] = wbuf_d[slot].astype(BF16)

    def ffn(nrows):
        pieces = [x_ref[pl.ds(s, nrows, stride=TILE_ROWS), :] for s in range(TILE_ROWS)]
        x = jnp.concatenate(pieces, axis=1).astype(BF16)
        gu = jnp.dot(x, s_wgu[...], preferred_element_type=F32) + bgu_ref[0]
        g = jnp.minimum(gu[:, :D_FF], SWIGLU_LIMIT)
        up = jnp.clip(gu[:, D_FF:], -SWIGLU_LIMIT, SWIGLU_LIMIT)
        hdn = (up + 1.0) * (g * jax.nn.sigmoid(SWIGLU_ALPHA * g))
        out = jnp.dot(hdn.astype(BF16), s_wd[...], preferred_element_type=F32) + bd_ref[0]
        for s in range(TILE_ROWS):
            o_ref[pl.ds(s, nrows, stride=TILE_ROWS), :] = out[:, s * LANES:(s + 1) * LANES]
        if nrows < MOE_BLOCK:
            o_ref[nrows * TILE_ROWS:, :] = jnp.zeros(((MOE_BLOCK - nrows) * TILE_ROWS, LANES), F32)

    in_use = i < used_ref[0]
    half_empty = valid_ref[i] <= MOE_BLOCK // 2

    @pl.when(jnp.logical_and(in_use, jnp.logical_not(half_empty)))
    def _():
        ffn(MOE_BLOCK)

    @pl.when(jnp.logical_and(in_use, half_empty))
    def _():
        ffn(MOE_BLOCK // 2)


def _experts(meta, buf, wgu, bgu, wd, bd, n_blocks):
    blk_rows = MOE_BLOCK * TILE_ROWS
    block_e, n_used = meta[:n_blocks, 0], meta[:1, 2]
    first, nxt, slot = meta[:n_blocks, 3], meta[:n_blocks, 4], meta[:n_blocks, 5]
    valid = meta[:n_blocks, 1]

    def in_row_map(i, be, nu, *_):
        return (jnp.minimum(i, nu[0] - 1), 0)

    def b_map(i, be, *_):
        return (be[i], 0, 0)

    grid_spec = pltpu.PrefetchScalarGridSpec(
        num_scalar_prefetch=6,
        grid=(n_blocks,),
        in_specs=[pl.BlockSpec((blk_rows, LANES), in_row_map),
                  pl.BlockSpec(memory_space=pl.ANY),
                  pl.BlockSpec((1, 1, 2 * D_FF), b_map),
                  pl.BlockSpec(memory_space=pl.ANY),
                  pl.BlockSpec((1, 1, D_MODEL), b_map)],
        out_specs=pl.BlockSpec((blk_rows, LANES), lambda i, *_: (i, 0)),
        scratch_shapes=[pltpu.VMEM((2, D_MODEL, 2 * D_FF), F32), pltpu.VMEM((2, D_FF, D_MODEL), F32),
                        pltpu.VMEM((D_MODEL, 2 * D_FF), BF16), pltpu.VMEM((D_FF, D_MODEL), BF16),
                        pltpu.SemaphoreType.DMA((2, 2))],
    )
    return pl.pallas_call(
        _expert_kernel,
        grid_spec=grid_spec,
        out_shape=jax.ShapeDtypeStruct(buf.shape, F32),
        compiler_params=pltpu.CompilerParams(dimension_semantics=("arbitrary",),
                                             vmem_limit_bytes=VMEM_LIMIT),
        name="moe_experts",
    )(block_e, n_used, first, nxt, slot, valid, buf, wgu, bgu, wd, bd)


def _combine_kernel(spos_ref, runs_ref, next_runs_ref, gate_ref, x1_ref, rows_ref, nf_ref, yp_ref,
                    ys_ref, stage, gbuf, sem, *, batch, n_prompt):
    i = pl.program_id(0)
    tile = batch * CHUNK
    n_rows = tile * TOP_K * TILE_ROWS
    slot = i % 2

    def slot_rows(s):
        return stage.at[pl.ds(pl.multiple_of(s * n_rows, n_rows), n_rows)]

    @pl.when(i == 0)
    def _():
        _run_copies(runs_ref, rows_ref, slot_rows(0), sem.at[0], to_hbm=False)

    @pl.when(i + 1 < pl.num_programs(0))
    def _():
        _run_copies(next_runs_ref, rows_ref, slot_rows(1 - slot), sem.at[1 - slot], to_hbm=False)

    pltpu.make_async_copy(rows_ref.at[pl.ds(0, n_rows)], slot_rows(slot), sem.at[slot]).wait()

    def pick(j, carry):
        for k in range(TOP_K):
            gbuf[k, pl.ds(pl.multiple_of(j * TILE_ROWS, TILE_ROWS), TILE_ROWS), :] = (
                stage[pl.ds(pl.multiple_of(spos_ref[j * TOP_K + k], TILE_ROWS), TILE_ROWS), :])
        return carry

    lax.fori_loop(0, tile, pick, 0, unroll=16)

    gates = gate_ref[...]
    pieces = []
    for s in range(TILE_ROWS):
        piece = x1_ref[:, s * LANES:(s + 1) * LANES]
        for k in range(TOP_K):
            piece = piece + gates[:, k:k + 1] * gbuf[k, pl.ds(s, tile, stride=TILE_ROWS), :]
        pieces.append(piece)
    y = _rmsnorm(jnp.concatenate(pieces, axis=1), nf_ref[...]).reshape(batch, CHUNK, D_MODEL)

    @pl.when(i < n_prompt)
    def _():
        yp_ref[...] = y

    @pl.when(i >= n_prompt)
    def _():
        ys_ref[...] = y


def _combine(spos_flat, runs_flat, gates, x1, out_rows, norm_final, batch, lp, ls):
    tile = batch * CHUNK
    n_prompt = lp // CHUNK
    n_sample = ls // CHUNK
    last = n_prompt + n_sample - 1
    return pl.pallas_call(
        functools.partial(_combine_kernel, batch=batch, n_prompt=n_prompt),
        grid=(n_prompt + n_sample,),
        in_specs=[pl.BlockSpec((tile * TOP_K,), lambda i: (i,), memory_space=pltpu.SMEM),
                  pl.BlockSpec((SUBLANES * LANES,), lambda i: (i,), memory_space=pltpu.SMEM),
                  pl.BlockSpec((SUBLANES * LANES,), lambda i: (jnp.minimum(i + 1, last),),
                               memory_space=pltpu.SMEM),
                  pl.BlockSpec((tile, TOP_K), lambda i: (i, 0)),
                  pl.BlockSpec((tile, D_MODEL), lambda i: (i, 0)),
                  pl.BlockSpec(memory_space=pl.ANY),
                  pl.BlockSpec((1, D_MODEL), lambda i: (0, 0))],
        out_specs=(pl.BlockSpec((batch, CHUNK, D_MODEL), lambda i: (0, jnp.minimum(i, n_prompt - 1), 0)),
                   pl.BlockSpec((batch, CHUNK, D_MODEL), lambda i: (0, jnp.maximum(i - n_prompt, 0), 0))),
        out_shape=(jax.ShapeDtypeStruct((batch, lp, D_MODEL), F32),
                   jax.ShapeDtypeStruct((batch, ls, D_MODEL), F32)),
        scratch_shapes=[pltpu.VMEM((2 * tile * TOP_K * TILE_ROWS, LANES), F32),
                        pltpu.VMEM((TOP_K, tile * TILE_ROWS, LANES), F32),
                        pltpu.SemaphoreType.DMA((2,))],
        compiler_params=pltpu.CompilerParams(dimension_semantics=("arbitrary",),
                                             vmem_limit_bytes=VMEM_LIMIT),
        name="moe_combine",
    )(spos_flat, runs_flat, runs_flat, gates, x1, out_rows, norm_final)


def _prep_weights(norm_mix, w_in, b_ig, b_fg, s5_a_re, s5_a_im, s5_log_dt, s5_b_re, s5_b_im,
                  s5_c_re, s5_c_im, s5_d, w_s5_glu, ml_gn, w_ml_out, w_out, norm_moe, w_router,
                  b_router):
    w = w_in[0]
    o_ig = D_S5 + 4 * D_ML
    o_gs5 = o_ig + 2 * ML_HEADS
    w_a = w[:, :o_ig].astype(BF16)
    w_b = w[:, o_gs5:].astype(BF16)
    w_g = jnp.concatenate([w[:, o_ig:o_gs5], jnp.zeros((D_MODEL, LANES - 2 * ML_HEADS), F32)],
                          axis=1).astype(BF16)
    gate_bias = jnp.concatenate([b_ig[0], b_fg[0], jnp.zeros((LANES - 2 * ML_HEADS,), F32)])[None]

    ar, ai = s5_a_re[0], s5_a_im[0]
    dt = jnp.exp(s5_log_dt[0])[:, None]
    mag = jnp.exp(dt * ar)
    abar_re = mag * jnp.cos(dt * ai)
    abar_im = mag * jnp.sin(dt * ai)
    den = ar * ar + ai * ai
    fr = ((abar_re - 1.0) * ar + abar_im * ai) / den
    fi = (abar_im * ar - (abar_re - 1.0) * ai) / den
    br, bi = s5_b_re[0], s5_b_im[0]
    bbar_re = fr[..., None] * br - fi[..., None] * bi
    bbar_im = fr[..., None] * bi + fi[..., None] * br
    gh = S5_GROUPS // 2
    same_group = (jnp.arange(gh)[:, None, None, None] == jnp.arange(gh)[None, None, :, None])

    def blockdiag_in(bb):
        t = bb.reshape(2, gh, S5_STATE, S5_GROUP).transpose(0, 1, 3, 2)
        full = jnp.where(same_group[None], t[:, :, :, None, :], 0.0)
        return full.reshape(2, gh * S5_GROUP, gh * S5_STATE).astype(BF16)

    def blockdiag_out(cc):
        t = cc.reshape(2, gh, S5_GROUP, S5_STATE).transpose(0, 1, 3, 2)
        full = jnp.where(same_group[None], t[:, :, :, None, :], 0.0)
        return full.reshape(2, gh * S5_STATE, gh * S5_GROUP)

    cb = jnp.concatenate([blockdiag_out(s5_c_re[0]), -blockdiag_out(s5_c_im[0])], axis=1).astype(BF16)
    w_router_p = jnp.concatenate(
        [w_router[0], jnp.zeros((D_MODEL, LANES - N_EXPERTS), F32)], axis=1).astype(BF16)
    b_router_p = jnp.concatenate([b_router[0], jnp.zeros((LANES - N_EXPERTS,), F32)])[None]
    return dict(
        norm_mix=norm_mix[0][None], w_in_a=w_a, w_in_b=w_b, w_in_g=w_g, gate_bias=gate_bias,
        abar_re=abar_re.reshape(1, S5_CH), abar_im=abar_im.reshape(1, S5_CH),
        bd_re=blockdiag_in(bbar_re), bd_im=blockdiag_in(bbar_im), cb=cb,
        d_skip=s5_d[0][None], w_glu=w_s5_glu[0].astype(BF16), ml_gn=ml_gn[0][None],
        w_ml_out=w_ml_out[0].astype(BF16), w_out=w_out[0].astype(BF16),
        norm_moe=norm_moe[0][None], w_router=w_router_p, b_router=b_router_p)


def kernel(x_prompt, x_sample, state_s5_re, state_s5_im, state_ml_C, state_ml_n, state_ml_m, norm_mix, w_in, b_ig, b_fg, s5_a_re, s5_a_im, s5_log_dt, s5_b_re, s5_b_im, s5_c_re, s5_c_im, s5_d, w_s5_glu, ml_gn, w_ml_out, w_out, norm_moe, w_router, b_router, w_gate_up, b_gate_up, w_down, b_down, norm_final):
    w = _prep_weights(norm_mix, w_in, b_ig, b_fg, s5_a_re, s5_a_im, s5_log_dt, s5_b_re, s5_b_im,
                      s5_c_re, s5_c_im, s5_d, w_s5_glu, ml_gn, w_ml_out, w_out, norm_moe,
                      w_router, b_router)
    batch, lp, _ = x_prompt.shape
    ls = x_sample.shape[1]
    bh = batch * ML_HEADS
    cx0 = jnp.concatenate(
        [jnp.swapaxes(state_ml_C[0].reshape(bh, ML_HEAD_DIM, ML_HEAD_DIM), 1, 2),
         jnp.broadcast_to(state_ml_n[0].reshape(bh, ML_HEAD_DIM, 1), (bh, ML_HEAD_DIM, ML_HEAD_DIM))],
        axis=2)
    x1, xm, idx, gates, counts, hr, hi, cx, m = _mixer(
        x_prompt, x_sample,
        state_s5_re[0].reshape(batch, S5_CH), state_s5_im[0].reshape(batch, S5_CH), cx0,
        jnp.broadcast_to(state_ml_m[0].reshape(bh, 1), (bh, LANES)), w)
    c = jnp.swapaxes(cx[:, :, :, :ML_HEAD_DIM], 2, 3)
    n = cx[:, :, :, ML_HEAD_DIM]

    tile = batch * CHUNK
    t_all = idx.shape[0]
    n_blocks = -(-(t_all * TOP_K + N_EXPERTS * (MOE_BLOCK - 1)) // MOE_BLOCK)
    n_blocks_pad = -(-n_blocks // SUBLANES) * SUBLANES
    spos, runs, meta = _plan(counts, idx, tile, n_blocks_pad)
    spos_flat = spos.reshape(t_all * TOP_K)
    runs_flat = runs.reshape(-1)
    buf = _dispatch(meta[:n_blocks, 1], spos_flat, runs_flat, xm, tile, n_blocks)
    out_rows = _experts(meta, buf, w_gate_up[0], b_gate_up[0][:, None, :], w_down[0],
                        b_down[0][:, None, :], n_blocks)
    y_prompt, y_sample = _combine(spos_flat, runs_flat, gates, x1, out_rows, norm_final[None], batch,
                                  lp, ls)

    def states(p):
        return (hr[p].reshape(1, batch, S5_GROUPS, S5_STATE), hi[p].reshape(1, batch, S5_GROUPS, S5_STATE),
                c[p].reshape(1, batch, ML_HEADS, ML_HEAD_DIM, ML_HEAD_DIM),
                n[p].reshape(1, batch, ML_HEADS, ML_HEAD_DIM), m[p, :, 0].reshape(1, batch, ML_HEADS))

    return (y_prompt, y_sample) + states(0) + states(1)
```

```python
import functools

import jax
import jax.numpy as jnp
from jax import lax
from jax.experimental import pallas as pl
from jax.experimental.pallas import tpu as pltpu

F32 = jnp.float32
BF16 = jnp.bfloat16

D_MODEL = 1024
CHUNK = 64
EPS = 1e-6
D_S5 = 512
S5_GROUP = 16
S5_GROUPS = D_S5 // S5_GROUP
S5_STATE = 64
S5_CH = S5_GROUPS * S5_STATE
ML_HEADS = 4
ML_HEAD_DIM = 128
D_ML = ML_HEADS * ML_HEAD_DIM
N_EXPERTS = 32
TOP_K = 4
D_FF = 1024
SWIGLU_LIMIT = 7.0
SWIGLU_ALPHA = 1.702

LANES = 128
SUBLANES = 8
TILE_ROWS = D_MODEL // LANES

OFF_U = 0
OFF_Q = OFF_U + D_S5
OFF_K = OFF_Q + D_ML
OFF_V = OFF_K + D_ML
OFF_O = OFF_V + D_ML
OFF_GS5 = OFF_O + D_ML
OFF_GML = OFF_GS5 + D_MODEL
OFF_GATE = OFF_GML + D_MODEL

S5_HALF_IN = D_S5 // 2
S5_HALF_CH = S5_CH // 2
S5_HALF_TILES = S5_HALF_CH // LANES
S5_SCAN_TILES = 4
ML_GROUP = 2
MOE_BLOCK = 512
EXPERT_QUARTERS = 4
VMEM_LIMIT = 60 * 1024 * 1024


def _rmsnorm(x, w):
    return x * lax.rsqrt(jnp.mean(x * x, axis=-1, keepdims=True) + EPS) * w


def _const_spec(shape):
    nd = len(shape)
    return pl.BlockSpec(shape, lambda *_: (0,) * nd, pipeline_mode=pl.Buffered(1))


def _mixer_kernel(xp_ref, xs_ref, h0r_ref, h0i_ref, cx0_ref, m0_ref,
                  nmix_ref, wina_ref, winb_ref, wing_ref, gbias_ref, abr_ref, abi_ref, bdr_ref, bdi_ref, cb_ref,
                  dskip_ref, glu_ref, gn_ref, wml_ref, wout_ref, nmoe_ref, wr_ref, br_ref,
                  x1_ref, xm_ref, idx_ref, gate_ref, cnt_ref, hr_ref, hi_ref, cx_ref, m_ref,
                  s_bur, s_bui, s_tm, s_q, s_k, s_v, s_o, s_col, s_gated, s_cx, sem, *, batch, n_prompt):
    i = pl.program_id(0)
    rows = batch * CHUNK

    @pl.when(i == 0)
    def _():
        hr_ref[...] = jnp.zeros_like(hr_ref)
        hi_ref[...] = jnp.zeros_like(hi_ref)
        m_ref[...] = jnp.zeros_like(m_ref)
        s_cx[...] = jnp.zeros_like(s_cx)
        cnt_ref[...] = jnp.zeros_like(cnt_ref)

    @pl.when(i == n_prompt)
    def _():
        hr_ref[0] = h0r_ref[...]
        hi_ref[0] = h0i_ref[...]
        m_ref[0] = m0_ref[...]
        load = pltpu.make_async_copy(cx0_ref, s_cx, sem)
        load.start()
        load.wait()

    @pl.when(i < n_prompt)
    def _():
        x1_ref[...] = xp_ref[...].reshape(rows, D_MODEL)

    @pl.when(i >= n_prompt)
    def _():
        x1_ref[...] = xs_ref[...].reshape(rows, D_MODEL)

    xn = _rmsnorm(x1_ref[...], nmix_ref[...]).astype(BF16)

    def proj(off, width):
        if off >= OFF_GATE:
            w_cols = wing_ref[...]
        elif off >= OFF_GS5:
            w_cols = winb_ref[:, off - OFF_GS5:off - OFF_GS5 + width]
        else:
            w_cols = wina_ref[:, off:off + width]
        return jnp.dot(xn, w_cols, preferred_element_type=F32)

    gates = proj(OFF_GATE, LANES) + gbias_ref[...]
    lane_g = lax.broadcasted_iota(jnp.int32, (rows, LANES), 1)
    gg = jnp.where(lane_g < ML_HEADS, gates, jax.nn.log_sigmoid(gates))
    gt8 = gg.T[0:SUBLANES, :]
    pos = lax.broadcasted_iota(jnp.int32, (SUBLANES, rows), 1) % CHUNK
    cum = gt8
    shift = 1
    while shift < CHUNK:
        cum = cum + jnp.where(pos >= shift, pltpu.roll(cum, shift, axis=1), 0.0)
        shift *= 2
    g8 = gt8 - pltpu.roll(cum, ML_HEADS, axis=0)
    mx8 = g8
    shift = 1
    while shift < CHUNK:
        mx8 = jnp.maximum(mx8, jnp.where(pos >= shift, pltpu.roll(mx8, shift, axis=1), -jnp.inf))
        shift *= 2
    sub = lax.broadcasted_iota(jnp.int32, (SUBLANES, rows), 0)
    top8 = jnp.where(sub < ML_HEADS, g8, cum)
    s_col[...] = jnp.concatenate(
        [top8, mx8, jnp.zeros((LANES - 2 * SUBLANES, rows), F32)], axis=0).T
    s_q[...] = (proj(OFF_Q, D_ML) * (ML_HEAD_DIM ** -0.5)).astype(BF16)
    s_k[...] = proj(OFF_K, D_ML).astype(BF16)
    s_v[...] = proj(OFF_V, D_ML).astype(BF16)
    s_o[...] = proj(OFF_O, D_ML)

    tri = (lax.broadcasted_iota(jnp.int32, (CHUNK, CHUNK), 0)
           >= lax.broadcasted_iota(jnp.int32, (CHUNK, CHUNK), 1))[None]
    bdims = ((0,), (0,))

    def rowsl(b):
        return slice(b * CHUNK, (b + 1) * CHUNK)

    def headsl(h):
        return slice(h * ML_HEAD_DIM, (h + 1) * ML_HEAD_DIM)

    for g0 in range(0, batch, ML_GROUP):
        items = [(b, h) for b in range(g0, g0 + ML_GROUP) for h in range(ML_HEADS)]
        def col(j, items=items):
            return jnp.stack([jnp.broadcast_to(s_col[rowsl(b), j + h:j + h + 1], (CHUNK, LANES))
                              for b, h in items])

        g_c, b_c, mx_c = col(0), col(ML_HEADS), col(2 * ML_HEADS)
        g_r = jnp.stack([g8[h:h + 1, rowsl(b)] for b, h in items])
        m_prev = jnp.stack([m_ref[0, b * ML_HEADS + h:b * ML_HEADS + h + 1, :] for b, h in items])
        q3 = jnp.stack([s_q[rowsl(b), headsl(h)] for b, h in items])
        k3 = jnp.stack([s_k[rowsl(b), headsl(h)] for b, h in items])
        v3 = jnp.stack([s_v[rowsl(b), headsl(h)] for b, h in items])
        cx = jnp.stack([s_cx[b * ML_HEADS + h] for b, h in items])

        big_m = jnp.maximum(m_prev, mx_c)
        p = jnp.exp(jnp.where(tri, g_r - big_m[:, :, :CHUNK], -jnp.inf))
        w_inter = jnp.exp(m_prev - big_m)
        s = lax.dot_general(q3, k3, (((2,), (2,)), bdims), preferred_element_type=F32) * p
        cqx = lax.dot_general(q3, cx.astype(BF16), (((2,), (1,)), bdims), preferred_element_type=F32)
        num = (lax.dot_general(s.astype(BF16), v3, (((2,), (1,)), bdims), preferred_element_type=F32)
               + w_inter * cqx[:, :, :ML_HEAD_DIM])
        den_dot = jnp.sum(s, axis=2, keepdims=True) + w_inter * cqx[:, :, ML_HEAD_DIM:]
        m_t = b_c + big_m
        hout = num / jnp.maximum(jnp.abs(den_dot), jnp.exp(-m_t))

        m_last = big_m[:, CHUNK - 1:CHUNK, :]
        w_end = jnp.exp(g_c - m_last)
        decay = jnp.exp(m_prev - m_last)
        wvx = jnp.concatenate([w_end * v3.astype(F32), w_end], axis=2).astype(BF16)
        k_t = jnp.stack([s_k[rowsl(b), headsl(h)].astype(F32).T.astype(BF16) for b, h in items])
        cx_new = (jnp.concatenate([decay, decay], axis=2) * cx
                  + lax.dot_general(k_t, wvx, (((2,), (1,)), bdims), preferred_element_type=F32))
        m_new = m_t[:, CHUNK - 1:CHUNK, :]

        hc = hout - jnp.mean(hout, axis=2, keepdims=True)
        hn = hc * lax.rsqrt(jnp.mean(hc * hc, axis=2, keepdims=True) + EPS)
        for n, (b, h) in enumerate(items):
            bh = b * ML_HEADS + h
            s_cx[bh] = cx_new[n]
            m_ref[0, bh:bh + 1, :] = m_new[n]
            s_gated[rowsl(b), headsl(h)] = (jax.nn.sigmoid(s_o[rowsl(b), headsl(h)])
                                            * (hn[n] * gn_ref[:, headsl(h)])).astype(BF16)

    u = proj(OFF_U, D_S5)
    in_tiles = D_S5 // LANES
    for j in range(in_tiles):
        for b in range(batch):
            s_tm[j, pl.ds(b, CHUNK, stride=batch), :] = u[b * CHUNK:(b + 1) * CHUNK,
                                                          j * LANES:(j + 1) * LANES]
    ub = jnp.concatenate([s_tm[j] for j in range(in_tiles)], axis=1).astype(BF16)
    ys = []
    merge = []
    for k in range(2):
        uk = ub[:, k * S5_HALF_IN:(k + 1) * S5_HALF_IN]
        bur = jnp.dot(uk, bdr_ref[k], preferred_element_type=F32)
        bui = jnp.dot(uk, bdi_ref[k], preferred_element_type=F32)
        for j in range(S5_HALF_TILES):
            s_bur[j] = bur[:, j * LANES:(j + 1) * LANES]
            s_bui[j] = bui[:, j * LANES:(j + 1) * LANES]

        if k == 0:
            merge.append(jax.nn.sigmoid(proj(OFF_GML, D_MODEL)) * jnp.dot(
                s_gated[...], wml_ref[...], preferred_element_type=F32))
        else:
            merge.append(jax.nn.sigmoid(proj(OFF_GS5, D_MODEL)))

        for c in range(S5_HALF_TILES // S5_SCAN_TILES):
            tiles = range(c * S5_SCAN_TILES, (c + 1) * S5_SCAN_TILES)
            lanes = [slice(k * S5_HALF_CH + j * LANES, k * S5_HALF_CH + (j + 1) * LANES) for j in tiles]
            ar = [jnp.broadcast_to(abr_ref[:, ls], (batch, LANES)) for ls in lanes]
            ai = [jnp.broadcast_to(abi_ref[:, ls], (batch, LANES)) for ls in lanes]

            def step(t, carry, tiles=tiles, ar=ar, ai=ai):
                slab = pl.ds(pl.multiple_of(t * batch, batch), batch)
                out = []
                for n, j in enumerate(tiles):
                    hr, hi = carry[n]
                    nr = ar[n] * hr - ai[n] * hi + s_bur[j, slab, :]
                    ni = ar[n] * hi + ai[n] * hr + s_bui[j, slab, :]
                    s_bur[j, slab, :] = nr
                    s_bui[j, slab, :] = ni
                    out.append((nr, ni))
                return tuple(out)

            init = tuple((hr_ref[0, :, ls], hi_ref[0, :, ls]) for ls in lanes)
            fin = lax.fori_loop(0, CHUNK, step, init, unroll=True)
            for n, ls in enumerate(lanes):
                hr_ref[0, :, ls] = fin[n][0]
                hi_ref[0, :, ls] = fin[n][1]

        def states(ref):
            return jnp.concatenate([ref[j] for j in range(S5_HALF_TILES)], axis=1).astype(BF16)

        yk = jnp.dot(states(s_bur), cb_ref[k, :S5_HALF_CH, :], preferred_element_type=F32)
        yk = yk + jnp.dot(states(s_bui), cb_ref[k, S5_HALF_CH:, :], preferred_element_type=F32)
        ys.append(yk)
    y_tm = jnp.concatenate(ys, axis=1)
    for j in range(in_tiles):
        s_tm[j] = y_tm[:, j * LANES:(j + 1) * LANES]
    y = jnp.concatenate(
        [jnp.concatenate([s_tm[j, pl.ds(b, CHUNK, stride=batch), :] for j in range(in_tiles)], axis=1)
         for b in range(batch)], axis=0) + dskip_ref[...] * u
    glu = jnp.dot(jax.nn.gelu(y).astype(BF16), glu_ref[...], preferred_element_type=F32)
    y_s5 = glu[:, :D_MODEL] * jax.nn.sigmoid(glu[:, D_MODEL:])
    mix = merge[0] + merge[1] * y_s5

    x1 = x1_ref[...] + jnp.dot(mix.astype(BF16), wout_ref[...], preferred_element_type=F32)
    x1_ref[...] = x1

    xm = _rmsnorm(x1, nmoe_ref[...])
    logits = jnp.dot(xm.astype(BF16), wr_ref[...], preferred_element_type=F32) + br_ref[...]
    lane = lax.broadcasted_iota(jnp.int32, (rows, LANES), 1)
    lane_f = lane.astype(F32)
    logits = jnp.where(lane < N_EXPERTS, logits, -jnp.inf)
    vals, idxs = [], []
    for _ in range(TOP_K):
        mx = jnp.max(logits, axis=1, keepdims=True)
        am = jnp.min(jnp.where(logits == mx, lane_f, float(LANES)), axis=1, keepdims=True)
        vals.append(mx)
        idxs.append(am)
        logits = jnp.where(lane_f == am, -jnp.inf, logits)
    exps = [jnp.exp(v - vals[0]) for v in vals]
    esum = exps[0] + exps[1] + exps[2] + exps[3]
    idx_w = jnp.zeros((rows, LANES), F32)
    gate_w = jnp.zeros((rows, LANES), F32)
    for k in range(TOP_K):
        idx_w = jnp.where(lane == k, idxs[k], idx_w)
        gate_w = jnp.where(lane == k, exps[k] / esum, gate_w)
    idx_ref[...] = idx_w[:, :TOP_K].astype(jnp.int32)
    gate_ref[...] = gate_w[:, :TOP_K]
    chosen = jnp.zeros((rows, LANES), F32)
    for k in range(TOP_K):
        chosen = chosen + (lane_f == idxs[k]).astype(F32)
    cnt_ref[...] += jnp.sum(chosen, axis=0, keepdims=True)

    for s in range(TILE_ROWS):
        xm_ref[pl.ds(s, rows, stride=TILE_ROWS), :] = xm[:, s * LANES:(s + 1) * LANES]

    for phase_id, final_step in ((0, n_prompt - 1), (1, pl.num_programs(0) - 1)):
        @pl.when(i == final_step)
        def _(phase_id=phase_id):
            store = pltpu.make_async_copy(s_cx, cx_ref.at[phase_id], sem)
            store.start()
            store.wait()


def _mixer(x_prompt, x_sample, h0r, h0i, cx0, m0, w):
    batch, lp, _ = x_prompt.shape
    assert x_sample.shape[0] == batch and lp % CHUNK == 0 and x_sample.shape[1] % CHUNK == 0
    assert batch % ML_GROUP == 0
    n_prompt = lp // CHUNK
    n_sample = x_sample.shape[1] // CHUNK
    nblk = n_prompt + n_sample
    rows = batch * CHUNK
    t_all = nblk * rows
    bh = batch * ML_HEADS

    def phase(i):
        return jnp.minimum(i // n_prompt, 1)

    state_specs = [
        _const_spec((batch, S5_CH)), _const_spec((batch, S5_CH)),
        pl.BlockSpec(memory_space=pl.ANY),
        _const_spec((bh, LANES)),
    ]
    weights = (w['norm_mix'], w['w_in_a'], w['w_in_b'], w['w_in_g'], w['gate_bias'], w['abar_re'], w['abar_im'], w['bd_re'],
               w['bd_im'], w['cb'], w['d_skip'], w['w_glu'], w['ml_gn'], w['w_ml_out'], w['w_out'],
               w['norm_moe'], w['w_router'], w['b_router'])
    in_specs = ([pl.BlockSpec((batch, CHUNK, D_MODEL), lambda i: (0, jnp.minimum(i, n_prompt - 1), 0)),
                 pl.BlockSpec((batch, CHUNK, D_MODEL), lambda i: (0, jnp.maximum(i - n_prompt, 0), 0))]
                + state_specs + [_const_spec(a.shape) for a in weights])
    out_shape = (
        jax.ShapeDtypeStruct((t_all, D_MODEL), F32),
        jax.ShapeDtypeStruct((t_all * TILE_ROWS, LANES), F32),
        jax.ShapeDtypeStruct((t_all, TOP_K), jnp.int32),
        jax.ShapeDtypeStruct((t_all, TOP_K), F32),
        jax.ShapeDtypeStruct((1, LANES), F32),
        jax.ShapeDtypeStruct((2, batch, S5_CH), F32),
        jax.ShapeDtypeStruct((2, batch, S5_CH), F32),
        jax.ShapeDtypeStruct((2, bh, ML_HEAD_DIM, 2 * ML_HEAD_DIM), F32),
        jax.ShapeDtypeStruct((2, bh, LANES), F32),
    )
    out_specs = (
        pl.BlockSpec((rows, D_MODEL), lambda i: (i, 0)),
        pl.BlockSpec((rows * TILE_ROWS, LANES), lambda i: (i, 0)),
        pl.BlockSpec((rows, TOP_K), lambda i: (i, 0)),
        pl.BlockSpec((rows, TOP_K), lambda i: (i, 0)),
        pl.BlockSpec((1, LANES), lambda i: (0, 0)),
        pl.BlockSpec((1, batch, S5_CH), lambda i: (phase(i), 0, 0)),
        pl.BlockSpec((1, batch, S5_CH), lambda i: (phase(i), 0, 0)),
        pl.BlockSpec(memory_space=pl.ANY),
        pl.BlockSpec((1, bh, LANES), lambda i: (phase(i), 0, 0)),
    )
    scratch = [
        pltpu.VMEM((S5_HALF_TILES, rows, LANES), F32), pltpu.VMEM((S5_HALF_TILES, rows, LANES), F32),
        pltpu.VMEM((D_S5 // LANES, rows, LANES), F32),
        pltpu.VMEM((rows, D_ML), BF16), pltpu.VMEM((rows, D_ML), BF16), pltpu.VMEM((rows, D_ML), BF16),
        pltpu.VMEM((rows, D_ML), F32),
        pltpu.VMEM((rows, LANES), F32),
        pltpu.VMEM((rows, D_ML), BF16),
        pltpu.VMEM((bh, ML_HEAD_DIM, 2 * ML_HEAD_DIM), F32),
        pltpu.SemaphoreType.DMA(()),
    ]
    return pl.pallas_call(
        functools.partial(_mixer_kernel, batch=batch, n_prompt=n_prompt),
        grid=(nblk,),
        in_specs=in_specs,
        out_specs=out_specs,
        out_shape=out_shape,
        scratch_shapes=scratch,
        compiler_params=pltpu.CompilerParams(dimension_semantics=("arbitrary",),
                                             vmem_limit_bytes=VMEM_LIMIT),
        name="mixer",
    )(x_prompt, x_sample, h0r, h0i, cx0, m0, *weights)


def _plan_kernel(cnt_ref, idx_ref, spos_ref, runs_ref, meta_ref, s_run, s_pstart, s_lower, *, tile,
                 n_blocks_pad):
    i = pl.program_id(0)
    lane = lax.broadcasted_iota(jnp.int32, (tile, LANES), 1)
    idx = idx_ref[...]
    onehots = [(lane == idx[:, k:k + 1]).astype(F32) for k in range(TOP_K)]
    mask = onehots[0] + onehots[1] + onehots[2] + onehots[3]

    @pl.when(i == 0)
    def _():
        cnt = cnt_ref[...]
        nblk = jnp.floor((cnt + (MOE_BLOCK - 1)) * (1.0 / MOE_BLOCK))
        lane1 = lax.broadcasted_iota(jnp.int32, (1, LANES), 1)
        pend = nblk
        shift = 1
        while shift < LANES:
            pend = pend + jnp.where(lane1 >= shift, pltpu.roll(pend, shift, axis=1), 0.0)
            shift *= 2
        s_pstart[...] = (pend - nblk) * MOE_BLOCK
        s_run[...] = jnp.zeros_like(s_run)
        r = lax.broadcasted_iota(jnp.int32, (tile, tile), 0)
        c = lax.broadcasted_iota(jnp.int32, (tile, tile), 1)
        s_lower[...] = (c < r).astype(BF16)
        blk = lax.broadcasted_iota(jnp.int32, (n_blocks_pad, LANES), 0).astype(F32)
        lane_b = lax.broadcasted_iota(jnp.int32, (n_blocks_pad, LANES), 1)
        lane_f = lane_b.astype(F32)
        is_e = lane_b < N_EXPERTS
        done = jnp.logical_and(is_e, pend <= blk)
        e_of = jnp.minimum(jnp.sum(done.astype(F32), axis=1, keepdims=True), N_EXPERTS - 1.0)
        mine = lane_f == e_of
        blk_in_e = jnp.sum(jnp.where(mine, blk - (pend - nblk), 0.0), axis=1, keepdims=True)
        cnt_e = jnp.sum(jnp.where(mine, cnt, 0.0), axis=1, keepdims=True)
        valid = jnp.clip(cnt_e - blk_in_e * MOE_BLOCK, 0.0, float(MOE_BLOCK))
        used = jnp.sum(jnp.where(lane1 == N_EXPERTS - 1, pend, 0.0), axis=1, keepdims=True)
        owns = jnp.logical_and(is_e, nblk > 0.0)
        first = jnp.logical_and(blk_in_e == 0.0, blk[:, :1] < used).astype(F32)
        later = jnp.logical_and(owns, lane_f > e_of)
        nxt = jnp.min(jnp.where(later, lane_f, float(LANES)), axis=1, keepdims=True)
        nxt = jnp.where(nxt < float(LANES), nxt, -1.0)
        run = jnp.sum(jnp.logical_and(owns, lane_f < e_of).astype(F32), axis=1, keepdims=True)
        parity = run - 2.0 * jnp.floor(run * 0.5)
        meta = jnp.zeros((n_blocks_pad, LANES), F32)
        for col, val in enumerate((e_of, valid, used, first, nxt, parity)):
            meta = jnp.where(lane_b == col, val, meta)
        meta_ref[...] = meta.astype(jnp.int32)

    before = jnp.dot(s_lower[...], mask.astype(BF16), preferred_element_type=F32)
    tile_cnt = jnp.sum(mask, axis=0, keepdims=True)
    lane1 = lax.broadcasted_iota(jnp.int32, (1, LANES), 1)
    incl = tile_cnt
    shift = 1
    while shift < LANES:
        incl = incl + jnp.where(lane1 >= shift, pltpu.roll(incl, shift, axis=1), 0.0)
        shift *= 2
    tile_start = incl - tile_cnt
    parity = (i % 2).astype(F32)
    base = (before + tile_start + parity * (tile * TOP_K)) * TILE_ROWS
    spos = jnp.zeros((tile, LANES), F32)
    for k in range(TOP_K):
        pk = jnp.sum(onehots[k] * base, axis=1, keepdims=True)
        spos = jnp.where(lane == k, pk, spos)
    spos_ref[...] = spos[:, :TOP_K].astype(jnp.int32)
    sub = lax.broadcasted_iota(jnp.int32, (SUBLANES, LANES), 0)
    runs = jnp.where(sub == 0, tile_cnt,
                     jnp.where(sub == 1, s_pstart[...] + s_run[...],
                               jnp.where(sub == 2, tile_start, 0.0)))
    runs_ref[0] = runs.astype(jnp.int32)
    s_run[...] += tile_cnt


def _plan(counts, idx_all, tile, n_blocks_pad):
    t_all = idx_all.shape[0]
    return pl.pallas_call(
        functools.partial(_plan_kernel, tile=tile, n_blocks_pad=n_blocks_pad),
        grid=(t_all // tile,),
        in_specs=[pl.BlockSpec((1, LANES), lambda i: (0, 0)),
                  pl.BlockSpec((tile, TOP_K), lambda i: (i, 0))],
        out_specs=(pl.BlockSpec((tile, TOP_K), lambda i: (i, 0)),
                   pl.BlockSpec((1, SUBLANES, LANES), lambda i: (i, 0, 0)),
                   pl.BlockSpec((n_blocks_pad, LANES), lambda i: (0, 0))),
        out_shape=(jax.ShapeDtypeStruct((t_all, TOP_K), jnp.int32),
                   jax.ShapeDtypeStruct((t_all // tile, SUBLANES, LANES), jnp.int32),
                   jax.ShapeDtypeStruct((n_blocks_pad, LANES), jnp.int32)),
        scratch_shapes=[pltpu.VMEM((1, LANES), F32), pltpu.VMEM((1, LANES), F32),
                        pltpu.VMEM((tile, tile), BF16)],
        compiler_params=pltpu.CompilerParams(dimension_semantics=("arbitrary",)),
        name="moe_plan",
    )(counts, idx_all)


RUN_BITS = 10
RUN_SPLIT = 7


def _run_copies(runs_ref, sorted_hbm, stage, sem, to_hbm):
    def expert(e, carry):
        n = runs_ref[e]
        first_sorted = runs_ref[LANES + e]
        first_staged = runs_ref[2 * LANES + e]
        def piece(bit):
            size = (1 << bit) * TILE_ROWS
            done = lax.shift_left(lax.shift_right_logical(n, bit + 1), bit + 1)

            @pl.when(lax.bitwise_and(lax.shift_right_logical(n, bit), 1) == 1)
            def _():
                hbm = sorted_hbm.at[pl.ds(pl.multiple_of((first_sorted + done) * TILE_ROWS, TILE_ROWS), size)]
                vmem = stage.at[pl.ds(pl.multiple_of((first_staged + done) * TILE_ROWS, TILE_ROWS), size)]
                src, dst = (vmem, hbm) if to_hbm else (hbm, vmem)
                pltpu.make_async_copy(src, dst, sem).start(priority=bit % 2)

        @pl.when(n >= (1 << RUN_SPLIT))
        def _():
            for bit in reversed(range(RUN_SPLIT, RUN_BITS)):
                piece(bit)

        for bit in reversed(range(RUN_SPLIT)):
            piece(bit)
        return carry

    lax.fori_loop(0, N_EXPERTS, expert, 0)


def _dispatch_kernel(valid_ref, spos_ref, runs_ref, xm_ref, buf_ref, zeros, stage, sem, zsem, *, tile,
                     n_blocks):
    i = pl.program_id(0)
    slot = i % 2
    blk_rows = MOE_BLOCK * TILE_ROWS
    n_rows = tile * TOP_K * TILE_ROWS

    def drain(s):
        pltpu.make_async_copy(buf_ref.at[pl.ds(0, n_rows)], buf_ref.at[pl.ds(0, n_rows)], sem.at[s]).wait()

    @pl.when(pl.program_id(0) == 0)
    def _():
        zeros[...] = jnp.zeros_like(zeros)

        def block_copy(i):
            return pltpu.make_async_copy(
                zeros, buf_ref.at[pl.ds(pl.multiple_of(i * blk_rows, blk_rows), blk_rows)], zsem)

        def fill(i, carry):
            @pl.when(valid_ref[i] < MOE_BLOCK)
            def _():
                block_copy(i).start()
            return carry

        def fill_wait(i, carry):
            @pl.when(valid_ref[i] < MOE_BLOCK)
            def _():
                block_copy(i).wait()
            return carry

        lax.fori_loop(0, n_blocks, fill, 0)
        lax.fori_loop(0, n_blocks, fill_wait, 0)

    @pl.when(i >= 2)
    def _():
        drain(slot)

    def place(j, carry):
        tok = xm_ref[pl.ds(pl.multiple_of(j * TILE_ROWS, TILE_ROWS), TILE_ROWS), :]
        for k in range(TOP_K):
            stage[pl.ds(pl.multiple_of(spos_ref[j * TOP_K + k], TILE_ROWS), TILE_ROWS), :] = tok
        return carry

    lax.fori_loop(0, tile, place, 0, unroll=16)
    slot_rows = stage.at[pl.ds(pl.multiple_of(slot * n_rows, n_rows), n_rows)]
    _run_copies(runs_ref, buf_ref, slot_rows, sem.at[slot], to_hbm=True)

    last = pl.num_programs(0) - 1

    @pl.when(i == last)
    def _():
        drain(slot)

    @pl.when(jnp.logical_and(i == last, i >= 1))
    def _():
        drain(1 - slot)


def _dispatch(block_valid, spos_flat, runs_flat, xm_tiles, tile, n_blocks):
    t_all = xm_tiles.shape[0] // TILE_ROWS
    grid_spec = pltpu.PrefetchScalarGridSpec(
        num_scalar_prefetch=1,
        grid=(t_all // tile,),
        in_specs=[pl.BlockSpec((tile * TOP_K,), lambda i, va: (i,), memory_space=pltpu.SMEM),
                  pl.BlockSpec((SUBLANES * LANES,), lambda i, va: (i,), memory_space=pltpu.SMEM),
                  pl.BlockSpec((tile * TILE_ROWS, LANES), lambda i, va: (i, 0))],
        out_specs=pl.BlockSpec(memory_space=pl.ANY),
        scratch_shapes=[pltpu.VMEM((MOE_BLOCK * TILE_ROWS, LANES), F32),
                        pltpu.VMEM((2 * tile * TOP_K * TILE_ROWS, LANES), F32),
                        pltpu.SemaphoreType.DMA((2,)), pltpu.SemaphoreType.DMA(())],
    )
    return pl.pallas_call(
        functools.partial(_dispatch_kernel, tile=tile, n_blocks=n_blocks),
        grid_spec=grid_spec,
        out_shape=jax.ShapeDtypeStruct((n_blocks * MOE_BLOCK * TILE_ROWS, LANES), F32),
        compiler_params=pltpu.CompilerParams(dimension_semantics=("arbitrary",),
                                             vmem_limit_bytes=VMEM_LIMIT),
        name="moe_dispatch",
    )(block_valid, spos_flat, runs_flat, xm_tiles)


def _expert_kernel(be_ref, used_ref, first_ref, next_ref, slot_ref, valid_ref, x_ref, wgu_ref, bgu_ref,
                   wd_ref, bd_ref, o_ref, wbuf_gu, wbuf_d, s_wgu, s_wd, sem):
    i = pl.program_id(0)

    @pl.when(i >= used_ref[0])
    def _():
        o_ref[...] = jnp.zeros_like(o_ref)

    def weight_copies(e, slot):
        return (pltpu.make_async_copy(wgu_ref.at[e], wbuf_gu.at[slot], sem.at[0, slot]),
                pltpu.make_async_copy(wd_ref.at[e], wbuf_d.at[slot], sem.at[1, slot]))

    @pl.when(i == 0)
    def _():
        for cp in weight_copies(be_ref[0], slot_ref[0]):
            cp.start()

    @pl.when(first_ref[i] == 1)
    def _():
        slot = slot_ref[i]
        for cp in weight_copies(be_ref[i], slot):
            cp.wait()

        @pl.when(next_ref[i] >= 0)
        def _():
            for cp in weight_copies(next_ref[i], 1 - slot):
                cp.start()

        s_wgu[...] = wbuf_gu[slot].astype(BF16)
        s_wd[...] = wbuf_d[slot].astype(BF16)

    def ffn(nrows):
        pieces = [x_ref[pl.ds(s, nrows, stride=TILE_ROWS), :] for s in range(TILE_ROWS)]
        x = jnp.concatenate(pieces, axis=1).astype(BF16)
        gu = jnp.dot(x, s_wgu[...], preferred_element_type=F32) + bgu_ref[0]
        g = jnp.minimum(gu[:, :D_FF], SWIGLU_LIMIT)
        up = jnp.clip(gu[:, D_FF:], -SWIGLU_LIMIT, SWIGLU_LIMIT)
        hdn = (up + 1.0) * (g * jax.nn.sigmoid(SWIGLU_ALPHA * g))
        out = jnp.dot(hdn.astype(BF16), s_wd[...], preferred_element_type=F32) + bd_ref[0]
        for s in range(TILE_ROWS):
            o_ref[pl.ds(s, nrows, stride=TILE_ROWS), :] = out[:, s * LANES:(s + 1) * LANES]
        if nrows < MOE_BLOCK:
            o_ref[nrows * TILE_ROWS:, :] = jnp.zeros(((MOE_BLOCK - nrows) * TILE_ROWS, LANES), F32)

    in_use = i < used_ref[0]
    quarter = MOE_BLOCK // EXPERT_QUARTERS
    assert quarter & (quarter - 1) == 0
    quarters = lax.shift_right_logical(valid_ref[i] + (quarter - 1), quarter.bit_length() - 1)
    for nq in range(1, EXPERT_QUARTERS + 1):
        cond = (quarters <= 1) if nq == 1 else (quarters == nq)

        @pl.when(jnp.logical_and(in_use, cond))
        def _(nq=nq):
            ffn(nq * quarter)


def _experts(meta, buf, wgu, bgu, wd, bd, n_blocks):
    blk_rows = MOE_BLOCK * TILE_ROWS
    block_e, n_used = meta[:n_blocks, 0], meta[:1, 2]
    first, nxt, slot = meta[:n_blocks, 3], meta[:n_blocks, 4], meta[:n_blocks, 5]
    valid = meta[:n_blocks, 1]

    def in_row_map(i, be, nu, *_):
        return (jnp.minimum(i, nu[0] - 1), 0)

    def b_map(i, be, *_):
        return (be[i], 0, 0)

    grid_spec = pltpu.PrefetchScalarGridSpec(
        num_scalar_prefetch=6,
        grid=(n_blocks,),
        in_specs=[pl.BlockSpec((blk_rows, LANES), in_row_map),
                  pl.BlockSpec(memory_space=pl.ANY),
                  pl.BlockSpec((1, 1, 2 * D_FF), b_map),
                  pl.BlockSpec(memory_space=pl.ANY),
                  pl.BlockSpec((1, 1, D_MODEL), b_map)],
        out_specs=pl.BlockSpec((blk_rows, LANES), lambda i, *_: (i, 0)),
        scratch_shapes=[pltpu.VMEM((2, D_MODEL, 2 * D_FF), F32), pltpu.VMEM((2, D_FF, D_MODEL), F32),
                        pltpu.VMEM((D_MODEL, 2 * D_FF), BF16), pltpu.VMEM((D_FF, D_MODEL), BF16),
                        pltpu.SemaphoreType.DMA((2, 2))],
    )
    return pl.pallas_call(
        _expert_kernel,
        grid_spec=grid_spec,
        out_shape=jax.ShapeDtypeStruct(buf.shape, F32),
        compiler_params=pltpu.CompilerParams(dimension_semantics=("arbitrary",),
                                             vmem_limit_bytes=VMEM_LIMIT),
        name="moe_experts",
    )(block_e, n_used, first, nxt, slot, valid, buf, wgu, bgu, wd, bd)


def _combine_kernel(spos_ref, runs_ref, next_runs_ref, gate_ref, x1_ref, rows_ref, nf_ref, yp_ref,
                    ys_ref, stage, gbuf, sem, *, batch, n_prompt):
    i = pl.program_id(0)
    tile = batch * CHUNK
    n_rows = tile * TOP_K * TILE_ROWS
    slot = i % 2

    def slot_rows(s):
        return stage.at[pl.ds(pl.multiple_of(s * n_rows, n_rows), n_rows)]

    @pl.when(i == 0)
    def _():
        _run_copies(runs_ref, rows_ref, slot_rows(0), sem.at[0], to_hbm=False)

    @pl.when(i + 1 < pl.num_programs(0))
    def _():
        _run_copies(next_runs_ref, rows_ref, slot_rows(1 - slot), sem.at[1 - slot], to_hbm=False)

    pltpu.make_async_copy(rows_ref.at[pl.ds(0, n_rows)], slot_rows(slot), sem.at[slot]).wait()

    def pick(j, carry):
        for k in range(TOP_K):
            gbuf[k, pl.ds(pl.multiple_of(j * TILE_ROWS, TILE_ROWS), TILE_ROWS), :] = (
                stage[pl.ds(pl.multiple_of(spos_ref[j * TOP_K + k], TILE_ROWS), TILE_ROWS), :])
        return carry

    lax.fori_loop(0, tile, pick, 0, unroll=16)

    gates = gate_ref[...]
    pieces = []
    for s in range(TILE_ROWS):
        piece = x1_ref[:, s * LANES:(s + 1) * LANES]
        for k in range(TOP_K):
            piece = piece + gates[:, k:k + 1] * gbuf[k, pl.ds(s, tile, stride=TILE_ROWS), :]
        pieces.append(piece)
    y = _rmsnorm(jnp.concatenate(pieces, axis=1), nf_ref[...]).reshape(batch, CHUNK, D_MODEL)

    @pl.when(i < n_prompt)
    def _():
        yp_ref[...] = y

    @pl.when(i >= n_prompt)
    def _():
        ys_ref[...] = y


def _combine(spos_flat, runs_flat, gates, x1, out_rows, norm_final, batch, lp, ls):
    tile = batch * CHUNK
    n_prompt = lp // CHUNK
    n_sample = ls // CHUNK
    last = n_prompt + n_sample - 1
    return pl.pallas_call(
        functools.partial(_combine_kernel, batch=batch, n_prompt=n_prompt),
        grid=(n_prompt + n_sample,),
        in_specs=[pl.BlockSpec((tile * TOP_K,), lambda i: (i,), memory_space=pltpu.SMEM),
                  pl.BlockSpec((SUBLANES * LANES,), lambda i: (i,), memory_space=pltpu.SMEM),
                  pl.BlockSpec((SUBLANES * LANES,), lambda i: (jnp.minimum(i + 1, last),),
                               memory_space=pltpu.SMEM),
                  pl.BlockSpec((tile, TOP_K), lambda i: (i, 0)),
                  pl.BlockSpec((tile, D_MODEL), lambda i: (i, 0)),
                  pl.BlockSpec(memory_space=pl.ANY),
                  pl.BlockSpec((1, D_MODEL), lambda i: (0, 0))],
        out_specs=(pl.BlockSpec((batch, CHUNK, D_MODEL), lambda i: (0, jnp.minimum(i, n_prompt - 1), 0)),
                   pl.BlockSpec((batch, CHUNK, D_MODEL), lambda i: (0, jnp.maximum(i - n_prompt, 0), 0))),
        out_shape=(jax.ShapeDtypeStruct((batch, lp, D_MODEL), F32),
                   jax.ShapeDtypeStruct((batch, ls, D_MODEL), F32)),
        scratch_shapes=[pltpu.VMEM((2 * tile * TOP_K * TILE_ROWS, LANES), F32),
                        pltpu.VMEM((TOP_K, tile * TILE_ROWS, LANES), F32),
                        pltpu.SemaphoreType.DMA((2,))],
        compiler_params=pltpu.CompilerParams(dimension_semantics=("arbitrary",),
                                             vmem_limit_bytes=VMEM_LIMIT),
        name="moe_combine",
    )(spos_flat, runs_flat, runs_flat, gates, x1, out_rows, norm_final)


def _prep_weights(norm_mix, w_in, b_ig, b_fg, s5_a_re, s5_a_im, s5_log_dt, s5_b_re, s5_b_im,
                  s5_c_re, s5_c_im, s5_d, w_s5_glu, ml_gn, w_ml_out, w_out, norm_moe, w_router,
                  b_router):
    w = w_in[0]
    o_ig = D_S5 + 4 * D_ML
    o_gs5 = o_ig + 2 * ML_HEADS
    w_a = w[:, :o_ig].astype(BF16)
    w_b = w[:, o_gs5:].astype(BF16)
    w_g = jnp.concatenate([w[:, o_ig:o_gs5], jnp.zeros((D_MODEL, LANES - 2 * ML_HEADS), F32)],
                          axis=1).astype(BF16)
    gate_bias = jnp.concatenate([b_ig[0], b_fg[0], jnp.zeros((LANES - 2 * ML_HEADS,), F32)])[None]

    ar, ai = s5_a_re[0], s5_a_im[0]
    dt = jnp.exp(s5_log_dt[0])[:, None]
    mag = jnp.exp(dt * ar)
    abar_re = mag * jnp.cos(dt * ai)
    abar_im = mag * jnp.sin(dt * ai)
    den = ar * ar + ai * ai
    fr = ((abar_re - 1.0) * ar + abar_im * ai) / den
    fi = (abar_im * ar - (abar_re - 1.0) * ai) / den
    br, bi = s5_b_re[0], s5_b_im[0]
    bbar_re = fr[..., None] * br - fi[..., None] * bi
    bbar_im = fr[..., None] * bi + fi[..., None] * br
    gh = S5_GROUPS // 2
    same_group = (jnp.arange(gh)[:, None, None, None] == jnp.arange(gh)[None, None, :, None])

    def blockdiag_in(bb):
        t = bb.reshape(2, gh, S5_STATE, S5_GROUP).transpose(0, 1, 3, 2)
        full = jnp.where(same_group[None], t[:, :, :, None, :], 0.0)
        return full.reshape(2, gh * S5_GROUP, gh * S5_STATE).astype(BF16)

    def blockdiag_out(cc):
        t = cc.reshape(2, gh, S5_GROUP, S5_STATE).transpose(0, 1, 3, 2)
        full = jnp.where(same_group[None], t[:, :, :, None, :], 0.0)
        return full.reshape(2, gh * S5_STATE, gh * S5_GROUP)

    cb = jnp.concatenate([blockdiag_out(s5_c_re[0]), -blockdiag_out(s5_c_im[0])], axis=1).astype(BF16)
    w_router_p = jnp.concatenate(
        [w_router[0], jnp.zeros((D_MODEL, LANES - N_EXPERTS), F32)], axis=1).astype(BF16)
    b_router_p = jnp.concatenate([b_router[0], jnp.zeros((LANES - N_EXPERTS,), F32)])[None]
    return dict(
        norm_mix=norm_mix[0][None], w_in_a=w_a, w_in_b=w_b, w_in_g=w_g, gate_bias=gate_bias,
        abar_re=abar_re.reshape(1, S5_CH), abar_im=abar_im.reshape(1, S5_CH),
        bd_re=blockdiag_in(bbar_re), bd_im=blockdiag_in(bbar_im), cb=cb,
        d_skip=s5_d[0][None], w_glu=w_s5_glu[0].astype(BF16), ml_gn=ml_gn[0][None],
        w_ml_out=w_ml_out[0].astype(BF16), w_out=w_out[0].astype(BF16),
        norm_moe=norm_moe[0][None], w_router=w_router_p, b_router=b_router_p)


def kernel(x_prompt, x_sample, state_s5_re, state_s5_im, state_ml_C, state_ml_n, state_ml_m, norm_mix, w_in, b_ig, b_fg, s5_a_re, s5_a_im, s5_log_dt, s5_b_re, s5_b_im, s5_c_re, s5_c_im, s5_d, w_s5_glu, ml_gn, w_ml_out, w_out, norm_moe, w_router, b_router, w_gate_up, b_gate_up, w_down, b_down, norm_final):
    w = _prep_weights(norm_mix, w_in, b_ig, b_fg, s5_a_re, s5_a_im, s5_log_dt, s5_b_re, s5_b_im,
                      s5_c_re, s5_c_im, s5_d, w_s5_glu, ml_gn, w_ml_out, w_out, norm_moe,
                      w_router, b_router)
    batch, lp, _ = x_prompt.shape
    ls = x_sample.shape[1]
    bh = batch * ML_HEADS
    cx0 = jnp.concatenate(
        [jnp.swapaxes(state_ml_C[0].reshape(bh, ML_HEAD_DIM, ML_HEAD_DIM), 1, 2),
         jnp.broadcast_to(state_ml_n[0].reshape(bh, ML_HEAD_DIM, 1), (bh, ML_HEAD_DIM, ML_HEAD_DIM))],
        axis=2)
    x1, xm, idx, gates, counts, hr, hi, cx, m = _mixer(
        x_prompt, x_sample,
        state_s5_re[0].reshape(batch, S5_CH), state_s5_im[0].reshape(batch, S5_CH), cx0,
        jnp.broadcast_to(state_ml_m[0].reshape(bh, 1), (bh, LANES)), w)
    c = jnp.swapaxes(cx[:, :, :, :ML_HEAD_DIM], 2, 3)
    n = cx[:, :, :, ML_HEAD_DIM]

    tile = batch * CHUNK
    t_all = idx.shape[0]
    n_blocks = -(-(t_all * TOP_K + N_EXPERTS * (MOE_BLOCK - 1)) // MOE_BLOCK)
    n_blocks_pad = -(-n_blocks // SUBLANES) * SUBLANES
    spos, runs, meta = _plan(counts, idx, tile, n_blocks_pad)
    spos_flat = spos.reshape(t_all * TOP_K)
    runs_flat = runs.reshape(-1)
    buf = _dispatch(meta[:n_blocks, 1], spos_flat, runs_flat, xm, tile, n_blocks)
    out_rows = _experts(meta, buf, w_gate_up[0], b_gate_up[0][:, None, :], w_down[0],
                        b_down[0][:, None, :], n_blocks)
    y_prompt, y_sample = _combine(spos_flat, runs_flat, gates, x1, out_rows, norm_final[None], batch,
                                  lp, ls)

    def states(p):
        return (hr[p].reshape(1, batch, S5_GROUPS, S5_STATE), hi[p].reshape(1, batch, S5_GROUPS, S5_STATE),
                c[p].reshape(1, batch, ML_HEADS, ML_HEAD_DIM, ML_HEAD_DIM),
                n[p].reshape(1, batch, ML_HEADS, ML_HEAD_DIM), m[p, :, 0].reshape(1, batch, ML_HEADS))

    return (y_prompt, y_sample) + states(0) + states(1)
```

```python
import functools

import jax
import jax.numpy as jnp
from jax import lax
from jax.experimental import pallas as pl
from jax.experimental.pallas import tpu as pltpu

F32 = jnp.float32
BF16 = jnp.bfloat16

D_MODEL = 1024
CHUNK = 64
EPS = 1e-6
D_S5 = 512
S5_GROUP = 16
S5_GROUPS = D_S5 // S5_GROUP
S5_STATE = 64
S5_CH = S5_GROUPS * S5_STATE
ML_HEADS = 4
ML_HEAD_DIM = 128
D_ML = ML_HEADS * ML_HEAD_DIM
N_EXPERTS = 32
TOP_K = 4
D_FF = 1024
SWIGLU_LIMIT = 7.0
SWIGLU_ALPHA = 1.702

LANES = 128
SUBLANES = 8
TILE_ROWS = D_MODEL // LANES

OFF_U = 0
OFF_Q = OFF_U + D_S5
OFF_K = OFF_Q + D_ML
OFF_V = OFF_K + D_ML
OFF_O = OFF_V + D_ML
OFF_GS5 = OFF_O + D_ML
OFF_GML = OFF_GS5 + D_MODEL
OFF_GATE = OFF_GML + D_MODEL

S5_HALF_IN = D_S5 // 2
S5_HALF_CH = S5_CH // 2
S5_HALF_TILES = S5_HALF_CH // LANES
S5_SCAN_TILES = 4
ML_GROUP = 2
MOE_BLOCK = 512
EXPERT_QUARTERS = 4
VMEM_LIMIT = 60 * 1024 * 1024


def _rmsnorm(x, w):
    return x * lax.rsqrt(jnp.mean(x * x, axis=-1, keepdims=True) + EPS) * w


def _const_spec(shape):
    nd = len(shape)
    return pl.BlockSpec(shape, lambda *_: (0,) * nd, pipeline_mode=pl.Buffered(1))


def _mixer_kernel(xp_ref, xs_ref, h0r_ref, h0i_ref, cx0_ref, m0_ref,
                  nmix_ref, wina_ref, winb_ref, wing_ref, gbias_ref, abr_ref, abi_ref, bdr_ref, bdi_ref, cb_ref,
                  dskip_ref, glu_ref, gn_ref, wml_ref, wout_ref, nmoe_ref, wr_ref, br_ref,
                  x1_ref, xm_ref, idx_ref, gate_ref, cnt_ref, hr_ref, hi_ref, cx_ref, m_ref,
                  s_bur, s_bui, s_tm, s_q, s_k, s_v, s_o, s_col, s_gated, s_cx, sem, *, batch, n_prompt):
    i = pl.program_id(0)
    rows = batch * CHUNK

    @pl.when(i == 0)
    def _():
        hr_ref[...] = jnp.zeros_like(hr_ref)
        hi_ref[...] = jnp.zeros_like(hi_ref)
        m_ref[...] = jnp.zeros_like(m_ref)
        s_cx[...] = jnp.zeros_like(s_cx)
        cnt_ref[...] = jnp.zeros_like(cnt_ref)

    @pl.when(i == n_prompt)
    def _():
        hr_ref[0] = h0r_ref[...]
        hi_ref[0] = h0i_ref[...]
        m_ref[0] = m0_ref[...]
        load = pltpu.make_async_copy(cx0_ref, s_cx, sem)
        load.start()
        load.wait()

    @pl.when(i < n_prompt)
    def _():
        x1_ref[...] = xp_ref[...].reshape(rows, D_MODEL)

    @pl.when(i >= n_prompt)
    def _():
        x1_ref[...] = xs_ref[...].reshape(rows, D_MODEL)

    xn = _rmsnorm(x1_ref[...], nmix_ref[...]).astype(BF16)

    def proj(off, width):
        if off >= OFF_GATE:
            w_cols = wing_ref[...]
        elif off >= OFF_GS5:
            w_cols = winb_ref[:, off - OFF_GS5:off - OFF_GS5 + width]
        else:
            w_cols = wina_ref[:, off:off + width]
        return jnp.dot(xn, w_cols, preferred_element_type=F32)

    gates = proj(OFF_GATE, LANES) + gbias_ref[...]
    lane_g = lax.broadcasted_iota(jnp.int32, (rows, LANES), 1)
    gg = jnp.where(lane_g < ML_HEADS, gates, jax.nn.log_sigmoid(gates))
    gt8 = gg.T[0:SUBLANES, :]
    pos = lax.broadcasted_iota(jnp.int32, (SUBLANES, rows), 1) % CHUNK
    cum = gt8
    shift = 1
    while shift < CHUNK:
        cum = cum + jnp.where(pos >= shift, pltpu.roll(cum, shift, axis=1), 0.0)
        shift *= 2
    g8 = gt8 - pltpu.roll(cum, ML_HEADS, axis=0)
    mx8 = g8
    shift = 1
    while shift < CHUNK:
        mx8 = jnp.maximum(mx8, jnp.where(pos >= shift, pltpu.roll(mx8, shift, axis=1), -jnp.inf))
        shift *= 2
    sub = lax.broadcasted_iota(jnp.int32, (SUBLANES, rows), 0)
    top8 = jnp.where(sub < ML_HEADS, g8, cum)
    s_col[...] = jnp.concatenate(
        [top8, mx8, jnp.zeros((LANES - 2 * SUBLANES, rows), F32)], axis=0).T
    s_q[...] = (proj(OFF_Q, D_ML) * (ML_HEAD_DIM ** -0.5)).astype(BF16)
    s_k[...] = proj(OFF_K, D_ML).astype(BF16)
    s_v[...] = proj(OFF_V, D_ML).astype(BF16)
    s_o[...] = proj(OFF_O, D_ML)

    tri = (lax.broadcasted_iota(jnp.int32, (CHUNK, CHUNK), 0)
           >= lax.broadcasted_iota(jnp.int32, (CHUNK, CHUNK), 1))[None]
    bdims = ((0,), (0,))

    def rowsl(b):
        return slice(b * CHUNK, (b + 1) * CHUNK)

    def headsl(h):
        return slice(h * ML_HEAD_DIM, (h + 1) * ML_HEAD_DIM)

    for g0 in range(0, batch, ML_GROUP):
        items = [(b, h) for b in range(g0, g0 + ML_GROUP) for h in range(ML_HEADS)]
        def col(j, items=items):
            return jnp.stack([jnp.broadcast_to(s_col[rowsl(b), j + h:j + h + 1], (CHUNK, LANES))
                              for b, h in items])

        g_c, b_c, mx_c = col(0), col(ML_HEADS), col(2 * ML_HEADS)
        g_r = jnp.stack([g8[h:h + 1, rowsl(b)] for b, h in items])
        m_prev = jnp.stack([m_ref[0, b * ML_HEADS + h:b * ML_HEADS + h + 1, :] for b, h in items])
        q3 = jnp.stack([s_q[rowsl(b), headsl(h)] for b, h in items])
        k3 = jnp.stack([s_k[rowsl(b), headsl(h)] for b, h in items])
        v3 = jnp.stack([s_v[rowsl(b), headsl(h)] for b, h in items])
        cx = jnp.stack([s_cx[b * ML_HEADS + h] for b, h in items])

        big_m = jnp.maximum(m_prev, mx_c)
        p = jnp.exp(jnp.where(tri, g_r - big_m[:, :, :CHUNK], -jnp.inf))
        w_inter = jnp.exp(m_prev - big_m)
        s = lax.dot_general(q3, k3, (((2,), (2,)), bdims), preferred_element_type=F32) * p
        cqx = lax.dot_general(q3, cx.astype(BF16), (((2,), (1,)), bdims), preferred_element_type=F32)
        num = (lax.dot_general(s.astype(BF16), v3, (((2,), (1,)), bdims), preferred_element_type=F32)
               + w_inter * cqx[:, :, :ML_HEAD_DIM])
        den_dot = jnp.sum(s, axis=2, keepdims=True) + w_inter * cqx[:, :, ML_HEAD_DIM:]
        m_t = b_c + big_m
        hout = num / jnp.maximum(jnp.abs(den_dot), jnp.exp(-m_t))

        m_last = big_m[:, CHUNK - 1:CHUNK, :]
        w_end = jnp.exp(g_c - m_last)
        decay = jnp.exp(m_prev - m_last)
        wvx = jnp.concatenate([w_end * v3.astype(F32), w_end], axis=2).astype(BF16)
        k_t = jnp.stack([s_k[rowsl(b), headsl(h)].astype(F32).T.astype(BF16) for b, h in items])
        cx_new = (jnp.concatenate([decay, decay], axis=2) * cx
                  + lax.dot_general(k_t, wvx, (((2,), (1,)), bdims), preferred_element_type=F32))
        m_new = m_t[:, CHUNK - 1:CHUNK, :]

        hc = hout - jnp.mean(hout, axis=2, keepdims=True)
        hn = hc * lax.rsqrt(jnp.mean(hc * hc, axis=2, keepdims=True) + EPS)
        for n, (b, h) in enumerate(items):
            bh = b * ML_HEADS + h
            s_cx[bh] = cx_new[n]
            m_ref[0, bh:bh + 1, :] = m_new[n]
            s_gated[rowsl(b), headsl(h)] = (jax.nn.sigmoid(s_o[rowsl(b), headsl(h)])
                                            * (hn[n] * gn_ref[:, headsl(h)])).astype(BF16)

    u = proj(OFF_U, D_S5)
    in_tiles = D_S5 // LANES
    for j in range(in_tiles):
        for b in range(batch):
            s_tm[j, pl.ds(b, CHUNK, stride=batch), :] = u[b * CHUNK:(b + 1) * CHUNK,
                                                          j * LANES:(j + 1) * LANES]
    ub = jnp.concatenate([s_tm[j] for j in range(in_tiles)], axis=1).astype(BF16)
    ys = []
    merge = []
    for k in range(2):
        uk = ub[:, k * S5_HALF_IN:(k + 1) * S5_HALF_IN]
        bur = jnp.dot(uk, bdr_ref[k], preferred_element_type=F32)
        bui = jnp.dot(uk, bdi_ref[k], preferred_element_type=F32)
        for j in range(S5_HALF_TILES):
            s_bur[j] = bur[:, j * LANES:(j + 1) * LANES]
            s_bui[j] = bui[:, j * LANES:(j + 1) * LANES]

        if k == 0:
            merge.append(jax.nn.sigmoid(proj(OFF_GML, D_MODEL)) * jnp.dot(
                s_gated[...], wml_ref[...], preferred_element_type=F32))
        else:
            merge.append(jax.nn.sigmoid(proj(OFF_GS5, D_MODEL)))

        for c in range(S5_HALF_TILES // S5_SCAN_TILES):
            tiles = range(c * S5_SCAN_TILES, (c + 1) * S5_SCAN_TILES)
            lanes = [slice(k * S5_HALF_CH + j * LANES, k * S5_HALF_CH + (j + 1) * LANES) for j in tiles]
            ar = [jnp.broadcast_to(abr_ref[:, ls], (batch, LANES)) for ls in lanes]
            ai = [jnp.broadcast_to(abi_ref[:, ls], (batch, LANES)) for ls in lanes]

            def step(t, carry, tiles=tiles, ar=ar, ai=ai):
                slab = pl.ds(pl.multiple_of(t * batch, batch), batch)
                out = []
                for n, j in enumerate(tiles):
                    hr, hi = carry[n]
                    nr = ar[n] * hr - ai[n] * hi + s_bur[j, slab, :]
                    ni = ar[n] * hi + ai[n] * hr + s_bui[j, slab, :]
                    s_bur[j, slab, :] = nr
                    s_bui[j, slab, :] = ni
                    out.append((nr, ni))
                return tuple(out)

            init = tuple((hr_ref[0, :, ls], hi_ref[0, :, ls]) for ls in lanes)
            fin = lax.fori_loop(0, CHUNK, step, init, unroll=True)
            for n, ls in enumerate(lanes):
                hr_ref[0, :, ls] = fin[n][0]
                hi_ref[0, :, ls] = fin[n][1]

        def states(ref):
            return jnp.concatenate([ref[j] for j in range(S5_HALF_TILES)], axis=1).astype(BF16)

        yk = jnp.dot(states(s_bur), cb_ref[k, :S5_HALF_CH, :], preferred_element_type=F32)
        yk = yk + jnp.dot(states(s_bui), cb_ref[k, S5_HALF_CH:, :], preferred_element_type=F32)
        ys.append(yk)
    y_tm = jnp.concatenate(ys, axis=1)
    for j in range(in_tiles):
        s_tm[j] = y_tm[:, j * LANES:(j + 1) * LANES]
    y = jnp.concatenate(
        [jnp.concatenate([s_tm[j, pl.ds(b, CHUNK, stride=batch), :] for j in range(in_tiles)], axis=1)
         for b in range(batch)], axis=0) + dskip_ref[...] * u
    glu = jnp.dot(jax.nn.gelu(y).astype(BF16), glu_ref[...], preferred_element_type=F32)
    y_s5 = glu[:, :D_MODEL] * jax.nn.sigmoid(glu[:, D_MODEL:])
    mix = merge[0] + merge[1] * y_s5

    x1 = x1_ref[...] + jnp.dot(mix.astype(BF16), wout_ref[...], preferred_element_type=F32)
    x1_ref[...] = x1

    xm = _rmsnorm(x1, nmoe_ref[...])
    logits = jnp.dot(xm.astype(BF16), wr_ref[...], preferred_element_type=F32) + br_ref[...]
    lane = lax.broadcasted_iota(jnp.int32, (rows, LANES), 1)
    lane_f = lane.astype(F32)
    logits = jnp.where(lane < N_EXPERTS, logits, -jnp.inf)
    vals, idxs = [], []
    for _ in range(TOP_K):
        mx = jnp.max(logits, axis=1, keepdims=True)
        am = jnp.min(jnp.where(logits == mx, lane_f, float(LANES)), axis=1, keepdims=True)
        vals.append(mx)
        idxs.append(am)
        logits = jnp.where(lane_f == am, -jnp.inf, logits)
    exps = [jnp.exp(v - vals[0]) for v in vals]
    esum = exps[0] + exps[1] + exps[2] + exps[3]
    idx_w = jnp.zeros((rows, LANES), F32)
    gate_w = jnp.zeros((rows, LANES), F32)
    for k in range(TOP_K):
        idx_w = jnp.where(lane == k, idxs[k], idx_w)
        gate_w = jnp.where(lane == k, exps[k] / esum, gate_w)
    idx_ref[...] = idx_w[:, :TOP_K].astype(jnp.int32)
    gate_ref[...] = gate_w[:, :TOP_K]
    chosen = jnp.zeros((rows, LANES), F32)
    for k in range(TOP_K):
        chosen = chosen + (lane_f == idxs[k]).astype(F32)
    cnt_ref[...] += jnp.sum(chosen, axis=0, keepdims=True)

    for s in range(TILE_ROWS):
        xm_ref[pl.ds(s, rows, stride=TILE_ROWS), :] = xm[:, s * LANES:(s + 1) * LANES]

    for phase_id, final_step in ((0, n_prompt - 1), (1, pl.num_programs(0) - 1)):
        @pl.when(i == final_step)
        def _(phase_id=phase_id):
            store = pltpu.make_async_copy(s_cx, cx_ref.at[phase_id], sem)
            store.start()
            store.wait()


def _mixer(x_prompt, x_sample, h0r, h0i, cx0, m0, w):
    batch, lp, _ = x_prompt.shape
    assert x_sample.shape[0] == batch and lp % CHUNK == 0 and x_sample.shape[1] % CHUNK == 0
    assert batch % ML_GROUP == 0
    n_prompt = lp // CHUNK
    n_sample = x_sample.shape[1] // CHUNK
    nblk = n_prompt + n_sample
    rows = batch * CHUNK
    t_all = nblk * rows
    bh = batch * ML_HEADS

    def phase(i):
        return jnp.minimum(i // n_prompt, 1)

    state_specs = [
        _const_spec((batch, S5_CH)), _const_spec((batch, S5_CH)),
        pl.BlockSpec(memory_space=pl.ANY),
        _const_spec((bh, LANES)),
    ]
    weights = (w['norm_mix'], w['w_in_a'], w['w_in_b'], w['w_in_g'], w['gate_bias'], w['abar_re'], w['abar_im'], w['bd_re'],
               w['bd_im'], w['cb'], w['d_skip'], w['w_glu'], w['ml_gn'], w['w_ml_out'], w['w_out'],
               w['norm_moe'], w['w_router'], w['b_router'])
    in_specs = ([pl.BlockSpec((batch, CHUNK, D_MODEL), lambda i: (0, jnp.minimum(i, n_prompt - 1), 0)),
                 pl.BlockSpec((batch, CHUNK, D_MODEL), lambda i: (0, jnp.maximum(i - n_prompt, 0), 0))]
                + state_specs + [_const_spec(a.shape) for a in weights])
    out_shape = (
        jax.ShapeDtypeStruct((t_all, D_MODEL), F32),
        jax.ShapeDtypeStruct((t_all * TILE_ROWS, LANES), F32),
        jax.ShapeDtypeStruct((t_all, TOP_K), jnp.int32),
        jax.ShapeDtypeStruct((t_all, TOP_K), F32),
        jax.ShapeDtypeStruct((1, LANES), F32),
        jax.ShapeDtypeStruct((2, batch, S5_CH), F32),
        jax.ShapeDtypeStruct((2, batch, S5_CH), F32),
        jax.ShapeDtypeStruct((2, bh, ML_HEAD_DIM, 2 * ML_HEAD_DIM), F32),
        jax.ShapeDtypeStruct((2, bh, LANES), F32),
    )
    out_specs = (
        pl.BlockSpec((rows, D_MODEL), lambda i: (i, 0)),
        pl.BlockSpec((rows * TILE_ROWS, LANES), lambda i: (i, 0)),
        pl.BlockSpec((rows, TOP_K), lambda i: (i, 0)),
        pl.BlockSpec((rows, TOP_K), lambda i: (i, 0)),
        pl.BlockSpec((1, LANES), lambda i: (0, 0)),
        pl.BlockSpec((1, batch, S5_CH), lambda i: (phase(i), 0, 0)),
        pl.BlockSpec((1, batch, S5_CH), lambda i: (phase(i), 0, 0)),
        pl.BlockSpec(memory_space=pl.ANY),
        pl.BlockSpec((1, bh, LANES), lambda i: (phase(i), 0, 0)),
    )
    scratch = [
        pltpu.VMEM((S5_HALF_TILES, rows, LANES), F32), pltpu.VMEM((S5_HALF_TILES, rows, LANES), F32),
        pltpu.VMEM((D_S5 // LANES, rows, LANES), F32),
        pltpu.VMEM((rows, D_ML), BF16), pltpu.VMEM((rows, D_ML), BF16), pltpu.VMEM((rows, D_ML), BF16),
        pltpu.VMEM((rows, D_ML), F32),
        pltpu.VMEM((rows, LANES), F32),
        pltpu.VMEM((rows, D_ML), BF16),
        pltpu.VMEM((bh, ML_HEAD_DIM, 2 * ML_HEAD_DIM), F32),
        pltpu.SemaphoreType.DMA(()),
    ]
    return pl.pallas_call(
        functools.partial(_mixer_kernel, batch=batch, n_prompt=n_prompt),
        grid=(nblk,),
        in_specs=in_specs,
        out_specs=out_specs,
        out_shape=out_shape,
        scratch_shapes=scratch,
        compiler_params=pltpu.CompilerParams(dimension_semantics=("arbitrary",),
                                             vmem_limit_bytes=VMEM_LIMIT),
        name="mixer",
    )(x_prompt, x_sample, h0r, h0i, cx0, m0, *weights)


def _plan_kernel(cnt_ref, idx_ref, spos_ref, runs_ref, meta_ref, s_run, s_pstart, s_lower, s_upper, *,
                 tile, n_blocks_pad):
    i = pl.program_id(0)
    lane = lax.broadcasted_iota(jnp.int32, (tile, LANES), 1)
    idx = idx_ref[...]
    onehots = [(lane == idx[:, k:k + 1]).astype(F32) for k in range(TOP_K)]
    mask = onehots[0] + onehots[1] + onehots[2] + onehots[3]

    @pl.when(i == 0)
    def _():
        cnt = cnt_ref[...]
        nblk = jnp.floor((cnt + (MOE_BLOCK - 1)) * (1.0 / MOE_BLOCK))
        lane1 = lax.broadcasted_iota(jnp.int32, (1, LANES), 1)
        pend = nblk
        shift = 1
        while shift < LANES:
            pend = pend + jnp.where(lane1 >= shift, pltpu.roll(pend, shift, axis=1), 0.0)
            shift *= 2
        s_pstart[...] = (pend - nblk) * MOE_BLOCK
        s_run[...] = jnp.zeros_like(s_run)
        r = lax.broadcasted_iota(jnp.int32, (tile, tile), 0)
        c = lax.broadcasted_iota(jnp.int32, (tile, tile), 1)
        s_lower[...] = (c < r).astype(BF16)
        s_upper[...] = (lax.broadcasted_iota(jnp.int32, (LANES, LANES), 0)
                        < lax.broadcasted_iota(jnp.int32, (LANES, LANES), 1)).astype(BF16)
        blk = lax.broadcasted_iota(jnp.int32, (n_blocks_pad, LANES), 0).astype(F32)
        lane_b = lax.broadcasted_iota(jnp.int32, (n_blocks_pad, LANES), 1)
        lane_f = lane_b.astype(F32)
        is_e = lane_b < N_EXPERTS
        done = jnp.logical_and(is_e, pend <= blk)
        e_of = jnp.minimum(jnp.sum(done.astype(F32), axis=1, keepdims=True), N_EXPERTS - 1.0)
        mine = lane_f == e_of
        blk_in_e = jnp.sum(jnp.where(mine, blk - (pend - nblk), 0.0), axis=1, keepdims=True)
        cnt_e = jnp.sum(jnp.where(mine, cnt, 0.0), axis=1, keepdims=True)
        valid = jnp.clip(cnt_e - blk_in_e * MOE_BLOCK, 0.0, float(MOE_BLOCK))
        used = jnp.sum(jnp.where(lane1 == N_EXPERTS - 1, pend, 0.0), axis=1, keepdims=True)
        owns = jnp.logical_and(is_e, nblk > 0.0)
        first = jnp.logical_and(blk_in_e == 0.0, blk[:, :1] < used).astype(F32)
        later = jnp.logical_and(owns, lane_f > e_of)
        nxt = jnp.min(jnp.where(later, lane_f, float(LANES)), axis=1, keepdims=True)
        nxt = jnp.where(nxt < float(LANES), nxt, -1.0)
        run = jnp.sum(jnp.logical_and(owns, lane_f < e_of).astype(F32), axis=1, keepdims=True)
        parity = run - 2.0 * jnp.floor(run * 0.5)
        meta = jnp.zeros((n_blocks_pad, LANES), F32)
        for col, val in enumerate((e_of, valid, used, first, nxt, parity)):
            meta = jnp.where(lane_b == col, val, meta)
        meta_ref[...] = meta.astype(jnp.int32)

    before = jnp.dot(s_lower[...], mask.astype(BF16), preferred_element_type=F32)
    tile_cnt = jnp.sum(mask, axis=0, keepdims=True)
    tile_start = jnp.sum(jnp.dot(mask.astype(BF16), s_upper[...], preferred_element_type=F32),
                         axis=0, keepdims=True)
    parity = (i % 2).astype(F32)
    base = (before + tile_start + parity * (tile * TOP_K)) * TILE_ROWS
    spos = jnp.zeros((tile, LANES), F32)
    for k in range(TOP_K):
        pk = jnp.sum(onehots[k] * base, axis=1, keepdims=True)
        spos = jnp.where(lane == k, pk, spos)
    spos_ref[...] = spos[:, :TOP_K].astype(jnp.int32)
    sub = lax.broadcasted_iota(jnp.int32, (SUBLANES, LANES), 0)
    runs = jnp.where(sub == 0, tile_cnt,
                     jnp.where(sub == 1, s_pstart[...] + s_run[...],
                               jnp.where(sub == 2, tile_start, 0.0)))
    runs_ref[0] = runs.astype(jnp.int32)
    s_run[...] += tile_cnt


def _plan(counts, idx_all, tile, n_blocks_pad):
    t_all = idx_all.shape[0]
    return pl.pallas_call(
        functools.partial(_plan_kernel, tile=tile, n_blocks_pad=n_blocks_pad),
        grid=(t_all // tile,),
        in_specs=[pl.BlockSpec((1, LANES), lambda i: (0, 0)),
                  pl.BlockSpec((tile, TOP_K), lambda i: (i, 0))],
        out_specs=(pl.BlockSpec((tile, TOP_K), lambda i: (i, 0)),
                   pl.BlockSpec((1, SUBLANES, LANES), lambda i: (i, 0, 0)),
                   pl.BlockSpec((n_blocks_pad, LANES), lambda i: (0, 0))),
        out_shape=(jax.ShapeDtypeStruct((t_all, TOP_K), jnp.int32),
                   jax.ShapeDtypeStruct((t_all // tile, SUBLANES, LANES), jnp.int32),
                   jax.ShapeDtypeStruct((n_blocks_pad, LANES), jnp.int32)),
        scratch_shapes=[pltpu.VMEM((1, LANES), F32), pltpu.VMEM((1, LANES), F32),
                        pltpu.VMEM((tile, tile), BF16), pltpu.VMEM((LANES, LANES), BF16)],
        compiler_params=pltpu.CompilerParams(dimension_semantics=("arbitrary",)),
        name="moe_plan",
    )(counts, idx_all)


RUN_BITS = 10
RUN_SPLIT = 7


def _run_copies(runs_ref, sorted_hbm, stage, sem, to_hbm):
    def expert(e, carry):
        n = runs_ref[e]
        first_sorted = runs_ref[LANES + e]
        first_staged = runs_ref[2 * LANES + e]
        def piece(bit):
            size = (1 << bit) * TILE_ROWS
            done = lax.shift_left(lax.shift_right_logical(n, bit + 1), bit + 1)

            @pl.when(lax.bitwise_and(lax.shift_right_logical(n, bit), 1) == 1)
            def _():
                hbm = sorted_hbm.at[pl.ds(pl.multiple_of((first_sorted + done) * TILE_ROWS, TILE_ROWS), size)]
                vmem = stage.at[pl.ds(pl.multiple_of((first_staged + done) * TILE_ROWS, TILE_ROWS), size)]
                src, dst = (vmem, hbm) if to_hbm else (hbm, vmem)
                pltpu.make_async_copy(src, dst, sem).start(priority=bit % 2)

        @pl.when(n >= (1 << RUN_SPLIT))
        def _():
            for bit in reversed(range(RUN_SPLIT, RUN_BITS)):
                piece(bit)

        for bit in reversed(range(RUN_SPLIT)):
            piece(bit)
        return carry

    lax.fori_loop(0, N_EXPERTS, expert, 0)


def _dispatch_kernel(valid_ref, spos_ref, runs_ref, xm_ref, buf_ref, zeros, stage, sem, zsem, *, tile,
                     n_blocks, n_tiles):
    i = pl.program_id(0)
    slot = i % 2
    n_rows = tile * TOP_K * TILE_ROWS

    def drain(s):
        pltpu.make_async_copy(buf_ref.at[pl.ds(0, n_rows)], buf_ref.at[pl.ds(0, n_rows)], sem.at[s]).wait()

    pad_rows = (n_blocks * MOE_BLOCK - n_tiles * tile * TOP_K) * TILE_ROWS

    @pl.when(i == 0)
    def _():
        zeros[...] = jnp.zeros_like(zeros)

        def fill(b, carry):
            filled = valid_ref[b]
            n = MOE_BLOCK - filled
            for bit in reversed(range(RUN_BITS)):
                size = (1 << bit) * TILE_ROWS
                done = lax.shift_left(lax.shift_right_logical(n, bit + 1), bit + 1)

                @pl.when(lax.bitwise_and(lax.shift_right_logical(n, bit), 1) == 1)
                def _(size=size, done=done, bit=bit):
                    first = (b * MOE_BLOCK + filled + done) * TILE_ROWS
                    pltpu.make_async_copy(
                        zeros.at[pl.ds(0, size)],
                        buf_ref.at[pl.ds(pl.multiple_of(first, TILE_ROWS), size)], zsem).start(
                            priority=bit % 2)
            return carry

        lax.fori_loop(0, n_blocks, fill, 0)

    @pl.when(i >= 2)
    def _():
        drain(slot)

    def place(j, carry):
        tok = xm_ref[pl.ds(pl.multiple_of(j * TILE_ROWS, TILE_ROWS), TILE_ROWS), :]
        for k in range(TOP_K):
            stage[pl.ds(pl.multiple_of(spos_ref[j * TOP_K + k], TILE_ROWS), TILE_ROWS), :] = tok
        return carry

    lax.fori_loop(0, tile, place, 0, unroll=16)
    slot_rows = stage.at[pl.ds(pl.multiple_of(slot * n_rows, n_rows), n_rows)]
    _run_copies(runs_ref, buf_ref, slot_rows, sem.at[slot], to_hbm=True)

    last = pl.num_programs(0) - 1

    @pl.when(i == last)
    def _():
        drain(slot)
        if pad_rows:
            pltpu.make_async_copy(buf_ref.at[pl.ds(0, pad_rows)], buf_ref.at[pl.ds(0, pad_rows)],
                                  zsem).wait()

    @pl.when(jnp.logical_and(i == last, i >= 1))
    def _():
        drain(1 - slot)


def _dispatch(block_valid, spos_flat, runs_flat, xm_tiles, tile, n_blocks):
    t_all = xm_tiles.shape[0] // TILE_ROWS
    grid_spec = pltpu.PrefetchScalarGridSpec(
        num_scalar_prefetch=1,
        grid=(t_all // tile,),
        in_specs=[pl.BlockSpec((tile * TOP_K,), lambda i, va: (i,), memory_space=pltpu.SMEM),
                  pl.BlockSpec((SUBLANES * LANES,), lambda i, va: (i,), memory_space=pltpu.SMEM),
                  pl.BlockSpec((tile * TILE_ROWS, LANES), lambda i, va: (i, 0))],
        out_specs=pl.BlockSpec(memory_space=pl.ANY),
        scratch_shapes=[pltpu.VMEM((MOE_BLOCK * TILE_ROWS, LANES), F32),
                        pltpu.VMEM((2 * tile * TOP_K * TILE_ROWS, LANES), F32),
                        pltpu.SemaphoreType.DMA((2,)), pltpu.SemaphoreType.DMA(())],
    )
    return pl.pallas_call(
        functools.partial(_dispatch_kernel, tile=tile, n_blocks=n_blocks, n_tiles=t_all // tile),
        grid_spec=grid_spec,
        out_shape=jax.ShapeDtypeStruct((n_blocks * MOE_BLOCK * TILE_ROWS, LANES), F32),
        compiler_params=pltpu.CompilerParams(dimension_semantics=("arbitrary",),
                                             vmem_limit_bytes=VMEM_LIMIT),
        name="moe_dispatch",
    )(block_valid, spos_flat, runs_flat, xm_tiles)


def _expert_kernel(be_ref, used_ref, first_ref, next_ref, slot_ref, valid_ref, x_ref, wgu_ref, bgu_ref,
                   wd_ref, bd_ref, o_ref, wbuf_gu, wbuf_d, s_wgu, s_wd, sem):
    i = pl.program_id(0)

    @pl.when(i >= used_ref[0])
    def _():
        o_ref[...] = jnp.zeros_like(o_ref)

    def weight_copies(e, slot):
        return (pltpu.make_async_copy(wgu_ref.at[e], wbuf_gu.at[slot], sem.at[0, slot]),
                pltpu.make_async_copy(wd_ref.at[e], wbuf_d.at[slot], sem.at[1, slot]))

    @pl.when(i == 0)
    def _():
        for cp in weight_copies(be_ref[0], slot_ref[0]):
            cp.start()

    @pl.when(first_ref[i] == 1)
    def _():
        slot = slot_ref[i]
        for cp in weight_copies(be_ref[i], slot):
            cp.wait()

        @pl.when(next_ref[i] >= 0)
        def _():
            for cp in weight_copies(next_ref[i], 1 - slot):
                cp.start()

        s_wgu[...] = wbuf_gu[slot].astype(BF16)
        s_wd[...] = wbuf_d[slot].astype(BF16)

    def ffn(nrows):
        pieces = [x_ref[pl.ds(s, nrows, stride=TILE_ROWS), :] for s in range(TILE_ROWS)]
        x = jnp.concatenate(pieces, axis=1).astype(BF16)
        gu = jnp.dot(x, s_wgu[...], preferred_element_type=F32) + bgu_ref[0]
        g = jnp.minimum(gu[:, :D_FF], SWIGLU_LIMIT)
        up = jnp.clip(gu[:, D_FF:], -SWIGLU_LIMIT, SWIGLU_LIMIT)
        hdn = (up + 1.0) * (g * jax.nn.sigmoid(SWIGLU_ALPHA * g))
        out = jnp.dot(hdn.astype(BF16), s_wd[...], preferred_element_type=F32) + bd_ref[0]
        for s in range(TILE_ROWS):
            o_ref[pl.ds(s, nrows, stride=TILE_ROWS), :] = out[:, s * LANES:(s + 1) * LANES]
        if nrows < MOE_BLOCK:
            o_ref[nrows * TILE_ROWS:, :] = jnp.zeros(((MOE_BLOCK - nrows) * TILE_ROWS, LANES), F32)

    in_use = i < used_ref[0]
    quarter = MOE_BLOCK // EXPERT_QUARTERS
    assert quarter & (quarter - 1) == 0
    quarters = lax.shift_right_logical(valid_ref[i] + (quarter - 1), quarter.bit_length() - 1)
    for nq in range(1, EXPERT_QUARTERS + 1):
        cond = (quarters <= 1) if nq == 1 else (quarters == nq)

        @pl.when(jnp.logical_and(in_use, cond))
        def _(nq=nq):
            ffn(nq * quarter)


def _experts(meta, buf, wgu, bgu, wd, bd, n_blocks):
    blk_rows = MOE_BLOCK * TILE_ROWS
    block_e, n_used = meta[:n_blocks, 0], meta[:1, 2]
    first, nxt, slot = meta[:n_blocks, 3], meta[:n_blocks, 4], meta[:n_blocks, 5]
    valid = meta[:n_blocks, 1]

    def in_row_map(i, be, nu, *_):
        return (jnp.minimum(i, nu[0] - 1), 0)

    def b_map(i, be, *_):
        return (be[i], 0, 0)

    grid_spec = pltpu.PrefetchScalarGridSpec(
        num_scalar_prefetch=6,
        grid=(n_blocks,),
        in_specs=[pl.BlockSpec((blk_rows, LANES), in_row_map),
                  pl.BlockSpec(memory_space=pl.ANY),
                  pl.BlockSpec((1, 1, 2 * D_FF), b_map),
                  pl.BlockSpec(memory_space=pl.ANY),
                  pl.BlockSpec((1, 1, D_MODEL), b_map)],
        out_specs=pl.BlockSpec((blk_rows, LANES), lambda i, *_: (i, 0)),
        scratch_shapes=[pltpu.VMEM((2, D_MODEL, 2 * D_FF), F32), pltpu.VMEM((2, D_FF, D_MODEL), F32),
                        pltpu.VMEM((D_MODEL, 2 * D_FF), BF16), pltpu.VMEM((D_FF, D_MODEL), BF16),
                        pltpu.SemaphoreType.DMA((2, 2))],
    )
    return pl.pallas_call(
        _expert_kernel,
        grid_spec=grid_spec,
        out_shape=jax.ShapeDtypeStruct(buf.shape, F32),
        compiler_params=pltpu.CompilerParams(dimension_semantics=("arbitrary",),
                                             vmem_limit_bytes=VMEM_LIMIT),
        name="moe_experts",
    )(block_e, n_used, first, nxt, slot, valid, buf, wgu, bgu, wd, bd)


def _combine_kernel(spos_ref, runs_ref, next_runs_ref, gate_ref, x1_ref, rows_ref, nf_ref, yp_ref,
                    ys_ref, stage, gbuf, sem, *, batch, n_prompt):
    i = pl.program_id(0)
    tile = batch * CHUNK
    n_rows = tile * TOP_K * TILE_ROWS
    slot = i % 2

    def slot_rows(s):
        return stage.at[pl.ds(pl.multiple_of(s * n_rows, n_rows), n_rows)]

    @pl.when(i == 0)
    def _():
        _run_copies(runs_ref, rows_ref, slot_rows(0), sem.at[0], to_hbm=False)

    @pl.when(i + 1 < pl.num_programs(0))
    def _():
        _run_copies(next_runs_ref, rows_ref, slot_rows(1 - slot), sem.at[1 - slot], to_hbm=False)

    pltpu.make_async_copy(rows_ref.at[pl.ds(0, n_rows)], slot_rows(slot), sem.at[slot]).wait()

    def pick(j, carry):
        for k in range(TOP_K):
            gbuf[k, pl.ds(pl.multiple_of(j * TILE_ROWS, TILE_ROWS), TILE_ROWS), :] = (
                stage[pl.ds(pl.multiple_of(spos_ref[j * TOP_K + k], TILE_ROWS), TILE_ROWS), :])
        return carry

    lax.fori_loop(0, tile, pick, 0, unroll=16)

    gates = gate_ref[...]
    pieces = []
    for s in range(TILE_ROWS):
        piece = x1_ref[:, s * LANES:(s + 1) * LANES]
        for k in range(TOP_K):
            piece = piece + gates[:, k:k + 1] * gbuf[k, pl.ds(s, tile, stride=TILE_ROWS), :]
        pieces.append(piece)
    y = _rmsnorm(jnp.concatenate(pieces, axis=1), nf_ref[...]).reshape(batch, CHUNK, D_MODEL)

    @pl.when(i < n_prompt)
    def _():
        yp_ref[...] = y

    @pl.when(i >= n_prompt)
    def _():
        ys_ref[...] = y


def _combine(spos_flat, runs_flat, gates, x1, out_rows, norm_final, batch, lp, ls):
    tile = batch * CHUNK
    n_prompt = lp // CHUNK
    n_sample = ls // CHUNK
    last = n_prompt + n_sample - 1
    return pl.pallas_call(
        functools.partial(_combine_kernel, batch=batch, n_prompt=n_prompt),
        grid=(n_prompt + n_sample,),
        in_specs=[pl.BlockSpec((tile * TOP_K,), lambda i: (i,), memory_space=pltpu.SMEM),
                  pl.BlockSpec((SUBLANES * LANES,), lambda i: (i,), memory_space=pltpu.SMEM),
                  pl.BlockSpec((SUBLANES * LANES,), lambda i: (jnp.minimum(i + 1, last),),
                               memory_space=pltpu.SMEM),
                  pl.BlockSpec((tile, TOP_K), lambda i: (i, 0)),
                  pl.BlockSpec((tile, D_MODEL), lambda i: (i, 0)),
                  pl.BlockSpec(memory_space=pl.ANY),
                  pl.BlockSpec((1, D_MODEL), lambda i: (0, 0))],
        out_specs=(pl.BlockSpec((batch, CHUNK, D_MODEL), lambda i: (0, jnp.minimum(i, n_prompt - 1), 0)),
                   pl.BlockSpec((batch, CHUNK, D_MODEL), lambda i: (0, jnp.maximum(i - n_prompt, 0), 0))),
        out_shape=(jax.ShapeDtypeStruct((batch, lp, D_MODEL), F32),
                   jax.ShapeDtypeStruct((batch, ls, D_MODEL), F32)),
        scratch_shapes=[pltpu.VMEM((2 * tile * TOP_K * TILE_ROWS, LANES), F32),
                        pltpu.VMEM((TOP_K, tile * TILE_ROWS, LANES), F32),
                        pltpu.SemaphoreType.DMA((2,))],
        compiler_params=pltpu.CompilerParams(dimension_semantics=("arbitrary",),
                                             vmem_limit_bytes=VMEM_LIMIT),
        name="moe_combine",
    )(spos_flat, runs_flat, runs_flat, gates, x1, out_rows, norm_final)


def _prep_weights(norm_mix, w_in, b_ig, b_fg, s5_a_re, s5_a_im, s5_log_dt, s5_b_re, s5_b_im,
                  s5_c_re, s5_c_im, s5_d, w_s5_glu, ml_gn, w_ml_out, w_out, norm_moe, w_router,
                  b_router):
    w = w_in[0]
    o_ig = D_S5 + 4 * D_ML
    o_gs5 = o_ig + 2 * ML_HEADS
    w_a = w[:, :o_ig].astype(BF16)
    w_b = w[:, o_gs5:].astype(BF16)
    w_g = jnp.concatenate([w[:, o_ig:o_gs5], jnp.zeros((D_MODEL, LANES - 2 * ML_HEADS), F32)],
                          axis=1).astype(BF16)
    gate_bias = jnp.concatenate([b_ig[0], b_fg[0], jnp.zeros((LANES - 2 * ML_HEADS,), F32)])[None]

    ar, ai = s5_a_re[0], s5_a_im[0]
    dt = jnp.exp(s5_log_dt[0])[:, None]
    mag = jnp.exp(dt * ar)
    abar_re = mag * jnp.cos(dt * ai)
    abar_im = mag * jnp.sin(dt * ai)
    den = ar * ar + ai * ai
    fr = ((abar_re - 1.0) * ar + abar_im * ai) / den
    fi = (abar_im * ar - (abar_re - 1.0) * ai) / den
    br, bi = s5_b_re[0], s5_b_im[0]
    bbar_re = fr[..., None] * br - fi[..., None] * bi
    bbar_im = fr[..., None] * bi + fi[..., None] * br
    gh = S5_GROUPS // 2
    same_group = (jnp.arange(gh)[:, None, None, None] == jnp.arange(gh)[None, None, :, None])

    def blockdiag_in(bb):
        t = bb.reshape(2, gh, S5_STATE, S5_GROUP).transpose(0, 1, 3, 2)
        full = jnp.where(same_group[None], t[:, :, :, None, :], 0.0)
        return full.reshape(2, gh * S5_GROUP, gh * S5_STATE).astype(BF16)

    def blockdiag_out(cc):
        t = cc.reshape(2, gh, S5_GROUP, S5_STATE).transpose(0, 1, 3, 2)
        full = jnp.where(same_group[None], t[:, :, :, None, :], 0.0)
        return full.reshape(2, gh * S5_STATE, gh * S5_GROUP)

    cb = jnp.concatenate([blockdiag_out(s5_c_re[0]), -blockdiag_out(s5_c_im[0])], axis=1).astype(BF16)
    w_router_p = jnp.concatenate(
        [w_router[0], jnp.zeros((D_MODEL, LANES - N_EXPERTS), F32)], axis=1).astype(BF16)
    b_router_p = jnp.concatenate([b_router[0], jnp.zeros((LANES - N_EXPERTS,), F32)])[None]
    return dict(
        norm_mix=norm_mix[0][None], w_in_a=w_a, w_in_b=w_b, w_in_g=w_g, gate_bias=gate_bias,
        abar_re=abar_re.reshape(1, S5_CH), abar_im=abar_im.reshape(1, S5_CH),
        bd_re=blockdiag_in(bbar_re), bd_im=blockdiag_in(bbar_im), cb=cb,
        d_skip=s5_d[0][None], w_glu=w_s5_glu[0].astype(BF16), ml_gn=ml_gn[0][None],
        w_ml_out=w_ml_out[0].astype(BF16), w_out=w_out[0].astype(BF16),
        norm_moe=norm_moe[0][None], w_router=w_router_p, b_router=b_router_p)


def kernel(x_prompt, x_sample, state_s5_re, state_s5_im, state_ml_C, state_ml_n, state_ml_m, norm_mix, w_in, b_ig, b_fg, s5_a_re, s5_a_im, s5_log_dt, s5_b_re, s5_b_im, s5_c_re, s5_c_im, s5_d, w_s5_glu, ml_gn, w_ml_out, w_out, norm_moe, w_router, b_router, w_gate_up, b_gate_up, w_down, b_down, norm_final):
    w = _prep_weights(norm_mix, w_in, b_ig, b_fg, s5_a_re, s5_a_im, s5_log_dt, s5_b_re, s5_b_im,
                      s5_c_re, s5_c_im, s5_d, w_s5_glu, ml_gn, w_ml_out, w_out, norm_moe,
                      w_router, b_router)
    batch, lp, _ = x_prompt.shape
    ls = x_sample.shape[1]
    bh = batch * ML_HEADS
    cx0 = jnp.concatenate(
        [jnp.swapaxes(state_ml_C[0].reshape(bh, ML_HEAD_DIM, ML_HEAD_DIM), 1, 2),
         jnp.broadcast_to(state_ml_n[0].reshape(bh, ML_HEAD_DIM, 1), (bh, ML_HEAD_DIM, ML_HEAD_DIM))],
        axis=2)
    x1, xm, idx, gates, counts, hr, hi, cx, m = _mixer(
        x_prompt, x_sample,
        state_s5_re[0].reshape(batch, S5_CH), state_s5_im[0].reshape(batch, S5_CH), cx0,
        jnp.broadcast_to(state_ml_m[0].reshape(bh, 1), (bh, LANES)), w)
    c = jnp.swapaxes(cx[:, :, :, :ML_HEAD_DIM], 2, 3)
    n = cx[:, :, :, ML_HEAD_DIM]

    tile = batch * CHUNK
    t_all = idx.shape[0]
    n_blocks = -(-(t_all * TOP_K + N_EXPERTS * (MOE_BLOCK - 1)) // MOE_BLOCK)
    n_blocks_pad = -(-n_blocks // SUBLANES) * SUBLANES
    spos, runs, meta = _plan(counts, idx, tile, n_blocks_pad)
    spos_flat = spos.reshape(t_all * TOP_K)
    runs_flat = runs.reshape(-1)
    buf = _dispatch(meta[:n_blocks, 1], spos_flat, runs_flat, xm, tile, n_blocks)
    out_rows = _experts(meta, buf, w_gate_up[0], b_gate_up[0][:, None, :], w_down[0],
                        b_down[0][:, None, :], n_blocks)
    y_prompt, y_sample = _combine(spos_flat, runs_flat, gates, x1, out_rows, norm_final[None], batch,
                                  lp, ls)

    def states(p):
        return (hr[p].reshape(1, batch, S5_GROUPS, S5_STATE), hi[p].reshape(1, batch, S5_GROUPS, S5_STATE),
                c[p].reshape(1, batch, ML_HEADS, ML_HEAD_DIM, ML_HEAD_DIM),
                n[p].reshape(1, batch, ML_HEADS, ML_HEAD_DIM), m[p, :, 0].reshape(1, batch, ML_HEADS))

    return (y_prompt, y_sample) + states(0) + states(1)
```

```python
import functools

import jax
import jax.numpy as jnp
from jax import lax
from jax.experimental import pallas as pl
from jax.experimental.pallas import tpu as pltpu

F32 = jnp.float32
BF16 = jnp.bfloat16

D_MODEL = 1024
CHUNK = 64
EPS = 1e-6
D_S5 = 512
S5_GROUP = 16
S5_GROUPS = D_S5 // S5_GROUP
S5_STATE = 64
S5_CH = S5_GROUPS * S5_STATE
ML_HEADS = 4
ML_HEAD_DIM = 128
D_ML = ML_HEADS * ML_HEAD_DIM
N_EXPERTS = 32
TOP_K = 4
D_FF = 1024
SWIGLU_LIMIT = 7.0
SWIGLU_ALPHA = 1.702

LANES = 128
SUBLANES = 8
TILE_ROWS = D_MODEL // LANES

OFF_U = 0
OFF_Q = OFF_U + D_S5
OFF_K = OFF_Q + D_ML
OFF_V = OFF_K + D_ML
OFF_O = OFF_V + D_ML
OFF_GS5 = OFF_O + D_ML
OFF_GML = OFF_GS5 + D_MODEL
OFF_GATE = OFF_GML + D_MODEL

S5_HALF_IN = D_S5 // 2
S5_HALF_CH = S5_CH // 2
S5_HALF_TILES = S5_HALF_CH // LANES
S5_SCAN_TILES = 4
ML_GROUP = 2
MOE_BLOCK = 512
EXPERT_QUARTERS = 4
VMEM_LIMIT = 60 * 1024 * 1024


def _rmsnorm(x, w):
    return x * lax.rsqrt(jnp.mean(x * x, axis=-1, keepdims=True) + EPS) * w


def _const_spec(shape):
    nd = len(shape)
    return pl.BlockSpec(shape, lambda *_: (0,) * nd, pipeline_mode=pl.Buffered(1))


def _mixer_kernel(xp_ref, xs_ref, h0r_ref, h0i_ref, cx0_ref, m0_ref,
                  nmix_ref, wina_ref, winb_ref, wing_ref, gbias_ref, abr_ref, abi_ref, bdr_ref, bdi_ref, cb_ref,
                  dskip_ref, glu_ref, gn_ref, wml_ref, wout_ref, nmoe_ref, wr_ref, br_ref,
                  x1_ref, xm_ref, idx_ref, gate_ref, cnt_ref, hr_ref, hi_ref, cx_ref, m_ref,
                  s_bur, s_bui, s_tm, s_q, s_k, s_v, s_o, s_col, s_gated, s_cx, sem, *, batch, n_prompt):
    i = pl.program_id(0)
    rows = batch * CHUNK

    @pl.when(i == 0)
    def _():
        hr_ref[...] = jnp.zeros_like(hr_ref)
        hi_ref[...] = jnp.zeros_like(hi_ref)
        m_ref[...] = jnp.zeros_like(m_ref)
        s_cx[...] = jnp.zeros_like(s_cx)
        cnt_ref[...] = jnp.zeros_like(cnt_ref)

    @pl.when(i == n_prompt)
    def _():
        hr_ref[0] = h0r_ref[...]
        hi_ref[0] = h0i_ref[...]
        m_ref[0] = m0_ref[...]
        load = pltpu.make_async_copy(cx0_ref, s_cx, sem)
        load.start()
        load.wait()

    @pl.when(i < n_prompt)
    def _():
        x1_ref[...] = xp_ref[...].reshape(rows, D_MODEL)

    @pl.when(i >= n_prompt)
    def _():
        x1_ref[...] = xs_ref[...].reshape(rows, D_MODEL)

    xn = _rmsnorm(x1_ref[...], nmix_ref[...]).astype(BF16)

    def proj(off, width):
        if off >= OFF_GATE:
            w_cols = wing_ref[...]
        elif off >= OFF_GS5:
            w_cols = winb_ref[:, off - OFF_GS5:off - OFF_GS5 + width]
        else:
            w_cols = wina_ref[:, off:off + width]
        return jnp.dot(xn, w_cols, preferred_element_type=F32)

    gates = proj(OFF_GATE, LANES) + gbias_ref[...]
    lane_g = lax.broadcasted_iota(jnp.int32, (rows, LANES), 1)
    gg = jnp.where(lane_g < ML_HEADS, gates, jax.nn.log_sigmoid(gates))
    gt8 = gg.T[0:SUBLANES, :]
    pos = lax.broadcasted_iota(jnp.int32, (SUBLANES, rows), 1) % CHUNK
    cum = gt8
    shift = 1
    while shift < CHUNK:
        cum = cum + jnp.where(pos >= shift, pltpu.roll(cum, shift, axis=1), 0.0)
        shift *= 2
    g8 = gt8 - pltpu.roll(cum, ML_HEADS, axis=0)
    mx8 = g8
    shift = 1
    while shift < CHUNK:
        mx8 = jnp.maximum(mx8, jnp.where(pos >= shift, pltpu.roll(mx8, shift, axis=1), -jnp.inf))
        shift *= 2
    sub = lax.broadcasted_iota(jnp.int32, (SUBLANES, rows), 0)
    top8 = jnp.where(sub < ML_HEADS, g8, cum)
    s_col[...] = jnp.concatenate(
        [top8, mx8, jnp.zeros((LANES - 2 * SUBLANES, rows), F32)], axis=0).T
    s_q[...] = (proj(OFF_Q, D_ML) * (ML_HEAD_DIM ** -0.5)).astype(BF16)
    s_k[...] = proj(OFF_K, D_ML).astype(BF16)
    s_v[...] = proj(OFF_V, D_ML).astype(BF16)
    s_o[...] = proj(OFF_O, D_ML)

    tri = (lax.broadcasted_iota(jnp.int32, (CHUNK, CHUNK), 0)
           >= lax.broadcasted_iota(jnp.int32, (CHUNK, CHUNK), 1))[None]
    bdims = ((0,), (0,))

    def rowsl(b):
        return slice(b * CHUNK, (b + 1) * CHUNK)

    def headsl(h):
        return slice(h * ML_HEAD_DIM, (h + 1) * ML_HEAD_DIM)

    for g0 in range(0, batch, ML_GROUP):
        items = [(b, h) for b in range(g0, g0 + ML_GROUP) for h in range(ML_HEADS)]
        def col(j, items=items):
            return jnp.stack([jnp.broadcast_to(s_col[rowsl(b), j + h:j + h + 1], (CHUNK, LANES))
                              for b, h in items])

        g_c, b_c, mx_c = col(0), col(ML_HEADS), col(2 * ML_HEADS)
        g_r = jnp.stack([g8[h:h + 1, rowsl(b)] for b, h in items])
        m_prev = jnp.stack([m_ref[0, b * ML_HEADS + h:b * ML_HEADS + h + 1, :] for b, h in items])
        q3 = jnp.stack([s_q[rowsl(b), headsl(h)] for b, h in items])
        k3 = jnp.stack([s_k[rowsl(b), headsl(h)] for b, h in items])
        v3 = jnp.stack([s_v[rowsl(b), headsl(h)] for b, h in items])
        cx = jnp.stack([s_cx[b * ML_HEADS + h] for b, h in items])

        big_m = jnp.maximum(m_prev, mx_c)
        p = jnp.exp(jnp.where(tri, g_r - big_m[:, :, :CHUNK], -jnp.inf))
        w_inter = jnp.exp(m_prev - big_m)
        s = lax.dot_general(q3, k3, (((2,), (2,)), bdims), preferred_element_type=F32) * p
        cqx = lax.dot_general(q3, cx.astype(BF16), (((2,), (1,)), bdims), preferred_element_type=F32)
        num = (lax.dot_general(s.astype(BF16), v3, (((2,), (1,)), bdims), preferred_element_type=F32)
               + w_inter * cqx[:, :, :ML_HEAD_DIM])
        den_dot = jnp.sum(s, axis=2, keepdims=True) + w_inter * cqx[:, :, ML_HEAD_DIM:]
        m_t = b_c + big_m
        hout = num / jnp.maximum(jnp.abs(den_dot), jnp.exp(-m_t))

        m_last = big_m[:, CHUNK - 1:CHUNK, :]
        w_end = jnp.exp(g_c - m_last)
        decay = jnp.exp(m_prev - m_last)
        wvx = jnp.concatenate([w_end * v3.astype(F32), w_end], axis=2).astype(BF16)
        k_t = jnp.stack([s_k[rowsl(b), headsl(h)].astype(F32).T.astype(BF16) for b, h in items])
        cx_new = (jnp.concatenate([decay, decay], axis=2) * cx
                  + lax.dot_general(k_t, wvx, (((2,), (1,)), bdims), preferred_element_type=F32))
        m_new = m_t[:, CHUNK - 1:CHUNK, :]

        hc = hout - jnp.mean(hout, axis=2, keepdims=True)
        hn = hc * lax.rsqrt(jnp.mean(hc * hc, axis=2, keepdims=True) + EPS)
        for n, (b, h) in enumerate(items):
            bh = b * ML_HEADS + h
            s_cx[bh] = cx_new[n]
            m_ref[0, bh:bh + 1, :] = m_new[n]
            s_gated[rowsl(b), headsl(h)] = (jax.nn.sigmoid(s_o[rowsl(b), headsl(h)])
                                            * (hn[n] * gn_ref[:, headsl(h)])).astype(BF16)

    u = proj(OFF_U, D_S5)
    in_tiles = D_S5 // LANES
    for j in range(in_tiles):
        for b in range(batch):
            s_tm[j, pl.ds(b, CHUNK, stride=batch), :] = u[b * CHUNK:(b + 1) * CHUNK,
                                                          j * LANES:(j + 1) * LANES]
    ub = jnp.concatenate([s_tm[j] for j in range(in_tiles)], axis=1).astype(BF16)
    ys = []
    merge = []
    for k in range(2):
        uk = ub[:, k * S5_HALF_IN:(k + 1) * S5_HALF_IN]
        bur = jnp.dot(uk, bdr_ref[k], preferred_element_type=F32)
        bui = jnp.dot(uk, bdi_ref[k], preferred_element_type=F32)
        for j in range(S5_HALF_TILES):
            s_bur[j] = bur[:, j * LANES:(j + 1) * LANES]
            s_bui[j] = bui[:, j * LANES:(j + 1) * LANES]

        if k == 0:
            merge.append(jax.nn.sigmoid(proj(OFF_GML, D_MODEL)) * jnp.dot(
                s_gated[...], wml_ref[...], preferred_element_type=F32))
        else:
            merge.append(jax.nn.sigmoid(proj(OFF_GS5, D_MODEL)))

        for c in range(S5_HALF_TILES // S5_SCAN_TILES):
            tiles = range(c * S5_SCAN_TILES, (c + 1) * S5_SCAN_TILES)
            lanes = [slice(k * S5_HALF_CH + j * LANES, k * S5_HALF_CH + (j + 1) * LANES) for j in tiles]
            ar = [jnp.broadcast_to(abr_ref[:, ls], (batch, LANES)) for ls in lanes]
            ai = [jnp.broadcast_to(abi_ref[:, ls], (batch, LANES)) for ls in lanes]

            def step(t, carry, tiles=tiles, ar=ar, ai=ai):
                slab = pl.ds(pl.multiple_of(t * batch, batch), batch)
                out = []
                for n, j in enumerate(tiles):
                    hr, hi = carry[n]
                    nr = ar[n] * hr - ai[n] * hi + s_bur[j, slab, :]
                    ni = ar[n] * hi + ai[n] * hr + s_bui[j, slab, :]
                    s_bur[j, slab, :] = nr
                    s_bui[j, slab, :] = ni
                    out.append((nr, ni))
                return tuple(out)

            init = tuple((hr_ref[0, :, ls], hi_ref[0, :, ls]) for ls in lanes)
            fin = lax.fori_loop(0, CHUNK, step, init, unroll=True)
            for n, ls in enumerate(lanes):
                hr_ref[0, :, ls] = fin[n][0]
                hi_ref[0, :, ls] = fin[n][1]

        def states(ref):
            return jnp.concatenate([ref[j] for j in range(S5_HALF_TILES)], axis=1).astype(BF16)

        yk = jnp.dot(states(s_bur), cb_ref[k, :S5_HALF_CH, :], preferred_element_type=F32)
        yk = yk + jnp.dot(states(s_bui), cb_ref[k, S5_HALF_CH:, :], preferred_element_type=F32)
        ys.append(yk)
    y_tm = jnp.concatenate(ys, axis=1)
    for j in range(in_tiles):
        s_tm[j] = y_tm[:, j * LANES:(j + 1) * LANES]
    y = jnp.concatenate(
        [jnp.concatenate([s_tm[j, pl.ds(b, CHUNK, stride=batch), :] for j in range(in_tiles)], axis=1)
         for b in range(batch)], axis=0) + dskip_ref[...] * u
    glu = jnp.dot(jax.nn.gelu(y).astype(BF16), glu_ref[...], preferred_element_type=F32)
    y_s5 = glu[:, :D_MODEL] * jax.nn.sigmoid(glu[:, D_MODEL:])
    mix = merge[0] + merge[1] * y_s5

    x1 = x1_ref[...] + jnp.dot(mix.astype(BF16), wout_ref[...], preferred_element_type=F32)
    x1_ref[...] = x1

    xm = _rmsnorm(x1, nmoe_ref[...])
    logits = jnp.dot(xm.astype(BF16), wr_ref[...], preferred_element_type=F32) + br_ref[...]
    lane = lax.broadcasted_iota(jnp.int32, (rows, LANES), 1)
    lane_f = lane.astype(F32)
    logits = jnp.where(lane < N_EXPERTS, logits, -jnp.inf)
    vals, idxs = [], []
    for _ in range(TOP_K):
        mx = jnp.max(logits, axis=1, keepdims=True)
        am = jnp.min(jnp.where(logits == mx, lane_f, float(LANES)), axis=1, keepdims=True)
        vals.append(mx)
        idxs.append(am)
        logits = jnp.where(lane_f == am, -jnp.inf, logits)
    exps = [jnp.exp(v - vals[0]) for v in vals]
    esum = exps[0] + exps[1] + exps[2] + exps[3]
    idx_w = jnp.zeros((rows, LANES), F32)
    gate_w = jnp.zeros((rows, LANES), F32)
    for k in range(TOP_K):
        idx_w = jnp.where(lane == k, idxs[k], idx_w)
        gate_w = jnp.where(lane == k, exps[k] / esum, gate_w)
    idx_ref[...] = idx_w[:, :TOP_K].astype(jnp.int32)
    gate_ref[...] = gate_w[:, :TOP_K]
    chosen = jnp.zeros((rows, LANES), F32)
    for k in range(TOP_K):
        chosen = chosen + (lane_f == idxs[k]).astype(F32)
    cnt_ref[...] += jnp.sum(chosen, axis=0, keepdims=True)

    for s in range(TILE_ROWS):
        xm_ref[pl.ds(s, rows, stride=TILE_ROWS), :] = xm[:, s * LANES:(s + 1) * LANES]

    for phase_id, final_step in ((0, n_prompt - 1), (1, pl.num_programs(0) - 1)):
        @pl.when(i == final_step)
        def _(phase_id=phase_id):
            store = pltpu.make_async_copy(s_cx, cx_ref.at[phase_id], sem)
            store.start()
            store.wait()


def _mixer(x_prompt, x_sample, h0r, h0i, cx0, m0, w):
    batch, lp, _ = x_prompt.shape
    assert x_sample.shape[0] == batch and lp % CHUNK == 0 and x_sample.shape[1] % CHUNK == 0
    assert batch % ML_GROUP == 0
    n_prompt = lp // CHUNK
    n_sample = x_sample.shape[1] // CHUNK
    nblk = n_prompt + n_sample
    rows = batch * CHUNK
    t_all = nblk * rows
    bh = batch * ML_HEADS

    def phase(i):
        return jnp.minimum(i // n_prompt, 1)

    state_specs = [
        _const_spec((batch, S5_CH)), _const_spec((batch, S5_CH)),
        pl.BlockSpec(memory_space=pl.ANY),
        _const_spec((bh, LANES)),
    ]
    weights = (w['norm_mix'], w['w_in_a'], w['w_in_b'], w['w_in_g'], w['gate_bias'], w['abar_re'], w['abar_im'], w['bd_re'],
               w['bd_im'], w['cb'], w['d_skip'], w['w_glu'], w['ml_gn'], w['w_ml_out'], w['w_out'],
               w['norm_moe'], w['w_router'], w['b_router'])
    in_specs = ([pl.BlockSpec((batch, CHUNK, D_MODEL), lambda i: (0, jnp.minimum(i, n_prompt - 1), 0)),
                 pl.BlockSpec((batch, CHUNK, D_MODEL), lambda i: (0, jnp.maximum(i - n_prompt, 0), 0))]
                + state_specs + [_const_spec(a.shape) for a in weights])
    out_shape = (
        jax.ShapeDtypeStruct((t_all, D_MODEL), F32),
        jax.ShapeDtypeStruct((t_all * TILE_ROWS, LANES), F32),
        jax.ShapeDtypeStruct((t_all, TOP_K), jnp.int32),
        jax.ShapeDtypeStruct((t_all, TOP_K), F32),
        jax.ShapeDtypeStruct((1, LANES), F32),
        jax.ShapeDtypeStruct((2, batch, S5_CH), F32),
        jax.ShapeDtypeStruct((2, batch, S5_CH), F32),
        jax.ShapeDtypeStruct((2, bh, ML_HEAD_DIM, 2 * ML_HEAD_DIM), F32),
        jax.ShapeDtypeStruct((2, bh, LANES), F32),
    )
    out_specs = (
        pl.BlockSpec((rows, D_MODEL), lambda i: (i, 0)),
        pl.BlockSpec((rows * TILE_ROWS, LANES), lambda i: (i, 0)),
        pl.BlockSpec((rows, TOP_K), lambda i: (i, 0)),
        pl.BlockSpec((rows, TOP_K), lambda i: (i, 0)),
        pl.BlockSpec((1, LANES), lambda i: (0, 0)),
        pl.BlockSpec((1, batch, S5_CH), lambda i: (phase(i), 0, 0)),
        pl.BlockSpec((1, batch, S5_CH), lambda i: (phase(i), 0, 0)),
        pl.BlockSpec(memory_space=pl.ANY),
        pl.BlockSpec((1, bh, LANES), lambda i: (phase(i), 0, 0)),
    )
    scratch = [
        pltpu.VMEM((S5_HALF_TILES, rows, LANES), F32), pltpu.VMEM((S5_HALF_TILES, rows, LANES), F32),
        pltpu.VMEM((D_S5 // LANES, rows, LANES), F32),
        pltpu.VMEM((rows, D_ML), BF16), pltpu.VMEM((rows, D_ML), BF16), pltpu.VMEM((rows, D_ML), BF16),
        pltpu.VMEM((rows, D_ML), F32),
        pltpu.VMEM((rows, LANES), F32),
        pltpu.VMEM((rows, D_ML), BF16),
        pltpu.VMEM((bh, ML_HEAD_DIM, 2 * ML_HEAD_DIM), F32),
        pltpu.SemaphoreType.DMA(()),
    ]
    return pl.pallas_call(
        functools.partial(_mixer_kernel, batch=batch, n_prompt=n_prompt),
        grid=(nblk,),
        in_specs=in_specs,
        out_specs=out_specs,
        out_shape=out_shape,
        scratch_shapes=scratch,
        compiler_params=pltpu.CompilerParams(dimension_semantics=("arbitrary",),
                                             vmem_limit_bytes=VMEM_LIMIT),
        name="mixer",
    )(x_prompt, x_sample, h0r, h0i, cx0, m0, *weights)


def _plan_kernel(cnt_ref, idx_ref, spos_ref, runs_ref, meta_ref, s_run, s_pstart, s_lower, s_upper, *,
                 tile, n_blocks_pad):
    i = pl.program_id(0)
    lane = lax.broadcasted_iota(jnp.int32, (tile, LANES), 1)
    idx = idx_ref[...]
    onehots = [(lane == idx[:, k:k + 1]).astype(F32) for k in range(TOP_K)]
    mask = onehots[0] + onehots[1] + onehots[2] + onehots[3]

    @pl.when(i == 0)
    def _():
        cnt = cnt_ref[...]
        nblk = jnp.floor((cnt + (MOE_BLOCK - 1)) * (1.0 / MOE_BLOCK))
        lane1 = lax.broadcasted_iota(jnp.int32, (1, LANES), 1)
        pend = nblk
        shift = 1
        while shift < LANES:
            pend = pend + jnp.where(lane1 >= shift, pltpu.roll(pend, shift, axis=1), 0.0)
            shift *= 2
        s_pstart[...] = (pend - nblk) * MOE_BLOCK
        s_run[...] = jnp.zeros_like(s_run)
        r = lax.broadcasted_iota(jnp.int32, (tile, tile), 0)
        c = lax.broadcasted_iota(jnp.int32, (tile, tile), 1)
        s_lower[...] = (c < r).astype(BF16)
        s_upper[...] = (lax.broadcasted_iota(jnp.int32, (LANES, LANES), 0)
                        < lax.broadcasted_iota(jnp.int32, (LANES, LANES), 1)).astype(BF16)
        blk = lax.broadcasted_iota(jnp.int32, (n_blocks_pad, LANES), 0).astype(F32)
        lane_b = lax.broadcasted_iota(jnp.int32, (n_blocks_pad, LANES), 1)
        lane_f = lane_b.astype(F32)
        is_e = lane_b < N_EXPERTS
        done = jnp.logical_and(is_e, pend <= blk)
        e_of = jnp.minimum(jnp.sum(done.astype(F32), axis=1, keepdims=True), N_EXPERTS - 1.0)
        mine = lane_f == e_of
        blk_in_e = jnp.sum(jnp.where(mine, blk - (pend - nblk), 0.0), axis=1, keepdims=True)
        cnt_e = jnp.sum(jnp.where(mine, cnt, 0.0), axis=1, keepdims=True)
        valid = jnp.clip(cnt_e - blk_in_e * MOE_BLOCK, 0.0, float(MOE_BLOCK))
        used = jnp.sum(jnp.where(lane1 == N_EXPERTS - 1, pend, 0.0), axis=1, keepdims=True)
        owns = jnp.logical_and(is_e, nblk > 0.0)
        first = jnp.logical_and(blk_in_e == 0.0, blk[:, :1] < used).astype(F32)
        later = jnp.logical_and(owns, lane_f > e_of)
        nxt = jnp.min(jnp.where(later, lane_f, float(LANES)), axis=1, keepdims=True)
        nxt = jnp.where(nxt < float(LANES), nxt, -1.0)
        run = jnp.sum(jnp.logical_and(owns, lane_f < e_of).astype(F32), axis=1, keepdims=True)
        parity = run - 2.0 * jnp.floor(run * 0.5)
        meta = jnp.zeros((n_blocks_pad, LANES), F32)
        for col, val in enumerate((e_of, valid, used, first, nxt, parity)):
            meta = jnp.where(lane_b == col, val, meta)
        meta_ref[...] = meta.astype(jnp.int32)

    before = jnp.dot(s_lower[...], mask.astype(BF16), preferred_element_type=F32)
    tile_cnt = jnp.sum(mask, axis=0, keepdims=True)
    tile_start = jnp.sum(jnp.dot(mask.astype(BF16), s_upper[...], preferred_element_type=F32),
                         axis=0, keepdims=True)
    parity = (i % 2).astype(F32)
    base = (before + tile_start + parity * (tile * TOP_K)) * TILE_ROWS
    spos = jnp.zeros((tile, LANES), F32)
    for k in range(TOP_K):
        pk = jnp.sum(onehots[k] * base, axis=1, keepdims=True)
        spos = jnp.where(lane == k, pk, spos)
    spos_ref[...] = spos[:, :TOP_K].astype(jnp.int32)
    sub = lax.broadcasted_iota(jnp.int32, (SUBLANES, LANES), 0)
    runs = jnp.where(sub == 0, tile_cnt,
                     jnp.where(sub == 1, s_pstart[...] + s_run[...],
                               jnp.where(sub == 2, tile_start, 0.0)))
    runs_ref[0] = runs.astype(jnp.int32)
    s_run[...] += tile_cnt


def _plan(counts, idx_all, tile, n_blocks_pad):
    t_all = idx_all.shape[0]
    return pl.pallas_call(
        functools.partial(_plan_kernel, tile=tile, n_blocks_pad=n_blocks_pad),
        grid=(t_all // tile,),
        in_specs=[pl.BlockSpec((1, LANES), lambda i: (0, 0)),
                  pl.BlockSpec((tile, TOP_K), lambda i: (i, 0))],
        out_specs=(pl.BlockSpec((tile, TOP_K), lambda i: (i, 0)),
                   pl.BlockSpec((1, SUBLANES, LANES), lambda i: (i, 0, 0)),
                   pl.BlockSpec((n_blocks_pad, LANES), lambda i: (0, 0))),
        out_shape=(jax.ShapeDtypeStruct((t_all, TOP_K), jnp.int32),
                   jax.ShapeDtypeStruct((t_all // tile, SUBLANES, LANES), jnp.int32),
                   jax.ShapeDtypeStruct((n_blocks_pad, LANES), jnp.int32)),
        scratch_shapes=[pltpu.VMEM((1, LANES), F32), pltpu.VMEM((1, LANES), F32),
                        pltpu.VMEM((tile, tile), BF16), pltpu.VMEM((LANES, LANES), BF16)],
        compiler_params=pltpu.CompilerParams(dimension_semantics=("arbitrary",)),
        name="moe_plan",
    )(counts, idx_all)


COMBINE_PARTS = 4
RUN_BITS = 10
RUN_SPLIT = 7


def _run_copies(runs_ref, sorted_hbm, stage, sem, to_hbm):
    def expert(e, carry):
        n = runs_ref[e]
        first_sorted = runs_ref[LANES + e]
        first_staged = runs_ref[2 * LANES + e]
        def piece(bit):
            size = (1 << bit) * TILE_ROWS
            done = lax.shift_left(lax.shift_right_logical(n, bit + 1), bit + 1)

            @pl.when(lax.bitwise_and(lax.shift_right_logical(n, bit), 1) == 1)
            def _():
                hbm = sorted_hbm.at[pl.ds(pl.multiple_of((first_sorted + done) * TILE_ROWS, TILE_ROWS), size)]
                vmem = stage.at[pl.ds(pl.multiple_of((first_staged + done) * TILE_ROWS, TILE_ROWS), size)]
                src, dst = (vmem, hbm) if to_hbm else (hbm, vmem)
                pltpu.make_async_copy(src, dst, sem).start(priority=bit % 2)

        @pl.when(n >= (1 << RUN_SPLIT))
        def _():
            for bit in reversed(range(RUN_SPLIT, RUN_BITS)):
                piece(bit)

        for bit in reversed(range(RUN_SPLIT)):
            piece(bit)
        return carry

    lax.fori_loop(0, N_EXPERTS, expert, 0)


def _dispatch_kernel(valid_ref, spos_ref, runs_ref, xm_ref, buf_ref, zeros, stage, sem, zsem, *, tile,
                     n_blocks, n_tiles):
    i = pl.program_id(0)
    slot = i % 2
    n_rows = tile * TOP_K * TILE_ROWS

    def drain(s):
        pltpu.make_async_copy(buf_ref.at[pl.ds(0, n_rows)], buf_ref.at[pl.ds(0, n_rows)], sem.at[s]).wait()

    pad_rows = (n_blocks * MOE_BLOCK - n_tiles * tile * TOP_K) * TILE_ROWS

    @pl.when(i == 0)
    def _():
        zeros[...] = jnp.zeros_like(zeros)

        def fill(b, carry):
            filled = valid_ref[b]
            n = MOE_BLOCK - filled
            for bit in reversed(range(RUN_BITS)):
                size = (1 << bit) * TILE_ROWS
                done = lax.shift_left(lax.shift_right_logical(n, bit + 1), bit + 1)

                @pl.when(lax.bitwise_and(lax.shift_right_logical(n, bit), 1) == 1)
                def _(size=size, done=done, bit=bit):
                    first = (b * MOE_BLOCK + filled + done) * TILE_ROWS
                    pltpu.make_async_copy(
                        zeros.at[pl.ds(0, size)],
                        buf_ref.at[pl.ds(pl.multiple_of(first, TILE_ROWS), size)], zsem).start(
                            priority=bit % 2)
            return carry

        lax.fori_loop(0, n_blocks, fill, 0)

    @pl.when(i >= 2)
    def _():
        drain(slot)

    def place(j, carry):
        tok = xm_ref[pl.ds(pl.multiple_of(j * TILE_ROWS, TILE_ROWS), TILE_ROWS), :]
        for k in range(TOP_K):
            stage[pl.ds(pl.multiple_of(spos_ref[j * TOP_K + k], TILE_ROWS), TILE_ROWS), :] = tok
        return carry

    lax.fori_loop(0, tile, place, 0, unroll=16)
    slot_rows = stage.at[pl.ds(pl.multiple_of(slot * n_rows, n_rows), n_rows)]
    _run_copies(runs_ref, buf_ref, slot_rows, sem.at[slot], to_hbm=True)

    last = pl.num_programs(0) - 1

    @pl.when(i == last)
    def _():
        drain(slot)
        if pad_rows:
            pltpu.make_async_copy(buf_ref.at[pl.ds(0, pad_rows)], buf_ref.at[pl.ds(0, pad_rows)],
                                  zsem).wait()

    @pl.when(jnp.logical_and(i == last, i >= 1))
    def _():
        drain(1 - slot)


def _dispatch(block_valid, spos_flat, runs_flat, xm_tiles, tile, n_blocks):
    t_all = xm_tiles.shape[0] // TILE_ROWS
    grid_spec = pltpu.PrefetchScalarGridSpec(
        num_scalar_prefetch=1,
        grid=(t_all // tile,),
        in_specs=[pl.BlockSpec((tile * TOP_K,), lambda i, va: (i,), memory_space=pltpu.SMEM),
                  pl.BlockSpec((SUBLANES * LANES,), lambda i, va: (i,), memory_space=pltpu.SMEM),
                  pl.BlockSpec((tile * TILE_ROWS, LANES), lambda i, va: (i, 0))],
        out_specs=pl.BlockSpec(memory_space=pl.ANY),
        scratch_shapes=[pltpu.VMEM((MOE_BLOCK * TILE_ROWS, LANES), F32),
                        pltpu.VMEM((2 * tile * TOP_K * TILE_ROWS, LANES), F32),
                        pltpu.SemaphoreType.DMA((2,)), pltpu.SemaphoreType.DMA(())],
    )
    return pl.pallas_call(
        functools.partial(_dispatch_kernel, tile=tile, n_blocks=n_blocks, n_tiles=t_all // tile),
        grid_spec=grid_spec,
        out_shape=jax.ShapeDtypeStruct((n_blocks * MOE_BLOCK * TILE_ROWS, LANES), F32),
        compiler_params=pltpu.CompilerParams(dimension_semantics=("arbitrary",),
                                             vmem_limit_bytes=VMEM_LIMIT),
        name="moe_dispatch",
    )(block_valid, spos_flat, runs_flat, xm_tiles)


def _expert_kernel(be_ref, used_ref, first_ref, next_ref, slot_ref, valid_ref, x_ref, wgu_ref, bgu_ref,
                   wd_ref, bd_ref, o_ref, wbuf_gu, wbuf_d, s_wgu, s_wd, sem):
    i = pl.program_id(0)

    @pl.when(i >= used_ref[0])
    def _():
        o_ref[...] = jnp.zeros_like(o_ref)

    def weight_copies(e, slot):
        return (pltpu.make_async_copy(wgu_ref.at[e], wbuf_gu.at[slot], sem.at[0, slot]),
                pltpu.make_async_copy(wd_ref.at[e], wbuf_d.at[slot], sem.at[1, slot]))

    @pl.when(i == 0)
    def _():
        for cp in weight_copies(be_ref[0], slot_ref[0]):
            cp.start()

    @pl.when(first_ref[i] == 1)
    def _():
        slot = slot_ref[i]
        for cp in weight_copies(be_ref[i], slot):
            cp.wait()

        @pl.when(next_ref[i] >= 0)
        def _():
            for cp in weight_copies(next_ref[i], 1 - slot):
                cp.start()

        s_wgu[...] = wbuf_gu[slot].astype(BF16)
        s_wd[...] = wbuf_d[slot].astype(BF16)

    def ffn(nrows):
        pieces = [x_ref[pl.ds(s, nrows, stride=TILE_ROWS), :] for s in range(TILE_ROWS)]
        x = jnp.concatenate(pieces, axis=1).astype(BF16)
        gu = jnp.dot(x, s_wgu[...], preferred_element_type=F32) + bgu_ref[0]
        g = jnp.minimum(gu[:, :D_FF], SWIGLU_LIMIT)
        up = jnp.clip(gu[:, D_FF:], -SWIGLU_LIMIT, SWIGLU_LIMIT)
        hdn = (up + 1.0) * (g * jax.nn.sigmoid(SWIGLU_ALPHA * g))
        out = jnp.dot(hdn.astype(BF16), s_wd[...], preferred_element_type=F32) + bd_ref[0]
        for s in range(TILE_ROWS):
            o_ref[pl.ds(s, nrows, stride=TILE_ROWS), :] = out[:, s * LANES:(s + 1) * LANES]
        if nrows < MOE_BLOCK:
            o_ref[nrows * TILE_ROWS:, :] = jnp.zeros(((MOE_BLOCK - nrows) * TILE_ROWS, LANES), F32)

    in_use = i < used_ref[0]
    quarter = MOE_BLOCK // EXPERT_QUARTERS
    assert quarter & (quarter - 1) == 0
    quarters = lax.shift_right_logical(valid_ref[i] + (quarter - 1), quarter.bit_length() - 1)
    for nq in range(1, EXPERT_QUARTERS + 1):
        cond = (quarters <= 1) if nq == 1 else (quarters == nq)

        @pl.when(jnp.logical_and(in_use, cond))
        def _(nq=nq):
            ffn(nq * quarter)


def _experts(meta, buf, wgu, bgu, wd, bd, n_blocks):
    blk_rows = MOE_BLOCK * TILE_ROWS
    block_e, n_used = meta[:n_blocks, 0], meta[:1, 2]
    first, nxt, slot = meta[:n_blocks, 3], meta[:n_blocks, 4], meta[:n_blocks, 5]
    valid = meta[:n_blocks, 1]

    def in_row_map(i, be, nu, *_):
        return (jnp.minimum(i, nu[0] - 1), 0)

    def b_map(i, be, *_):
        return (be[i], 0, 0)

    grid_spec = pltpu.PrefetchScalarGridSpec(
        num_scalar_prefetch=6,
        grid=(n_blocks,),
        in_specs=[pl.BlockSpec((blk_rows, LANES), in_row_map),
                  pl.BlockSpec(memory_space=pl.ANY),
                  pl.BlockSpec((1, 1, 2 * D_FF), b_map),
                  pl.BlockSpec(memory_space=pl.ANY),
                  pl.BlockSpec((1, 1, D_MODEL), b_map)],
        out_specs=pl.BlockSpec((blk_rows, LANES), lambda i, *_: (i, 0)),
        scratch_shapes=[pltpu.VMEM((2, D_MODEL, 2 * D_FF), F32), pltpu.VMEM((2, D_FF, D_MODEL), F32),
                        pltpu.VMEM((D_MODEL, 2 * D_FF), BF16), pltpu.VMEM((D_FF, D_MODEL), BF16),
                        pltpu.SemaphoreType.DMA((2, 2))],
    )
    return pl.pallas_call(
        _expert_kernel,
        grid_spec=grid_spec,
        out_shape=jax.ShapeDtypeStruct(buf.shape, F32),
        compiler_params=pltpu.CompilerParams(dimension_semantics=("arbitrary",),
                                             vmem_limit_bytes=VMEM_LIMIT),
        name="moe_experts",
    )(block_e, n_used, first, nxt, slot, valid, buf, wgu, bgu, wd, bd)


def _combine_kernel(spos_ref, runs_ref, next_runs_ref, gate_ref, x1_ref, rows_ref, nf_ref, yp_ref,
                    ys_ref, stage, *rest, batch, n_prompt):
    *gbufs, sem = rest
    i = pl.program_id(0)
    tile = batch * CHUNK
    n_rows = tile * TOP_K * TILE_ROWS
    slot = i % 2

    def slot_rows(s):
        return stage.at[pl.ds(pl.multiple_of(s * n_rows, n_rows), n_rows)]

    @pl.when(i == 0)
    def _():
        _run_copies(runs_ref, rows_ref, slot_rows(0), sem.at[0], to_hbm=False)

    @pl.when(i + 1 < pl.num_programs(0))
    def _():
        _run_copies(next_runs_ref, rows_ref, slot_rows(1 - slot), sem.at[1 - slot], to_hbm=False)

    pltpu.make_async_copy(rows_ref.at[pl.ds(0, n_rows)], slot_rows(slot), sem.at[slot]).wait()

    part = tile // len(gbufs)
    gates = gate_ref[...]

    def reduce_part(q):
        rows = slice(q * part, (q + 1) * part)
        pieces = []
        for s in range(TILE_ROWS):
            piece = x1_ref[rows, s * LANES:(s + 1) * LANES]
            for k in range(TOP_K):
                piece = piece + gates[rows, k:k + 1] * gbufs[q][k, pl.ds(s, part, stride=TILE_ROWS), :]
            pieces.append(piece)
        return _rmsnorm(jnp.concatenate(pieces, axis=1), nf_ref[...])

    outs = []
    for q, gbuf in enumerate(gbufs):
        for j in range(part):
            for k in range(TOP_K):
                p = spos_ref[(q * part + j) * TOP_K + k]
                gbuf[k, j * TILE_ROWS:(j + 1) * TILE_ROWS, :] = (
                    stage[pl.ds(pl.multiple_of(p, TILE_ROWS), TILE_ROWS), :])
        if q:
            outs.append(reduce_part(q - 1))
    outs.append(reduce_part(len(gbufs) - 1))
    y = jnp.concatenate(outs, axis=0).reshape(batch, CHUNK, D_MODEL)

    @pl.when(i < n_prompt)
    def _():
        yp_ref[...] = y

    @pl.when(i >= n_prompt)
    def _():
        ys_ref[...] = y


def _combine(spos_flat, runs_flat, gates, x1, out_rows, norm_final, batch, lp, ls):
    tile = batch * CHUNK
    n_prompt = lp // CHUNK
    n_sample = ls // CHUNK
    last = n_prompt + n_sample - 1
    return pl.pallas_call(
        functools.partial(_combine_kernel, batch=batch, n_prompt=n_prompt),
        grid=(n_prompt + n_sample,),
        in_specs=[pl.BlockSpec((tile * TOP_K,), lambda i: (i,), memory_space=pltpu.SMEM),
                  pl.BlockSpec((SUBLANES * LANES,), lambda i: (i,), memory_space=pltpu.SMEM),
                  pl.BlockSpec((SUBLANES * LANES,), lambda i: (jnp.minimum(i + 1, last),),
                               memory_space=pltpu.SMEM),
                  pl.BlockSpec((tile, TOP_K), lambda i: (i, 0)),
                  pl.BlockSpec((tile, D_MODEL), lambda i: (i, 0)),
                  pl.BlockSpec(memory_space=pl.ANY),
                  pl.BlockSpec((1, D_MODEL), lambda i: (0, 0))],
        out_specs=(pl.BlockSpec((batch, CHUNK, D_MODEL), lambda i: (0, jnp.minimum(i, n_prompt - 1), 0)),
                   pl.BlockSpec((batch, CHUNK, D_MODEL), lambda i: (0, jnp.maximum(i - n_prompt, 0), 0))),
        out_shape=(jax.ShapeDtypeStruct((batch, lp, D_MODEL), F32),
                   jax.ShapeDtypeStruct((batch, ls, D_MODEL), F32)),
        scratch_shapes=([pltpu.VMEM((2 * tile * TOP_K * TILE_ROWS, LANES), F32)]
                        + [pltpu.VMEM((TOP_K, tile // COMBINE_PARTS * TILE_ROWS, LANES), F32)
                           for _ in range(COMBINE_PARTS)]
                        + [pltpu.SemaphoreType.DMA((2,))]),
        compiler_params=pltpu.CompilerParams(dimension_semantics=("arbitrary",),
                                             vmem_limit_bytes=VMEM_LIMIT),
        name="moe_combine",
    )(spos_flat, runs_flat, runs_flat, gates, x1, out_rows, norm_final)


def _prep_weights(norm_mix, w_in, b_ig, b_fg, s5_a_re, s5_a_im, s5_log_dt, s5_b_re, s5_b_im,
                  s5_c_re, s5_c_im, s5_d, w_s5_glu, ml_gn, w_ml_out, w_out, norm_moe, w_router,
                  b_router):
    w = w_in[0]
    o_ig = D_S5 + 4 * D_ML
    o_gs5 = o_ig + 2 * ML_HEADS
    w_a = w[:, :o_ig].astype(BF16)
    w_b = w[:, o_gs5:].astype(BF16)
    w_g = jnp.concatenate([w[:, o_ig:o_gs5], jnp.zeros((D_MODEL, LANES - 2 * ML_HEADS), F32)],
                          axis=1).astype(BF16)
    gate_bias = jnp.concatenate([b_ig[0], b_fg[0], jnp.zeros((LANES - 2 * ML_HEADS,), F32)])[None]

    ar, ai = s5_a_re[0], s5_a_im[0]
    dt = jnp.exp(s5_log_dt[0])[:, None]
    mag = jnp.exp(dt * ar)
    abar_re = mag * jnp.cos(dt * ai)
    abar_im = mag * jnp.sin(dt * ai)
    den = ar * ar + ai * ai
    fr = ((abar_re - 1.0) * ar + abar_im * ai) / den
    fi = (abar_im * ar - (abar_re - 1.0) * ai) / den
    br, bi = s5_b_re[0], s5_b_im[0]
    bbar_re = fr[..., None] * br - fi[..., None] * bi
    bbar_im = fr[..., None] * bi + fi[..., None] * br
    gh = S5_GROUPS // 2
    same_group = (jnp.arange(gh)[:, None, None, None] == jnp.arange(gh)[None, None, :, None])

    def blockdiag_in(bb):
        t = bb.reshape(2, gh, S5_STATE, S5_GROUP).transpose(0, 1, 3, 2)
        full = jnp.where(same_group[None], t[:, :, :, None, :], 0.0)
        return full.reshape(2, gh * S5_GROUP, gh * S5_STATE).astype(BF16)

    def blockdiag_out(cc):
        t = cc.reshape(2, gh, S5_GROUP, S5_STATE).transpose(0, 1, 3, 2)
        full = jnp.where(same_group[None], t[:, :, :, None, :], 0.0)
        return full.reshape(2, gh * S5_STATE, gh * S5_GROUP)

    cb = jnp.concatenate([blockdiag_out(s5_c_re[0]), -blockdiag_out(s5_c_im[0])], axis=1).astype(BF16)
    w_router_p = jnp.concatenate(
        [w_router[0], jnp.zeros((D_MODEL, LANES - N_EXPERTS), F32)], axis=1).astype(BF16)
    b_router_p = jnp.concatenate([b_router[0], jnp.zeros((LANES - N_EXPERTS,), F32)])[None]
    return dict(
        norm_mix=norm_mix[0][None], w_in_a=w_a, w_in_b=w_b, w_in_g=w_g, gate_bias=gate_bias,
        abar_re=abar_re.reshape(1, S5_CH), abar_im=abar_im.reshape(1, S5_CH),
        bd_re=blockdiag_in(bbar_re), bd_im=blockdiag_in(bbar_im), cb=cb,
        d_skip=s5_d[0][None], w_glu=w_s5_glu[0].astype(BF16), ml_gn=ml_gn[0][None],
        w_ml_out=w_ml_out[0].astype(BF16), w_out=w_out[0].astype(BF16),
        norm_moe=norm_moe[0][None], w_router=w_router_p, b_router=b_router_p)


def kernel(x_prompt, x_sample, state_s5_re, state_s5_im, state_ml_C, state_ml_n, state_ml_m, norm_mix, w_in, b_ig, b_fg, s5_a_re, s5_a_im, s5_log_dt, s5_b_re, s5_b_im, s5_c_re, s5_c_im, s5_d, w_s5_glu, ml_gn, w_ml_out, w_out, norm_moe, w_router, b_router, w_gate_up, b_gate_up, w_down, b_down, norm_final):
    w = _prep_weights(norm_mix, w_in, b_ig, b_fg, s5_a_re, s5_a_im, s5_log_dt, s5_b_re, s5_b_im,
                      s5_c_re, s5_c_im, s5_d, w_s5_glu, ml_gn, w_ml_out, w_out, norm_moe,
                      w_router, b_router)
    batch, lp, _ = x_prompt.shape
    ls = x_sample.shape[1]
    bh = batch * ML_HEADS
    cx0 = jnp.concatenate(
        [jnp.swapaxes(state_ml_C[0].reshape(bh, ML_HEAD_DIM, ML_HEAD_DIM), 1, 2),
         jnp.broadcast_to(state_ml_n[0].reshape(bh, ML_HEAD_DIM, 1), (bh, ML_HEAD_DIM, ML_HEAD_DIM))],
        axis=2)
    x1, xm, idx, gates, counts, hr, hi, cx, m = _mixer(
        x_prompt, x_sample,
        state_s5_re[0].reshape(batch, S5_CH), state_s5_im[0].reshape(batch, S5_CH), cx0,
        jnp.broadcast_to(state_ml_m[0].reshape(bh, 1), (bh, LANES)), w)
    c = jnp.swapaxes(cx[:, :, :, :ML_HEAD_DIM], 2, 3)
    n = cx[:, :, :, ML_HEAD_DIM]

    tile = batch * CHUNK
    t_all = idx.shape[0]
    n_blocks = -(-(t_all * TOP_K + N_EXPERTS * (MOE_BLOCK - 1)) // MOE_BLOCK)
    n_blocks_pad = -(-n_blocks // SUBLANES) * SUBLANES
    spos, runs, meta = _plan(counts, idx, tile, n_blocks_pad)
    spos_flat = spos.reshape(t_all * TOP_K)
    runs_flat = runs.reshape(-1)
    buf = _dispatch(meta[:n_blocks, 1], spos_flat, runs_flat, xm, tile, n_blocks)
    out_rows = _experts(meta, buf, w_gate_up[0], b_gate_up[0][:, None, :], w_down[0],
                        b_down[0][:, None, :], n_blocks)
    y_prompt, y_sample = _combine(spos_flat, runs_flat, gates, x1, out_rows, norm_final[None], batch,
                                  lp, ls)

    def states(p):
        return (hr[p].reshape(1, batch, S5_GROUPS, S5_STATE), hi[p].reshape(1, batch, S5_GROUPS, S5_STATE),
                c[p].reshape(1, batch, ML_HEADS, ML_HEAD_DIM, ML_HEAD_DIM),
                n[p].reshape(1, batch, ML_HEADS, ML_HEAD_DIM), m[p, :, 0].reshape(1, batch, ML_HEADS))

    return (y_prompt, y_sample) + states(0) + states(1)
```

```python
import functools

import jax
import jax.numpy as jnp
from jax import lax
from jax.experimental import pallas as pl
from jax.experimental.pallas import tpu as pltpu

F32 = jnp.float32
BF16 = jnp.bfloat16

D_MODEL = 1024
CHUNK = 64
EPS = 1e-6
D_S5 = 512
S5_GROUP = 16
S5_GROUPS = D_S5 // S5_GROUP
S5_STATE = 64
S5_CH = S5_GROUPS * S5_STATE
ML_HEADS = 4
ML_HEAD_DIM = 128
D_ML = ML_HEADS * ML_HEAD_DIM
N_EXPERTS = 32
TOP_K = 4
D_FF = 1024
SWIGLU_LIMIT = 7.0
SWIGLU_ALPHA = 1.702

LANES = 128
SUBLANES = 8
TILE_ROWS = D_MODEL // LANES

OFF_U = 0
OFF_Q = OFF_U + D_S5
OFF_K = OFF_Q + D_ML
OFF_V = OFF_K + D_ML
OFF_O = OFF_V + D_ML
OFF_GS5 = OFF_O + D_ML
OFF_GML = OFF_GS5 + D_MODEL
OFF_GATE = OFF_GML + D_MODEL

S5_HALF_IN = D_S5 // 2
S5_HALF_CH = S5_CH // 2
S5_HALF_TILES = S5_HALF_CH // LANES
S5_SCAN_TILES = 4
ML_GROUP = 2
MOE_BLOCK = 512
EXPERT_QUARTERS = 4
VMEM_LIMIT = 60 * 1024 * 1024


def _rmsnorm(x, w):
    return x * lax.rsqrt(jnp.mean(x * x, axis=-1, keepdims=True) + EPS) * w


def _const_spec(shape):
    nd = len(shape)
    return pl.BlockSpec(shape, lambda *_: (0,) * nd, pipeline_mode=pl.Buffered(1))


def _mixer_kernel(xp_ref, xs_ref, h0r_ref, h0i_ref, cx0_ref, m0_ref,
                  nmix_ref, wina_ref, winb_ref, wing_ref, gbias_ref, abr_ref, abi_ref, bdr_ref, bdi_ref, cb_ref,
                  dskip_ref, glu_ref, gn_ref, wml_ref, wout_ref, nmoe_ref, wr_ref, br_ref,
                  x1_ref, xm_ref, idx_ref, gate_ref, cnt_ref, hr_ref, hi_ref, cx_ref, m_ref,
                  s_bur, s_bui, s_tm, s_q, s_k, s_v, s_o, s_col, s_gated, s_cx, sem, *, batch, n_prompt):
    i = pl.program_id(0)
    rows = batch * CHUNK

    @pl.when(i == 0)
    def _():
        hr_ref[...] = jnp.zeros_like(hr_ref)
        hi_ref[...] = jnp.zeros_like(hi_ref)
        m_ref[...] = jnp.zeros_like(m_ref)
        s_cx[...] = jnp.zeros_like(s_cx)
        cnt_ref[...] = jnp.zeros_like(cnt_ref)

    @pl.when(i == n_prompt)
    def _():
        hr_ref[0] = h0r_ref[...]
        hi_ref[0] = h0i_ref[...]
        m_ref[0] = m0_ref[...]
        load = pltpu.make_async_copy(cx0_ref, s_cx, sem)
        load.start()
        load.wait()

    @pl.when(i < n_prompt)
    def _():
        x1_ref[...] = xp_ref[...].reshape(rows, D_MODEL)

    @pl.when(i >= n_prompt)
    def _():
        x1_ref[...] = xs_ref[...].reshape(rows, D_MODEL)

    xn = _rmsnorm(x1_ref[...], nmix_ref[...]).astype(BF16)

    def proj(off, width):
        if off >= OFF_GATE:
            w_cols = wing_ref[...]
        elif off >= OFF_GS5:
            w_cols = winb_ref[:, off - OFF_GS5:off - OFF_GS5 + width]
        else:
            w_cols = wina_ref[:, off:off + width]
        return jnp.dot(xn, w_cols, preferred_element_type=F32)

    gates = proj(OFF_GATE, LANES) + gbias_ref[...]
    lane_g = lax.broadcasted_iota(jnp.int32, (rows, LANES), 1)
    gg = jnp.where(lane_g < ML_HEADS, gates, jax.nn.log_sigmoid(gates))
    gt8 = gg.T[0:SUBLANES, :]
    pos = lax.broadcasted_iota(jnp.int32, (SUBLANES, rows), 1) % CHUNK
    cum = gt8
    shift = 1
    while shift < CHUNK:
        cum = cum + jnp.where(pos >= shift, pltpu.roll(cum, shift, axis=1), 0.0)
        shift *= 2
    g8 = gt8 - pltpu.roll(cum, ML_HEADS, axis=0)
    mx8 = g8
    shift = 1
    while shift < CHUNK:
        mx8 = jnp.maximum(mx8, jnp.where(pos >= shift, pltpu.roll(mx8, shift, axis=1), -jnp.inf))
        shift *= 2
    sub = lax.broadcasted_iota(jnp.int32, (SUBLANES, rows), 0)
    top8 = jnp.where(sub < ML_HEADS, g8, cum)
    s_col[...] = jnp.concatenate(
        [top8, mx8, jnp.zeros((LANES - 2 * SUBLANES, rows), F32)], axis=0).T
    s_q[...] = (proj(OFF_Q, D_ML) * (ML_HEAD_DIM ** -0.5)).astype(BF16)
    s_k[...] = proj(OFF_K, D_ML).astype(BF16)
    s_v[...] = proj(OFF_V, D_ML).astype(BF16)
    s_o[...] = proj(OFF_O, D_ML)

    tri = (lax.broadcasted_iota(jnp.int32, (CHUNK, CHUNK), 0)
           >= lax.broadcasted_iota(jnp.int32, (CHUNK, CHUNK), 1))[None]
    bdims = ((0,), (0,))

    def rowsl(b):
        return slice(b * CHUNK, (b + 1) * CHUNK)

    def headsl(h):
        return slice(h * ML_HEAD_DIM, (h + 1) * ML_HEAD_DIM)

    for g0 in range(0, batch, ML_GROUP):
        items = [(b, h) for b in range(g0, g0 + ML_GROUP) for h in range(ML_HEADS)]
        def col(j, items=items):
            return jnp.stack([jnp.broadcast_to(s_col[rowsl(b), j + h:j + h + 1], (CHUNK, LANES))
                              for b, h in items])

        g_c, b_c, mx_c = col(0), col(ML_HEADS), col(2 * ML_HEADS)
        g_r = jnp.stack([g8[h:h + 1, rowsl(b)] for b, h in items])
        m_prev = jnp.stack([m_ref[0, b * ML_HEADS + h:b * ML_HEADS + h + 1, :] for b, h in items])
        q3 = jnp.stack([s_q[rowsl(b), headsl(h)] for b, h in items])
        k3 = jnp.stack([s_k[rowsl(b), headsl(h)] for b, h in items])
        v3 = jnp.stack([s_v[rowsl(b), headsl(h)] for b, h in items])
        cx = jnp.stack([s_cx[b * ML_HEADS + h] for b, h in items])

        big_m = jnp.maximum(m_prev, mx_c)
        p = jnp.exp(jnp.where(tri, g_r - big_m[:, :, :CHUNK], -jnp.inf))
        w_inter = jnp.exp(m_prev - big_m)
        s = lax.dot_general(q3, k3, (((2,), (2,)), bdims), preferred_element_type=F32) * p
        cqx = lax.dot_general(q3, cx.astype(BF16), (((2,), (1,)), bdims), preferred_element_type=F32)
        num = (lax.dot_general(s.astype(BF16), v3, (((2,), (1,)), bdims), preferred_element_type=F32)
               + w_inter * cqx[:, :, :ML_HEAD_DIM])
        den_dot = jnp.sum(s, axis=2, keepdims=True) + w_inter * cqx[:, :, ML_HEAD_DIM:]
        m_t = b_c + big_m
        hout = num / jnp.maximum(jnp.abs(den_dot), jnp.exp(-m_t))

        m_last = big_m[:, CHUNK - 1:CHUNK, :]
        w_end = jnp.exp(g_c - m_last)
        decay = jnp.exp(m_prev - m_last)
        wvx = jnp.concatenate([w_end * v3.astype(F32), w_end], axis=2).astype(BF16)
        k_t = jnp.stack([s_k[rowsl(b), headsl(h)].astype(F32).T.astype(BF16) for b, h in items])
        cx_new = (jnp.concatenate([decay, decay], axis=2) * cx
                  + lax.dot_general(k_t, wvx, (((2,), (1,)), bdims), preferred_element_type=F32))
        m_new = m_t[:, CHUNK - 1:CHUNK, :]

        hc = hout - jnp.mean(hout, axis=2, keepdims=True)
        hn = hc * lax.rsqrt(jnp.mean(hc * hc, axis=2, keepdims=True) + EPS)
        for n, (b, h) in enumerate(items):
            bh = b * ML_HEADS + h
            s_cx[bh] = cx_new[n]
            m_ref[0, bh:bh + 1, :] = m_new[n]
            s_gated[rowsl(b), headsl(h)] = (jax.nn.sigmoid(s_o[rowsl(b), headsl(h)])
                                            * (hn[n] * gn_ref[:, headsl(h)])).astype(BF16)

    u = proj(OFF_U, D_S5)
    in_tiles = D_S5 // LANES
    for j in range(in_tiles):
        for b in range(batch):
            s_tm[j, pl.ds(b, CHUNK, stride=batch), :] = u[b * CHUNK:(b + 1) * CHUNK,
                                                          j * LANES:(j + 1) * LANES]
    ub = jnp.concatenate([s_tm[j] for j in range(in_tiles)], axis=1).astype(BF16)
    ys = []
    merge = []
    for k in range(2):
        uk = ub[:, k * S5_HALF_IN:(k + 1) * S5_HALF_IN]
        bur = jnp.dot(uk, bdr_ref[k], preferred_element_type=F32)
        bui = jnp.dot(uk, bdi_ref[k], preferred_element_type=F32)
        for j in range(S5_HALF_TILES):
            s_bur[j] = bur[:, j * LANES:(j + 1) * LANES]
            s_bui[j] = bui[:, j * LANES:(j + 1) * LANES]

        if k == 0:
            merge.append(jax.nn.sigmoid(proj(OFF_GML, D_MODEL)) * jnp.dot(
                s_gated[...], wml_ref[...], preferred_element_type=F32))
        else:
            merge.append(jax.nn.sigmoid(proj(OFF_GS5, D_MODEL)))

        for c in range(S5_HALF_TILES // S5_SCAN_TILES):
            tiles = range(c * S5_SCAN_TILES, (c + 1) * S5_SCAN_TILES)
            lanes = [slice(k * S5_HALF_CH + j * LANES, k * S5_HALF_CH + (j + 1) * LANES) for j in tiles]
            ar = [jnp.broadcast_to(abr_ref[:, ls], (batch, LANES)) for ls in lanes]
            ai = [jnp.broadcast_to(abi_ref[:, ls], (batch, LANES)) for ls in lanes]

            def step(t, carry, tiles=tiles, ar=ar, ai=ai):
                slab = pl.ds(pl.multiple_of(t * batch, batch), batch)
                out = []
                for n, j in enumerate(tiles):
                    hr, hi = carry[n]
                    nr = ar[n] * hr - ai[n] * hi + s_bur[j, slab, :]
                    ni = ar[n] * hi + ai[n] * hr + s_bui[j, slab, :]
                    s_bur[j, slab, :] = nr
                    s_bui[j, slab, :] = ni
                    out.append((nr, ni))
                return tuple(out)

            init = tuple((hr_ref[0, :, ls], hi_ref[0, :, ls]) for ls in lanes)
            fin = lax.fori_loop(0, CHUNK, step, init, unroll=True)
            for n, ls in enumerate(lanes):
                hr_ref[0, :, ls] = fin[n][0]
                hi_ref[0, :, ls] = fin[n][1]

        def states(ref):
            return jnp.concatenate([ref[j] for j in range(S5_HALF_TILES)], axis=1).astype(BF16)

        yk = jnp.dot(states(s_bur), cb_ref[k, :S5_HALF_CH, :], preferred_element_type=F32)
        yk = yk + jnp.dot(states(s_bui), cb_ref[k, S5_HALF_CH:, :], preferred_element_type=F32)
        ys.append(yk)
    y_tm = jnp.concatenate(ys, axis=1)
    for j in range(in_tiles):
        s_tm[j] = y_tm[:, j * LANES:(j + 1) * LANES]
    y = jnp.concatenate(
        [jnp.concatenate([s_tm[j, pl.ds(b, CHUNK, stride=batch), :] for j in range(in_tiles)], axis=1)
         for b in range(batch)], axis=0) + dskip_ref[...] * u
    glu = jnp.dot(jax.nn.gelu(y).astype(BF16), glu_ref[...], preferred_element_type=F32)
    y_s5 = glu[:, :D_MODEL] * jax.nn.sigmoid(glu[:, D_MODEL:])
    mix = merge[0] + merge[1] * y_s5

    x1 = x1_ref[...] + jnp.dot(mix.astype(BF16), wout_ref[...], preferred_element_type=F32)
    x1_ref[...] = x1

    xm = _rmsnorm(x1, nmoe_ref[...])
    logits = jnp.dot(xm.astype(BF16), wr_ref[...], preferred_element_type=F32) + br_ref[...]
    lane = lax.broadcasted_iota(jnp.int32, (rows, LANES), 1)
    lane_f = lane.astype(F32)
    logits = jnp.where(lane < N_EXPERTS, logits, -jnp.inf)
    vals, idxs = [], []
    for _ in range(TOP_K):
        mx = jnp.max(logits, axis=1, keepdims=True)
        am = jnp.min(jnp.where(logits == mx, lane_f, float(LANES)), axis=1, keepdims=True)
        vals.append(mx)
        idxs.append(am)
        logits = jnp.where(lane_f == am, -jnp.inf, logits)
    exps = [jnp.exp(v - vals[0]) for v in vals]
    esum = exps[0] + exps[1] + exps[2] + exps[3]
    idx_w = jnp.zeros((rows, LANES), F32)
    gate_w = jnp.zeros((rows, LANES), F32)
    for k in range(TOP_K):
        idx_w = jnp.where(lane == k, idxs[k], idx_w)
        gate_w = jnp.where(lane == k, exps[k] / esum, gate_w)
    idx_ref[...] = idx_w[:, :TOP_K].astype(jnp.int32)
    gate_ref[...] = gate_w[:, :TOP_K]
    chosen = jnp.zeros((rows, LANES), F32)
    for k in range(TOP_K):
        chosen = chosen + (lane_f == idxs[k]).astype(F32)
    cnt_ref[...] += jnp.sum(chosen, axis=0, keepdims=True)

    for s in range(TILE_ROWS):
        xm_ref[pl.ds(s, rows, stride=TILE_ROWS), :] = xm[:, s * LANES:(s + 1) * LANES]

    for phase_id, final_step in ((0, n_prompt - 1), (1, pl.num_programs(0) - 1)):
        @pl.when(i == final_step)
        def _(phase_id=phase_id):
            store = pltpu.make_async_copy(s_cx, cx_ref.at[phase_id], sem)
            store.start()
            store.wait()


def _mixer(x_prompt, x_sample, h0r, h0i, cx0, m0, w):
    batch, lp, _ = x_prompt.shape
    assert x_sample.shape[0] == batch and lp % CHUNK == 0 and x_sample.shape[1] % CHUNK == 0
    assert batch % ML_GROUP == 0
    n_prompt = lp // CHUNK
    n_sample = x_sample.shape[1] // CHUNK
    nblk = n_prompt + n_sample
    rows = batch * CHUNK
    t_all = nblk * rows
    bh = batch * ML_HEADS

    def phase(i):
        return jnp.minimum(i // n_prompt, 1)

    state_specs = [
        _const_spec((batch, S5_CH)), _const_spec((batch, S5_CH)),
        pl.BlockSpec(memory_space=pl.ANY),
        _const_spec((bh, LANES)),
    ]
    weights = (w['norm_mix'], w['w_in_a'], w['w_in_b'], w['w_in_g'], w['gate_bias'], w['abar_re'], w['abar_im'], w['bd_re'],
               w['bd_im'], w['cb'], w['d_skip'], w['w_glu'], w['ml_gn'], w['w_ml_out'], w['w_out'],
               w['norm_moe'], w['w_router'], w['b_router'])
    in_specs = ([pl.BlockSpec((batch, CHUNK, D_MODEL), lambda i: (0, jnp.minimum(i, n_prompt - 1), 0)),
                 pl.BlockSpec((batch, CHUNK, D_MODEL), lambda i: (0, jnp.maximum(i - n_prompt, 0), 0))]
                + state_specs + [_const_spec(a.shape) for a in weights])
    out_shape = (
        jax.ShapeDtypeStruct((t_all, D_MODEL), F32),
        jax.ShapeDtypeStruct((t_all * TILE_ROWS, LANES), F32),
        jax.ShapeDtypeStruct((t_all, TOP_K), jnp.int32),
        jax.ShapeDtypeStruct((t_all, TOP_K), F32),
        jax.ShapeDtypeStruct((1, LANES), F32),
        jax.ShapeDtypeStruct((2, batch, S5_CH), F32),
        jax.ShapeDtypeStruct((2, batch, S5_CH), F32),
        jax.ShapeDtypeStruct((2, bh, ML_HEAD_DIM, 2 * ML_HEAD_DIM), F32),
        jax.ShapeDtypeStruct((2, bh, LANES), F32),
    )
    out_specs = (
        pl.BlockSpec((rows, D_MODEL), lambda i: (i, 0)),
        pl.BlockSpec((rows * TILE_ROWS, LANES), lambda i: (i, 0)),
        pl.BlockSpec((rows, TOP_K), lambda i: (i, 0)),
        pl.BlockSpec((rows, TOP_K), lambda i: (i, 0)),
        pl.BlockSpec((1, LANES), lambda i: (0, 0)),
        pl.BlockSpec((1, batch, S5_CH), lambda i: (phase(i), 0, 0)),
        pl.BlockSpec((1, batch, S5_CH), lambda i: (phase(i), 0, 0)),
        pl.BlockSpec(memory_space=pl.ANY),
        pl.BlockSpec((1, bh, LANES), lambda i: (phase(i), 0, 0)),
    )
    scratch = [
        pltpu.VMEM((S5_HALF_TILES, rows, LANES), F32), pltpu.VMEM((S5_HALF_TILES, rows, LANES), F32),
        pltpu.VMEM((D_S5 // LANES, rows, LANES), F32),
        pltpu.VMEM((rows, D_ML), BF16), pltpu.VMEM((rows, D_ML), BF16), pltpu.VMEM((rows, D_ML), BF16),
        pltpu.VMEM((rows, D_ML), F32),
        pltpu.VMEM((rows, LANES), F32),
        pltpu.VMEM((rows, D_ML), BF16),
        pltpu.VMEM((bh, ML_HEAD_DIM, 2 * ML_HEAD_DIM), F32),
        pltpu.SemaphoreType.DMA(()),
    ]
    return pl.pallas_call(
        functools.partial(_mixer_kernel, batch=batch, n_prompt=n_prompt),
        grid=(nblk,),
        in_specs=in_specs,
        out_specs=out_specs,
        out_shape=out_shape,
        scratch_shapes=scratch,
        compiler_params=pltpu.CompilerParams(dimension_semantics=("arbitrary",),
                                             vmem_limit_bytes=VMEM_LIMIT),
        name="mixer",
    )(x_prompt, x_sample, h0r, h0i, cx0, m0, *weights)


def _plan_kernel(cnt_ref, idx_ref, spos_ref, runs_ref, meta_ref, s_run, s_pstart, s_lower, s_upper, *,
                 tile, n_blocks_pad):
    i = pl.program_id(0)
    lane = lax.broadcasted_iota(jnp.int32, (tile, LANES), 1)
    idx = idx_ref[...]
    onehots = [(lane == idx[:, k:k + 1]).astype(F32) for k in range(TOP_K)]
    mask = onehots[0] + onehots[1] + onehots[2] + onehots[3]

    @pl.when(i == 0)
    def _():
        cnt = cnt_ref[...]
        nblk = jnp.floor((cnt + (MOE_BLOCK - 1)) * (1.0 / MOE_BLOCK))
        lane1 = lax.broadcasted_iota(jnp.int32, (1, LANES), 1)
        pend = nblk
        shift = 1
        while shift < LANES:
            pend = pend + jnp.where(lane1 >= shift, pltpu.roll(pend, shift, axis=1), 0.0)
            shift *= 2
        s_pstart[...] = (pend - nblk) * MOE_BLOCK
        s_run[...] = jnp.zeros_like(s_run)
        r = lax.broadcasted_iota(jnp.int32, (tile, tile), 0)
        c = lax.broadcasted_iota(jnp.int32, (tile, tile), 1)
        s_lower[...] = (c < r).astype(BF16)
        s_upper[...] = (lax.broadcasted_iota(jnp.int32, (LANES, LANES), 0)
                        < lax.broadcasted_iota(jnp.int32, (LANES, LANES), 1)).astype(BF16)
        blk = lax.broadcasted_iota(jnp.int32, (n_blocks_pad, LANES), 0).astype(F32)
        lane_b = lax.broadcasted_iota(jnp.int32, (n_blocks_pad, LANES), 1)
        lane_f = lane_b.astype(F32)
        is_e = lane_b < N_EXPERTS
        done = jnp.logical_and(is_e, pend <= blk)
        e_of = jnp.minimum(jnp.sum(done.astype(F32), axis=1, keepdims=True), N_EXPERTS - 1.0)
        mine = lane_f == e_of
        blk_in_e = jnp.sum(jnp.where(mine, blk - (pend - nblk), 0.0), axis=1, keepdims=True)
        cnt_e = jnp.sum(jnp.where(mine, cnt, 0.0), axis=1, keepdims=True)
        valid = jnp.clip(cnt_e - blk_in_e * MOE_BLOCK, 0.0, float(MOE_BLOCK))
        used = jnp.sum(jnp.where(lane1 == N_EXPERTS - 1, pend, 0.0), axis=1, keepdims=True)
        owns = jnp.logical_and(is_e, nblk > 0.0)
        first = jnp.logical_and(blk_in_e == 0.0, blk[:, :1] < used).astype(F32)
        later = jnp.logical_and(owns, lane_f > e_of)
        nxt = jnp.min(jnp.where(later, lane_f, float(LANES)), axis=1, keepdims=True)
        nxt = jnp.where(nxt < float(LANES), nxt, -1.0)
        run = jnp.sum(jnp.logical_and(owns, lane_f < e_of).astype(F32), axis=1, keepdims=True)
        parity = run - 2.0 * jnp.floor(run * 0.5)
        meta = jnp.zeros((n_blocks_pad, LANES), F32)
        for col, val in enumerate((e_of, valid, used, first, nxt, parity)):
            meta = jnp.where(lane_b == col, val, meta)
        meta_ref[...] = meta.astype(jnp.int32)

    before = jnp.dot(s_lower[...], mask.astype(BF16), preferred_element_type=F32)
    tile_cnt = jnp.sum(mask, axis=0, keepdims=True)
    tile_start = jnp.sum(jnp.dot(mask.astype(BF16), s_upper[...], preferred_element_type=F32),
                         axis=0, keepdims=True)
    parity = (i % 2).astype(F32)
    base = (before + tile_start + parity * (tile * TOP_K)) * TILE_ROWS
    spos = jnp.zeros((tile, LANES), F32)
    for k in range(TOP_K):
        pk = jnp.sum(onehots[k] * base, axis=1, keepdims=True)
        spos = jnp.where(lane == k, pk, spos)
    spos_ref[...] = spos[:, :TOP_K].astype(jnp.int32)
    sub = lax.broadcasted_iota(jnp.int32, (SUBLANES, LANES), 0)
    runs = jnp.where(sub == 0, tile_cnt,
                     jnp.where(sub == 1, s_pstart[...] + s_run[...],
                               jnp.where(sub == 2, tile_start, 0.0)))
    runs_ref[0] = runs.astype(jnp.int32)
    s_run[...] += tile_cnt


def _plan(counts, idx_all, tile, n_blocks_pad):
    t_all = idx_all.shape[0]
    return pl.pallas_call(
        functools.partial(_plan_kernel, tile=tile, n_blocks_pad=n_blocks_pad),
        grid=(t_all // tile,),
        in_specs=[pl.BlockSpec((1, LANES), lambda i: (0, 0)),
                  pl.BlockSpec((tile, TOP_K), lambda i: (i, 0))],
        out_specs=(pl.BlockSpec((tile, TOP_K), lambda i: (i, 0)),
                   pl.BlockSpec((1, SUBLANES, LANES), lambda i: (i, 0, 0)),
                   pl.BlockSpec((n_blocks_pad, LANES), lambda i: (0, 0))),
        out_shape=(jax.ShapeDtypeStruct((t_all, TOP_K), jnp.int32),
                   jax.ShapeDtypeStruct((t_all // tile, SUBLANES, LANES), jnp.int32),
                   jax.ShapeDtypeStruct((n_blocks_pad, LANES), jnp.int32)),
        scratch_shapes=[pltpu.VMEM((1, LANES), F32), pltpu.VMEM((1, LANES), F32),
                        pltpu.VMEM((tile, tile), BF16), pltpu.VMEM((LANES, LANES), BF16)],
        compiler_params=pltpu.CompilerParams(dimension_semantics=("arbitrary",)),
        name="moe_plan",
    )(counts, idx_all)


COMBINE_PARTS = 4
RUN_BITS = 10
RUN_SPLIT = 7


def _run_copies(runs_ref, sorted_hbm, stage, sem, to_hbm):
    def expert(e, carry):
        n = runs_ref[e]
        first_sorted = runs_ref[LANES + e]
        first_staged = runs_ref[2 * LANES + e]
        def piece(bit):
            size = (1 << bit) * TILE_ROWS
            done = lax.shift_left(lax.shift_right_logical(n, bit + 1), bit + 1)

            @pl.when(lax.bitwise_and(lax.shift_right_logical(n, bit), 1) == 1)
            def _():
                hbm = sorted_hbm.at[pl.ds(pl.multiple_of((first_sorted + done) * TILE_ROWS, TILE_ROWS), size)]
                vmem = stage.at[pl.ds(pl.multiple_of((first_staged + done) * TILE_ROWS, TILE_ROWS), size)]
                src, dst = (vmem, hbm) if to_hbm else (hbm, vmem)
                pltpu.make_async_copy(src, dst, sem).start(priority=bit % 2)

        @pl.when(n >= (1 << RUN_SPLIT))
        def _():
            for bit in reversed(range(RUN_SPLIT, RUN_BITS)):
                piece(bit)

        for bit in reversed(range(RUN_SPLIT)):
            piece(bit)
        return carry

    lax.fori_loop(0, N_EXPERTS, expert, 0)


def _dispatch_kernel(valid_ref, spos_ref, runs_ref, xm_ref, buf_ref, zeros, stage, sem, zsem, *, tile,
                     n_blocks, n_tiles):
    i = pl.program_id(0)
    slot = i % 2
    n_rows = tile * TOP_K * TILE_ROWS

    def drain(s):
        pltpu.make_async_copy(buf_ref.at[pl.ds(0, n_rows)], buf_ref.at[pl.ds(0, n_rows)], sem.at[s]).wait()

    pad_rows = (n_blocks * MOE_BLOCK - n_tiles * tile * TOP_K) * TILE_ROWS

    @pl.when(i == 0)
    def _():
        zeros[...] = jnp.zeros_like(zeros)

        def fill(b, carry):
            filled = valid_ref[b]
            n = MOE_BLOCK - filled
            for bit in reversed(range(RUN_BITS)):
                size = (1 << bit) * TILE_ROWS
                done = lax.shift_left(lax.shift_right_logical(n, bit + 1), bit + 1)

                @pl.when(lax.bitwise_and(lax.shift_right_logical(n, bit), 1) == 1)
                def _(size=size, done=done, bit=bit):
                    first = (b * MOE_BLOCK + filled + done) * TILE_ROWS
                    pltpu.make_async_copy(
                        zeros.at[pl.ds(0, size)],
                        buf_ref.at[pl.ds(pl.multiple_of(first, TILE_ROWS), size)], zsem).start(
                            priority=bit % 2)
            return carry

        lax.fori_loop(0, n_blocks, fill, 0)

    @pl.when(i >= 2)
    def _():
        drain(slot)

    for j in range(tile):
        tok = xm_ref[j * TILE_ROWS:(j + 1) * TILE_ROWS, :]
        for k in range(TOP_K):
            stage[pl.ds(pl.multiple_of(spos_ref[j * TOP_K + k], TILE_ROWS), TILE_ROWS), :] = tok
    slot_rows = stage.at[pl.ds(pl.multiple_of(slot * n_rows, n_rows), n_rows)]
    _run_copies(runs_ref, buf_ref, slot_rows, sem.at[slot], to_hbm=True)

    last = pl.num_programs(0) - 1

    @pl.when(i == last)
    def _():
        drain(slot)
        if pad_rows:
            pltpu.make_async_copy(buf_ref.at[pl.ds(0, pad_rows)], buf_ref.at[pl.ds(0, pad_rows)],
                                  zsem).wait()

    @pl.when(jnp.logical_and(i == last, i >= 1))
    def _():
        drain(1 - slot)


def _dispatch(block_valid, spos_flat, runs_flat, xm_tiles, tile, n_blocks):
    t_all = xm_tiles.shape[0] // TILE_ROWS
    grid_spec = pltpu.PrefetchScalarGridSpec(
        num_scalar_prefetch=1,
        grid=(t_all // tile,),
        in_specs=[pl.BlockSpec((tile * TOP_K,), lambda i, va: (i,), memory_space=pltpu.SMEM),
                  pl.BlockSpec((SUBLANES * LANES,), lambda i, va: (i,), memory_space=pltpu.SMEM),
                  pl.BlockSpec((tile * TILE_ROWS, LANES), lambda i, va: (i, 0))],
        out_specs=pl.BlockSpec(memory_space=pl.ANY),
        scratch_shapes=[pltpu.VMEM((MOE_BLOCK * TILE_ROWS, LANES), F32),
                        pltpu.VMEM((2 * tile * TOP_K * TILE_ROWS, LANES), F32),
                        pltpu.SemaphoreType.DMA((2,)), pltpu.SemaphoreType.DMA(())],
    )
    return pl.pallas_call(
        functools.partial(_dispatch_kernel, tile=tile, n_blocks=n_blocks, n_tiles=t_all // tile),
        grid_spec=grid_spec,
        out_shape=jax.ShapeDtypeStruct((n_blocks * MOE_BLOCK * TILE_ROWS, LANES), F32),
        compiler_params=pltpu.CompilerParams(dimension_semantics=("arbitrary",),
                                             vmem_limit_bytes=VMEM_LIMIT),
        name="moe_dispatch",
    )(block_valid, spos_flat, runs_flat, xm_tiles)


def _expert_kernel(be_ref, used_ref, first_ref, next_ref, slot_ref, valid_ref, x_ref, wgu_ref, bgu_ref,
                   wd_ref, bd_ref, o_ref, wbuf_gu, wbuf_d, s_wgu, s_wd, sem):
    i = pl.program_id(0)

    @pl.when(i >= used_ref[0])
    def _():
        o_ref[...] = jnp.zeros_like(o_ref)

    def weight_copies(e, slot):
        return (pltpu.make_async_copy(wgu_ref.at[e], wbuf_gu.at[slot], sem.at[0, slot]),
                pltpu.make_async_copy(wd_ref.at[e], wbuf_d.at[slot], sem.at[1, slot]))

    @pl.when(i == 0)
    def _():
        for cp in weight_copies(be_ref[0], slot_ref[0]):
            cp.start()

    @pl.when(first_ref[i] == 1)
    def _():
        slot = slot_ref[i]
        for cp in weight_copies(be_ref[i], slot):
            cp.wait()

        @pl.when(next_ref[i] >= 0)
        def _():
            for cp in weight_copies(next_ref[i], 1 - slot):
                cp.start()

        s_wgu[...] = wbuf_gu[slot].astype(BF16)
        s_wd[...] = wbuf_d[slot].astype(BF16)

    def ffn(nrows):
        pieces = [x_ref[pl.ds(s, nrows, stride=TILE_ROWS), :] for s in range(TILE_ROWS)]
        x = jnp.concatenate(pieces, axis=1).astype(BF16)
        gu = jnp.dot(x, s_wgu[...], preferred_element_type=F32) + bgu_ref[0]
        g = jnp.minimum(gu[:, :D_FF], SWIGLU_LIMIT)
        up = jnp.clip(gu[:, D_FF:], -SWIGLU_LIMIT, SWIGLU_LIMIT)
        hdn = (up + 1.0) * (g * jax.nn.sigmoid(SWIGLU_ALPHA * g))
        out = jnp.dot(hdn.astype(BF16), s_wd[...], preferred_element_type=F32) + bd_ref[0]
        for s in range(TILE_ROWS):
            o_ref[pl.ds(s, nrows, stride=TILE_ROWS), :] = out[:, s * LANES:(s + 1) * LANES]
        if nrows < MOE_BLOCK:
            o_ref[nrows * TILE_ROWS:, :] = jnp.zeros(((MOE_BLOCK - nrows) * TILE_ROWS, LANES), F32)

    in_use = i < used_ref[0]
    quarter = MOE_BLOCK // EXPERT_QUARTERS
    assert quarter & (quarter - 1) == 0
    quarters = lax.shift_right_logical(valid_ref[i] + (quarter - 1), quarter.bit_length() - 1)
    for nq in range(1, EXPERT_QUARTERS + 1):
        cond = (quarters <= 1) if nq == 1 else (quarters == nq)

        @pl.when(jnp.logical_and(in_use, cond))
        def _(nq=nq):
            ffn(nq * quarter)


def _experts(meta, buf, wgu, bgu, wd, bd, n_blocks):
    blk_rows = MOE_BLOCK * TILE_ROWS
    block_e, n_used = meta[:n_blocks, 0], meta[:1, 2]
    first, nxt, slot = meta[:n_blocks, 3], meta[:n_blocks, 4], meta[:n_blocks, 5]
    valid = meta[:n_blocks, 1]

    def in_row_map(i, be, nu, *_):
        return (jnp.minimum(i, nu[0] - 1), 0)

    def b_map(i, be, *_):
        return (be[i], 0, 0)

    grid_spec = pltpu.PrefetchScalarGridSpec(
        num_scalar_prefetch=6,
        grid=(n_blocks,),
        in_specs=[pl.BlockSpec((blk_rows, LANES), in_row_map),
                  pl.BlockSpec(memory_space=pl.ANY),
                  pl.BlockSpec((1, 1, 2 * D_FF), b_map),
                  pl.BlockSpec(memory_space=pl.ANY),
                  pl.BlockSpec((1, 1, D_MODEL), b_map)],
        out_specs=pl.BlockSpec((blk_rows, LANES), lambda i, *_: (i, 0)),
        scratch_shapes=[pltpu.VMEM((2, D_MODEL, 2 * D_FF), F32), pltpu.VMEM((2, D_FF, D_MODEL), F32),
                        pltpu.VMEM((D_MODEL, 2 * D_FF), BF16), pltpu.VMEM((D_FF, D_MODEL), BF16),
                        pltpu.SemaphoreType.DMA((2, 2))],
    )
    return pl.pallas_call(
        _expert_kernel,
        grid_spec=grid_spec,
        out_shape=jax.ShapeDtypeStruct(buf.shape, F32),
        compiler_params=pltpu.CompilerParams(dimension_semantics=("arbitrary",),
                                             vmem_limit_bytes=VMEM_LIMIT),
        name="moe_experts",
    )(block_e, n_used, first, nxt, slot, valid, buf, wgu, bgu, wd, bd)


def _combine_kernel(spos_ref, runs_ref, next_runs_ref, gate_ref, x1_ref, rows_ref, nf_ref, yp_ref,
                    ys_ref, stage, *rest, batch, n_prompt):
    *gbufs, sem = rest
    i = pl.program_id(0)
    tile = batch * CHUNK
    n_rows = tile * TOP_K * TILE_ROWS
    slot = i % 2

    def slot_rows(s):
        return stage.at[pl.ds(pl.multiple_of(s * n_rows, n_rows), n_rows)]

    @pl.when(i == 0)
    def _():
        _run_copies(runs_ref, rows_ref, slot_rows(0), sem.at[0], to_hbm=False)

    @pl.when(i + 1 < pl.num_programs(0))
    def _():
        _run_copies(next_runs_ref, rows_ref, slot_rows(1 - slot), sem.at[1 - slot], to_hbm=False)

    pltpu.make_async_copy(rows_ref.at[pl.ds(0, n_rows)], slot_rows(slot), sem.at[slot]).wait()

    part = tile // len(gbufs)
    gates = gate_ref[...]

    def reduce_part(q):
        rows = slice(q * part, (q + 1) * part)
        pieces = []
        for s in range(TILE_ROWS):
            piece = x1_ref[rows, s * LANES:(s + 1) * LANES]
            for k in range(TOP_K):
                piece = piece + gates[rows, k:k + 1] * gbufs[q][k, pl.ds(s, part, stride=TILE_ROWS), :]
            pieces.append(piece)
        return _rmsnorm(jnp.concatenate(pieces, axis=1), nf_ref[...])

    outs = []
    for q, gbuf in enumerate(gbufs):
        for j in range(part):
            for k in range(TOP_K):
                p = spos_ref[(q * part + j) * TOP_K + k]
                gbuf[k, j * TILE_ROWS:(j + 1) * TILE_ROWS, :] = (
                    stage[pl.ds(pl.multiple_of(p, TILE_ROWS), TILE_ROWS), :])
        if q:
            outs.append(reduce_part(q - 1))
    outs.append(reduce_part(len(gbufs) - 1))
    y = jnp.concatenate(outs, axis=0).reshape(batch, CHUNK, D_MODEL)

    @pl.when(i < n_prompt)
    def _():
        yp_ref[...] = y

    @pl.when(i >= n_prompt)
    def _():
        ys_ref[...] = y


def _combine(spos_flat, runs_flat, gates, x1, out_rows, norm_final, batch, lp, ls):
    tile = batch * CHUNK
    n_prompt = lp // CHUNK
    n_sample = ls // CHUNK
    last = n_prompt + n_sample - 1
    return pl.pallas_call(
        functools.partial(_combine_kernel, batch=batch, n_prompt=n_prompt),
        grid=(n_prompt + n_sample,),
        in_specs=[pl.BlockSpec((tile * TOP_K,), lambda i: (i,), memory_space=pltpu.SMEM),
                  pl.BlockSpec((SUBLANES * LANES,), lambda i: (i,), memory_space=pltpu.SMEM),
                  pl.BlockSpec((SUBLANES * LANES,), lambda i: (jnp.minimum(i + 1, last),),
                               memory_space=pltpu.SMEM),
                  pl.BlockSpec((tile, TOP_K), lambda i: (i, 0)),
                  pl.BlockSpec((tile, D_MODEL), lambda i: (i, 0)),
                  pl.BlockSpec(memory_space=pl.ANY),
                  pl.BlockSpec((1, D_MODEL), lambda i: (0, 0))],
        out_specs=(pl.BlockSpec((batch, CHUNK, D_MODEL), lambda i: (0, jnp.minimum(i, n_prompt - 1), 0)),
                   pl.BlockSpec((batch, CHUNK, D_MODEL), lambda i: (0, jnp.maximum(i - n_prompt, 0), 0))),
        out_shape=(jax.ShapeDtypeStruct((batch, lp, D_MODEL), F32),
                   jax.ShapeDtypeStruct((batch, ls, D_MODEL), F32)),
        scratch_shapes=([pltpu.VMEM((2 * tile * TOP_K * TILE_ROWS, LANES), F32)]
                        + [pltpu.VMEM((TOP_K, tile // COMBINE_PARTS * TILE_ROWS, LANES), F32)
                           for _ in range(COMBINE_PARTS)]
                        + [pltpu.SemaphoreType.DMA((2,))]),
        compiler_params=pltpu.CompilerParams(dimension_semantics=("arbitrary",),
                                             vmem_limit_bytes=VMEM_LIMIT),
        name="moe_combine",
    )(spos_flat, runs_flat, runs_flat, gates, x1, out_rows, norm_final)


def _prep_weights(norm_mix, w_in, b_ig, b_fg, s5_a_re, s5_a_im, s5_log_dt, s5_b_re, s5_b_im,
                  s5_c_re, s5_c_im, s5_d, w_s5_glu, ml_gn, w_ml_out, w_out, norm_moe, w_router,
                  b_router):
    w = w_in[0]
    o_ig = D_S5 + 4 * D_ML
    o_gs5 = o_ig + 2 * ML_HEADS
    w_a = w[:, :o_ig].astype(BF16)
    w_b = w[:, o_gs5:].astype(BF16)
    w_g = jnp.concatenate([w[:, o_ig:o_gs5], jnp.zeros((D_MODEL, LANES - 2 * ML_HEADS), F32)],
                          axis=1).astype(BF16)
    gate_bias = jnp.concatenate([b_ig[0], b_fg[0], jnp.zeros((LANES - 2 * ML_HEADS,), F32)])[None]

    ar, ai = s5_a_re[0], s5_a_im[0]
    dt = jnp.exp(s5_log_dt[0])[:, None]
    mag = jnp.exp(dt * ar)
    abar_re = mag * jnp.cos(dt * ai)
    abar_im = mag * jnp.sin(dt * ai)
    den = ar * ar + ai * ai
    fr = ((abar_re - 1.0) * ar + abar_im * ai) / den
    fi = (abar_im * ar - (abar_re - 1.0) * ai) / den
    br, bi = s5_b_re[0], s5_b_im[0]
    bbar_re = fr[..., None] * br - fi[..., None] * bi
    bbar_im = fr[..., None] * bi + fi[..., None] * br
    gh = S5_GROUPS // 2
    same_group = (jnp.arange(gh)[:, None, None, None] == jnp.arange(gh)[None, None, :, None])

    def blockdiag_in(bb):
        t = bb.reshape(2, gh, S5_STATE, S5_GROUP).transpose(0, 1, 3, 2)
        full = jnp.where(same_group[None], t[:, :, :, None, :], 0.0)
        return full.reshape(2, gh * S5_GROUP, gh * S5_STATE).astype(BF16)

    def blockdiag_out(cc):
        t = cc.reshape(2, gh, S5_GROUP, S5_STATE).transpose(0, 1, 3, 2)
        full = jnp.where(same_group[None], t[:, :, :, None, :], 0.0)
        return full.reshape(2, gh * S5_STATE, gh * S5_GROUP)

    cb = jnp.concatenate([blockdiag_out(s5_c_re[0]), -blockdiag_out(s5_c_im[0])], axis=1).astype(BF16)
    w_router_p = jnp.concatenate(
        [w_router[0], jnp.zeros((D_MODEL, LANES - N_EXPERTS), F32)], axis=1).astype(BF16)
    b_router_p = jnp.concatenate([b_router[0], jnp.zeros((LANES - N_EXPERTS,), F32)])[None]
    return dict(
        norm_mix=norm_mix[0][None], w_in_a=w_a, w_in_b=w_b, w_in_g=w_g, gate_bias=gate_bias,
        abar_re=abar_re.reshape(1, S5_CH), abar_im=abar_im.reshape(1, S5_CH),
        bd_re=blockdiag_in(bbar_re), bd_im=blockdiag_in(bbar_im), cb=cb,
        d_skip=s5_d[0][None], w_glu=w_s5_glu[0].astype(BF16), ml_gn=ml_gn[0][None],
        w_ml_out=w_ml_out[0].astype(BF16), w_out=w_out[0].astype(BF16),
        norm_moe=norm_moe[0][None], w_router=w_router_p, b_router=b_router_p)


def kernel(x_prompt, x_sample, state_s5_re, state_s5_im, state_ml_C, state_ml_n, state_ml_m, norm_mix, w_in, b_ig, b_fg, s5_a_re, s5_a_im, s5_log_dt, s5_b_re, s5_b_im, s5_c_re, s5_c_im, s5_d, w_s5_glu, ml_gn, w_ml_out, w_out, norm_moe, w_router, b_router, w_gate_up, b_gate_up, w_down, b_down, norm_final):
    w = _prep_weights(norm_mix, w_in, b_ig, b_fg, s5_a_re, s5_a_im, s5_log_dt, s5_b_re, s5_b_im,
                      s5_c_re, s5_c_im, s5_d, w_s5_glu, ml_gn, w_ml_out, w_out, norm_moe,
                      w_router, b_router)
    batch, lp, _ = x_prompt.shape
    ls = x_sample.shape[1]
    bh = batch * ML_HEADS
    cx0 = jnp.concatenate(
        [jnp.swapaxes(state_ml_C[0].reshape(bh, ML_HEAD_DIM, ML_HEAD_DIM), 1, 2),
         jnp.broadcast_to(state_ml_n[0].reshape(bh, ML_HEAD_DIM, 1), (bh, ML_HEAD_DIM, ML_HEAD_DIM))],
        axis=2)
    x1, xm, idx, gates, counts, hr, hi, cx, m = _mixer(
        x_prompt, x_sample,
        state_s5_re[0].reshape(batch, S5_CH), state_s5_im[0].reshape(batch, S5_CH), cx0,
        jnp.broadcast_to(state_ml_m[0].reshape(bh, 1), (bh, LANES)), w)
    c = jnp.swapaxes(cx[:, :, :, :ML_HEAD_DIM], 2, 3)
    n = cx[:, :, :, ML_HEAD_DIM]

    tile = batch * CHUNK
    t_all = idx.shape[0]
    n_blocks = -(-(t_all * TOP_K + N_EXPERTS * (MOE_BLOCK - 1)) // MOE_BLOCK)
    n_blocks_pad = -(-n_blocks // SUBLANES) * SUBLANES
    spos, runs, meta = _plan(counts, idx, tile, n_blocks_pad)
    spos_flat = spos.reshape(t_all * TOP_K)
    runs_flat = runs.reshape(-1)
    buf = _dispatch(meta[:n_blocks, 1], spos_flat, runs_flat, xm, tile, n_blocks)
    out_rows = _experts(meta, buf, w_gate_up[0], b_gate_up[0][:, None, :], w_down[0],
                        b_down[0][:, None, :], n_blocks)
    y_prompt, y_sample = _combine(spos_flat, runs_flat, gates, x1, out_rows, norm_final[None], batch,
                                  lp, ls)

    def states(p):
        return (hr[p].reshape(1, batch, S5_GROUPS, S5_STATE), hi[p].reshape(1, batch, S5_GROUPS, S5_STATE),
                c[p].reshape(1, batch, ML_HEADS, ML_HEAD_DIM, ML_HEAD_DIM),
                n[p].reshape(1, batch, ML_HEADS, ML_HEAD_DIM), m[p, :, 0].reshape(1, batch, ML_HEADS))

    return (y_prompt, y_sample) + states(0) + states(1)
```
